```python
import jax, jax.numpy as jnp
from jax import lax
import numpy as np

D_MODEL = 1024
BATCH = 8
SEQ = 2048
DEPTH = 1
DEC_BATCH = 32
DEC_SEQ = 32
PAST_LEN = 4096

CHUNK = 64
Q_BLOCK = 128
FOX_HEADS = 8
FOX_HEAD_DIM = 64
FOX_WIDTH = FOX_HEADS * FOX_HEAD_DIM
GLA_HEADS = 4
GLA_DK = 64
GLA_DV = 128
GLA_KW = GLA_HEADS * GLA_DK
GLA_VW = GLA_HEADS * GLA_DV
GLA_GATE_RANK = 16
GLA_GATE_TEMP = 16.0
MIX_WIDTH = FOX_WIDTH + GLA_VW
D_FF = -(-8 * D_MODEL // (3 * 256)) * 256
EPS = 1e-6

SPLIT_SIZES = (FOX_WIDTH, FOX_WIDTH, FOX_WIDTH, FOX_HEADS,
               GLA_KW, GLA_KW, GLA_VW, GLA_GATE_RANK, GLA_VW)
SPLIT_IDX = tuple(int(s) for s in np.cumsum(SPLIT_SIZES)[:-1])
IN_WIDTH = int(sum(SPLIT_SIZES))

kernel_name = 'fox_gla_parallel_heads_stream_step'


def _rmsnorm(x, g):
    xf = x.astype(jnp.float32)
    y = xf * lax.rsqrt(jnp.mean(jnp.square(xf), axis=-1, keepdims=True) + EPS)
    return (y * g.astype(jnp.float32)).astype(x.dtype)


def _project(h, w_in, w_gate2, b_gate2, b_forget):
    B, T, _ = h.shape
    z = h @ w_in
    fq, fk, fv, fl, gq, gk, gv, gg, gr = jnp.split(z, SPLIT_IDX, axis=-1)
    heads = lambda a, n, d: a.reshape(B, T, n, d)
    logf = jax.nn.log_sigmoid((fl + b_forget).astype(jnp.float32))
    glog = jax.nn.log_sigmoid((gg @ w_gate2 + b_gate2).astype(jnp.float32)) / GLA_GATE_TEMP
    return (heads(fq, FOX_HEADS, FOX_HEAD_DIM), heads(fk, FOX_HEADS, FOX_HEAD_DIM),
            heads(fv, FOX_HEADS, FOX_HEAD_DIM), logf,
            heads(gq, GLA_HEADS, GLA_DK), heads(gk, GLA_HEADS, GLA_DK),
            heads(gv, GLA_HEADS, GLA_DV), heads(glog, GLA_HEADS, GLA_DK), gr)


def _fox_prompt(q, k, v, logf):
    B, T, H, D = q.shape
    scale = D ** -0.5
    c = jnp.cumsum(logf, axis=1).transpose(0, 2, 1)
    n_blocks = T // Q_BLOCK
    q_blocks = q.reshape(B, n_blocks, Q_BLOCK, H, D).swapaxes(0, 1)
    c_blocks = c.reshape(B, H, n_blocks, Q_BLOCK).transpose(2, 0, 1, 3)
    key_pos = jnp.arange(T)

    def one_block(args):
        blk, qb, cb = args
        s = jnp.einsum('bqhd,bkhd->bhqk', qb, k).astype(jnp.float32) * scale
        s = s + cb[..., None] - c[:, :, None, :]
        q_pos = blk * Q_BLOCK + jnp.arange(Q_BLOCK)
        s = jnp.where(key_pos[None, :] <= q_pos[:, None], s, -jnp.inf)
        p = jax.nn.softmax(s, axis=-1).astype(v.dtype)
        return jnp.einsum('bhqk,bkhd->bqhd', p, v)

    o = lax.map(one_block, (jnp.arange(n_blocks), q_blocks, c_blocks))
    return o.swapaxes(0, 1).reshape(B, T, H * D)


def _fox_sample(q, k, v, logf, ck, cv, clogf):
    B, T, H, D = q.shape
    P = ck.shape[1]
    scale = D ** -0.5
    clogf = clogf.astype(jnp.float32)
    c_new = jnp.cumsum(logf, axis=1).transpose(0, 2, 1)
    suffix = (jnp.cumsum(clogf[:, ::-1], axis=1)[:, ::-1] - clogf).transpose(0, 2, 1)
    s_past = (jnp.einsum('bqhd,bkhd->bhqk', q, ck).astype(jnp.float32) * scale
              + c_new[..., None] + suffix[:, :, None, :])
    s_new = (jnp.einsum('bqhd,bkhd->bhqk', q, k).astype(jnp.float32) * scale
             + c_new[..., None] - c_new[:, :, None, :])
    s_new = jnp.where(jnp.tril(jnp.ones((T, T), bool)), s_new, -jnp.inf)
    p = jax.nn.softmax(jnp.concatenate([s_past, s_new], axis=-1), axis=-1).astype(v.dtype)
    o = (jnp.einsum('bhqk,bkhd->bqhd', p[..., :P], cv)
         + jnp.einsum('bhqk,bkhd->bqhd', p[..., P:], v))
    return o.reshape(B, T, H * D)


def _gla_chunk(S, inp):
    q, k, v, g = inp
    L = q.shape[1]
    qf = q.astype(jnp.float32) * (GLA_DK ** -0.5)
    kf = k.astype(jnp.float32)
    vf = v.astype(jnp.float32)
    b = jnp.cumsum(g, axis=1)
    o_inter = jnp.einsum('blhk,bhkv->blhv', qf * jnp.exp(b), S)
    tri = jnp.tril(jnp.ones((L, L), bool))[None, :, :, None, None]
    diff = b[:, :, None] - b[:, None, :]
    decay = jnp.exp(jnp.where(tri, diff, -jnp.inf))
    A = jnp.einsum('bthk,bshk,btshk->bhts', qf, kf, decay)
    o_intra = jnp.einsum('bhts,bshv->bthv', A, vf)
    b_last = b[:, -1]
    k_dec = kf * jnp.exp(b_last[:, None] - b)
    S_new = jnp.exp(b_last)[..., None] * S + jnp.einsum('blhk,blhv->bhkv', k_dec, vf)
    return S_new, o_inter + o_intra


def _gla_prompt(q, k, v, g):
    B, T = q.shape[:2]
    n = T // CHUNK
    to_chunks = lambda a: a.reshape(B, n, CHUNK, *a.shape[2:]).swapaxes(0, 1)
    S0 = jnp.zeros((B, GLA_HEADS, GLA_DK, GLA_DV), jnp.float32)
    S, o = lax.scan(_gla_chunk, S0, (to_chunks(q), to_chunks(k), to_chunks(v), to_chunks(g)))
    return o.swapaxes(0, 1).reshape(B, T, GLA_HEADS, GLA_DV), S


def _merge(fox_o, gla_o, gr, gla_norm_g, w_out):
    B, T = fox_o.shape[:2]
    gn = gla_o * lax.rsqrt(jnp.mean(jnp.square(gla_o), axis=-1, keepdims=True) + EPS)
    gated = gn.reshape(B, T, GLA_VW) * gla_norm_g.astype(jnp.float32) * jax.nn.silu(gr.astype(jnp.float32))
    mix = jnp.concatenate([fox_o, gated.astype(fox_o.dtype)], axis=-1)
    return mix @ w_out


def _swiglu(h, w_gate, w_up, w_down):
    return (jax.nn.silu(h @ w_gate) * (h @ w_up)) @ w_down


def setup_inputs(seed: int = 0) -> dict:
    key = jax.random.key(seed)
    ks = jax.random.split(key, 20)
    nrm = lambda k, shape, s=1.0: s * jax.random.normal(k, shape, jnp.float32)
    return {
        'x_prompt': nrm(ks[0], (BATCH, SEQ, D_MODEL)),
        'x_sample': nrm(ks[1], (DEC_BATCH, DEC_SEQ, D_MODEL)),
        'cache_fox_k': nrm(ks[2], (DEPTH, DEC_BATCH, PAST_LEN, FOX_HEADS, FOX_HEAD_DIM)),
        'cache_fox_v': nrm(ks[3], (DEPTH, DEC_BATCH, PAST_LEN, FOX_HEADS, FOX_HEAD_DIM)),
        'cache_fox_logf': jax.nn.log_sigmoid(1.0 + nrm(ks[4], (DEPTH, DEC_BATCH, PAST_LEN, FOX_HEADS))),
        'state_gla': nrm(ks[5], (DEPTH, DEC_BATCH, GLA_HEADS, GLA_DK, GLA_DV), 0.5),
        'norm1_g': 1.0 + nrm(ks[6], (DEPTH, D_MODEL), 0.02),
        'w_in': nrm(ks[7], (DEPTH, D_MODEL, IN_WIDTH), D_MODEL ** -0.5),
        'w_gate2': nrm(ks[8], (DEPTH, GLA_GATE_RANK, GLA_KW), GLA_GATE_RANK ** -0.5),
        'b_gate2': nrm(ks[9], (DEPTH, GLA_KW), 0.1),
        'b_forget': 1.0 + nrm(ks[10], (DEPTH, FOX_HEADS), 0.5),
        'gla_norm_g': 1.0 + nrm(ks[11], (DEPTH, GLA_VW), 0.02),
        'w_out': nrm(ks[12], (DEPTH, MIX_WIDTH, D_MODEL), MIX_WIDTH ** -0.5),
        'norm2_g': 1.0 + nrm(ks[13], (DEPTH, D_MODEL), 0.02),
        'w_gate': nrm(ks[14], (DEPTH, D_MODEL, D_FF), D_MODEL ** -0.5),
        'w_up': nrm(ks[15], (DEPTH, D_MODEL, D_FF), D_MODEL ** -0.5),
        'w_down': nrm(ks[16], (DEPTH, D_FF, D_MODEL), D_FF ** -0.5),
        'final_norm_g': 1.0 + nrm(ks[17], (D_MODEL,), 0.02),
    }


def reference(x_prompt, x_sample, cache_fox_k, cache_fox_v, cache_fox_logf, state_gla,
              norm1_g, w_in, w_gate2, b_gate2, b_forget, gla_norm_g, w_out,
              norm2_g, w_gate, w_up, w_down, final_norm_g):
    y_p, y_s = x_prompt, x_sample
    k_p, v_p, lf_p, s_p = [], [], [], []
    k_s, v_s, lf_s, s_s = [], [], [], []
    for layer in range(DEPTH):
        h = _rmsnorm(y_p, norm1_g[layer])
        fq, fk, fv, logf, gq, gk, gv, glog, gr = _project(h, w_in[layer], w_gate2[layer],
                                                          b_gate2[layer], b_forget[layer])
        fox_o = _fox_prompt(fq, fk, fv, logf)
        gla_o, S = _gla_prompt(gq, gk, gv, glog)
        y_p = y_p + _merge(fox_o, gla_o, gr, gla_norm_g[layer], w_out[layer])
        y_p = y_p + _swiglu(_rmsnorm(y_p, norm2_g[layer]), w_gate[layer], w_up[layer], w_down[layer])
        k_p.append(fk); v_p.append(fv); lf_p.append(logf); s_p.append(S)

        h = _rmsnorm(y_s, norm1_g[layer])
        fq, fk, fv, logf, gq, gk, gv, glog, gr = _project(h, w_in[layer], w_gate2[layer],
                                                          b_gate2[layer], b_forget[layer])
        fox_o = _fox_sample(fq, fk, fv, logf, cache_fox_k[layer], cache_fox_v[layer],
                            cache_fox_logf[layer])
        S, gla_o = _gla_chunk(state_gla[layer].astype(jnp.float32), (gq, gk, gv, glog))
        y_s = y_s + _merge(fox_o, gla_o, gr, gla_norm_g[layer], w_out[layer])
        y_s = y_s + _swiglu(_rmsnorm(y_s, norm2_g[layer]), w_gate[layer], w_up[layer], w_down[layer])
        k_s.append(fk); v_s.append(fv); lf_s.append(logf); s_s.append(S)

    y_prompt = _rmsnorm(y_p, final_norm_g)
    y_sample = _rmsnorm(y_s, final_norm_g)
    return (y_prompt, y_sample,
            jnp.stack(k_p), jnp.stack(v_p), jnp.stack(lf_p), jnp.stack(s_p),
            jnp.stack(k_s), jnp.stack(v_s), jnp.stack(lf_s), jnp.stack(s_s))
```

```python
import functools

import numpy as np
import jax
import jax.numpy as jnp
from jax import lax
from jax.experimental import pallas as pl
from jax.experimental.pallas import tpu as pltpu

D_MODEL = 1024
FOX_HEADS = 8
FOX_HEAD_DIM = 64
FOX_WIDTH = FOX_HEADS * FOX_HEAD_DIM
GLA_HEADS = 4
GLA_DK = 64
GLA_DV = 128
GLA_KW = GLA_HEADS * GLA_DK
GLA_VW = GLA_HEADS * GLA_DV
GLA_GATE_RANK = 16
GLA_GATE_TEMP = 16.0
D_FF = 2816
EPS = 1e-6

LANES = 128
GLA_SUB = 16
VMEM_LIMIT = 56 * 1024 * 1024

F32 = jnp.float32
BF16 = jnp.bfloat16
NEG_BIG = -1e30


def _log_sigmoid(x):
    return jnp.minimum(x, 0.0) - jnp.log1p(jnp.exp(-jnp.abs(x)))


def _sigmoid(x):
    return 1.0 / (1.0 + jnp.exp(-x))


def _split3_f32(x):
    hi = x.astype(BF16).astype(F32)
    r = x - hi
    mid = r.astype(BF16).astype(F32)
    lo = (r - mid).astype(BF16).astype(F32)
    return hi, mid, lo


def _dot(a, b):
    return jnp.dot(a, b, preferred_element_type=F32)


def _dot_nt(a, b):
    return lax.dot_general(a, b, (((1,), (1,)), ((), ())), preferred_element_type=F32)


def _dot_tn(a, b):
    return lax.dot_general(a, b, (((0,), (0,)), ((), ())), preferred_element_type=F32)


def _rms(x, g):
    return x * lax.rsqrt(jnp.mean(x * x, axis=-1, keepdims=True) + EPS) * g


def _proj_kernel(x_ref, g1_ref, wa_ref, wb_ref, wc_ref, wflt_ref, wg2_ref, bg2_ref,
                 bfc_ref, bfr_ref, tri_ref,
                 q_ref, k_ref, v_ref, logf_ref, ct_ref, gq_ref, gk_ref, gv_ref, glog_ref, gr_ref,
                 carry_ref, *, tiles_per_seq):
    i = pl.program_id(0)
    h = _rms(x_ref[...], g1_ref[...]).astype(BF16)

    za = _dot(h, wa_ref[...])
    q_ref[...] = (za[:, :FOX_WIDTH] * (FOX_HEAD_DIM ** -0.5)).astype(BF16)
    k_ref[...] = za[:, FOX_WIDTH:2 * FOX_WIDTH]
    v_ref[...] = za[:, 2 * FOX_WIDTH:]

    zb = _dot(h, wb_ref[...])
    gq_ref[...] = zb[:, :GLA_KW]
    gk_ref[...] = zb[:, GLA_KW:2 * GLA_KW]
    gv_ref[...] = zb[:, 2 * GLA_KW:2 * GLA_KW + GLA_VW].astype(BF16)
    gr_ref[...] = zb[:, 2 * GLA_KW + GLA_VW:]

    zc = _dot(h, wc_ref[...])
    logf_ref[...] = _log_sigmoid(zc[:, :FOX_HEADS] + bfr_ref[...])
    gpre = _dot(zc.astype(BF16), wg2_ref[...]) + bg2_ref[...]
    glog_ref[...] = _log_sigmoid(gpre) * (1.0 / GLA_GATE_TEMP)

    logf_t = _log_sigmoid(_dot_nt(wflt_ref[...], h) + bfc_ref[...])
    parts = jnp.concatenate(_split3_f32(logf_t), axis=0).astype(BF16)
    cs = _dot(parts, tri_ref[...])
    cs = cs[0:8] + cs[8:16] + cs[16:24]

    @pl.when(i % tiles_per_seq == 0)
    def _():
        carry_ref[...] = jnp.zeros_like(carry_ref)

    ct = cs + carry_ref[:, 0:1]
    ct_ref[...] = ct
    carry_ref[...] = jnp.broadcast_to(ct[:, ct.shape[1] - 1:], carry_ref.shape)


def _proj(x2d, seq_len, prm, tm=512):
    n = x2d.shape[0]
    tm = min(tm, n)
    seg = seq_len
    tiles_per_seq = max(seg // tm, 1)
    idx = np.arange(tm)
    tri = ((idx[:, None] <= idx[None, :]) & (idx[:, None] // seg == idx[None, :] // seg))
    tri = jnp.asarray(tri.astype(np.float32), dtype=BF16)
    const = lambda shape: pl.BlockSpec(shape, lambda i: (0, 0))
    row = lambda w: pl.BlockSpec((tm, w), lambda i: (i, 0))
    out_shape = (
        jax.ShapeDtypeStruct((n, FOX_WIDTH), BF16),
        jax.ShapeDtypeStruct((n, FOX_WIDTH), F32),
        jax.ShapeDtypeStruct((n, FOX_WIDTH), F32),
        jax.ShapeDtypeStruct((n, FOX_HEADS), F32),
        jax.ShapeDtypeStruct((FOX_HEADS, n), F32),
        jax.ShapeDtypeStruct((n, GLA_KW), F32),
        jax.ShapeDtypeStruct((n, GLA_KW), F32),
        jax.ShapeDtypeStruct((n, GLA_VW), BF16),
        jax.ShapeDtypeStruct((n, GLA_KW), F32),
        jax.ShapeDtypeStruct((n, GLA_VW), F32),
    )
    out_specs = (row(FOX_WIDTH), row(FOX_WIDTH), row(FOX_WIDTH), row(FOX_HEADS),
                 pl.BlockSpec((FOX_HEADS, tm), lambda i: (0, i)),
                 row(GLA_KW), row(GLA_KW), row(GLA_VW), row(GLA_KW), row(GLA_VW))
    in_specs = [row(D_MODEL), const((1, D_MODEL)), const(prm['wa'].shape), const(prm['wb'].shape),
                const(prm['wc'].shape), const(prm['wflt'].shape), const(prm['wg2'].shape),
                const((1, GLA_KW)), const((FOX_HEADS, 1)), const((1, FOX_HEADS)), const((tm, tm))]
    return pl.pallas_call(
        functools.partial(_proj_kernel, tiles_per_seq=tiles_per_seq),
        grid=(n // tm,),
        in_specs=in_specs, out_specs=out_specs, out_shape=out_shape,
        scratch_shapes=[pltpu.VMEM((FOX_HEADS, LANES), F32)],
        compiler_params=pltpu.CompilerParams(dimension_semantics=("arbitrary",),
                                             vmem_limit_bytes=VMEM_LIMIT),
        name="proj",
    )(x2d, prm['g1'], prm['wa'], prm['wb'], prm['wc'], prm['wflt'], prm['wg2'], prm['bg2'],
      prm['bfc'], prm['bfr'], tri)


def _fox_prompt_kernel(q_ref, k_ref, v_ref, c_ref, o_ref, kb_ref, vb_ref, *, tq):
    p = pl.program_id(1)
    qi = pl.program_id(2)

    @pl.when(qi == 0)
    def _():
        kb_ref[...] = k_ref[...].astype(BF16)
        vb_ref[...] = v_ref[...].astype(BF16)

    q = q_ref[...]
    lane = lax.broadcasted_iota(jnp.int32, (tq, LANES), 1)
    row = lax.broadcasted_iota(jnp.int32, (tq, tq), 0)
    col = lax.broadcasted_iota(jnp.int32, (tq, tq), 1)
    causal = col <= row
    outs = []
    for hh in range(2):
        first = lane < FOX_HEAD_DIM
        qh = jnp.where(first if hh == 0 else jnp.logical_not(first), q, jnp.zeros_like(q))
        h = 2 * p + hh
        cref = c_ref[h, qi][:, 0:1]

        def tile(j, carry, masked):
            m, l, acc = carry
            k0 = pl.multiple_of(j * tq, tq)
            kt = kb_ref[pl.ds(k0, tq), :]
            vt = vb_ref[pl.ds(k0, tq), :]
            s = _dot_nt(qh, kt) + (cref - c_ref[h, j])
            if masked:
                s = jnp.where(causal, s, -jnp.inf)
            m_new = jnp.maximum(m, jnp.max(s, axis=-1, keepdims=True))
            alpha = jnp.exp(m - m_new)
            pm = jnp.exp(s - m_new)
            l = alpha * l + jnp.sum(pm, axis=-1, keepdims=True)
            acc = alpha * acc + _dot(pm.astype(BF16), vt)
            return m_new, l, acc

        init = (jnp.full((tq, 1), NEG_BIG, F32), jnp.zeros((tq, 1), F32), jnp.zeros((tq, LANES), F32))
        carry = lax.fori_loop(0, qi, lambda j, c: tile(j, c, False), init)
        _, l, acc = tile(qi, carry, True)
        outs.append(acc / l)
    o_ref[...] = jnp.where(lane < FOX_HEAD_DIM, outs[0], outs[1]).astype(BF16)


def _fox_prompt(q, k, v, ct, batch, seq, tq=256):
    n = batch * seq
    nq = seq // tq
    pairs = FOX_WIDTH // LANES
    return pl.pallas_call(
        functools.partial(_fox_prompt_kernel, tq=tq),
        grid=(batch, pairs, nq),
        in_specs=[pl.BlockSpec((tq, LANES), lambda b, p, i: (b * nq + i, p)),
                  pl.BlockSpec((seq, LANES), lambda b, p, i: (b, p)),
                  pl.BlockSpec((seq, LANES), lambda b, p, i: (b, p)),
                  pl.BlockSpec((FOX_HEADS, nq, 1, tq), lambda b, p, i: (0, b, 0, 0))],
        out_specs=pl.BlockSpec((tq, LANES), lambda b, p, i: (b * nq + i, p)),
        out_shape=jax.ShapeDtypeStruct((n, FOX_WIDTH), BF16),
        scratch_shapes=[pltpu.VMEM((seq, LANES), BF16), pltpu.VMEM((seq, LANES), BF16)],
        compiler_params=pltpu.CompilerParams(
            dimension_semantics=("arbitrary", "arbitrary", "arbitrary"),
            vmem_limit_bytes=VMEM_LIMIT),
        name="fox_prompt",
    )(q, k, v, ct.reshape(FOX_HEADS, n // tq, 1, tq))


def _fox_sample_kernel(q_ref, kn_ref, vn_ref, cn_ref, lft_ref, mlow_ref, ck_ref, cv_ref,
                       o_ref, qbd_ref, suf_ref, m_ref, l_ref, acc_ref, *, tp, nt, tn):
    j = pl.program_id(1)
    rows = FOX_HEADS * tn
    blk = 2 * LANES

    @pl.when(j == 0)
    def _init():
        q = q_ref[...]
        lane = lax.broadcasted_iota(jnp.int32, q.shape, 1)
        for h in range(FOX_HEADS):
            sel = (lane >= h * FOX_HEAD_DIM) & (lane < (h + 1) * FOX_HEAD_DIM)
            qbd_ref[h * tn:(h + 1) * tn, :] = jnp.where(sel, q, jnp.zeros_like(q))
        m_ref[...] = jnp.full(m_ref.shape, NEG_BIG, F32)
        l_ref[...] = jnp.zeros(l_ref.shape, F32)
        acc_ref[...] = jnp.zeros(acc_ref.shape, F32)
        carry = jnp.zeros((FOX_HEADS, 1), F32)
        per_tile = tp // blk
        for b in reversed(range(nt * per_tile)):
            x = lft_ref[:, b * blk:(b + 1) * blk]
            parts = jnp.concatenate(_split3_f32(x), axis=0).astype(BF16)
            y = _dot(parts, mlow_ref[...])
            off = (b % per_tile) * blk
            suf_ref[b // per_tile, :, off:off + blk] = y[0:8] + y[8:16] + y[16:24] + carry
            carry = carry + jnp.sum(x, axis=1, keepdims=True)

    def step(kt, vt, bias):
        s = _dot_nt(qbd_ref[...], kt) + bias
        m_old = m_ref[...]
        m_new = jnp.maximum(m_old, jnp.max(s, axis=-1, keepdims=True))
        alpha = jnp.exp(m_old - m_new)
        pm = jnp.exp(s - m_new)
        l_ref[...] = alpha * l_ref[...] + jnp.sum(pm, axis=-1, keepdims=True)
        acc_ref[...] = alpha * acc_ref[...] + _dot(pm.astype(BF16), vt)
        m_ref[...] = m_new

    suf = suf_ref[j]
    bias = jnp.broadcast_to(suf[:, None, :], (FOX_HEADS, tn, tp)).reshape(rows, tp)
    step(ck_ref[...].astype(BF16), cv_ref[...].astype(BF16), bias)

    @pl.when(j == nt - 1)
    def _fin():
        cn = cn_ref[...]
        bias_n = jnp.broadcast_to(-cn[:, None, :], (FOX_HEADS, tn, tn)).reshape(rows, tn)
        r = lax.broadcasted_iota(jnp.int32, (rows, tn), 0)
        c = lax.broadcasted_iota(jnp.int32, (rows, tn), 1)
        bias_n = jnp.where(c <= r % tn, bias_n, -jnp.inf)
        step(kn_ref[...].astype(BF16), vn_ref[...].astype(BF16), bias_n)
        o = acc_ref[...] / l_ref[...]
        lane = lax.broadcasted_iota(jnp.int32, (tn, FOX_WIDTH), 1)
        out = jnp.zeros((tn, FOX_WIDTH), F32)
        for h in range(FOX_HEADS):
            sel = (lane >= h * FOX_HEAD_DIM) & (lane < (h + 1) * FOX_HEAD_DIM)
            out = out + jnp.where(sel, o[h * tn:(h + 1) * tn, :], 0.0)
        o_ref[...] = out.astype(BF16)


def _fox_sample(q, kn, vn, cn, lft, ck, cv, batch, tn, past, tp=1024):
    nt = past // tp
    rows = FOX_HEADS * tn
    blk = 2 * LANES
    idx = np.arange(blk)
    mlow = jnp.asarray((idx[:, None] > idx[None, :]).astype(np.float32), dtype=BF16)
    per_b = lambda w: pl.BlockSpec((tn, w), lambda b, j: (b, 0))
    return pl.pallas_call(
        functools.partial(_fox_sample_kernel, tp=tp, nt=nt, tn=tn),
        grid=(batch, nt),
        in_specs=[per_b(FOX_WIDTH), per_b(FOX_WIDTH), per_b(FOX_WIDTH),
                  pl.BlockSpec((None, FOX_HEADS, tn), lambda b, j: (b, 0, 0)),
                  pl.BlockSpec((None, FOX_HEADS, past), lambda b, j: (b, 0, 0)),
                  pl.BlockSpec((blk, blk), lambda b, j: (0, 0)),
                  pl.BlockSpec((tp, FOX_WIDTH), lambda b, j: (b * nt + j, 0)),
                  pl.BlockSpec((tp, FOX_WIDTH), lambda b, j: (b * nt + j, 0))],
        out_specs=per_b(FOX_WIDTH),
        out_shape=jax.ShapeDtypeStruct((batch * tn, FOX_WIDTH), BF16),
        scratch_shapes=[pltpu.VMEM((rows, FOX_WIDTH), BF16),
                        pltpu.VMEM((nt, FOX_HEADS, tp), F32),
                        pltpu.VMEM((rows, 1), F32), pltpu.VMEM((rows, 1), F32),
                        pltpu.VMEM((rows, FOX_WIDTH), F32)],
        compiler_params=pltpu.CompilerParams(dimension_semantics=("arbitrary", "arbitrary"),
                                             vmem_limit_bytes=VMEM_LIMIT),
        name="fox_sample",
    )(q, kn, vn, cn, lft, mlow, ck, cv)


def _gla_kernel(q_ref, k_ref, v_ref, g_ref, r_ref, s0_ref, gn_ref, tri_ref, e_ref,
                o_ref, s_ref, st_ref, ksh_ref, bsh_ref, vsh_ref, *, tb):
    t = pl.program_id(1)
    nsub = tb // GLA_SUB
    pad = GLA_SUB

    @pl.when(t == 0)
    def _():
        st_ref[...] = s0_ref[...].reshape(GLA_KW, GLA_DV).T

    g3 = jnp.concatenate(_split3_f32(g_ref[...]), axis=1).astype(BF16)
    cum = _dot(tri_ref[...], g3)
    cum = cum[:, :GLA_KW] + cum[:, GLA_KW:2 * GLA_KW] + cum[:, 2 * GLA_KW:]
    b = cum[:tb]
    bl = cum[tb:]

    q = q_ref[...] * (GLA_DK ** -0.5)
    k = k_ref[...]
    vf = v_ref[...].astype(F32)
    qt = (q * jnp.exp(b)).astype(BF16)
    kt = (k * jnp.exp(bl - b)).astype(BF16)
    dl = jnp.exp(bl)

    zpad = lambda w: jnp.zeros((pad, w), F32)
    ksh_ref[0:pad, :] = zpad(GLA_KW)
    bsh_ref[0:pad, :] = zpad(GLA_KW)
    vsh_ref[0:pad, :] = zpad(GLA_VW)
    ksh_ref[pad:, :] = k
    bsh_ref[pad:, :] = b
    vsh_ref[pad:, :] = vf
    rowpos = lax.broadcasted_iota(jnp.int32, (tb, GLA_KW), 0) % GLA_SUB
    e = e_ref[...]
    o = _dot((q * k).astype(BF16), e) * vf
    for d in range(1, GLA_SUB):
        k_d = ksh_ref[pad - d:pad - d + tb, :]
        b_d = bsh_ref[pad - d:pad - d + tb, :]
        v_d = vsh_ref[pad - d:pad - d + tb, :]
        dec = jnp.exp(jnp.where(rowpos >= d, b - b_d, -jnp.inf))
        o = o + _dot((q * k_d * dec).astype(BF16), e) * v_d

    lane = lax.broadcasted_iota(jnp.int32, (GLA_SUB, GLA_KW), 1)
    head_sel = [(lane >= h * GLA_DK) & (lane < (h + 1) * GLA_DK) for h in range(GLA_HEADS)]
    st = st_ref[...]
    inter = []
    for c in range(nsub):
        rs = slice(c * GLA_SUB, (c + 1) * GLA_SUB)
        qc = qt[rs]
        kc = kt[rs]
        vc = v_ref[rs, :]
        zero = jnp.zeros_like(qc)
        q4 = jnp.concatenate([jnp.where(sel, qc, zero) for sel in head_sel], axis=0)
        oi = _dot_nt(q4, st.astype(BF16))
        inter.append(jnp.concatenate([oi[h * GLA_SUB:(h + 1) * GLA_SUB] for h in range(GLA_HEADS)],
                                     axis=1))
        upd = None
        for h in range(GLA_HEADS):
            u = _dot_tn(vc[:, h * GLA_DV:(h + 1) * GLA_DV], jnp.where(head_sel[h], kc, zero))
            upd = u if upd is None else upd + u
        st = dl[c * GLA_SUB:c * GLA_SUB + 1, :] * st + upd
    st_ref[...] = st
    o = o + jnp.concatenate(inter, axis=0)

    s_ref[...] = st.T.reshape(s_ref.shape)

    r = r_ref[...]
    gate = gn_ref[...] * (r * _sigmoid(r))
    outs = []
    for h in range(GLA_HEADS):
        oh = o[:, h * GLA_DV:(h + 1) * GLA_DV]
        outs.append(oh * lax.rsqrt(jnp.mean(oh * oh, axis=-1, keepdims=True) + EPS))
    o_ref[...] = (jnp.concatenate(outs, axis=1) * gate).astype(BF16)


def _gla(gq, gk, gv, glog, gr, s0, gn, batch, seq, tb=128):
    tb = min(tb, seq)
    nt = seq // tb
    n = batch * seq
    idx = np.arange(tb)
    same = idx[:, None] // GLA_SUB == idx[None, :] // GLA_SUB
    tri = np.concatenate([same & (idx[None, :] <= idx[:, None]), same], axis=0)
    tri = jnp.asarray(tri.astype(np.float32), dtype=BF16)
    e = (np.arange(GLA_KW)[:, None] // GLA_DK) == (np.arange(GLA_VW)[None, :] // GLA_DV)
    e = jnp.asarray(e.astype(np.float32), dtype=BF16)
    row = lambda w: pl.BlockSpec((tb, w), lambda b, t: (b * nt + t, 0))
    state = pl.BlockSpec((None, GLA_HEADS, GLA_DK, GLA_DV), lambda b, t: (b, 0, 0, 0))
    const = lambda shape: pl.BlockSpec(shape, lambda b, t: (0, 0))
    return pl.pallas_call(
        functools.partial(_gla_kernel, tb=tb),
        grid=(batch, nt),
        in_specs=[row(GLA_KW), row(GLA_KW), row(GLA_VW), row(GLA_KW), row(GLA_VW), state,
                  const((1, GLA_VW)), const(tri.shape), const(e.shape)],
        out_specs=(row(GLA_VW), state),
        out_shape=(jax.ShapeDtypeStruct((n, GLA_VW), BF16),
                   jax.ShapeDtypeStruct((batch, GLA_HEADS, GLA_DK, GLA_DV), F32)),
        scratch_shapes=[pltpu.VMEM((GLA_DV, GLA_KW), F32),
                        pltpu.VMEM((tb + GLA_SUB, GLA_KW), F32),
                        pltpu.VMEM((tb + GLA_SUB, GLA_KW), F32),
                        pltpu.VMEM((tb + GLA_SUB, GLA_VW), F32)],
        compiler_params=pltpu.CompilerParams(dimension_semantics=("arbitrary", "arbitrary"),
                                             vmem_limit_bytes=VMEM_LIMIT),
        name="gla",
    )(gq, gk, gv, glog, gr, s0, gn, tri, e)


def _ffn_kernel(x_ref, fo_ref, go_ref, wo_ref, g2_ref, wg_ref, wu_ref, wd_ref, gf_ref,
                y_ref, a_ref, *, chunk):
    y1 = (x_ref[...] + _dot(fo_ref[...], wo_ref[0:FOX_WIDTH, :])
          + _dot(go_ref[...], wo_ref[FOX_WIDTH:, :]))
    h2 = _rms(y1, g2_ref[...]).astype(BF16)
    for c in range(D_FF // chunk):
        cs = slice(c * chunk, (c + 1) * chunk)
        u = _dot(h2, wg_ref[:, cs])
        w = _dot(h2, wu_ref[:, cs])
        a_ref[:, cs] = (u * _sigmoid(u) * w).astype(BF16)
    y2 = y1 + _dot(a_ref[...], wd_ref[...])
    y_ref[...] = _rms(y2, gf_ref[...])


def _ffn(x2d, fo, go, prm, tm=512, chunk=256):
    n = x2d.shape[0]
    tm = min(tm, n)
    row = lambda w: pl.BlockSpec((tm, w), lambda i: (i, 0))
    const = lambda shape: pl.BlockSpec(shape, lambda i: (0, 0), pipeline_mode=pl.Buffered(1))
    return pl.pallas_call(
        functools.partial(_ffn_kernel, chunk=chunk),
        grid=(n // tm,),
        in_specs=[row(D_MODEL), row(FOX_WIDTH), row(GLA_VW), const((D_MODEL, D_MODEL)),
                  const((1, D_MODEL)), const((D_MODEL, D_FF)), const((D_MODEL, D_FF)),
                  const((D_FF, D_MODEL)), const((1, D_MODEL))],
        out_specs=row(D_MODEL),
        out_shape=jax.ShapeDtypeStruct((n, D_MODEL), F32),
        scratch_shapes=[pltpu.VMEM((tm, D_FF), BF16)],
        compiler_params=pltpu.CompilerParams(dimension_semantics=("arbitrary",),
                                             vmem_limit_bytes=VMEM_LIMIT),
        name="ffn",
    )(x2d, fo, go, prm['wo'], prm['g2'], prm['wg'], prm['wu'], prm['wd'], prm['gf'])


def _layer_params(layer, norm1_g, w_in, w_gate2, b_gate2, b_forget, gla_norm_g, w_out,
                  norm2_g, w_gate, w_up, w_down, final_norm_g):
    w = w_in[layer]
    o_fl = 3 * FOX_WIDTH
    o_gq = o_fl + FOX_HEADS
    o_gk = o_gq + GLA_KW
    o_gv = o_gk + GLA_KW
    o_gg = o_gv + GLA_VW
    o_gr = o_gg + GLA_GATE_RANK
    wfl = w[:, o_fl:o_gq]
    wgg = w[:, o_gg:o_gr]
    wc = jnp.concatenate([wfl, wgg, jnp.zeros((D_MODEL, LANES - FOX_HEADS - GLA_GATE_RANK), F32)], axis=1)
    wg2 = jnp.zeros((LANES, GLA_KW), F32).at[FOX_HEADS:FOX_HEADS + GLA_GATE_RANK].set(w_gate2[layer])
    return dict(
        g1=norm1_g[layer].reshape(1, D_MODEL),
        wa=w[:, :o_fl].astype(BF16),
        wb=jnp.concatenate([w[:, o_gq:o_gg], w[:, o_gr:]], axis=1).astype(BF16),
        wc=wc.astype(BF16),
        wflt=wfl.T.astype(BF16),
        wg2=wg2.astype(BF16),
        bg2=b_gate2[layer].reshape(1, GLA_KW),
        bfc=b_forget[layer].reshape(FOX_HEADS, 1),
        bfr=b_forget[layer].reshape(1, FOX_HEADS),
        gn=gla_norm_g[layer].reshape(1, GLA_VW),
        wo=w_out[layer].astype(BF16),
        g2=norm2_g[layer].reshape(1, D_MODEL),
        wg=w_gate[layer].astype(BF16),
        wu=w_up[layer].astype(BF16),
        wd=w_down[layer].astype(BF16),
        gf=final_norm_g.reshape(1, D_MODEL),
    )


def kernel(x_prompt, x_sample, cache_fox_k, cache_fox_v, cache_fox_logf, state_gla, norm1_g, w_in,
           w_gate2, b_gate2, b_forget, gla_norm_g, w_out, norm2_g, w_gate, w_up, w_down, final_norm_g):
    depth = w_in.shape[0]
    assert depth == 1, "the final rmsnorm is fused into the layer's ffn kernel"
    bp, tp_, _ = x_prompt.shape
    bs, ts, _ = x_sample.shape
    past = cache_fox_k.shape[2]
    layer = 0
    prm = _layer_params(layer, norm1_g, w_in, w_gate2, b_gate2, b_forget, gla_norm_g, w_out,
                        norm2_g, w_gate, w_up, w_down, final_norm_g)

    xp = x_prompt.reshape(bp * tp_, D_MODEL)
    q, k_p, v_p, lf_p, ct, gq, gk, gv, glog, gr = _proj(xp, tp_, prm)
    fox_o = _fox_prompt(q, k_p, v_p, ct, bp, tp_)
    s0 = jnp.zeros((bp, GLA_HEADS, GLA_DK, GLA_DV), F32)
    gla_o, s_p = _gla(gq, gk, gv, glog, gr, s0, prm['gn'], bp, tp_)
    y_p = _ffn(xp, fox_o, gla_o, prm)

    xs = x_sample.reshape(bs * ts, D_MODEL)
    q, k_s, v_s, lf_s, ct, gq, gk, gv, glog, gr = _proj(xs, ts, prm)
    cn = ct.reshape(FOX_HEADS, bs, ts).transpose(1, 0, 2)
    lft = cache_fox_logf[layer].astype(F32).transpose(0, 2, 1)
    ck = cache_fox_k[layer].reshape(bs * past, FOX_WIDTH)
    cv = cache_fox_v[layer].reshape(bs * past, FOX_WIDTH)
    fox_o = _fox_sample(q, k_s, v_s, cn, lft, ck, cv, bs, ts, past)
    gla_o, s_s = _gla(gq, gk, gv, glog, gr, state_gla[layer].astype(F32), prm['gn'], bs, ts)
    y_s = _ffn(xs, fox_o, gla_o, prm)

    heads = lambda a, b, t: a.reshape(1, b, t, FOX_HEADS, FOX_HEAD_DIM)
    return (y_p.reshape(bp, tp_, D_MODEL), y_s.reshape(bs, ts, D_MODEL),
            heads(k_p, bp, tp_), heads(v_p, bp, tp_), lf_p.reshape(1, bp, tp_, FOX_HEADS), s_p[None],
            heads(k_s, bs, ts), heads(v_s, bs, ts), lf_s.reshape(1, bs, ts, FOX_HEADS), s_s[None])
```

```python
import functools

import numpy as np
import jax
import jax.numpy as jnp
from jax import lax
from jax.experimental import pallas as pl
from jax.experimental.pallas import tpu as pltpu

D_MODEL = 1024
FOX_HEADS = 8
FOX_HEAD_DIM = 64
FOX_WIDTH = FOX_HEADS * FOX_HEAD_DIM
GLA_HEADS = 4
GLA_DK = 64
GLA_DV = 128
GLA_KW = GLA_HEADS * GLA_DK
GLA_VW = GLA_HEADS * GLA_DV
GLA_GATE_RANK = 16
GLA_GATE_TEMP = 16.0
D_FF = 2816
EPS = 1e-6

LANES = 128
GLA_SUB = 16
VMEM_LIMIT = 56 * 1024 * 1024

F32 = jnp.float32
BF16 = jnp.bfloat16
NEG_BIG = -1e30


def _log_sigmoid(x):
    return jnp.minimum(x, 0.0) - jnp.log1p(jnp.exp(-jnp.abs(x)))


def _sigmoid(x):
    return 1.0 / (1.0 + jnp.exp(-x))


def _split3_f32(x):
    hi = x.astype(BF16).astype(F32)
    r = x - hi
    mid = r.astype(BF16).astype(F32)
    lo = (r - mid).astype(BF16).astype(F32)
    return hi, mid, lo


def _dot(a, b):
    return jnp.dot(a, b, preferred_element_type=F32)


def _dot_nt(a, b):
    return lax.dot_general(a, b, (((1,), (1,)), ((), ())), preferred_element_type=F32)


def _dot_tn(a, b):
    return lax.dot_general(a, b, (((0,), (0,)), ((), ())), preferred_element_type=F32)


def _rms(x, g):
    return x * lax.rsqrt(jnp.mean(x * x, axis=-1, keepdims=True) + EPS) * g


def _softmax_step(s, carry, pv):
    m, l, acc = carry
    m_new = jnp.maximum(m, jnp.max(s, axis=-1, keepdims=True))
    alpha = jnp.exp(m - m_new)
    pm = jnp.exp(s - m_new)
    l = alpha * l + jnp.sum(pm, axis=-1, keepdims=True)
    acc = alpha * acc + pv(pm.astype(BF16))
    return m_new, l, acc


def _proj_kernel(x_ref, g1_ref, wq_ref, wkv_ref, wb_ref, wc_ref, wfl_ref, wg2_ref, bg2_ref,
                 bfc_ref, tri_ref,
                 q_ref, k_ref, v_ref, lft_ref, ct_ref, gq_ref, gk_ref, gv_ref, glog_ref, gr_ref,
                 carry_ref, *, tiles_per_seq, kv_transposed):
    i = pl.program_id(0)
    h = _rms(x_ref[...], g1_ref[...]).astype(BF16)
    tm = h.shape[0]

    q_ref[...] = (_dot_nt(h, wq_ref[...]) * (FOX_HEAD_DIM ** -0.5)).astype(BF16)
    if kv_transposed:
        kvt = _dot_nt(wkv_ref[...], h)
        k_ref[...] = kvt[:FOX_WIDTH].reshape(FOX_HEADS, FOX_HEAD_DIM, tm)
        v_ref[...] = kvt[FOX_WIDTH:].reshape(FOX_HEADS, FOX_HEAD_DIM, tm)
    else:
        kv = _dot_nt(h, wkv_ref[...])
        k_ref[...] = kv[:, :FOX_WIDTH]
        v_ref[...] = kv[:, FOX_WIDTH:]

    zb = _dot_nt(h, wb_ref[...])
    gq_ref[...] = zb[:, :GLA_KW]
    gk_ref[...] = zb[:, GLA_KW:2 * GLA_KW]
    gv_ref[...] = zb[:, 2 * GLA_KW:2 * GLA_KW + GLA_VW].astype(BF16)
    gr_ref[...] = zb[:, 2 * GLA_KW + GLA_VW:]

    zc = _dot_nt(h, wc_ref[...])
    gpre = _dot(zc.astype(BF16), wg2_ref[...]) + bg2_ref[...]
    glog_ref[...] = _log_sigmoid(gpre) * (1.0 / GLA_GATE_TEMP)

    logf_t = _log_sigmoid(_dot_nt(wfl_ref[...], h) + bfc_ref[...])
    lft_ref[...] = logf_t
    parts = jnp.concatenate(_split3_f32(logf_t), axis=0).astype(BF16)
    cs = _dot(parts, tri_ref[...])
    cs = cs[0:8] + cs[8:16] + cs[16:24]

    @pl.when(i % tiles_per_seq == 0)
    def _():
        carry_ref[...] = jnp.zeros_like(carry_ref)

    ct = cs + carry_ref[:, 0:1]
    ct_ref[...] = ct
    carry_ref[...] = jnp.broadcast_to(ct[:, tm - 1:], carry_ref.shape)


def _proj(x2d, batch, seq_len, prm, kv_transposed, tm=512):
    n = x2d.shape[0]
    tm = min(tm, n)
    tiles_per_seq = max(seq_len // tm, 1)
    idx = np.arange(tm)
    tri = ((idx[:, None] <= idx[None, :]) & (idx[:, None] // seq_len == idx[None, :] // seq_len))
    tri = jnp.asarray(tri.astype(np.float32), dtype=BF16)
    const = lambda a: pl.BlockSpec(a.shape, lambda i: (0, 0))
    row = lambda w: pl.BlockSpec((tm, w), lambda i: (i, 0))
    col = pl.BlockSpec((FOX_HEADS, tm), lambda i: (0, i))
    if kv_transposed:
        kv_shape = jax.ShapeDtypeStruct((batch, FOX_HEADS, FOX_HEAD_DIM, seq_len), F32)
        kv_spec = pl.BlockSpec((None, FOX_HEADS, FOX_HEAD_DIM, tm),
                               lambda i: (i // tiles_per_seq, 0, 0, i % tiles_per_seq))
    else:
        kv_shape = jax.ShapeDtypeStruct((n, FOX_WIDTH), F32)
        kv_spec = row(FOX_WIDTH)
    out_shape = (
        jax.ShapeDtypeStruct((n, FOX_WIDTH), BF16),
        kv_shape, kv_shape,
        jax.ShapeDtypeStruct((FOX_HEADS, n), F32),
        jax.ShapeDtypeStruct((FOX_HEADS, n), F32),
        jax.ShapeDtypeStruct((n, GLA_KW), F32),
        jax.ShapeDtypeStruct((n, GLA_KW), F32),
        jax.ShapeDtypeStruct((n, GLA_VW), BF16),
        jax.ShapeDtypeStruct((n, GLA_KW), F32),
        jax.ShapeDtypeStruct((n, GLA_VW), F32),
    )
    out_specs = (row(FOX_WIDTH), kv_spec, kv_spec, col, col,
                 row(GLA_KW), row(GLA_KW), row(GLA_VW), row(GLA_KW), row(GLA_VW))
    names = ('g1', 'wq', 'wkv', 'wb', 'wc', 'wfl', 'wg2', 'bg2', 'bfc')
    return pl.pallas_call(
        functools.partial(_proj_kernel, tiles_per_seq=tiles_per_seq, kv_transposed=kv_transposed),
        grid=(n // tm,),
        in_specs=[row(D_MODEL)] + [const(prm[k]) for k in names] + [const(tri)],
        out_specs=out_specs, out_shape=out_shape,
        scratch_shapes=[pltpu.VMEM((FOX_HEADS, LANES), F32)],
        compiler_params=pltpu.CompilerParams(dimension_semantics=("arbitrary",),
                                             vmem_limit_bytes=VMEM_LIMIT),
        name="proj",
    )(x2d, *[prm[k] for k in names], tri)


def _fox_prompt_kernel(q_ref, k_ref, v_ref, c_ref, o_ref, kb_ref, vb_ref, *, tk):
    p = pl.program_id(1)
    qi = pl.program_id(2)
    nk = kb_ref.shape[0]

    @pl.when(qi == 0)
    def _():
        for jj in range(nk):
            ts = slice(jj * tk, (jj + 1) * tk)
            kb_ref[jj] = k_ref[:, :, ts].reshape(LANES, tk).astype(BF16)
            vb_ref[jj] = v_ref[:, :, ts].reshape(LANES, tk).astype(BF16)

    lane = lax.broadcasted_iota(jnp.int32, (tk, LANES), 1)
    first = lane < FOX_HEAD_DIM
    row = lax.broadcasted_iota(jnp.int32, (tk, tk), 0)
    col = lax.broadcasted_iota(jnp.int32, (tk, tk), 1)
    causal = col <= row

    chains = [(hh, part) for hh in range(2) for part in range(2)]
    qs, crefs = [], []
    for hh, part in chains:
        q = q_ref[part * tk:(part + 1) * tk, :]
        qs.append(jnp.where(first if hh == 0 else jnp.logical_not(first), q, jnp.zeros_like(q)))
        crefs.append(c_ref[2 * p + hh, 2 * qi + part][:, 0:1])

    def step(j, carries, active, masked):
        kt = kb_ref[j]
        vt = vb_ref[j]
        out = list(carries)
        for n, (hh, part) in enumerate(chains):
            if not active[part]:
                continue
            s = _dot(qs[n], kt) + (crefs[n] - c_ref[2 * p + hh, j])
            if masked[part]:
                s = jnp.where(causal, s, -jnp.inf)
            out[n] = _softmax_step(s, carries[n], lambda pm: _dot_nt(pm, vt))
        return tuple(out)

    init = tuple((jnp.full((tk, 1), NEG_BIG, F32), jnp.zeros((tk, 1), F32),
                  jnp.zeros((tk, LANES), F32)) for _ in chains)
    carries = lax.fori_loop(0, 2 * qi, lambda j, c: step(j, c, (True, True), (False, False)), init)
    carries = step(2 * qi, carries, (True, True), (True, False))
    carries = step(2 * qi + 1, carries, (False, True), (False, True))

    res = [acc / l for (_, l, acc) in carries]
    for part in range(2):
        o_ref[part * tk:(part + 1) * tk, :] = jnp.where(first, res[part], res[2 + part]).astype(BF16)


def _fox_prompt(q, kt, vt, ct, batch, seq, tk=256):
    n = batch * seq
    tq = 2 * tk
    nq = seq // tq
    nk = seq // tk
    pairs = FOX_WIDTH // LANES
    kv_spec = pl.BlockSpec((None, 2, FOX_HEAD_DIM, seq), lambda b, p, i: (b, p, 0, 0))
    return pl.pallas_call(
        functools.partial(_fox_prompt_kernel, tk=tk),
        grid=(batch, pairs, nq),
        in_specs=[pl.BlockSpec((tq, LANES), lambda b, p, i: (b * nq + i, p)), kv_spec, kv_spec,
                  pl.BlockSpec((FOX_HEADS, nk, 1, tk), lambda b, p, i: (0, b, 0, 0))],
        out_specs=pl.BlockSpec((tq, LANES), lambda b, p, i: (b * nq + i, p)),
        out_shape=jax.ShapeDtypeStruct((n, FOX_WIDTH), BF16),
        scratch_shapes=[pltpu.VMEM((nk, LANES, tk), BF16), pltpu.VMEM((nk, LANES, tk), BF16)],
        compiler_params=pltpu.CompilerParams(
            dimension_semantics=("arbitrary", "arbitrary", "arbitrary"),
            vmem_limit_bytes=VMEM_LIMIT),
        name="fox_prompt",
    )(q, kt, vt, ct.reshape(FOX_HEADS, n // tk, 1, tk))


def _fox_sample_kernel(q_ref, kn_ref, vn_ref, cn_ref, lft_ref, mlow_ref, ck_ref, cv_ref,
                       o_ref, qh_ref, suf_ref, m_ref, l_ref, acc_ref, *, tp, nt, tn):
    j = pl.program_id(1)
    blk = 2 * LANES
    hd = FOX_HEAD_DIM

    @pl.when(j == 0)
    def _init():
        for h in range(FOX_HEADS):
            qh_ref[h] = q_ref[:, h * hd:(h + 1) * hd]
        m_ref[...] = jnp.full(m_ref.shape, NEG_BIG, F32)
        l_ref[...] = jnp.zeros(l_ref.shape, F32)
        acc_ref[...] = jnp.zeros(acc_ref.shape, F32)
        carry = jnp.zeros((FOX_HEADS, 1), F32)
        per_tile = tp // blk
        for b in reversed(range(nt * per_tile)):
            x = lft_ref[:, b * blk:(b + 1) * blk]
            parts = jnp.concatenate(_split3_f32(x), axis=0).astype(BF16)
            y = _dot(parts, mlow_ref[...])
            off = (b % per_tile) * blk
            suf_ref[b // per_tile, :, off:off + blk] = y[0:8] + y[8:16] + y[16:24] + carry
            carry = carry + jnp.sum(x, axis=1, keepdims=True)

    def update(h, s, pv):
        m, l, acc = _softmax_step(s, (m_ref[h], l_ref[h], acc_ref[h]), pv)
        m_ref[h] = m
        l_ref[h] = l
        acc_ref[h] = acc

    suf = suf_ref[j]
    for h in range(FOX_HEADS):
        kt = ck_ref[h].astype(BF16)
        vt = cv_ref[h].astype(BF16)
        update(h, _dot(qh_ref[h], kt) + suf[h:h + 1, :], lambda pm: _dot_nt(pm, vt))

    @pl.when(j == nt - 1)
    def _fin():
        cn = cn_ref[...]
        r = lax.broadcasted_iota(jnp.int32, (tn, tn), 0)
        c = lax.broadcasted_iota(jnp.int32, (tn, tn), 1)
        outs = []
        for h in range(FOX_HEADS):
            kn = kn_ref[:, h * hd:(h + 1) * hd].astype(BF16)
            vn = vn_ref[:, h * hd:(h + 1) * hd].astype(BF16)
            s = jnp.where(c <= r, _dot_nt(qh_ref[h], kn) - cn[h:h + 1, :], -jnp.inf)
            update(h, s, lambda pm: _dot(pm, vn))
            outs.append(acc_ref[h] / l_ref[h])
        o_ref[...] = jnp.concatenate(outs, axis=1).astype(BF16)


def _fox_sample(q, kn, vn, cn, lft, ck, cv, batch, tn, past, tp=1024):
    nt = past // tp
    blk = 2 * LANES
    idx = np.arange(blk)
    mlow = jnp.asarray((idx[:, None] > idx[None, :]).astype(np.float32), dtype=BF16)
    per_b = lambda w: pl.BlockSpec((tn, w), lambda b, j: (b, 0))
    cache = pl.BlockSpec((None, FOX_HEADS, FOX_HEAD_DIM, tp), lambda b, j: (b, 0, 0, j))
    return pl.pallas_call(
        functools.partial(_fox_sample_kernel, tp=tp, nt=nt, tn=tn),
        grid=(batch, nt),
        in_specs=[per_b(FOX_WIDTH), per_b(FOX_WIDTH), per_b(FOX_WIDTH),
                  pl.BlockSpec((None, FOX_HEADS, tn), lambda b, j: (b, 0, 0)),
                  pl.BlockSpec((None, FOX_HEADS, past), lambda b, j: (b, 0, 0)),
                  pl.BlockSpec((blk, blk), lambda b, j: (0, 0)),
                  cache, cache],
        out_specs=per_b(FOX_WIDTH),
        out_shape=jax.ShapeDtypeStruct((batch * tn, FOX_WIDTH), BF16),
        scratch_shapes=[pltpu.VMEM((FOX_HEADS, tn, FOX_HEAD_DIM), BF16),
                        pltpu.VMEM((nt, FOX_HEADS, tp), F32),
                        pltpu.VMEM((FOX_HEADS, tn, 1), F32), pltpu.VMEM((FOX_HEADS, tn, 1), F32),
                        pltpu.VMEM((FOX_HEADS, tn, FOX_HEAD_DIM), F32)],
        compiler_params=pltpu.CompilerParams(dimension_semantics=("arbitrary", "arbitrary"),
                                             vmem_limit_bytes=VMEM_LIMIT),
        name="fox_sample",
    )(q, kn, vn, cn, lft, mlow, ck, cv)


def _gla_kernel(q_ref, k_ref, v_ref, g_ref, r_ref, s0_ref, gn_ref, tri_ref, e_ref,
                o_ref, s_ref, st_ref, ksh_ref, bsh_ref, vsh_ref, *, tb):
    t = pl.program_id(1)
    nsub = tb // GLA_SUB
    pad = GLA_SUB

    @pl.when(t == 0)
    def _():
        st_ref[...] = s0_ref[...].reshape(GLA_KW, GLA_DV).T

    g3 = jnp.concatenate(_split3_f32(g_ref[...]), axis=1).astype(BF16)
    cum = _dot(tri_ref[...], g3)
    cum = cum[:, :GLA_KW] + cum[:, GLA_KW:2 * GLA_KW] + cum[:, 2 * GLA_KW:]
    b = cum[:tb]
    bl = cum[tb:]

    q = q_ref[...] * (GLA_DK ** -0.5)
    k = k_ref[...]
    vf = v_ref[...].astype(F32)
    qt = (q * jnp.exp(b)).astype(BF16)
    kt = (k * jnp.exp(bl - b)).astype(BF16)
    dl = jnp.exp(bl)

    zpad = lambda w: jnp.zeros((pad, w), F32)
    ksh_ref[0:pad, :] = zpad(GLA_KW)
    bsh_ref[0:pad, :] = zpad(GLA_KW)
    vsh_ref[0:pad, :] = zpad(GLA_VW)
    ksh_ref[pad:, :] = k
    bsh_ref[pad:, :] = b
    vsh_ref[pad:, :] = vf
    rowpos = lax.broadcasted_iota(jnp.int32, (tb, GLA_KW), 0) % GLA_SUB
    e = e_ref[...]
    o = _dot((q * k).astype(BF16), e) * vf
    for d in range(1, GLA_SUB):
        k_d = ksh_ref[pad - d:pad - d + tb, :]
        b_d = bsh_ref[pad - d:pad - d + tb, :]
        v_d = vsh_ref[pad - d:pad - d + tb, :]
        dec = jnp.exp(jnp.where(rowpos >= d, b - b_d, -jnp.inf))
        o = o + _dot((q * k_d * dec).astype(BF16), e) * v_d

    lane = lax.broadcasted_iota(jnp.int32, (GLA_SUB, GLA_KW), 1)
    head_sel = [(lane >= h * GLA_DK) & (lane < (h + 1) * GLA_DK) for h in range(GLA_HEADS)]
    st = st_ref[...]
    inter = []
    for c in range(nsub):
        rs = slice(c * GLA_SUB, (c + 1) * GLA_SUB)
        qc = qt[rs]
        kc = kt[rs]
        vc = v_ref[rs, :]
        zero = jnp.zeros_like(qc)
        q4 = jnp.concatenate([jnp.where(sel, qc, zero) for sel in head_sel], axis=0)
        oi = _dot_nt(q4, st.astype(BF16))
        inter.append(jnp.concatenate([oi[h * GLA_SUB:(h + 1) * GLA_SUB] for h in range(GLA_HEADS)],
                                     axis=1))
        upd = None
        for h in range(GLA_HEADS):
            u = _dot_tn(vc[:, h * GLA_DV:(h + 1) * GLA_DV], jnp.where(head_sel[h], kc, zero))
            upd = u if upd is None else upd + u
        st = dl[c * GLA_SUB:c * GLA_SUB + 1, :] * st + upd
    st_ref[...] = st
    o = o + jnp.concatenate(inter, axis=0)

    s_ref[...] = st.T.reshape(s_ref.shape)

    r = r_ref[...]
    gate = gn_ref[...] * (r * _sigmoid(r))
    outs = []
    for h in range(GLA_HEADS):
        oh = o[:, h * GLA_DV:(h + 1) * GLA_DV]
        outs.append(oh * lax.rsqrt(jnp.mean(oh * oh, axis=-1, keepdims=True) + EPS))
    o_ref[...] = (jnp.concatenate(outs, axis=1) * gate).astype(BF16)


def _gla(gq, gk, gv, glog, gr, s0, gn, batch, seq, tb=128):
    tb = min(tb, seq)
    nt = seq // tb
    n = batch * seq
    idx = np.arange(tb)
    same = idx[:, None] // GLA_SUB == idx[None, :] // GLA_SUB
    tri = np.concatenate([same & (idx[None, :] <= idx[:, None]), same], axis=0)
    tri = jnp.asarray(tri.astype(np.float32), dtype=BF16)
    e = (np.arange(GLA_KW)[:, None] // GLA_DK) == (np.arange(GLA_VW)[None, :] // GLA_DV)
    e = jnp.asarray(e.astype(np.float32), dtype=BF16)
    row = lambda w: pl.BlockSpec((tb, w), lambda b, t: (b * nt + t, 0))
    state = pl.BlockSpec((None, GLA_HEADS, GLA_DK, GLA_DV), lambda b, t: (b, 0, 0, 0))
    const = lambda shape: pl.BlockSpec(shape, lambda b, t: (0, 0))
    return pl.pallas_call(
        functools.partial(_gla_kernel, tb=tb),
        grid=(batch, nt),
        in_specs=[row(GLA_KW), row(GLA_KW), row(GLA_VW), row(GLA_KW), row(GLA_VW), state,
                  const((1, GLA_VW)), const(tri.shape), const(e.shape)],
        out_specs=(row(GLA_VW), state),
        out_shape=(jax.ShapeDtypeStruct((n, GLA_VW), BF16),
                   jax.ShapeDtypeStruct((batch, GLA_HEADS, GLA_DK, GLA_DV), F32)),
        scratch_shapes=[pltpu.VMEM((GLA_DV, GLA_KW), F32),
                        pltpu.VMEM((tb + GLA_SUB, GLA_KW), F32),
                        pltpu.VMEM((tb + GLA_SUB, GLA_KW), F32),
                        pltpu.VMEM((tb + GLA_SUB, GLA_VW), F32)],
        compiler_params=pltpu.CompilerParams(dimension_semantics=("arbitrary", "arbitrary"),
                                             vmem_limit_bytes=VMEM_LIMIT),
        name="gla",
    )(gq, gk, gv, glog, gr, s0, gn, tri, e)


def _ffn_kernel(x_ref, fo_ref, go_ref, wo_ref, g2_ref, wg_ref, wu_ref, wd_ref, gf_ref,
                y_ref, a_ref, *, chunk):
    y1 = (x_ref[...] + _dot(fo_ref[...], wo_ref[0:FOX_WIDTH, :])
          + _dot(go_ref[...], wo_ref[FOX_WIDTH:, :]))
    h2 = _rms(y1, g2_ref[...]).astype(BF16)
    for c in range(D_FF // chunk):
        cs = slice(c * chunk, (c + 1) * chunk)
        u = _dot(h2, wg_ref[:, cs])
        w = _dot(h2, wu_ref[:, cs])
        a_ref[:, cs] = (u * _sigmoid(u) * w).astype(BF16)
    y2 = y1 + _dot(a_ref[...], wd_ref[...])
    y_ref[...] = _rms(y2, gf_ref[...])


def _ffn(x2d, fo, go, prm, tm=512, chunk=256):
    n = x2d.shape[0]
    tm = min(tm, n)
    row = lambda w: pl.BlockSpec((tm, w), lambda i: (i, 0))
    const = lambda shape: pl.BlockSpec(shape, lambda i: (0, 0), pipeline_mode=pl.Buffered(1))
    return pl.pallas_call(
        functools.partial(_ffn_kernel, chunk=chunk),
        grid=(n // tm,),
        in_specs=[row(D_MODEL), row(FOX_WIDTH), row(GLA_VW), const((D_MODEL, D_MODEL)),
                  const((1, D_MODEL)), const((D_MODEL, D_FF)), const((D_MODEL, D_FF)),
                  const((D_FF, D_MODEL)), const((1, D_MODEL))],
        out_specs=row(D_MODEL),
        out_shape=jax.ShapeDtypeStruct((n, D_MODEL), F32),
        scratch_shapes=[pltpu.VMEM((tm, D_FF), BF16)],
        compiler_params=pltpu.CompilerParams(dimension_semantics=("arbitrary",),
                                             vmem_limit_bytes=VMEM_LIMIT),
        name="ffn",
    )(x2d, fo, go, prm['wo'], prm['g2'], prm['wg'], prm['wu'], prm['wd'], prm['gf'])


def _layer_params(layer, norm1_g, w_in, w_gate2, b_gate2, b_forget, gla_norm_g, w_out,
                  norm2_g, w_gate, w_up, w_down, final_norm_g):
    wt = jnp.transpose(w_in[layer])
    o_fl = 3 * FOX_WIDTH
    o_gq = o_fl + FOX_HEADS
    o_gg = o_gq + 2 * GLA_KW + GLA_VW
    o_gr = o_gg + GLA_GATE_RANK
    wc = jnp.zeros((LANES, D_MODEL), F32).at[:GLA_GATE_RANK].set(wt[o_gg:o_gr])
    wg2 = jnp.zeros((LANES, GLA_KW), F32).at[:GLA_GATE_RANK].set(w_gate2[layer])
    return dict(
        g1=norm1_g[layer].reshape(1, D_MODEL),
        wq=wt[:FOX_WIDTH].astype(BF16),
        wkv=wt[FOX_WIDTH:o_fl].astype(BF16),
        wb=jnp.concatenate([wt[o_gq:o_gg], wt[o_gr:]], axis=0).astype(BF16),
        wc=wc.astype(BF16),
        wfl=wt[o_fl:o_gq].astype(BF16),
        wg2=wg2.astype(BF16),
        bg2=b_gate2[layer].reshape(1, GLA_KW),
        bfc=b_forget[layer].reshape(FOX_HEADS, 1),
        gn=gla_norm_g[layer].reshape(1, GLA_VW),
        wo=w_out[layer].astype(BF16),
        g2=norm2_g[layer].reshape(1, D_MODEL),
        wg=w_gate[layer].astype(BF16),
        wu=w_up[layer].astype(BF16),
        wd=w_down[layer].astype(BF16),
        gf=final_norm_g.reshape(1, D_MODEL),
    )


def kernel(x_prompt, x_sample, cache_fox_k, cache_fox_v, cache_fox_logf, state_gla, norm1_g, w_in,
           w_gate2, b_gate2, b_forget, gla_norm_g, w_out, norm2_g, w_gate, w_up, w_down, final_norm_g):
    depth = w_in.shape[0]
    assert depth == 1, "the final rmsnorm is fused into the layer's ffn kernel"
    bp, tp_, _ = x_prompt.shape
    bs, ts, _ = x_sample.shape
    past = cache_fox_k.shape[2]
    layer = 0
    prm = _layer_params(layer, norm1_g, w_in, w_gate2, b_gate2, b_forget, gla_norm_g, w_out,
                        norm2_g, w_gate, w_up, w_down, final_norm_g)
    by_time = lambda a, b, t: a.reshape(FOX_HEADS, b, t).transpose(1, 2, 0)[None]

    xp = x_prompt.reshape(bp * tp_, D_MODEL)
    q, kt_p, vt_p, lf_p, ct, gq, gk, gv, glog, gr = _proj(xp, bp, tp_, prm, True)
    fox_o = _fox_prompt(q, kt_p, vt_p, ct, bp, tp_)
    s0 = jnp.zeros((bp, GLA_HEADS, GLA_DK, GLA_DV), F32)
    gla_o, s_p = _gla(gq, gk, gv, glog, gr, s0, prm['gn'], bp, tp_)
    y_p = _ffn(xp, fox_o, gla_o, prm)

    xs = x_sample.reshape(bs * ts, D_MODEL)
    q, k_s, v_s, lf_s, ct, gq, gk, gv, glog, gr = _proj(xs, bs, ts, prm, False)
    cn = ct.reshape(FOX_HEADS, bs, ts).transpose(1, 0, 2)
    lft = cache_fox_logf[layer].astype(F32).transpose(0, 2, 1)
    ck = cache_fox_k[layer].transpose(0, 2, 3, 1)
    cv = cache_fox_v[layer].transpose(0, 2, 3, 1)
    fox_o = _fox_sample(q, k_s, v_s, cn, lft, ck, cv, bs, ts, past)
    gla_o, s_s = _gla(gq, gk, gv, glog, gr, state_gla[layer].astype(F32), prm['gn'], bs, ts)
    y_s = _ffn(xs, fox_o, gla_o, prm)

    heads = lambda a, b, t: a.reshape(1, b, t, FOX_HEADS, FOX_HEAD_DIM)
    return (y_p.reshape(bp, tp_, D_MODEL), y_s.reshape(bs, ts, D_MODEL),
            kt_p.transpose(0, 3, 1, 2)[None], vt_p.transpose(0, 3, 1, 2)[None],
            by_time(lf_p, bp, tp_), s_p[None],
            heads(k_s, bs, ts), heads(v_s, bs, ts), by_time(lf_s, bs, ts), s_s[None])
```

```python
import functools

import numpy as np
import jax
import jax.numpy as jnp
from jax import lax
from jax.experimental import pallas as pl
from jax.experimental.pallas import tpu as pltpu

D_MODEL = 1024
FOX_HEADS = 8
FOX_HEAD_DIM = 64
FOX_WIDTH = FOX_HEADS * FOX_HEAD_DIM
GLA_HEADS = 4
GLA_DK = 64
GLA_DV = 128
GLA_KW = GLA_HEADS * GLA_DK
GLA_VW = GLA_HEADS * GLA_DV
GLA_GATE_RANK = 16
GLA_GATE_TEMP = 16.0
D_FF = 2816
EPS = 1e-6

LANES = 128
GLA_SUB = 16
VMEM_LIMIT = 56 * 1024 * 1024

F32 = jnp.float32
BF16 = jnp.bfloat16
NEG_BIG = -1e30


def _log_sigmoid(x):
    return jnp.minimum(x, 0.0) - jnp.log1p(jnp.exp(-jnp.abs(x)))


def _sigmoid(x):
    return 1.0 / (1.0 + jnp.exp(-x))


def _split3_f32(x):
    hi = x.astype(BF16).astype(F32)
    r = x - hi
    mid = r.astype(BF16).astype(F32)
    lo = (r - mid).astype(BF16).astype(F32)
    return hi, mid, lo


def _dot(a, b):
    return jnp.dot(a, b, preferred_element_type=F32)


def _dot_nt(a, b):
    return lax.dot_general(a, b, (((1,), (1,)), ((), ())), preferred_element_type=F32)


def _dot_tn(a, b):
    return lax.dot_general(a, b, (((0,), (0,)), ((), ())), preferred_element_type=F32)


def _rms(x, g):
    return x * lax.rsqrt(jnp.mean(x * x, axis=-1, keepdims=True) + EPS) * g


def _softmax_step(s, carry, pv):
    m, l, acc = carry
    m_new = jnp.maximum(m, jnp.max(s, axis=-1, keepdims=True))
    alpha = jnp.exp(m - m_new)
    pm = jnp.exp(s - m_new)
    l = alpha * l + jnp.sum(pm, axis=-1, keepdims=True)
    acc = alpha * acc + pv(pm.astype(BF16))
    return m_new, l, acc


def _proj_kernel(x_ref, g1_ref, wq_ref, wkv_ref, wb_ref, wc_ref, wfl_ref, wg2_ref, bg2_ref,
                 bfc_ref, tri_ref,
                 *rest, tiles_per_seq, time_minor):
    i = pl.program_id(0)
    h = _rms(x_ref[...], g1_ref[...]).astype(BF16)
    tm = h.shape[0]
    scale = FOX_HEAD_DIM ** -0.5

    if time_minor:
        (q_ref, k_ref, v_ref, ktok_ref, lft_ref, ct_ref, gq_ref, gk_ref, gv_ref, glog_ref, gr_ref,
         carry_ref) = rest
        q_ref[...] = (_dot_nt(wq_ref[...], h) * scale).astype(BF16)
        kvt = _dot_nt(wkv_ref[...], h)
        k_ref[...] = kvt[:FOX_WIDTH].reshape(FOX_HEADS, FOX_HEAD_DIM, tm)
        v_ref[...] = kvt[FOX_WIDTH:].reshape(FOX_HEADS, FOX_HEAD_DIM, tm)
        ktok_ref[...] = _dot_nt(h, wkv_ref[0:FOX_WIDTH, :]).astype(BF16)
    else:
        (q_ref, k_ref, v_ref, lft_ref, ct_ref, gq_ref, gk_ref, gv_ref, glog_ref, gr_ref,
         carry_ref) = rest
        q_ref[...] = (_dot_nt(h, wq_ref[...]) * scale).astype(BF16)
        kv = _dot_nt(h, wkv_ref[...])
        k_ref[...] = kv[:, :FOX_WIDTH]
        v_ref[...] = kv[:, FOX_WIDTH:]

    zb = _dot_nt(h, wb_ref[...])
    gq_ref[...] = zb[:, :GLA_KW]
    gk_ref[...] = zb[:, GLA_KW:2 * GLA_KW]
    gv_ref[...] = zb[:, 2 * GLA_KW:2 * GLA_KW + GLA_VW].astype(BF16)
    gr_ref[...] = zb[:, 2 * GLA_KW + GLA_VW:]

    zc = _dot_nt(h, wc_ref[...])
    gpre = _dot(zc.astype(BF16), wg2_ref[...]) + bg2_ref[...]
    glog_ref[...] = _log_sigmoid(gpre) * (1.0 / GLA_GATE_TEMP)

    logf_t = _log_sigmoid(_dot_nt(wfl_ref[...], h) + bfc_ref[...])
    lft_ref[...] = logf_t
    parts = jnp.concatenate(_split3_f32(logf_t), axis=0).astype(BF16)
    cs = _dot(parts, tri_ref[...])
    cs = cs[0:8] + cs[8:16] + cs[16:24]

    @pl.when(i % tiles_per_seq == 0)
    def _():
        carry_ref[...] = jnp.zeros_like(carry_ref)

    ct = cs + carry_ref[:, 0:1]
    ct_ref[...] = ct
    carry_ref[...] = jnp.broadcast_to(ct[:, tm - 1:], carry_ref.shape)


def _proj(x2d, batch, seq_len, prm, time_minor, tm=512):
    n = x2d.shape[0]
    tm = min(tm, n)
    tiles_per_seq = max(seq_len // tm, 1)
    idx = np.arange(tm)
    tri = ((idx[:, None] <= idx[None, :]) & (idx[:, None] // seq_len == idx[None, :] // seq_len))
    tri = jnp.asarray(tri.astype(np.float32), dtype=BF16)
    const = lambda a: pl.BlockSpec(a.shape, lambda i: (0, 0))
    row = lambda w: pl.BlockSpec((tm, w), lambda i: (i, 0))
    col = pl.BlockSpec((FOX_HEADS, tm), lambda i: (0, i))
    sds = jax.ShapeDtypeStruct
    by_seq = lambda i: (i // tiles_per_seq, 0, i % tiles_per_seq)
    if time_minor:
        kv_shape = sds((batch, FOX_HEADS, FOX_HEAD_DIM, seq_len), F32)
        kv_spec = pl.BlockSpec((None, FOX_HEADS, FOX_HEAD_DIM, tm),
                               lambda i: (i // tiles_per_seq, 0, 0, i % tiles_per_seq))
        fox = [(sds((batch, FOX_WIDTH, seq_len), BF16), pl.BlockSpec((None, FOX_WIDTH, tm), by_seq)),
               (kv_shape, kv_spec), (kv_shape, kv_spec),
               (sds((n, FOX_WIDTH), BF16), row(FOX_WIDTH))]
    else:
        fox = [(sds((n, FOX_WIDTH), BF16), row(FOX_WIDTH)),
               (sds((n, FOX_WIDTH), F32), row(FOX_WIDTH)),
               (sds((n, FOX_WIDTH), F32), row(FOX_WIDTH))]
    outs = fox + [
        (sds((FOX_HEADS, n), F32), col),
        (sds((FOX_HEADS, n), F32), col),
        (sds((n, GLA_KW), F32), row(GLA_KW)),
        (sds((n, GLA_KW), F32), row(GLA_KW)),
        (sds((n, GLA_VW), BF16), row(GLA_VW)),
        (sds((n, GLA_KW), F32), row(GLA_KW)),
        (sds((n, GLA_VW), F32), row(GLA_VW)),
    ]
    names = ('g1', 'wq', 'wkv', 'wb', 'wc', 'wfl', 'wg2', 'bg2', 'bfc')
    return pl.pallas_call(
        functools.partial(_proj_kernel, tiles_per_seq=tiles_per_seq, time_minor=time_minor),
        grid=(n // tm,),
        in_specs=[row(D_MODEL)] + [const(prm[k]) for k in names] + [const(tri)],
        out_specs=tuple(s for _, s in outs), out_shape=tuple(s for s, _ in outs),
        scratch_shapes=[pltpu.VMEM((FOX_HEADS, LANES), F32)],
        compiler_params=pltpu.CompilerParams(dimension_semantics=("arbitrary",),
                                             vmem_limit_bytes=VMEM_LIMIT),
        name="proj",
    )(x2d, *[prm[k] for k in names], tri)


def _fox_prompt_kernel(q_ref, k_ref, v_ref, c_ref, o_ref, vb_ref, cb_ref,
                       sa_ref, sb_ref, pa_ref, pb_ref, acc_ref, *, tk):
    p = pl.program_id(1)
    qi = pl.program_id(2)
    nk = vb_ref.shape[0]

    @pl.when(qi == 0)
    def _():
        for jj in range(nk):
            vb_ref[jj] = v_ref[:, :, jj * tk:(jj + 1) * tk].reshape(LANES, tk).astype(BF16)
        parts = jnp.concatenate(_split3_f32(c_ref[...]), axis=0).astype(BF16)
        r = lax.broadcasted_iota(jnp.int32, (3 * FOX_HEADS, LANES), 0) % FOX_HEADS
        for hh in range(2):
            sel = jnp.where(r == 2 * p + hh, -1.0, 0.0).astype(BF16)
            cb_ref[hh] = _dot_tn(parts, sel)

    rowh = lax.broadcasted_iota(jnp.int32, (LANES, tk), 0) < FOX_HEAD_DIM
    key = lax.broadcasted_iota(jnp.int32, (tk, tk), 0)
    qry = lax.broadcasted_iota(jnp.int32, (tk, tk), 1)
    causal = key <= qry

    chains = [(hh, part) for hh in range(2) for part in range(2)]
    qs = []
    for hh, part in chains:
        q = q_ref[:, part * tk:(part + 1) * tk]
        qs.append(jnp.where(rowh if hh == 0 else jnp.logical_not(rowh), q, jnp.zeros_like(q)))

    everyone = list(range(len(chains)))
    second = [n for n in everyone if chains[n][1] == 1]

    def scores(j, s_ref, live):
        kt = k_ref[pl.ds(pl.multiple_of(j * tk, tk), tk), :]
        for n in live:
            s_ref[n] = _dot(kt, qs[n])

    def values(j, p_ref, live):
        vt = vb_ref[j]
        return [_dot(vt, p_ref[n]) if n in live else None for n in everyone]

    def softmax(j, s_ref, p_ref, stats, live, masked_part):
        bias = [jnp.tile(cb_ref[hh, pl.ds(pl.multiple_of(j * tk, tk), tk), :], (1, tk // LANES))
                for hh in range(2)]
        stats, alphas = list(stats), [None] * len(chains)
        for n in live:
            hh, part = chains[n]
            s = s_ref[n] + bias[hh]
            if part == masked_part:
                s = jnp.where(causal, s, -jnp.inf)
            m, l = stats[n]
            m_new = jnp.maximum(m, jnp.max(s, axis=0, keepdims=True))
            alphas[n] = jnp.exp(m - m_new)
            pm = jnp.exp(s - m_new)
            p_ref[n] = pm.astype(BF16)
            stats[n] = (m_new, alphas[n] * l + jnp.sum(pm, axis=0, keepdims=True))
        return stats, alphas

    def rescale(alphas, pv, live):
        for n in live:
            acc_ref[n] = alphas[n] * acc_ref[n] + pv[n]

    def half(j, s_cur, s_nxt, p_cur, p_prv, stats, alphas, live_next):
        scores(j + 1, s_nxt, live_next)
        pv = values(jnp.maximum(j - 1, 0), p_prv, everyone)
        stats, alphas_new = softmax(j, s_cur, p_cur, stats, everyone, None)
        rescale(alphas, pv, everyone)
        return stats, alphas_new

    def body(jj, carry):
        stats, alphas = carry
        stats, alphas = half(2 * jj, sa_ref, sb_ref, pa_ref, pb_ref, stats, alphas, everyone)
        return half(2 * jj + 1, sb_ref, sa_ref, pb_ref, pa_ref, stats, alphas, everyone)

    pb_ref[...] = jnp.zeros(pb_ref.shape, BF16)
    acc_ref[...] = jnp.zeros(acc_ref.shape, F32)
    scores(0, sa_ref, everyone)
    init = ([(jnp.full((1, tk), NEG_BIG, F32), jnp.zeros((1, tk), F32)) for _ in chains],
            [jnp.ones((1, tk), F32) for _ in chains])
    stats, alphas = lax.fori_loop(0, qi, body, init)
    j0 = 2 * qi
    scores(j0 + 1, sb_ref, second)
    pv = values(jnp.maximum(j0 - 1, 0), pb_ref, everyone)
    stats, alphas_a = softmax(j0, sa_ref, pa_ref, stats, everyone, 0)
    rescale(alphas, pv, everyone)
    pv = values(j0, pa_ref, everyone)
    stats, alphas_b = softmax(j0 + 1, sb_ref, pb_ref, stats, second, 1)
    rescale(alphas_a, pv, everyone)
    rescale(alphas_b, values(j0 + 1, pb_ref, second), second)

    carries = [(None, stats[n][1], acc_ref[n]) for n in everyone]

    res = [acc / l for (_, l, acc) in carries]
    hd = FOX_HEAD_DIM
    for part in range(2):
        o_ref[:, part * tk:(part + 1) * tk] = jnp.concatenate(
            [res[part][:hd], res[2 + part][hd:]], axis=0).astype(BF16)


def _fox_prompt(qt, ktok, vt, ct, batch, seq, tk=256):
    tq = 2 * tk
    nq = seq // tq
    nk = seq // tk
    pairs = FOX_WIDTH // LANES
    return pl.pallas_call(
        functools.partial(_fox_prompt_kernel, tk=tk),
        grid=(batch, pairs, nq),
        in_specs=[pl.BlockSpec((None, LANES, tq), lambda b, p, i: (b, p, i)),
                  pl.BlockSpec((seq, LANES), lambda b, p, i: (b, p)),
                  pl.BlockSpec((None, 2, FOX_HEAD_DIM, seq), lambda b, p, i: (b, p, 0, 0)),
                  pl.BlockSpec((FOX_HEADS, seq), lambda b, p, i: (0, b))],
        out_specs=pl.BlockSpec((None, LANES, tq), lambda b, p, i: (b, p, i)),
        out_shape=jax.ShapeDtypeStruct((batch, FOX_WIDTH, seq), BF16),
        scratch_shapes=[pltpu.VMEM((nk, LANES, tk), BF16), pltpu.VMEM((2, seq, LANES), F32),
                        pltpu.VMEM((4, tk, tk), F32), pltpu.VMEM((4, tk, tk), F32),
                        pltpu.VMEM((4, tk, tk), BF16), pltpu.VMEM((4, tk, tk), BF16),
                        pltpu.VMEM((4, LANES, tk), F32)],
        compiler_params=pltpu.CompilerParams(
            dimension_semantics=("arbitrary", "arbitrary", "arbitrary"),
            vmem_limit_bytes=VMEM_LIMIT),
        name="fox_prompt",
    )(qt, ktok, vt, ct)


def _fox_sample_kernel(q_ref, kn_ref, vn_ref, cn_ref, lft_ref, mlow_ref, ck_ref, cv_ref,
                       o_ref, qh_ref, suf_ref, m_ref, l_ref, acc_ref, *, tp, nt, tn):
    j = pl.program_id(1)
    blk = 2 * LANES
    hd = FOX_HEAD_DIM

    @pl.when(j == 0)
    def _init():
        for h in range(FOX_HEADS):
            qh_ref[h] = q_ref[:, h * hd:(h + 1) * hd]
        m_ref[...] = jnp.full(m_ref.shape, NEG_BIG, F32)
        l_ref[...] = jnp.zeros(l_ref.shape, F32)
        acc_ref[...] = jnp.zeros(acc_ref.shape, F32)
        carry = jnp.zeros((FOX_HEADS, 1), F32)
        per_tile = tp // blk
        for b in reversed(range(nt * per_tile)):
            x = lft_ref[:, b * blk:(b + 1) * blk]
            parts = jnp.concatenate(_split3_f32(x), axis=0).astype(BF16)
            y = _dot(parts, mlow_ref[...])
            off = (b % per_tile) * blk
            suf_ref[b // per_tile, :, off:off + blk] = y[0:8] + y[8:16] + y[16:24] + carry
            carry = carry + jnp.sum(x, axis=1, keepdims=True)

    def update(s, pv):
        m, l, acc = _softmax_step(s, (m_ref[...], l_ref[...], acc_ref[...]), pv)
        m_ref[...] = m
        l_ref[...] = l
        acc_ref[...] = acc

    bmm = lambda a, b, ca, cb: lax.dot_general(a, b, (((ca,), (cb,)), ((0,), (0,))),
                                               preferred_element_type=F32)
    qh = qh_ref[...]

    kt = ck_ref[...].astype(BF16)
    vt = cv_ref[...].astype(BF16)
    s = bmm(qh, kt, 2, 1).reshape(FOX_HEADS * tn, tp) + jnp.repeat(suf_ref[j], tn, axis=0)
    update(s.reshape(FOX_HEADS, tn, tp), lambda pm: bmm(pm, vt, 2, 2))

    @pl.when(j == nt - 1)
    def _fin():
        per_head = lambda ref: jnp.stack([ref[:, h * hd:(h + 1) * hd] for h in range(FOX_HEADS)],
                                         axis=0).astype(BF16)
        kn = per_head(kn_ref)
        vn = per_head(vn_ref)
        r = lax.broadcasted_iota(jnp.int32, (FOX_HEADS, tn, tn), 1)
        c = lax.broadcasted_iota(jnp.int32, (FOX_HEADS, tn, tn), 2)
        s = jnp.where(c <= r, bmm(qh, kn, 2, 2) - cn_ref[...][:, None, :], -jnp.inf)
        update(s, lambda pm: bmm(pm, vn, 2, 1))
        o = acc_ref[...] / l_ref[...]
        o_ref[...] = jnp.concatenate([o[h] for h in range(FOX_HEADS)], axis=1).astype(BF16)


def _fox_sample(q, kn, vn, cn, lft, ck, cv, batch, tn, past, tp=1024):
    nt = past // tp
    blk = 2 * LANES
    idx = np.arange(blk)
    mlow = jnp.asarray((idx[:, None] > idx[None, :]).astype(np.float32), dtype=BF16)
    per_b = lambda w: pl.BlockSpec((tn, w), lambda b, j: (b, 0))
    cache = pl.BlockSpec((None, FOX_HEADS, FOX_HEAD_DIM, tp), lambda b, j: (b, 0, 0, j))
    return pl.pallas_call(
        functools.partial(_fox_sample_kernel, tp=tp, nt=nt, tn=tn),
        grid=(batch, nt),
        in_specs=[per_b(FOX_WIDTH), per_b(FOX_WIDTH), per_b(FOX_WIDTH),
                  pl.BlockSpec((None, FOX_HEADS, tn), lambda b, j: (b, 0, 0)),
                  pl.BlockSpec((None, FOX_HEADS, past), lambda b, j: (b, 0, 0)),
                  pl.BlockSpec((blk, blk), lambda b, j: (0, 0)),
                  cache, cache],
        out_specs=per_b(FOX_WIDTH),
        out_shape=jax.ShapeDtypeStruct((batch * tn, FOX_WIDTH), BF16),
        scratch_shapes=[pltpu.VMEM((FOX_HEADS, tn, FOX_HEAD_DIM), BF16),
                        pltpu.VMEM((nt, FOX_HEADS, tp), F32),
                        pltpu.VMEM((FOX_HEADS, tn, 1), F32), pltpu.VMEM((FOX_HEADS, tn, 1), F32),
                        pltpu.VMEM((FOX_HEADS, tn, FOX_HEAD_DIM), F32)],
        compiler_params=pltpu.CompilerParams(dimension_semantics=("arbitrary", "arbitrary"),
                                             vmem_limit_bytes=VMEM_LIMIT),
        name="fox_sample",
    )(q, kn, vn, cn, lft, mlow, ck, cv)


def _gla_kernel(q_ref, k_ref, v_ref, g_ref, r_ref, s0_ref, gn_ref, tri_ref, e_ref,
                o_ref, s_ref, st_ref, ksh_ref, bsh_ref, vsh_ref, *, tb):
    t = pl.program_id(1)
    nsub = tb // GLA_SUB
    pad = GLA_SUB

    @pl.when(t == 0)
    def _():
        st_ref[...] = s0_ref[...].reshape(GLA_KW, GLA_DV).T

    g3 = jnp.concatenate(_split3_f32(g_ref[...]), axis=1).astype(BF16)
    cum = _dot(tri_ref[...], g3)
    cum = cum[:, :GLA_KW] + cum[:, GLA_KW:2 * GLA_KW] + cum[:, 2 * GLA_KW:]
    b = cum[:tb]
    bl = cum[tb:]

    q = q_ref[...] * (GLA_DK ** -0.5)
    k = k_ref[...]
    vf = v_ref[...].astype(F32)
    qt = (q * jnp.exp(b)).astype(BF16)
    kt = (k * jnp.exp(bl - b)).astype(BF16)
    dl = jnp.exp(bl)

    zpad = lambda w: jnp.zeros((pad, w), F32)
    ksh_ref[0:pad, :] = zpad(GLA_KW)
    bsh_ref[0:pad, :] = zpad(GLA_KW)
    vsh_ref[0:pad, :] = zpad(GLA_VW)
    ksh_ref[pad:, :] = k
    bsh_ref[pad:, :] = b
    vsh_ref[pad:, :] = vf
    rowpos = lax.broadcasted_iota(jnp.int32, (tb, GLA_KW), 0) % GLA_SUB
    e = e_ref[...]
    o = _dot((q * k).astype(BF16), e) * vf
    for d in range(1, GLA_SUB):
        k_d = ksh_ref[pad - d:pad - d + tb, :]
        b_d = bsh_ref[pad - d:pad - d + tb, :]
        v_d = vsh_ref[pad - d:pad - d + tb, :]
        dec = jnp.exp(jnp.where(rowpos >= d, b - b_d, -jnp.inf))
        o = o + _dot((q * k_d * dec).astype(BF16), e) * v_d

    lane = lax.broadcasted_iota(jnp.int32, (GLA_SUB, GLA_KW), 1)
    head_sel = [(lane >= h * GLA_DK) & (lane < (h + 1) * GLA_DK) for h in range(GLA_HEADS)]
    st = st_ref[...]
    inter = []
    for c in range(nsub):
        rs = slice(c * GLA_SUB, (c + 1) * GLA_SUB)
        qc = qt[rs]
        kc = kt[rs]
        vc = v_ref[rs, :]
        zero = jnp.zeros_like(qc)
        q4 = jnp.concatenate([jnp.where(sel, qc, zero) for sel in head_sel], axis=0)
        oi = _dot_nt(q4, st.astype(BF16))
        inter.append(jnp.concatenate([oi[h * GLA_SUB:(h + 1) * GLA_SUB] for h in range(GLA_HEADS)],
                                     axis=1))
        upd = None
        for h in range(GLA_HEADS):
            u = _dot_tn(vc[:, h * GLA_DV:(h + 1) * GLA_DV], jnp.where(head_sel[h], kc, zero))
            upd = u if upd is None else upd + u
        st = dl[c * GLA_SUB:c * GLA_SUB + 1, :] * st + upd
    st_ref[...] = st
    o = o + jnp.concatenate(inter, axis=0)

    s_ref[...] = st.T.reshape(s_ref.shape)

    r = r_ref[...]
    gate = gn_ref[...] * (r * _sigmoid(r))
    outs = []
    for h in range(GLA_HEADS):
        oh = o[:, h * GLA_DV:(h + 1) * GLA_DV]
        outs.append(oh * lax.rsqrt(jnp.mean(oh * oh, axis=-1, keepdims=True) + EPS))
    o_ref[...] = (jnp.concatenate(outs, axis=1) * gate).astype(BF16)


def _gla(gq, gk, gv, glog, gr, s0, gn, batch, seq, tb=128):
    tb = min(tb, seq)
    nt = seq // tb
    n = batch * seq
    idx = np.arange(tb)
    same = idx[:, None] // GLA_SUB == idx[None, :] // GLA_SUB
    tri = np.concatenate([same & (idx[None, :] <= idx[:, None]), same], axis=0)
    tri = jnp.asarray(tri.astype(np.float32), dtype=BF16)
    e = (np.arange(GLA_KW)[:, None] // GLA_DK) == (np.arange(GLA_VW)[None, :] // GLA_DV)
    e = jnp.asarray(e.astype(np.float32), dtype=BF16)
    row = lambda w: pl.BlockSpec((tb, w), lambda b, t: (b * nt + t, 0))
    state = pl.BlockSpec((None, GLA_HEADS, GLA_DK, GLA_DV), lambda b, t: (b, 0, 0, 0))
    const = lambda shape: pl.BlockSpec(shape, lambda b, t: (0, 0))
    return pl.pallas_call(
        functools.partial(_gla_kernel, tb=tb),
        grid=(batch, nt),
        in_specs=[row(GLA_KW), row(GLA_KW), row(GLA_VW), row(GLA_KW), row(GLA_VW), state,
                  const((1, GLA_VW)), const(tri.shape), const(e.shape)],
        out_specs=(row(GLA_VW), state),
        out_shape=(jax.ShapeDtypeStruct((n, GLA_VW), BF16),
                   jax.ShapeDtypeStruct((batch, GLA_HEADS, GLA_DK, GLA_DV), F32)),
        scratch_shapes=[pltpu.VMEM((GLA_DV, GLA_KW), F32),
                        pltpu.VMEM((tb + GLA_SUB, GLA_KW), F32),
                        pltpu.VMEM((tb + GLA_SUB, GLA_KW), F32),
                        pltpu.VMEM((tb + GLA_SUB, GLA_VW), F32)],
        compiler_params=pltpu.CompilerParams(dimension_semantics=("arbitrary", "arbitrary"),
                                             vmem_limit_bytes=VMEM_LIMIT),
        name="gla",
    )(gq, gk, gv, glog, gr, s0, gn, tri, e)


def _ffn_kernel(x_ref, fo_ref, go_ref, wo_ref, g2_ref, wg_ref, wu_ref, wd_ref, gf_ref,
                y_ref, a_ref, *, chunk, fox_time_minor):
    fox_dot = _dot_tn if fox_time_minor else _dot
    y1 = (x_ref[...] + fox_dot(fo_ref[...], wo_ref[0:FOX_WIDTH, :])
          + _dot(go_ref[...], wo_ref[FOX_WIDTH:, :]))
    h2 = _rms(y1, g2_ref[...]).astype(BF16)
    for c in range(D_FF // chunk):
        cs = slice(c * chunk, (c + 1) * chunk)
        u = _dot(h2, wg_ref[:, cs])
        w = _dot(h2, wu_ref[:, cs])
        a_ref[:, cs] = (u * _sigmoid(u) * w).astype(BF16)
    y2 = y1 + _dot(a_ref[...], wd_ref[...])
    y_ref[...] = _rms(y2, gf_ref[...])


def _ffn(x2d, fo, go, prm, tm=512, chunk=256):
    n = x2d.shape[0]
    tm = min(tm, n)
    row = lambda w: pl.BlockSpec((tm, w), lambda i: (i, 0))
    const = lambda shape: pl.BlockSpec(shape, lambda i: (0, 0), pipeline_mode=pl.Buffered(1))
    fox_time_minor = fo.ndim == 3
    if fox_time_minor:
        tiles_per_seq = fo.shape[2] // tm
        fo_spec = pl.BlockSpec((None, FOX_WIDTH, tm),
                               lambda i: (i // tiles_per_seq, 0, i % tiles_per_seq))
    else:
        fo_spec = row(FOX_WIDTH)
    return pl.pallas_call(
        functools.partial(_ffn_kernel, chunk=chunk, fox_time_minor=fox_time_minor),
        grid=(n // tm,),
        in_specs=[row(D_MODEL), fo_spec, row(GLA_VW), const((D_MODEL, D_MODEL)),
                  const((1, D_MODEL)), const((D_MODEL, D_FF)), const((D_MODEL, D_FF)),
                  const((D_FF, D_MODEL)), const((1, D_MODEL))],
        out_specs=row(D_MODEL),
        out_shape=jax.ShapeDtypeStruct((n, D_MODEL), F32),
        scratch_shapes=[pltpu.VMEM((tm, D_FF), BF16)],
        compiler_params=pltpu.CompilerParams(dimension_semantics=("arbitrary",),
                                             vmem_limit_bytes=VMEM_LIMIT),
        name="ffn",
    )(x2d, fo, go, prm['wo'], prm['g2'], prm['wg'], prm['wu'], prm['wd'], prm['gf'])


def _layer_params(layer, norm1_g, w_in, w_gate2, b_gate2, b_forget, gla_norm_g, w_out,
                  norm2_g, w_gate, w_up, w_down, final_norm_g):
    wt = jnp.transpose(w_in[layer])
    o_fl = 3 * FOX_WIDTH
    o_gq = o_fl + FOX_HEADS
    o_gg = o_gq + 2 * GLA_KW + GLA_VW
    o_gr = o_gg + GLA_GATE_RANK
    wc = jnp.zeros((LANES, D_MODEL), F32).at[:GLA_GATE_RANK].set(wt[o_gg:o_gr])
    wg2 = jnp.zeros((LANES, GLA_KW), F32).at[:GLA_GATE_RANK].set(w_gate2[layer])
    return dict(
        g1=norm1_g[layer].reshape(1, D_MODEL),
        wq=wt[:FOX_WIDTH].astype(BF16),
        wkv=wt[FOX_WIDTH:o_fl].astype(BF16),
        wb=jnp.concatenate([wt[o_gq:o_gg], wt[o_gr:]], axis=0).astype(BF16),
        wc=wc.astype(BF16),
        wfl=wt[o_fl:o_gq].astype(BF16),
        wg2=wg2.astype(BF16),
        bg2=b_gate2[layer].reshape(1, GLA_KW),
        bfc=b_forget[layer].reshape(FOX_HEADS, 1),
        gn=gla_norm_g[layer].reshape(1, GLA_VW),
        wo=w_out[layer].astype(BF16),
        g2=norm2_g[layer].reshape(1, D_MODEL),
        wg=w_gate[layer].astype(BF16),
        wu=w_up[layer].astype(BF16),
        wd=w_down[layer].astype(BF16),
        gf=final_norm_g.reshape(1, D_MODEL),
    )


def kernel(x_prompt, x_sample, cache_fox_k, cache_fox_v, cache_fox_logf, state_gla, norm1_g, w_in,
           w_gate2, b_gate2, b_forget, gla_norm_g, w_out, norm2_g, w_gate, w_up, w_down, final_norm_g):
    depth = w_in.shape[0]
    assert depth == 1, "the final rmsnorm is fused into the layer's ffn kernel"
    bp, tp_, _ = x_prompt.shape
    bs, ts, _ = x_sample.shape
    past = cache_fox_k.shape[2]
    layer = 0
    prm = _layer_params(layer, norm1_g, w_in, w_gate2, b_gate2, b_forget, gla_norm_g, w_out,
                        norm2_g, w_gate, w_up, w_down, final_norm_g)
    by_time = lambda a, b, t: a.reshape(FOX_HEADS, b, t).transpose(1, 2, 0)[None]

    xp = x_prompt.reshape(bp * tp_, D_MODEL)
    qt, kt_p, vt_p, ktok, lf_p, ct, gq, gk, gv, glog, gr = _proj(xp, bp, tp_, prm, True)
    fox_o = _fox_prompt(qt, ktok, vt_p, ct, bp, tp_)
    s0 = jnp.zeros((bp, GLA_HEADS, GLA_DK, GLA_DV), F32)
    gla_o, s_p = _gla(gq, gk, gv, glog, gr, s0, prm['gn'], bp, tp_)
    y_p = _ffn(xp, fox_o, gla_o, prm)

    xs = x_sample.reshape(bs * ts, D_MODEL)
    q, k_s, v_s, lf_s, ct, gq, gk, gv, glog, gr = _proj(xs, bs, ts, prm, False)
    cn = ct.reshape(FOX_HEADS, bs, ts).transpose(1, 0, 2)
    lft = cache_fox_logf[layer].astype(F32).transpose(0, 2, 1)
    ck = cache_fox_k[layer].transpose(0, 2, 3, 1)
    cv = cache_fox_v[layer].transpose(0, 2, 3, 1)
    fox_o = _fox_sample(q, k_s, v_s, cn, lft, ck, cv, bs, ts, past)
    gla_o, s_s = _gla(gq, gk, gv, glog, gr, state_gla[layer].astype(F32), prm['gn'], bs, ts)
    y_s = _ffn(xs, fox_o, gla_o, prm)

    heads = lambda a, b, t: a.reshape(1, b, t, FOX_HEADS, FOX_HEAD_DIM)
    return (y_p.reshape(bp, tp_, D_MODEL), y_s.reshape(bs, ts, D_MODEL),
            kt_p.transpose(0, 3, 1, 2)[None], vt_p.transpose(0, 3, 1, 2)[None],
            by_time(lf_p, bp, tp_), s_p[None],
            heads(k_s, bs, ts), heads(v_s, bs, ts), by_time(lf_s, bs, ts), s_s[None])
```

```python
import functools

import numpy as np
import jax
import jax.numpy as jnp
from jax import lax
from jax.experimental import pallas as pl
from jax.experimental.pallas import tpu as pltpu

D_MODEL = 1024
FOX_HEADS = 8
FOX_HEAD_DIM = 64
FOX_WIDTH = FOX_HEADS * FOX_HEAD_DIM
GLA_HEADS = 4
GLA_DK = 64
GLA_DV = 128
GLA_KW = GLA_HEADS * GLA_DK
GLA_VW = GLA_HEADS * GLA_DV
GLA_GATE_RANK = 16
GLA_GATE_TEMP = 16.0
D_FF = 2816
EPS = 1e-6

LANES = 128
LOG2E = 1.4426950408889634
VMEM_LIMIT = 56 * 1024 * 1024

F32 = jnp.float32
BF16 = jnp.bfloat16
NEG_BIG = -1e30


def _log_sigmoid(x):
    return jnp.minimum(x, 0.0) - jnp.log1p(jnp.exp(-jnp.abs(x)))


def _sigmoid(x):
    return 1.0 / (1.0 + jnp.exp(-x))


def _split3_f32(x):
    hi = x.astype(BF16).astype(F32)
    r = x - hi
    mid = r.astype(BF16).astype(F32)
    lo = (r - mid).astype(BF16).astype(F32)
    return hi, mid, lo


def _dot(a, b):
    return jnp.dot(a, b, preferred_element_type=F32)


def _dot_nt(a, b):
    return lax.dot_general(a, b, (((1,), (1,)), ((), ())), preferred_element_type=F32)


def _dot_tn(a, b):
    return lax.dot_general(a, b, (((0,), (0,)), ((), ())), preferred_element_type=F32)


def _rms(x, g):
    return x * lax.rsqrt(jnp.mean(x * x, axis=-1, keepdims=True) + EPS) * g


def _softmax_step(s, carry, pv):
    m, l, acc = carry
    m_new = jnp.maximum(m, jnp.max(s, axis=-1, keepdims=True))
    alpha = jnp.exp(m - m_new)
    pm = jnp.exp(s - m_new)
    l = alpha * l + jnp.sum(pm, axis=-1, keepdims=True)
    acc = alpha * acc + pv(pm.astype(BF16))
    return m_new, l, acc


def _proj_kernel(x_ref, g1_ref, wq_ref, wkv_ref, wb_ref, wc_ref, wfl_ref, wg2_ref, bg2_ref,
                 bfc_ref, tri_ref,
                 *rest, tiles_per_seq, time_minor):
    i = pl.program_id(0)
    h = _rms(x_ref[...], g1_ref[...]).astype(BF16)
    tm = h.shape[0]
    scale = FOX_HEAD_DIM ** -0.5

    if time_minor:
        (q_ref, k_ref, v_ref, ktok_ref, lft_ref, ct_ref, gq_ref, gk_ref, gv_ref, glog_ref, gr_ref,
         carry_ref) = rest
        q_ref[...] = (_dot_nt(wq_ref[...], h) * scale).astype(BF16)
        kvt = _dot_nt(wkv_ref[...], h)
        k_ref[...] = kvt[:FOX_WIDTH].reshape(FOX_HEADS, FOX_HEAD_DIM, tm)
        v_ref[...] = kvt[FOX_WIDTH:].reshape(FOX_HEADS, FOX_HEAD_DIM, tm)
        ktok_ref[...] = _dot_nt(h, wkv_ref[0:FOX_WIDTH, :]).astype(BF16)
    else:
        (q_ref, k_ref, v_ref, lft_ref, ct_ref, gq_ref, gk_ref, gv_ref, glog_ref, gr_ref,
         carry_ref) = rest
        q_ref[...] = (_dot_nt(h, wq_ref[...]) * scale).astype(BF16)
        kv = _dot_nt(h, wkv_ref[...])
        k_ref[...] = kv[:, :FOX_WIDTH]
        v_ref[...] = kv[:, FOX_WIDTH:]

    zb = _dot_nt(h, wb_ref[...])
    gq_ref[...] = zb[:, :GLA_KW]
    gk_ref[...] = zb[:, GLA_KW:2 * GLA_KW]
    gv_ref[...] = zb[:, 2 * GLA_KW:2 * GLA_KW + GLA_VW].astype(BF16)
    gr_ref[...] = zb[:, 2 * GLA_KW + GLA_VW:]

    zc = _dot_nt(h, wc_ref[...])
    gpre = _dot(zc.astype(BF16), wg2_ref[...]) + bg2_ref[...]
    glog_ref[...] = _log_sigmoid(gpre) * (1.0 / GLA_GATE_TEMP)

    logf_t = _log_sigmoid(_dot_nt(wfl_ref[...], h) + bfc_ref[...])
    lft_ref[...] = logf_t
    parts = jnp.concatenate(_split3_f32(logf_t), axis=0).astype(BF16)
    cs = _dot(parts, tri_ref[...])
    cs = cs[0:8] + cs[8:16] + cs[16:24]

    @pl.when(i % tiles_per_seq == 0)
    def _():
        carry_ref[...] = jnp.zeros_like(carry_ref)

    ct = cs + carry_ref[:, 0:1]
    ct_ref[...] = ct
    carry_ref[...] = jnp.broadcast_to(ct[:, tm - 1:], carry_ref.shape)


def _proj(x2d, batch, seq_len, prm, time_minor, tm=512):
    n = x2d.shape[0]
    tm = min(tm, n)
    tiles_per_seq = max(seq_len // tm, 1)
    idx = np.arange(tm)
    tri = ((idx[:, None] <= idx[None, :]) & (idx[:, None] // seq_len == idx[None, :] // seq_len))
    tri = jnp.asarray(tri.astype(np.float32), dtype=BF16)
    const = lambda a: pl.BlockSpec(a.shape, lambda i: (0, 0))
    row = lambda w: pl.BlockSpec((tm, w), lambda i: (i, 0))
    col = pl.BlockSpec((FOX_HEADS, tm), lambda i: (0, i))
    sds = jax.ShapeDtypeStruct
    by_seq = lambda i: (i // tiles_per_seq, 0, i % tiles_per_seq)
    if time_minor:
        kv_shape = sds((batch, FOX_HEADS, FOX_HEAD_DIM, seq_len), F32)
        kv_spec = pl.BlockSpec((None, FOX_HEADS, FOX_HEAD_DIM, tm),
                               lambda i: (i // tiles_per_seq, 0, 0, i % tiles_per_seq))
        fox = [(sds((batch, FOX_WIDTH, seq_len), BF16), pl.BlockSpec((None, FOX_WIDTH, tm), by_seq)),
               (kv_shape, kv_spec), (kv_shape, kv_spec),
               (sds((n, FOX_WIDTH), BF16), row(FOX_WIDTH))]
    else:
        fox = [(sds((n, FOX_WIDTH), BF16), row(FOX_WIDTH)),
               (sds((n, FOX_WIDTH), F32), row(FOX_WIDTH)),
               (sds((n, FOX_WIDTH), F32), row(FOX_WIDTH))]
    outs = fox + [
        (sds((FOX_HEADS, n), F32), col),
        (sds((FOX_HEADS, n), F32), col),
        (sds((n, GLA_KW), F32), row(GLA_KW)),
        (sds((n, GLA_KW), F32), row(GLA_KW)),
        (sds((n, GLA_VW), BF16), row(GLA_VW)),
        (sds((n, GLA_KW), F32), row(GLA_KW)),
        (sds((n, GLA_VW), F32), row(GLA_VW)),
    ]
    names = ('g1', 'wq', 'wkv', 'wb', 'wc', 'wfl', 'wg2', 'bg2', 'bfc')
    return pl.pallas_call(
        functools.partial(_proj_kernel, tiles_per_seq=tiles_per_seq, time_minor=time_minor),
        grid=(n // tm,),
        in_specs=[row(D_MODEL)] + [const(prm[k]) for k in names] + [const(tri)],
        out_specs=tuple(s for _, s in outs), out_shape=tuple(s for s, _ in outs),
        scratch_shapes=[pltpu.VMEM((FOX_HEADS, LANES), F32)],
        compiler_params=pltpu.CompilerParams(dimension_semantics=("arbitrary",),
                                             vmem_limit_bytes=VMEM_LIMIT),
        name="proj",
    )(x2d, *[prm[k] for k in names], tri)


def _fox_prompt_kernel(q_ref, k_ref, v_ref, c_ref, o_ref, vb_ref, cb_ref,
                       sa_ref, sb_ref, pa_ref, pb_ref, acc_ref, *, tk):
    p = pl.program_id(1)
    qi = pl.program_id(2)
    nk = vb_ref.shape[0]

    @pl.when(qi == 0)
    def _():
        for jj in range(nk):
            vb_ref[jj] = v_ref[:, :, jj * tk:(jj + 1) * tk].reshape(LANES, tk).astype(BF16)
        parts = jnp.concatenate(_split3_f32(c_ref[...]), axis=0).astype(BF16)
        r = lax.broadcasted_iota(jnp.int32, (3 * FOX_HEADS, LANES), 0) % FOX_HEADS
        for hh in range(2):
            sel = jnp.where(r == 2 * p + hh, -1.0, 0.0).astype(BF16)
            cb_ref[hh] = _dot_tn(parts, sel)

    rowh = lax.broadcasted_iota(jnp.int32, (LANES, tk), 0) < FOX_HEAD_DIM
    key = lax.broadcasted_iota(jnp.int32, (tk, tk), 0)
    qry = lax.broadcasted_iota(jnp.int32, (tk, tk), 1)
    causal = key <= qry

    chains = [(hh, part) for hh in range(2) for part in range(2)]
    qs = []
    for hh, part in chains:
        q = q_ref[:, part * tk:(part + 1) * tk]
        qs.append(jnp.where(rowh if hh == 0 else jnp.logical_not(rowh), q, jnp.zeros_like(q)))

    everyone = list(range(len(chains)))
    second = [n for n in everyone if chains[n][1] == 1]

    def scores(j, s_ref, live):
        kt = k_ref[pl.ds(pl.multiple_of(j * tk, tk), tk), :]
        for n in live:
            s_ref[n] = _dot(kt, qs[n])

    def values(j, p_ref, live):
        vt = vb_ref[j]
        return [_dot(vt, p_ref[n]) if n in live else None for n in everyone]

    def softmax(j, s_ref, p_ref, stats, live, masked_part):
        bias = [jnp.tile(cb_ref[hh, pl.ds(pl.multiple_of(j * tk, tk), tk), :], (1, tk // LANES))
                for hh in range(2)]
        stats, alphas = list(stats), [None] * len(chains)
        for n in live:
            hh, part = chains[n]
            s = s_ref[n] + bias[hh]
            if part == masked_part:
                s = jnp.where(causal, s, -jnp.inf)
            m, l = stats[n]
            m_new = jnp.maximum(m, jnp.max(s, axis=0, keepdims=True))
            alphas[n] = jnp.exp(m - m_new)
            pm = jnp.exp(s - m_new)
            p_ref[n] = pm.astype(BF16)
            stats[n] = (m_new, alphas[n] * l + jnp.sum(pm, axis=0, keepdims=True))
        return stats, alphas

    def rescale(alphas, pv, live):
        for n in live:
            acc_ref[n] = alphas[n] * acc_ref[n] + pv[n]

    def half(j, s_cur, s_nxt, p_cur, p_prv, stats, alphas, live_next):
        scores(j + 1, s_nxt, live_next)
        pv = values(jnp.maximum(j - 1, 0), p_prv, everyone)
        stats, alphas_new = softmax(j, s_cur, p_cur, stats, everyone, None)
        rescale(alphas, pv, everyone)
        return stats, alphas_new

    def body(jj, carry):
        stats, alphas = carry
        stats, alphas = half(2 * jj, sa_ref, sb_ref, pa_ref, pb_ref, stats, alphas, everyone)
        return half(2 * jj + 1, sb_ref, sa_ref, pb_ref, pa_ref, stats, alphas, everyone)

    pb_ref[...] = jnp.zeros(pb_ref.shape, BF16)
    acc_ref[...] = jnp.zeros(acc_ref.shape, F32)
    scores(0, sa_ref, everyone)
    init = ([(jnp.full((1, tk), NEG_BIG, F32), jnp.zeros((1, tk), F32)) for _ in chains],
            [jnp.ones((1, tk), F32) for _ in chains])
    stats, alphas = lax.fori_loop(0, qi, body, init)
    j0 = 2 * qi
    scores(j0 + 1, sb_ref, second)
    pv = values(jnp.maximum(j0 - 1, 0), pb_ref, everyone)
    stats, alphas_a = softmax(j0, sa_ref, pa_ref, stats, everyone, 0)
    rescale(alphas, pv, everyone)
    pv = values(j0, pa_ref, everyone)
    stats, alphas_b = softmax(j0 + 1, sb_ref, pb_ref, stats, second, 1)
    rescale(alphas_a, pv, everyone)
    rescale(alphas_b, values(j0 + 1, pb_ref, second), second)

    carries = [(None, stats[n][1], acc_ref[n]) for n in everyone]

    res = [acc / l for (_, l, acc) in carries]
    hd = FOX_HEAD_DIM
    for part in range(2):
        o_ref[:, part * tk:(part + 1) * tk] = jnp.concatenate(
            [res[part][:hd], res[2 + part][hd:]], axis=0).astype(BF16)


def _fox_prompt(qt, ktok, vt, ct, batch, seq, tk=256):
    tq = 2 * tk
    nq = seq // tq
    nk = seq // tk
    pairs = FOX_WIDTH // LANES
    return pl.pallas_call(
        functools.partial(_fox_prompt_kernel, tk=tk),
        grid=(batch, pairs, nq),
        in_specs=[pl.BlockSpec((None, LANES, tq), lambda b, p, i: (b, p, i)),
                  pl.BlockSpec((seq, LANES), lambda b, p, i: (b, p)),
                  pl.BlockSpec((None, 2, FOX_HEAD_DIM, seq), lambda b, p, i: (b, p, 0, 0)),
                  pl.BlockSpec((FOX_HEADS, seq), lambda b, p, i: (0, b))],
        out_specs=pl.BlockSpec((None, LANES, tq), lambda b, p, i: (b, p, i)),
        out_shape=jax.ShapeDtypeStruct((batch, FOX_WIDTH, seq), BF16),
        scratch_shapes=[pltpu.VMEM((nk, LANES, tk), BF16), pltpu.VMEM((2, seq, LANES), F32),
                        pltpu.VMEM((4, tk, tk), F32), pltpu.VMEM((4, tk, tk), F32),
                        pltpu.VMEM((4, tk, tk), BF16), pltpu.VMEM((4, tk, tk), BF16),
                        pltpu.VMEM((4, LANES, tk), F32)],
        compiler_params=pltpu.CompilerParams(
            dimension_semantics=("arbitrary", "arbitrary", "arbitrary"),
            vmem_limit_bytes=VMEM_LIMIT),
        name="fox_prompt",
    )(qt, ktok, vt, ct)


def _fox_sample_kernel(q_ref, kn_ref, vn_ref, cn_ref, lft_ref, mlow_ref, ck_ref, cv_ref,
                       o_ref, qh_ref, suf_ref, m_ref, l_ref, acc_ref, *, tp, nt, tn):
    j = pl.program_id(1)
    blk = 2 * LANES
    hd = FOX_HEAD_DIM

    @pl.when(j == 0)
    def _init():
        for h in range(FOX_HEADS):
            qh_ref[h] = q_ref[:, h * hd:(h + 1) * hd]
        m_ref[...] = jnp.full(m_ref.shape, NEG_BIG, F32)
        l_ref[...] = jnp.zeros(l_ref.shape, F32)
        acc_ref[...] = jnp.zeros(acc_ref.shape, F32)
        carry = jnp.zeros((FOX_HEADS, 1), F32)
        per_tile = tp // blk
        for b in reversed(range(nt * per_tile)):
            x = lft_ref[:, b * blk:(b + 1) * blk]
            parts = jnp.concatenate(_split3_f32(x), axis=0).astype(BF16)
            y = _dot(parts, mlow_ref[...])
            off = (b % per_tile) * blk
            suf_ref[b // per_tile, :, off:off + blk] = y[0:8] + y[8:16] + y[16:24] + carry
            carry = carry + jnp.sum(x, axis=1, keepdims=True)

    def update(s, pv):
        m, l, acc = _softmax_step(s, (m_ref[...], l_ref[...], acc_ref[...]), pv)
        m_ref[...] = m
        l_ref[...] = l
        acc_ref[...] = acc

    bmm = lambda a, b, ca, cb: lax.dot_general(a, b, (((ca,), (cb,)), ((0,), (0,))),
                                               preferred_element_type=F32)
    qh = qh_ref[...]

    kt = ck_ref[...].astype(BF16)
    vt = cv_ref[...].astype(BF16)
    s = bmm(qh, kt, 2, 1).reshape(FOX_HEADS * tn, tp) + jnp.repeat(suf_ref[j], tn, axis=0)
    update(s.reshape(FOX_HEADS, tn, tp), lambda pm: bmm(pm, vt, 2, 2))

    @pl.when(j == nt - 1)
    def _fin():
        per_head = lambda ref: jnp.stack([ref[:, h * hd:(h + 1) * hd] for h in range(FOX_HEADS)],
                                         axis=0).astype(BF16)
        kn = per_head(kn_ref)
        vn = per_head(vn_ref)
        r = lax.broadcasted_iota(jnp.int32, (FOX_HEADS, tn, tn), 1)
        c = lax.broadcasted_iota(jnp.int32, (FOX_HEADS, tn, tn), 2)
        s = jnp.where(c <= r, bmm(qh, kn, 2, 2) - cn_ref[...][:, None, :], -jnp.inf)
        update(s, lambda pm: bmm(pm, vn, 2, 1))
        o = acc_ref[...] / l_ref[...]
        o_ref[...] = jnp.concatenate([o[h] for h in range(FOX_HEADS)], axis=1).astype(BF16)


def _fox_sample(q, kn, vn, cn, lft, ck, cv, batch, tn, past, tp=1024):
    nt = past // tp
    blk = 2 * LANES
    idx = np.arange(blk)
    mlow = jnp.asarray((idx[:, None] > idx[None, :]).astype(np.float32), dtype=BF16)
    per_b = lambda w: pl.BlockSpec((tn, w), lambda b, j: (b, 0))
    cache = pl.BlockSpec((None, FOX_HEADS, FOX_HEAD_DIM, tp), lambda b, j: (b, 0, 0, j))
    return pl.pallas_call(
        functools.partial(_fox_sample_kernel, tp=tp, nt=nt, tn=tn),
        grid=(batch, nt),
        in_specs=[per_b(FOX_WIDTH), per_b(FOX_WIDTH), per_b(FOX_WIDTH),
                  pl.BlockSpec((None, FOX_HEADS, tn), lambda b, j: (b, 0, 0)),
                  pl.BlockSpec((None, FOX_HEADS, past), lambda b, j: (b, 0, 0)),
                  pl.BlockSpec((blk, blk), lambda b, j: (0, 0)),
                  cache, cache],
        out_specs=per_b(FOX_WIDTH),
        out_shape=jax.ShapeDtypeStruct((batch * tn, FOX_WIDTH), BF16),
        scratch_shapes=[pltpu.VMEM((FOX_HEADS, tn, FOX_HEAD_DIM), BF16),
                        pltpu.VMEM((nt, FOX_HEADS, tp), F32),
                        pltpu.VMEM((FOX_HEADS, tn, 1), F32), pltpu.VMEM((FOX_HEADS, tn, 1), F32),
                        pltpu.VMEM((FOX_HEADS, tn, FOX_HEAD_DIM), F32)],
        compiler_params=pltpu.CompilerParams(dimension_semantics=("arbitrary", "arbitrary"),
                                             vmem_limit_bytes=VMEM_LIMIT),
        name="fox_sample",
    )(q, kn, vn, cn, lft, mlow, ck, cv)


def _gla_kernel(q_ref, k_ref, v_ref, g_ref, r_ref, s0_ref, gn_ref, w_ref, lv_ref,
                o_ref, s_ref, st_ref, *, tb, levels):
    t = pl.program_id(1)
    half = LANES // 2

    @pl.when(t == 0)
    def _():
        st_ref[...] = s0_ref[...].reshape(GLA_KW, GLA_DV).T

    r = r_ref[...]
    gate = gn_ref[...] * (r * _sigmoid(r))

    g3 = jnp.concatenate(_split3_f32(g_ref[...] * LOG2E), axis=0).astype(BF16)
    rows = w_ref.shape[0] // 2
    dec = jnp.exp2(jnp.concatenate([_dot(w_ref[0:rows, :], g3), _dot(w_ref[rows:, :], g3)], axis=0))
    from_start = dec[0:tb]
    to_end = dec[tb:2 * tb]

    q = q_ref[...] * (GLA_DK ** -0.5)
    k = k_ref[...]
    v = v_ref[...]
    row = lax.broadcasted_iota(jnp.int32, (tb, GLA_KW), 0)
    low = lax.broadcasted_iota(jnp.int32, (tb, LANES), 1) < half
    lv = lv_ref[...]

    def pair_scores(xq, xk):
        outs = []
        for p in range(GLA_HEADS // 2):
            a = xq[:, p * LANES:(p + 1) * LANES]
            zero = jnp.zeros_like(a)
            lhs = jnp.concatenate([jnp.where(low, a, zero), jnp.where(low, zero, a)], axis=0)
            outs.append(_dot_nt(lhs, xk[:, p * LANES:(p + 1) * LANES]))
        return outs

    here = lv == -1
    a = [jnp.where(here, r_, 0.0) for r_ in pair_scores(q.astype(BF16), k.astype(BF16))]
    for l in range(levels):
        upper = ((row >> l) & 1) == 1
        x = (jnp.where(upper, q, k) * dec[(l + 2) * tb:(l + 3) * tb]).astype(BF16)
        here = lv == l
        a = [jnp.where(here, r_, a_) for r_, a_ in zip(pair_scores(x, x), a)]
    o = jnp.concatenate(
        [_dot(a[h // 2][(h % 2) * tb:(h % 2 + 1) * tb].astype(BF16), v[:, h * GLA_DV:(h + 1) * GLA_DV])
         for h in range(GLA_HEADS)], axis=1)

    lane = lax.broadcasted_iota(jnp.int32, (tb, GLA_KW), 1)
    head_sel = [(lane >= h * GLA_DK) & (lane < (h + 1) * GLA_DK) for h in range(GLA_HEADS)]
    st = st_ref[...]
    qt = (q * from_start).astype(BF16)
    kt = (k * to_end).astype(BF16)
    zero = jnp.zeros_like(qt)
    q4 = jnp.concatenate([jnp.where(sel, qt, zero) for sel in head_sel], axis=0)
    oi = _dot_nt(q4, st.astype(BF16))
    o = o + jnp.concatenate([oi[h * tb:(h + 1) * tb] for h in range(GLA_HEADS)], axis=1)
    upd = None
    for h in range(GLA_HEADS):
        u = _dot_tn(v[:, h * GLA_DV:(h + 1) * GLA_DV], jnp.where(head_sel[h], kt, zero))
        upd = u if upd is None else upd + u
    st = from_start[tb - 1:tb, :] * st + upd
    st_ref[...] = st

    @pl.when(t == pl.num_programs(1) - 1)
    def _():
        s_ref[...] = st.T.reshape(s_ref.shape)

    outs = []
    for h in range(GLA_HEADS):
        oh = o[:, h * GLA_DV:(h + 1) * GLA_DV]
        outs.append(oh * lax.rsqrt(jnp.mean(oh * oh, axis=-1, keepdims=True) + EPS))
    o_ref[...] = (jnp.concatenate(outs, axis=1) * gate).astype(BF16)


def _gla(gq, gk, gv, glog, gr, s0, gn, batch, seq, tb=128):
    tb = min(tb, seq)
    nt = seq // tb
    n = batch * seq
    levels = tb.bit_length() - 1
    assert tb == 1 << levels
    ti = np.arange(tb)[:, None]
    si = np.arange(tb)[None, :]
    blocks = [si <= ti, si > ti]
    for l in range(levels):
        m = 1 << l
        mid = ti - ti % (2 * m) + m - 1
        upper = ti % (2 * m) >= m
        blocks.append(np.where(upper, (si > mid) & (si <= ti), (si > ti) & (si <= mid)))
    w = np.concatenate(blocks, axis=0).astype(np.float32)
    w = jnp.asarray(np.concatenate([w, w, w], axis=1), dtype=BF16)
    x = ti ^ si
    lv = np.where(si < ti, np.floor(np.log2(np.maximum(x, 1))).astype(np.int32),
                  np.where(si == ti, -1, -2)).astype(np.int32)
    lv = jnp.asarray(np.concatenate([lv, lv], axis=0))
    row = lambda w_: pl.BlockSpec((tb, w_), lambda b, t: (b * nt + t, 0))
    state = pl.BlockSpec((None, GLA_HEADS, GLA_DK, GLA_DV), lambda b, t: (b, 0, 0, 0))
    const = lambda shape: pl.BlockSpec(shape, lambda b, t: (0, 0))
    return pl.pallas_call(
        functools.partial(_gla_kernel, tb=tb, levels=levels),
        grid=(batch, nt),
        in_specs=[row(GLA_KW), row(GLA_KW), row(GLA_VW), row(GLA_KW), row(GLA_VW), state,
                  const((1, GLA_VW)), const(w.shape), const(lv.shape)],
        out_specs=(row(GLA_VW), state),
        out_shape=(jax.ShapeDtypeStruct((n, GLA_VW), BF16),
                   jax.ShapeDtypeStruct((batch, GLA_HEADS, GLA_DK, GLA_DV), F32)),
        scratch_shapes=[pltpu.VMEM((GLA_DV, GLA_KW), F32)],
        compiler_params=pltpu.CompilerParams(dimension_semantics=("arbitrary", "arbitrary"),
                                             vmem_limit_bytes=VMEM_LIMIT),
        name="gla",
    )(gq, gk, gv, glog, gr, s0, gn, w, lv)


def _ffn_kernel(x_ref, fo_ref, go_ref, wo_ref, g2_ref, wg_ref, wu_ref, wd_ref, gf_ref,
                y_ref, a_ref, *, chunk, fox_time_minor):
    fox_dot = _dot_tn if fox_time_minor else _dot
    y1 = (x_ref[...] + fox_dot(fo_ref[...], wo_ref[0:FOX_WIDTH, :])
          + _dot(go_ref[...], wo_ref[FOX_WIDTH:, :]))
    h2 = _rms(y1, g2_ref[...]).astype(BF16)
    for c in range(D_FF // chunk):
        cs = slice(c * chunk, (c + 1) * chunk)
        u = _dot(h2, wg_ref[:, cs])
        w = _dot(h2, wu_ref[:, cs])
        a_ref[:, cs] = (u * _sigmoid(u) * w).astype(BF16)
    y2 = y1 + _dot(a_ref[...], wd_ref[...])
    y_ref[...] = _rms(y2, gf_ref[...])


def _ffn(x2d, fo, go, prm, tm=512, chunk=256):
    n = x2d.shape[0]
    tm = min(tm, n)
    row = lambda w: pl.BlockSpec((tm, w), lambda i: (i, 0))
    const = lambda shape: pl.BlockSpec(shape, lambda i: (0, 0), pipeline_mode=pl.Buffered(1))
    fox_time_minor = fo.ndim == 3
    if fox_time_minor:
        tiles_per_seq = fo.shape[2] // tm
        fo_spec = pl.BlockSpec((None, FOX_WIDTH, tm),
                               lambda i: (i // tiles_per_seq, 0, i % tiles_per_seq))
    else:
        fo_spec = row(FOX_WIDTH)
    return pl.pallas_call(
        functools.partial(_ffn_kernel, chunk=chunk, fox_time_minor=fox_time_minor),
        grid=(n // tm,),
        in_specs=[row(D_MODEL), fo_spec, row(GLA_VW), const((D_MODEL, D_MODEL)),
                  const((1, D_MODEL)), const((D_MODEL, D_FF)), const((D_MODEL, D_FF)),
                  const((D_FF, D_MODEL)), const((1, D_MODEL))],
        out_specs=row(D_MODEL),
        out_shape=jax.ShapeDtypeStruct((n, D_MODEL), F32),
        scratch_shapes=[pltpu.VMEM((tm, D_FF), BF16)],
        compiler_params=pltpu.CompilerParams(dimension_semantics=("arbitrary",),
                                             vmem_limit_bytes=VMEM_LIMIT),
        name="ffn",
    )(x2d, fo, go, prm['wo'], prm['g2'], prm['wg'], prm['wu'], prm['wd'], prm['gf'])


def _layer_params(layer, norm1_g, w_in, w_gate2, b_gate2, b_forget, gla_norm_g, w_out,
                  norm2_g, w_gate, w_up, w_down, final_norm_g):
    wt = jnp.transpose(w_in[layer])
    o_fl = 3 * FOX_WIDTH
    o_gq = o_fl + FOX_HEADS
    o_gg = o_gq + 2 * GLA_KW + GLA_VW
    o_gr = o_gg + GLA_GATE_RANK
    wc = jnp.zeros((LANES, D_MODEL), F32).at[:GLA_GATE_RANK].set(wt[o_gg:o_gr])
    wg2 = jnp.zeros((LANES, GLA_KW), F32).at[:GLA_GATE_RANK].set(w_gate2[layer])
    return dict(
        g1=norm1_g[layer].reshape(1, D_MODEL),
        wq=wt[:FOX_WIDTH].astype(BF16),
        wkv=wt[FOX_WIDTH:o_fl].astype(BF16),
        wb=jnp.concatenate([wt[o_gq:o_gg], wt[o_gr:]], axis=0).astype(BF16),
        wc=wc.astype(BF16),
        wfl=wt[o_fl:o_gq].astype(BF16),
        wg2=wg2.astype(BF16),
        bg2=b_gate2[layer].reshape(1, GLA_KW),
        bfc=b_forget[layer].reshape(FOX_HEADS, 1),
        gn=gla_norm_g[layer].reshape(1, GLA_VW),
        wo=w_out[layer].astype(BF16),
        g2=norm2_g[layer].reshape(1, D_MODEL),
        wg=w_gate[layer].astype(BF16),
        wu=w_up[layer].astype(BF16),
        wd=w_down[layer].astype(BF16),
        gf=final_norm_g.reshape(1, D_MODEL),
    )


def kernel(x_prompt, x_sample, cache_fox_k, cache_fox_v, cache_fox_logf, state_gla, norm1_g, w_in,
           w_gate2, b_gate2, b_forget, gla_norm_g, w_out, norm2_g, w_gate, w_up, w_down, final_norm_g):
    depth = w_in.shape[0]
    assert depth == 1, "the final rmsnorm is fused into the layer's ffn kernel"
    bp, tp_, _ = x_prompt.shape
    bs, ts, _ = x_sample.shape
    past = cache_fox_k.shape[2]
    layer = 0
    prm = _layer_params(layer, norm1_g, w_in, w_gate2, b_gate2, b_forget, gla_norm_g, w_out,
                        norm2_g, w_gate, w_up, w_down, final_norm_g)
    by_time = lambda a, b, t: a.reshape(FOX_HEADS, b, t).transpose(1, 2, 0)[None]

    xp = x_prompt.reshape(bp * tp_, D_MODEL)
    qt, kt_p, vt_p, ktok, lf_p, ct, gq, gk, gv, glog, gr = _proj(xp, bp, tp_, prm, True)
    fox_o = _fox_prompt(qt, ktok, vt_p, ct, bp, tp_)
    s0 = jnp.zeros((bp, GLA_HEADS, GLA_DK, GLA_DV), F32)
    gla_o, s_p = _gla(gq, gk, gv, glog, gr, s0, prm['gn'], bp, tp_)
    y_p = _ffn(xp, fox_o, gla_o, prm)

    xs = x_sample.reshape(bs * ts, D_MODEL)
    q, k_s, v_s, lf_s, ct, gq, gk, gv, glog, gr = _proj(xs, bs, ts, prm, False)
    cn = ct.reshape(FOX_HEADS, bs, ts).transpose(1, 0, 2)
    lft = cache_fox_logf[layer].astype(F32).transpose(0, 2, 1)
    ck = cache_fox_k[layer].transpose(0, 2, 3, 1)
    cv = cache_fox_v[layer].transpose(0, 2, 3, 1)
    fox_o = _fox_sample(q, k_s, v_s, cn, lft, ck, cv, bs, ts, past)
    gla_o, s_s = _gla(gq, gk, gv, glog, gr, state_gla[layer].astype(F32), prm['gn'], bs, ts)
    y_s = _ffn(xs, fox_o, gla_o, prm)

    heads = lambda a, b, t: a.reshape(1, b, t, FOX_HEADS, FOX_HEAD_DIM)
    return (y_p.reshape(bp, tp_, D_MODEL), y_s.reshape(bs, ts, D_MODEL),
            kt_p.transpose(0, 3, 1, 2)[None], vt_p.transpose(0, 3, 1, 2)[None],
            by_time(lf_p, bp, tp_), s_p[None],
            heads(k_s, bs, ts), heads(v_s, bs, ts), by_time(lf_s, bs, ts), s_s[None])
```

```python
import functools

import numpy as np
import jax
import jax.numpy as jnp
from jax import lax
from jax.experimental import pallas as pl
from jax.experimental.pallas import tpu as pltpu

D_MODEL = 1024
FOX_HEADS = 8
FOX_HEAD_DIM = 64
FOX_WIDTH = FOX_HEADS * FOX_HEAD_DIM
GLA_HEADS = 4
GLA_DK = 64
GLA_DV = 128
GLA_KW = GLA_HEADS * GLA_DK
GLA_VW = GLA_HEADS * GLA_DV
GLA_GATE_RANK = 16
GLA_GATE_TEMP = 16.0
D_FF = 2816
EPS = 1e-6

LANES = 128
LOG2E = 1.4426950408889634
VMEM_LIMIT = 56 * 1024 * 1024

F32 = jnp.float32
BF16 = jnp.bfloat16
NEG_BIG = -1e30


def _log_sigmoid(x):
    return jnp.minimum(x, 0.0) - jnp.log1p(jnp.exp(-jnp.abs(x)))


def _sigmoid(x):
    return 1.0 / (1.0 + jnp.exp(-x))


def _split3_f32(x):
    hi = x.astype(BF16).astype(F32)
    r = x - hi
    mid = r.astype(BF16).astype(F32)
    lo = (r - mid).astype(BF16).astype(F32)
    return hi, mid, lo


def _dot(a, b):
    return jnp.dot(a, b, preferred_element_type=F32)


def _dot_nt(a, b):
    return lax.dot_general(a, b, (((1,), (1,)), ((), ())), preferred_element_type=F32)


def _dot_tn(a, b):
    return lax.dot_general(a, b, (((0,), (0,)), ((), ())), preferred_element_type=F32)


def _rms(x, g):
    return x * lax.rsqrt(jnp.mean(x * x, axis=-1, keepdims=True) + EPS) * g


def _softmax_step(s, carry, pv):
    m, l, acc = carry
    m_new = jnp.maximum(m, jnp.max(s, axis=-1, keepdims=True))
    alpha = jnp.exp(m - m_new)
    pm = jnp.exp(s - m_new)
    l = alpha * l + jnp.sum(pm, axis=-1, keepdims=True)
    acc = alpha * acc + pv(pm.astype(BF16))
    return m_new, l, acc


def _proj_kernel(x_ref, g1_ref, wq_ref, wkv_ref, wb_ref, wc_ref, wfl_ref, wg2_ref, bg2_ref,
                 bfc_ref, tri_ref,
                 *rest, tiles_per_seq, time_minor):
    carry_ref = rest[-1]

    @pl.when(pl.program_id(0) % tiles_per_seq == 0)
    def _():
        carry_ref[...] = jnp.zeros_like(carry_ref)

    h = _rms(x_ref[...], g1_ref[...]).astype(BF16)
    tm = h.shape[0]
    scale = FOX_HEAD_DIM ** -0.5

    zc = _dot_nt(h, wc_ref[...])
    fl_t = _dot_nt(wfl_ref[...], h)

    if time_minor:
        (q_ref, k_ref, v_ref, ktok_ref, lft_ref, ct_ref, gq_ref, gk_ref, gv_ref, glog_ref, gr_ref,
         carry_ref) = rest
        q_ref[...] = (_dot_nt(wq_ref[...], h) * (scale * LOG2E)).astype(BF16)
    else:
        (q_ref, k_ref, v_ref, lft_ref, ct_ref, gq_ref, gk_ref, gv_ref, glog_ref, gr_ref,
         carry_ref) = rest
        q_ref[...] = (_dot_nt(h, wq_ref[...]) * scale).astype(BF16)

    logf_t = _log_sigmoid(fl_t + bfc_ref[...])
    lft_ref[...] = logf_t
    parts = jnp.concatenate(_split3_f32(logf_t), axis=0).astype(BF16)
    gg = zc.astype(BF16)

    if time_minor:
        kvt = _dot_nt(wkv_ref[...], h)
        k_ref[...] = kvt[:FOX_WIDTH].reshape(FOX_HEADS, FOX_HEAD_DIM, tm)
        v_ref[...] = kvt[FOX_WIDTH:].reshape(FOX_HEADS, FOX_HEAD_DIM, tm)
        ktok_ref[...] = _dot_nt(h, wkv_ref[0:FOX_WIDTH, :]).astype(BF16)
    else:
        kv = _dot_nt(h, wkv_ref[...])
        k_ref[...] = kv[:, :FOX_WIDTH]
        v_ref[...] = kv[:, FOX_WIDTH:]

    gpre = _dot(gg, wg2_ref[...]) + bg2_ref[...]
    cs = _dot(parts, tri_ref[...])

    zb = _dot_nt(h, wb_ref[...])
    gq_ref[...] = zb[:, :GLA_KW]
    gk_ref[...] = zb[:, GLA_KW:2 * GLA_KW]
    gv_ref[...] = zb[:, 2 * GLA_KW:2 * GLA_KW + GLA_VW].astype(BF16)
    gr_ref[...] = zb[:, 2 * GLA_KW + GLA_VW:]

    glog_ref[...] = _log_sigmoid(gpre) * (1.0 / GLA_GATE_TEMP)

    ct = cs[0:8] + cs[8:16] + cs[16:24] + carry_ref[:, 0:1]
    ct_ref[...] = ct
    carry_ref[...] = jnp.broadcast_to(ct[:, tm - 1:], carry_ref.shape)


def _proj(x2d, batch, seq_len, prm, time_minor, tm=512):
    n = x2d.shape[0]
    tm = min(tm, n)
    tiles_per_seq = max(seq_len // tm, 1)
    idx = np.arange(tm)
    tri = ((idx[:, None] <= idx[None, :]) & (idx[:, None] // seq_len == idx[None, :] // seq_len))
    tri = jnp.asarray(tri.astype(np.float32), dtype=BF16)
    const = lambda a: pl.BlockSpec(a.shape, lambda i: (0, 0))
    row = lambda w: pl.BlockSpec((tm, w), lambda i: (i, 0))
    col = pl.BlockSpec((FOX_HEADS, tm), lambda i: (0, i))
    sds = jax.ShapeDtypeStruct
    by_seq = lambda i: (i // tiles_per_seq, 0, i % tiles_per_seq)
    if time_minor:
        kv_shape = sds((batch, FOX_HEADS, FOX_HEAD_DIM, seq_len), F32)
        kv_spec = pl.BlockSpec((None, FOX_HEADS, FOX_HEAD_DIM, tm),
                               lambda i: (i // tiles_per_seq, 0, 0, i % tiles_per_seq))
        fox = [(sds((batch, FOX_WIDTH, seq_len), BF16), pl.BlockSpec((None, FOX_WIDTH, tm), by_seq)),
               (kv_shape, kv_spec), (kv_shape, kv_spec),
               (sds((n, FOX_WIDTH), BF16), row(FOX_WIDTH))]
    else:
        fox = [(sds((n, FOX_WIDTH), BF16), row(FOX_WIDTH)),
               (sds((n, FOX_WIDTH), F32), row(FOX_WIDTH)),
               (sds((n, FOX_WIDTH), F32), row(FOX_WIDTH))]
    outs = fox + [
        (sds((FOX_HEADS, n), F32), col),
        (sds((FOX_HEADS, n), F32), col),
        (sds((n, GLA_KW), F32), row(GLA_KW)),
        (sds((n, GLA_KW), F32), row(GLA_KW)),
        (sds((n, GLA_VW), BF16), row(GLA_VW)),
        (sds((n, GLA_KW), F32), row(GLA_KW)),
        (sds((n, GLA_VW), F32), row(GLA_VW)),
    ]
    names = ('g1', 'wq', 'wkv', 'wb', 'wc', 'wfl', 'wg2', 'bg2', 'bfc')
    return pl.pallas_call(
        functools.partial(_proj_kernel, tiles_per_seq=tiles_per_seq, time_minor=time_minor),
        grid=(n // tm,),
        in_specs=[row(D_MODEL)] + [const(prm[k]) for k in names] + [const(tri)],
        out_specs=tuple(s for _, s in outs), out_shape=tuple(s for s, _ in outs),
        scratch_shapes=[pltpu.VMEM((FOX_HEADS, LANES), F32)],
        compiler_params=pltpu.CompilerParams(dimension_semantics=("arbitrary",),
                                             vmem_limit_bytes=VMEM_LIMIT),
        name="proj",
    )(x2d, *[prm[k] for k in names], tri)


def _fox_prompt_kernel(q_ref, k_ref, v_ref, c_ref, o_ref, kb_ref, vb_ref,
                       sa_ref, sb_ref, pa_ref, pb_ref, acc_ref, *, tk):
    p = pl.program_id(1)
    qi = pl.program_id(2)
    nk = vb_ref.shape[1]
    hd = FOX_HEAD_DIM
    spare = (hd, 0)

    @pl.when(qi == 0)
    def _():
        parts = jnp.concatenate(_split3_f32(c_ref[...] * (-LOG2E)), axis=0).astype(BF16)
        r = lax.broadcasted_iota(jnp.int32, (3 * FOX_HEADS, LANES), 0)
        ln = lax.broadcasted_iota(jnp.int32, (3 * FOX_HEADS, LANES), 1)
        k = k_ref[...]
        klane = lax.broadcasted_iota(jnp.int32, k.shape, 1)
        vrow = lax.broadcasted_iota(jnp.int32, (LANES, tk), 0)
        for hh in range(2):
            place = (r % FOX_HEADS == 2 * p + hh) & (ln == spare[hh] + r // FOX_HEADS)
            extra = _dot_tn(parts, jnp.where(place, 1.0, 0.0).astype(BF16))
            own = (klane < hd) if hh == 0 else (klane >= hd)
            kb_ref[hh] = jnp.where(own, k, extra.astype(BF16))
            for jj in range(nk):
                vt = v_ref[:, :, jj * tk:(jj + 1) * tk].reshape(LANES, tk)
                vb_ref[hh, jj] = jnp.where(vrow == spare[hh], 1.0, vt).astype(BF16)

    qrow = lax.broadcasted_iota(jnp.int32, (LANES, tk), 0)
    key = lax.broadcasted_iota(jnp.int32, (tk, tk), 0)
    qry = lax.broadcasted_iota(jnp.int32, (tk, tk), 1)
    causal = key <= qry

    chains = [(hh, part) for hh in range(2) for part in range(2)]
    qs = []
    for hh, part in chains:
        q = q_ref[:, part * tk:(part + 1) * tk]
        own = (qrow < hd) if hh == 0 else (qrow >= hd)
        ones = (qrow >= spare[hh]) & (qrow < spare[hh] + 3)
        qs.append(jnp.where(own, q, jnp.where(ones, 1.0, 0.0).astype(BF16)))

    everyone = list(range(len(chains)))
    second = [n for n in everyone if chains[n][1] == 1]

    def scores(j, s_ref, live):
        k0 = pl.multiple_of(j * tk, tk)
        for n in live:
            s_ref[n] = _dot(kb_ref[chains[n][0], pl.ds(k0, tk), :], qs[n])

    def values(j, p_ref, live):
        return [_dot(vb_ref[chains[n][0], j], p_ref[n]) if n in live else None for n in everyone]

    def softmax(s_ref, p_ref, ms, live, masked_part):
        ms, alphas = list(ms), [None] * len(chains)
        for n in live:
            s = s_ref[n]
            if chains[n][1] == masked_part:
                s = jnp.where(causal, s, -jnp.inf)
            m_new = jnp.maximum(ms[n], jnp.max(s, axis=0, keepdims=True))
            alphas[n] = jnp.exp2(ms[n] - m_new)
            p_ref[n] = jnp.exp2(s - m_new).astype(BF16)
            ms[n] = m_new
        return ms, alphas

    def rescale(alphas, pv, live):
        for n in live:
            acc_ref[n] = alphas[n] * acc_ref[n] + pv[n]

    def half(j, s_cur, s_nxt, p_cur, p_prv, ms, alphas):
        scores(j + 1, s_nxt, everyone)
        pv = values(jnp.maximum(j - 1, 0), p_prv, everyone)
        ms, alphas_new = softmax(s_cur, p_cur, ms, everyone, None)
        rescale(alphas, pv, everyone)
        return ms, alphas_new

    def body(jj, carry):
        ms, alphas = half(2 * jj, sa_ref, sb_ref, pa_ref, pb_ref, *carry)
        return half(2 * jj + 1, sb_ref, sa_ref, pb_ref, pa_ref, ms, alphas)

    pb_ref[...] = jnp.zeros(pb_ref.shape, BF16)
    acc_ref[...] = jnp.zeros(acc_ref.shape, F32)
    scores(0, sa_ref, everyone)
    init = ([jnp.full((1, tk), NEG_BIG, F32) for _ in chains], [jnp.ones((1, tk), F32) for _ in chains])
    ms, alphas = lax.fori_loop(0, qi, body, init)
    j0 = 2 * qi
    scores(j0 + 1, sb_ref, second)
    pv = values(jnp.maximum(j0 - 1, 0), pb_ref, everyone)
    ms, alphas_a = softmax(sa_ref, pa_ref, ms, everyone, 0)
    rescale(alphas, pv, everyone)
    pv = values(j0, pa_ref, everyone)
    ms, alphas_b = softmax(sb_ref, pb_ref, ms, second, 1)
    rescale(alphas_a, pv, everyone)
    rescale(alphas_b, values(j0 + 1, pb_ref, second), second)

    for part in range(2):
        a0 = acc_ref[part]
        a1 = acc_ref[2 + part]
        o_ref[:, part * tk:(part + 1) * tk] = jnp.concatenate(
            [a0[:hd] / a0[hd:hd + 1], a1[hd:] / a1[0:1]], axis=0).astype(BF16)


def _fox_prompt(qt, ktok, vt, ct, batch, seq, tk=256):
    tq = 2 * tk
    nq = seq // tq
    nk = seq // tk
    pairs = FOX_WIDTH // LANES
    return pl.pallas_call(
        functools.partial(_fox_prompt_kernel, tk=tk),
        grid=(batch, pairs, nq),
        in_specs=[pl.BlockSpec((None, LANES, tq), lambda b, p, i: (b, p, i)),
                  pl.BlockSpec((seq, LANES), lambda b, p, i: (b, p)),
                  pl.BlockSpec((None, 2, FOX_HEAD_DIM, seq), lambda b, p, i: (b, p, 0, 0)),
                  pl.BlockSpec((FOX_HEADS, seq), lambda b, p, i: (0, b))],
        out_specs=pl.BlockSpec((None, LANES, tq), lambda b, p, i: (b, p, i)),
        out_shape=jax.ShapeDtypeStruct((batch, FOX_WIDTH, seq), BF16),
        scratch_shapes=[pltpu.VMEM((2, seq, LANES), BF16), pltpu.VMEM((2, nk, LANES, tk), BF16),
                        pltpu.VMEM((4, tk, tk), F32), pltpu.VMEM((4, tk, tk), F32),
                        pltpu.VMEM((4, tk, tk), BF16), pltpu.VMEM((4, tk, tk), BF16),
                        pltpu.VMEM((4, LANES, tk), F32)],
        compiler_params=pltpu.CompilerParams(
            dimension_semantics=("arbitrary", "arbitrary", "arbitrary"),
            vmem_limit_bytes=VMEM_LIMIT),
        name="fox_prompt",
    )(qt, ktok, vt, ct)


def _fox_sample_kernel(q_ref, kn_ref, vn_ref, cn_ref, lft_ref, mlow_ref, ck_ref, cv_ref,
                       o_ref, qh_ref, suf_ref, m_ref, l_ref, acc_ref, *, tp, nt, tn):
    j = pl.program_id(1)
    blk = 2 * LANES
    hd = FOX_HEAD_DIM

    @pl.when(j == 0)
    def _init():
        for h in range(FOX_HEADS):
            qh_ref[h] = q_ref[:, h * hd:(h + 1) * hd]
        m_ref[...] = jnp.full(m_ref.shape, NEG_BIG, F32)
        l_ref[...] = jnp.zeros(l_ref.shape, F32)
        acc_ref[...] = jnp.zeros(acc_ref.shape, F32)
        carry = jnp.zeros((FOX_HEADS, 1), F32)
        per_tile = tp // blk
        for b in reversed(range(nt * per_tile)):
            x = lft_ref[:, b * blk:(b + 1) * blk]
            parts = jnp.concatenate(_split3_f32(x), axis=0).astype(BF16)
            y = _dot(parts, mlow_ref[...])
            off = (b % per_tile) * blk
            suf_ref[b // per_tile, :, off:off + blk] = y[0:8] + y[8:16] + y[16:24] + carry
            carry = carry + jnp.sum(x, axis=1, keepdims=True)

    def update(s, pv):
        m, l, acc = _softmax_step(s, (m_ref[...], l_ref[...], acc_ref[...]), pv)
        m_ref[...] = m
        l_ref[...] = l
        acc_ref[...] = acc

    bmm = lambda a, b, ca, cb: lax.dot_general(a, b, (((ca,), (cb,)), ((0,), (0,))),
                                               preferred_element_type=F32)
    qh = qh_ref[...]

    kt = ck_ref[...].astype(BF16)
    vt = cv_ref[...].astype(BF16)
    s = bmm(qh, kt, 2, 1).reshape(FOX_HEADS * tn, tp) + jnp.repeat(suf_ref[j], tn, axis=0)
    update(s.reshape(FOX_HEADS, tn, tp), lambda pm: bmm(pm, vt, 2, 2))

    @pl.when(j == nt - 1)
    def _fin():
        per_head = lambda ref: jnp.stack([ref[:, h * hd:(h + 1) * hd] for h in range(FOX_HEADS)],
                                         axis=0).astype(BF16)
        kn = per_head(kn_ref)
        vn = per_head(vn_ref)
        r = lax.broadcasted_iota(jnp.int32, (FOX_HEADS, tn, tn), 1)
        c = lax.broadcasted_iota(jnp.int32, (FOX_HEADS, tn, tn), 2)
        s = jnp.where(c <= r, bmm(qh, kn, 2, 2) - cn_ref[...][:, None, :], -jnp.inf)
        update(s, lambda pm: bmm(pm, vn, 2, 1))
        o = acc_ref[...] / l_ref[...]
        o_ref[...] = jnp.concatenate([o[h] for h in range(FOX_HEADS)], axis=1).astype(BF16)


def _fox_sample(q, kn, vn, cn, lft, ck, cv, batch, tn, past, tp=2048):
    nt = past // tp
    blk = 2 * LANES
    idx = np.arange(blk)
    mlow = jnp.asarray((idx[:, None] > idx[None, :]).astype(np.float32), dtype=BF16)
    per_b = lambda w: pl.BlockSpec((tn, w), lambda b, j: (b, 0))
    cache = pl.BlockSpec((None, FOX_HEADS, FOX_HEAD_DIM, tp), lambda b, j: (b, 0, 0, j))
    return pl.pallas_call(
        functools.partial(_fox_sample_kernel, tp=tp, nt=nt, tn=tn),
        grid=(batch, nt),
        in_specs=[per_b(FOX_WIDTH), per_b(FOX_WIDTH), per_b(FOX_WIDTH),
                  pl.BlockSpec((None, FOX_HEADS, tn), lambda b, j: (b, 0, 0)),
                  pl.BlockSpec((None, FOX_HEADS, past), lambda b, j: (b, 0, 0)),
                  pl.BlockSpec((blk, blk), lambda b, j: (0, 0)),
                  cache, cache],
        out_specs=per_b(FOX_WIDTH),
        out_shape=jax.ShapeDtypeStruct((batch * tn, FOX_WIDTH), BF16),
        scratch_shapes=[pltpu.VMEM((FOX_HEADS, tn, FOX_HEAD_DIM), BF16),
                        pltpu.VMEM((nt, FOX_HEADS, tp), F32),
                        pltpu.VMEM((FOX_HEADS, tn, 1), F32), pltpu.VMEM((FOX_HEADS, tn, 1), F32),
                        pltpu.VMEM((FOX_HEADS, tn, FOX_HEAD_DIM), F32)],
        compiler_params=pltpu.CompilerParams(dimension_semantics=("arbitrary", "arbitrary"),
                                             vmem_limit_bytes=VMEM_LIMIT),
        name="fox_sample",
    )(q, kn, vn, cn, lft, mlow, ck, cv)


def _gla_kernel(q_ref, k_ref, v_ref, g_ref, r_ref, s0_ref, gn_ref, w_ref, lv_ref,
                o_ref, s_ref, st_ref, *, tb, levels):
    t = pl.program_id(1)
    half = LANES // 2

    @pl.when(t == 0)
    def _():
        st_ref[...] = s0_ref[...].reshape(GLA_KW, GLA_DV).T

    r = r_ref[...]
    gate = gn_ref[...] * (r * _sigmoid(r))

    g3 = jnp.concatenate(_split3_f32(g_ref[...] * LOG2E), axis=0).astype(BF16)
    rows = w_ref.shape[0] // 2
    dec = jnp.exp2(jnp.concatenate([_dot(w_ref[0:rows, :], g3), _dot(w_ref[rows:, :], g3)], axis=0))
    from_start = dec[0:tb]
    to_end = dec[tb:2 * tb]

    q = q_ref[...] * (GLA_DK ** -0.5)
    k = k_ref[...]
    v = v_ref[...]
    row = lax.broadcasted_iota(jnp.int32, (tb, GLA_KW), 0)
    low = lax.broadcasted_iota(jnp.int32, (tb, LANES), 1) < half
    lv = lv_ref[...]

    def pair_scores(xq, xk):
        outs = []
        for p in range(GLA_HEADS // 2):
            a = xq[:, p * LANES:(p + 1) * LANES]
            zero = jnp.zeros_like(a)
            lhs = jnp.concatenate([jnp.where(low, a, zero), jnp.where(low, zero, a)], axis=0)
            outs.append(_dot_nt(lhs, xk[:, p * LANES:(p + 1) * LANES]))
        return outs

    here = lv == -1
    a = [jnp.where(here, r_, 0.0) for r_ in pair_scores(q.astype(BF16), k.astype(BF16))]
    for l in range(levels):
        upper = ((row >> l) & 1) == 1
        x = (jnp.where(upper, q, k) * dec[(l + 2) * tb:(l + 3) * tb]).astype(BF16)
        here = lv == l
        a = [jnp.where(here, r_, a_) for r_, a_ in zip(pair_scores(x, x), a)]
    o = jnp.concatenate(
        [_dot(a[h // 2][(h % 2) * tb:(h % 2 + 1) * tb].astype(BF16), v[:, h * GLA_DV:(h + 1) * GLA_DV])
         for h in range(GLA_HEADS)], axis=1)

    lane = lax.broadcasted_iota(jnp.int32, (tb, GLA_KW), 1)
    head_sel = [(lane >= h * GLA_DK) & (lane < (h + 1) * GLA_DK) for h in range(GLA_HEADS)]
    st = st_ref[...]
    qt = (q * from_start).astype(BF16)
    kt = (k * to_end).astype(BF16)
    zero = jnp.zeros_like(qt)
    q4 = jnp.concatenate([jnp.where(sel, qt, zero) for sel in head_sel], axis=0)
    oi = _dot_nt(q4, st.astype(BF16))
    o = o + jnp.concatenate([oi[h * tb:(h + 1) * tb] for h in range(GLA_HEADS)], axis=1)
    upd = None
    for h in range(GLA_HEADS):
        u = _dot_tn(v[:, h * GLA_DV:(h + 1) * GLA_DV], jnp.where(head_sel[h], kt, zero))
        upd = u if upd is None else upd + u
    st = from_start[tb - 1:tb, :] * st + upd
    st_ref[...] = st

    @pl.when(t == pl.num_programs(1) - 1)
    def _():
        s_ref[...] = st.T.reshape(s_ref.shape)

    outs = []
    for h in range(GLA_HEADS):
        oh = o[:, h * GLA_DV:(h + 1) * GLA_DV]
        outs.append(oh * lax.rsqrt(jnp.mean(oh * oh, axis=-1, keepdims=True) + EPS))
    o_ref[...] = (jnp.concatenate(outs, axis=1) * gate).astype(BF16)


def _gla(gq, gk, gv, glog, gr, s0, gn, batch, seq, tb=128):
    tb = min(tb, seq)
    nt = seq // tb
    n = batch * seq
    levels = tb.bit_length() - 1
    assert tb == 1 << levels
    ti = np.arange(tb)[:, None]
    si = np.arange(tb)[None, :]
    blocks = [si <= ti, si > ti]
    for l in range(levels):
        m = 1 << l
        mid = ti - ti % (2 * m) + m - 1
        upper = ti % (2 * m) >= m
        blocks.append(np.where(upper, (si > mid) & (si <= ti), (si > ti) & (si <= mid)))
    w = np.concatenate(blocks, axis=0).astype(np.float32)
    w = jnp.asarray(np.concatenate([w, w, w], axis=1), dtype=BF16)
    x = ti ^ si
    lv = np.where(si < ti, np.floor(np.log2(np.maximum(x, 1))).astype(np.int32),
                  np.where(si == ti, -1, -2)).astype(np.int32)
    lv = jnp.asarray(np.concatenate([lv, lv], axis=0))
    row = lambda w_: pl.BlockSpec((tb, w_), lambda b, t: (b * nt + t, 0))
    state = pl.BlockSpec((None, GLA_HEADS, GLA_DK, GLA_DV), lambda b, t: (b, 0, 0, 0))
    const = lambda shape: pl.BlockSpec(shape, lambda b, t: (0, 0))
    return pl.pallas_call(
        functools.partial(_gla_kernel, tb=tb, levels=levels),
        grid=(batch, nt),
        in_specs=[row(GLA_KW), row(GLA_KW), row(GLA_VW), row(GLA_KW), row(GLA_VW), state,
                  const((1, GLA_VW)), const(w.shape), const(lv.shape)],
        out_specs=(row(GLA_VW), state),
        out_shape=(jax.ShapeDtypeStruct((n, GLA_VW), BF16),
                   jax.ShapeDtypeStruct((batch, GLA_HEADS, GLA_DK, GLA_DV), F32)),
        scratch_shapes=[pltpu.VMEM((GLA_DV, GLA_KW), F32)],
        compiler_params=pltpu.CompilerParams(dimension_semantics=("arbitrary", "arbitrary"),
                                             vmem_limit_bytes=VMEM_LIMIT),
        name="gla",
    )(gq, gk, gv, glog, gr, s0, gn, w, lv)


def _ffn_kernel(x_ref, fo_ref, go_ref, wo_ref, g2_ref, wg_ref, wu_ref, wd_ref, gf_ref,
                y_ref, a_ref, *, chunk, fox_time_minor):
    fox_dot = _dot_tn if fox_time_minor else _dot
    y1 = (x_ref[...] + fox_dot(fo_ref[...], wo_ref[0:FOX_WIDTH, :])
          + _dot(go_ref[...], wo_ref[FOX_WIDTH:, :]))
    h2 = _rms(y1, g2_ref[...]).astype(BF16)
    for c in range(D_FF // chunk):
        cs = slice(c * chunk, (c + 1) * chunk)
        u = _dot(h2, wg_ref[:, cs])
        w = _dot(h2, wu_ref[:, cs])
        a_ref[:, cs] = (u * _sigmoid(u) * w).astype(BF16)
    y2 = y1 + _dot(a_ref[...], wd_ref[...])
    y_ref[...] = _rms(y2, gf_ref[...])


def _ffn(x2d, fo, go, prm, tm=512, chunk=256):
    n = x2d.shape[0]
    tm = min(tm, n)
    row = lambda w: pl.BlockSpec((tm, w), lambda i: (i, 0))
    const = lambda shape: pl.BlockSpec(shape, lambda i: (0, 0), pipeline_mode=pl.Buffered(1))
    fox_time_minor = fo.ndim == 3
    if fox_time_minor:
        tiles_per_seq = fo.shape[2] // tm
        fo_spec = pl.BlockSpec((None, FOX_WIDTH, tm),
                               lambda i: (i // tiles_per_seq, 0, i % tiles_per_seq))
    else:
        fo_spec = row(FOX_WIDTH)
    return pl.pallas_call(
        functools.partial(_ffn_kernel, chunk=chunk, fox_time_minor=fox_time_minor),
        grid=(n // tm,),
        in_specs=[row(D_MODEL), fo_spec, row(GLA_VW), const((D_MODEL, D_MODEL)),
                  const((1, D_MODEL)), const((D_MODEL, D_FF)), const((D_MODEL, D_FF)),
                  const((D_FF, D_MODEL)), const((1, D_MODEL))],
        out_specs=row(D_MODEL),
        out_shape=jax.ShapeDtypeStruct((n, D_MODEL), F32),
        scratch_shapes=[pltpu.VMEM((tm, D_FF), BF16)],
        compiler_params=pltpu.CompilerParams(dimension_semantics=("arbitrary",),
                                             vmem_limit_bytes=VMEM_LIMIT),
        name="ffn",
    )(x2d, fo, go, prm['wo'], prm['g2'], prm['wg'], prm['wu'], prm['wd'], prm['gf'])


def _layer_params(layer, norm1_g, w_in, w_gate2, b_gate2, b_forget, gla_norm_g, w_out,
                  norm2_g, w_gate, w_up, w_down, final_norm_g):
    wt = jnp.transpose(w_in[layer])
    o_fl = 3 * FOX_WIDTH
    o_gq = o_fl + FOX_HEADS
    o_gg = o_gq + 2 * GLA_KW + GLA_VW
    o_gr = o_gg + GLA_GATE_RANK
    wc = jnp.zeros((LANES, D_MODEL), F32).at[:GLA_GATE_RANK].set(wt[o_gg:o_gr])
    wg2 = jnp.zeros((LANES, GLA_KW), F32).at[:GLA_GATE_RANK].set(w_gate2[layer])
    return dict(
        g1=norm1_g[layer].reshape(1, D_MODEL),
        wq=wt[:FOX_WIDTH].astype(BF16),
        wkv=wt[FOX_WIDTH:o_fl].astype(BF16),
        wb=jnp.concatenate([wt[o_gq:o_gg], wt[o_gr:]], axis=0).astype(BF16),
        wc=wc.astype(BF16),
        wfl=wt[o_fl:o_gq].astype(BF16),
        wg2=wg2.astype(BF16),
        bg2=b_gate2[layer].reshape(1, GLA_KW),
        bfc=b_forget[layer].reshape(FOX_HEADS, 1),
        gn=gla_norm_g[layer].reshape(1, GLA_VW),
        wo=w_out[layer].astype(BF16),
        g2=norm2_g[layer].reshape(1, D_MODEL),
        wg=w_gate[layer].astype(BF16),
        wu=w_up[layer].astype(BF16),
        wd=w_down[layer].astype(BF16),
        gf=final_norm_g.reshape(1, D_MODEL),
    )


def kernel(x_prompt, x_sample, cache_fox_k, cache_fox_v, cache_fox_logf, state_gla, norm1_g, w_in,
           w_gate2, b_gate2, b_forget, gla_norm_g, w_out, norm2_g, w_gate, w_up, w_down, final_norm_g):
    depth = w_in.shape[0]
    assert depth == 1, "the final rmsnorm is fused into the layer's ffn kernel"
    bp, tp_, _ = x_prompt.shape
    bs, ts, _ = x_sample.shape
    past = cache_fox_k.shape[2]
    layer = 0
    prm = _layer_params(layer, norm1_g, w_in, w_gate2, b_gate2, b_forget, gla_norm_g, w_out,
                        norm2_g, w_gate, w_up, w_down, final_norm_g)
    by_time = lambda a, b, t: a.reshape(FOX_HEADS, b, t).transpose(1, 2, 0)[None]

    xp = x_prompt.reshape(bp * tp_, D_MODEL)
    qt, kt_p, vt_p, ktok, lf_p, ct, gq, gk, gv, glog, gr = _proj(xp, bp, tp_, prm, True)
    fox_o = _fox_prompt(qt, ktok, vt_p, ct, bp, tp_)
    s0 = jnp.zeros((bp, GLA_HEADS, GLA_DK, GLA_DV), F32)
    gla_o, s_p = _gla(gq, gk, gv, glog, gr, s0, prm['gn'], bp, tp_)
    y_p = _ffn(xp, fox_o, gla_o, prm)

    xs = x_sample.reshape(bs * ts, D_MODEL)
    q, k_s, v_s, lf_s, ct, gq, gk, gv, glog, gr = _proj(xs, bs, ts, prm, False)
    cn = ct.reshape(FOX_HEADS, bs, ts).transpose(1, 0, 2)
    lft = cache_fox_logf[layer].astype(F32).transpose(0, 2, 1)
    ck = cache_fox_k[layer].transpose(0, 2, 3, 1)
    cv = cache_fox_v[layer].transpose(0, 2, 3, 1)
    fox_o = _fox_sample(q, k_s, v_s, cn, lft, ck, cv, bs, ts, past)
    gla_o, s_s = _gla(gq, gk, gv, glog, gr, state_gla[layer].astype(F32), prm['gn'], bs, ts)
    y_s = _ffn(xs, fox_o, gla_o, prm)

    heads = lambda a, b, t: a.reshape(1, b, t, FOX_HEADS, FOX_HEAD_DIM)
    return (y_p.reshape(bp, tp_, D_MODEL), y_s.reshape(bs, ts, D_MODEL),
            kt_p.transpose(0, 3, 1, 2)[None], vt_p.transpose(0, 3, 1, 2)[None],
            by_time(lf_p, bp, tp_), s_p[None],
            heads(k_s, bs, ts), heads(v_s, bs, ts), by_time(lf_s, bs, ts), s_s[None])
```

```python
import functools

import numpy as np
import jax
import jax.numpy as jnp
from jax import lax
from jax.experimental import pallas as pl
from jax.experimental.pallas import tpu as pltpu

D_MODEL = 1024
FOX_HEADS = 8
FOX_HEAD_DIM = 64
FOX_WIDTH = FOX_HEADS * FOX_HEAD_DIM
GLA_HEADS = 4
GLA_DK = 64
GLA_DV = 128
GLA_KW = GLA_HEADS * GLA_DK
GLA_VW = GLA_HEADS * GLA_DV
GLA_GATE_RANK = 16
GLA_GATE_TEMP = 16.0
D_FF = 2816
EPS = 1e-6

LANES = 128
LOG2E = 1.4426950408889634
VMEM_LIMIT = 56 * 1024 * 1024

F32 = jnp.float32
BF16 = jnp.bfloat16
NEG_BIG = -1e30


def _log_sigmoid(x):
    return jnp.minimum(x, 0.0) - jnp.log1p(jnp.exp(-jnp.abs(x)))


def _sigmoid(x):
    return 1.0 / (1.0 + jnp.exp(-x))


def _split3_f32(x):
    hi = x.astype(BF16).astype(F32)
    r = x - hi
    mid = r.astype(BF16).astype(F32)
    lo = (r - mid).astype(BF16).astype(F32)
    return hi, mid, lo


def _dot(a, b):
    return jnp.dot(a, b, preferred_element_type=F32)


def _dot_nt(a, b):
    return lax.dot_general(a, b, (((1,), (1,)), ((), ())), preferred_element_type=F32)


def _dot_tn(a, b):
    return lax.dot_general(a, b, (((0,), (0,)), ((), ())), preferred_element_type=F32)


def _rms(x, g):
    return x * lax.rsqrt(jnp.mean(x * x, axis=-1, keepdims=True) + EPS) * g


def _softmax_step(s, carry, pv):
    m, l, acc = carry
    m_new = jnp.maximum(m, jnp.max(s, axis=-1, keepdims=True))
    alpha = jnp.exp(m - m_new)
    pm = jnp.exp(s - m_new)
    l = alpha * l + jnp.sum(pm, axis=-1, keepdims=True)
    acc = alpha * acc + pv(pm.astype(BF16))
    return m_new, l, acc


def _proj_kernel(x_ref, g1_ref, wq_ref, wkv_ref, wb_ref, wc_ref, wfl_ref, wg2_ref, bg2_ref,
                 bfc_ref, tri_ref,
                 *rest, tiles_per_seq, time_minor):
    carry_ref = rest[-1]

    @pl.when(pl.program_id(0) % tiles_per_seq == 0)
    def _():
        carry_ref[...] = jnp.zeros_like(carry_ref)

    h = _rms(x_ref[...], g1_ref[...]).astype(BF16)
    tm = h.shape[0]
    scale = FOX_HEAD_DIM ** -0.5

    zc = _dot_nt(h, wc_ref[...])
    fl_t = _dot_nt(wfl_ref[...], h)

    if time_minor:
        (q_ref, k_ref, v_ref, ktok_ref, lft_ref, ct_ref, gq_ref, gk_ref, gv_ref, glog_ref, gr_ref,
         carry_ref) = rest
        q_ref[...] = (_dot_nt(wq_ref[...], h) * (scale * LOG2E)).astype(BF16)
    else:
        (q_ref, k_ref, v_ref, lft_ref, ct_ref, gq_ref, gk_ref, gv_ref, glog_ref, gr_ref,
         carry_ref) = rest
        q_ref[...] = (_dot_nt(h, wq_ref[...]) * scale).astype(BF16)

    logf_t = _log_sigmoid(fl_t + bfc_ref[...])
    lft_ref[...] = logf_t
    parts = jnp.concatenate(_split3_f32(logf_t), axis=0).astype(BF16)
    gg = zc.astype(BF16)

    if time_minor:
        kvt = _dot_nt(wkv_ref[...], h)
        k_ref[...] = kvt[:FOX_WIDTH].reshape(FOX_HEADS, FOX_HEAD_DIM, tm)
        v_ref[...] = kvt[FOX_WIDTH:].reshape(FOX_HEADS, FOX_HEAD_DIM, tm)
        ktok_ref[...] = _dot_nt(h, wkv_ref[0:FOX_WIDTH, :]).astype(BF16)
    else:
        kv = _dot_nt(h, wkv_ref[...])
        k_ref[...] = kv[:, :FOX_WIDTH]
        v_ref[...] = kv[:, FOX_WIDTH:]

    gpre = _dot(gg, wg2_ref[...]) + bg2_ref[...]
    cs = _dot(parts, tri_ref[...])

    zb = _dot_nt(h, wb_ref[...])
    gq_ref[...] = zb[:, :GLA_KW]
    gk_ref[...] = zb[:, GLA_KW:2 * GLA_KW]
    gv_ref[...] = zb[:, 2 * GLA_KW:2 * GLA_KW + GLA_VW].astype(BF16)
    gr_ref[...] = zb[:, 2 * GLA_KW + GLA_VW:]

    glog_ref[...] = _log_sigmoid(gpre) * (1.0 / GLA_GATE_TEMP)

    ct = cs[0:8] + cs[8:16] + cs[16:24] + carry_ref[:, 0:1]
    ct_ref[...] = ct
    carry_ref[...] = jnp.broadcast_to(ct[:, tm - 1:], carry_ref.shape)


def _proj(x2d, batch, seq_len, prm, time_minor, tm=512):
    n = x2d.shape[0]
    tm = min(tm, n)
    tiles_per_seq = max(seq_len // tm, 1)
    idx = np.arange(tm)
    tri = ((idx[:, None] <= idx[None, :]) & (idx[:, None] // seq_len == idx[None, :] // seq_len))
    tri = jnp.asarray(tri.astype(np.float32), dtype=BF16)
    const = lambda a: pl.BlockSpec(a.shape, lambda i: (0, 0))
    row = lambda w: pl.BlockSpec((tm, w), lambda i: (i, 0))
    col = pl.BlockSpec((FOX_HEADS, tm), lambda i: (0, i))
    sds = jax.ShapeDtypeStruct
    if time_minor:
        kv_shape = sds((batch, FOX_HEADS, FOX_HEAD_DIM, seq_len), F32)
        kv_spec = pl.BlockSpec((None, FOX_HEADS, FOX_HEAD_DIM, tm),
                               lambda i: (i // tiles_per_seq, 0, 0, i % tiles_per_seq))
        fox = [(sds((batch, tiles_per_seq, FOX_WIDTH, tm), BF16),
                pl.BlockSpec((None, None, FOX_WIDTH, tm), lambda i: (i // tiles_per_seq, i % tiles_per_seq, 0, 0))),
               (kv_shape, kv_spec), (kv_shape, kv_spec),
               (sds((n, FOX_WIDTH), BF16), row(FOX_WIDTH))]
    else:
        fox = [(sds((n, FOX_WIDTH), BF16), row(FOX_WIDTH)),
               (sds((n, FOX_WIDTH), F32), row(FOX_WIDTH)),
               (sds((n, FOX_WIDTH), F32), row(FOX_WIDTH))]
    outs = fox + [
        (sds((FOX_HEADS, n), F32), col),
        (sds((FOX_HEADS, n), F32), col),
        (sds((n, GLA_KW), F32), row(GLA_KW)),
        (sds((n, GLA_KW), F32), row(GLA_KW)),
        (sds((n, GLA_VW), BF16), row(GLA_VW)),
        (sds((n, GLA_KW), F32), row(GLA_KW)),
        (sds((n, GLA_VW), F32), row(GLA_VW)),
    ]
    names = ('g1', 'wq', 'wkv', 'wb', 'wc', 'wfl', 'wg2', 'bg2', 'bfc')
    return pl.pallas_call(
        functools.partial(_proj_kernel, tiles_per_seq=tiles_per_seq, time_minor=time_minor),
        grid=(n // tm,),
        in_specs=[row(D_MODEL)] + [const(prm[k]) for k in names] + [const(tri)],
        out_specs=tuple(s for _, s in outs), out_shape=tuple(s for s, _ in outs),
        scratch_shapes=[pltpu.VMEM((FOX_HEADS, LANES), F32)],
        compiler_params=pltpu.CompilerParams(dimension_semantics=("arbitrary",),
                                             vmem_limit_bytes=VMEM_LIMIT),
        name="proj",
    )(x2d, *[prm[k] for k in names], tri)


def _fox_prompt_kernel(q_ref, k_ref, v_ref, c_ref, o_ref, kb_ref, vb_ref, s_ref, p_ref, acc_ref,
                       *, tk):
    p = pl.program_id(1)
    nq = q_ref.shape[0]
    nk = vb_ref.shape[1]
    hd = FOX_HEAD_DIM
    spare = (hd, 0)

    parts = jnp.concatenate(_split3_f32(c_ref[...] * (-LOG2E)), axis=0).astype(BF16)
    r = lax.broadcasted_iota(jnp.int32, (3 * FOX_HEADS, LANES), 0)
    ln = lax.broadcasted_iota(jnp.int32, (3 * FOX_HEADS, LANES), 1)
    k = k_ref[...]
    klane = lax.broadcasted_iota(jnp.int32, k.shape, 1)
    vrow = lax.broadcasted_iota(jnp.int32, (LANES, tk), 0)
    for hh in range(2):
        place = (r % FOX_HEADS == 2 * p + hh) & (ln == spare[hh] + r // FOX_HEADS)
        extra = _dot_tn(parts, jnp.where(place, 1.0, 0.0).astype(BF16))
        own = (klane < hd) if hh == 0 else (klane >= hd)
        kb_ref[hh] = jnp.where(own, k, extra.astype(BF16))
        for jj in range(nk):
            vt = v_ref[:, :, jj * tk:(jj + 1) * tk].reshape(LANES, tk)
            vb_ref[hh, jj] = jnp.where(vrow == spare[hh], 1.0, vt).astype(BF16)

    qrow = lax.broadcasted_iota(jnp.int32, (LANES, tk), 0)
    key = lax.broadcasted_iota(jnp.int32, (tk, tk), 0)
    qry = lax.broadcasted_iota(jnp.int32, (tk, tk), 1)
    causal = key <= qry

    chains = [(hh, part) for hh in range(2) for part in range(2)]
    everyone = list(range(len(chains)))
    second = [n for n in everyone if chains[n][1] == 1]

    def q_operands(qi):
        qs = []
        for hh, part in chains:
            q = q_ref[qi, :, part * tk:(part + 1) * tk]
            own = (qrow < hd) if hh == 0 else (qrow >= hd)
            ones = (qrow >= spare[hh]) & (qrow < spare[hh] + 3)
            qs.append(jnp.where(own, q, jnp.where(ones, 1.0, 0.0).astype(BF16)))
        return qs

    steps = []
    for qi in range(nq):
        steps += [(qi, j, everyone, None) for j in range(2 * qi)]
        steps += [(qi, 2 * qi, everyone, 0), (qi, 2 * qi + 1, second, 1)]

    qs_of = {}

    def scores(step, buf):
        qi, j, live, _ = step
        if qi not in qs_of:
            qs_of.clear()
            qs_of[qi] = q_operands(qi)
        for n in live:
            s_ref[buf, n] = _dot(kb_ref[chains[n][0], j * tk:(j + 1) * tk, :], qs_of[qi][n])

    def values(step, buf):
        _, j, live, _ = step
        return {n: _dot(vb_ref[chains[n][0], j], p_ref[buf, n]) for n in live}

    def finish(qi):
        for part in range(2):
            a0 = acc_ref[qi % 2, part]
            a1 = acc_ref[qi % 2, 2 + part]
            o_ref[qi, :, part * tk:(part + 1) * tk] = jnp.concatenate(
                [a0[:hd] / a0[hd:hd + 1], a1[hd:] / a1[0:1]], axis=0).astype(BF16)

    def accumulate(step, alphas, pv):
        qi, j, live, _ = step
        for n in live:
            acc_ref[qi % 2, n] = pv[n] if j == 0 else alphas[n] * acc_ref[qi % 2, n] + pv[n]

    ms = {}
    scores(steps[0], 0)
    prev, prev_alphas = None, None
    for t, step in enumerate(steps):
        qi, j, live, masked_part = step
        if t + 1 < len(steps):
            scores(steps[t + 1], (t + 1) % 2)
        pv = values(prev, (t - 1) % 2) if prev is not None else None
        alphas = {}
        for n in live:
            s = s_ref[t % 2, n]
            if chains[n][1] == masked_part:
                s = jnp.where(causal, s, -jnp.inf)
            m_new = jnp.max(s, axis=0, keepdims=True)
            if j > 0:
                m_new = jnp.maximum(ms[n], m_new)
                alphas[n] = jnp.exp2(ms[n] - m_new)
            p_ref[t % 2, n] = jnp.exp2(s - m_new).astype(BF16)
            ms[n] = m_new
        if prev is not None:
            accumulate(prev, prev_alphas, pv)
            if prev[0] != qi:
                finish(prev[0])
        prev, prev_alphas = step, alphas
    accumulate(prev, prev_alphas, values(prev, (len(steps) - 1) % 2))
    finish(prev[0])


def _fox_prompt(qt, ktok, vt, ct, batch, seq, tk=256):
    tq = 2 * tk
    nq = seq // tq
    nk = seq // tk
    pairs = FOX_WIDTH // LANES
    assert qt.shape == (batch, nq, FOX_WIDTH, tq)
    q_spec = pl.BlockSpec((None, nq, LANES, tq), lambda b, p: (b, 0, p, 0))
    return pl.pallas_call(
        functools.partial(_fox_prompt_kernel, tk=tk),
        grid=(batch, pairs),
        in_specs=[q_spec,
                  pl.BlockSpec((seq, LANES), lambda b, p: (b, p)),
                  pl.BlockSpec((None, 2, FOX_HEAD_DIM, seq), lambda b, p: (b, p, 0, 0)),
                  pl.BlockSpec((FOX_HEADS, seq), lambda b, p: (0, b))],
        out_specs=q_spec,
        out_shape=jax.ShapeDtypeStruct((batch, nq, FOX_WIDTH, tq), BF16),
        scratch_shapes=[pltpu.VMEM((2, seq, LANES), BF16), pltpu.VMEM((2, nk, LANES, tk), BF16),
                        pltpu.VMEM((2, 4, tk, tk), F32), pltpu.VMEM((2, 4, tk, tk), BF16),
                        pltpu.VMEM((2, 4, LANES, tk), F32)],
        compiler_params=pltpu.CompilerParams(dimension_semantics=("arbitrary", "arbitrary"),
                                             vmem_limit_bytes=VMEM_LIMIT),
        name="fox_prompt",
    )(qt, ktok, vt, ct)


def _fox_sample_kernel(q_ref, kn_ref, vn_ref, cn_ref, lft_ref, mlow_ref, ck_ref, cv_ref,
                       o_ref, qh_ref, suf_ref, m_ref, l_ref, acc_ref, *, tp, nt, tn):
    j = pl.program_id(1)
    blk = 2 * LANES
    hd = FOX_HEAD_DIM

    @pl.when(j == 0)
    def _init():
        for h in range(FOX_HEADS):
            qh_ref[h] = q_ref[:, h * hd:(h + 1) * hd]
        m_ref[...] = jnp.full(m_ref.shape, NEG_BIG, F32)
        l_ref[...] = jnp.zeros(l_ref.shape, F32)
        acc_ref[...] = jnp.zeros(acc_ref.shape, F32)
        carry = jnp.zeros((FOX_HEADS, 1), F32)
        per_tile = tp // blk
        for b in reversed(range(nt * per_tile)):
            x = lft_ref[:, b * blk:(b + 1) * blk]
            parts = jnp.concatenate(_split3_f32(x), axis=0).astype(BF16)
            y = _dot(parts, mlow_ref[...])
            off = (b % per_tile) * blk
            suf_ref[b // per_tile, :, off:off + blk] = y[0:8] + y[8:16] + y[16:24] + carry
            carry = carry + jnp.sum(x, axis=1, keepdims=True)

    def update(s, pv):
        m, l, acc = _softmax_step(s, (m_ref[...], l_ref[...], acc_ref[...]), pv)
        m_ref[...] = m
        l_ref[...] = l
        acc_ref[...] = acc

    bmm = lambda a, b, ca, cb: lax.dot_general(a, b, (((ca,), (cb,)), ((0,), (0,))),
                                               preferred_element_type=F32)
    qh = qh_ref[...]

    kt = ck_ref[...].astype(BF16)
    vt = cv_ref[...].astype(BF16)
    s = bmm(qh, kt, 2, 1).reshape(FOX_HEADS * tn, tp) + jnp.repeat(suf_ref[j], tn, axis=0)
    update(s.reshape(FOX_HEADS, tn, tp), lambda pm: bmm(pm, vt, 2, 2))

    @pl.when(j == nt - 1)
    def _fin():
        per_head = lambda ref: jnp.stack([ref[:, h * hd:(h + 1) * hd] for h in range(FOX_HEADS)],
                                         axis=0).astype(BF16)
        kn = per_head(kn_ref)
        vn = per_head(vn_ref)
        r = lax.broadcasted_iota(jnp.int32, (FOX_HEADS, tn, tn), 1)
        c = lax.broadcasted_iota(jnp.int32, (FOX_HEADS, tn, tn), 2)
        s = jnp.where(c <= r, bmm(qh, kn, 2, 2) - cn_ref[...][:, None, :], -jnp.inf)
        update(s, lambda pm: bmm(pm, vn, 2, 1))
        o = acc_ref[...] / l_ref[...]
        o_ref[...] = jnp.concatenate([o[h] for h in range(FOX_HEADS)], axis=1).astype(BF16)


def _fox_sample(q, kn, vn, cn, lft, ck, cv, batch, tn, past, tp=2048):
    nt = past // tp
    blk = 2 * LANES
    idx = np.arange(blk)
    mlow = jnp.asarray((idx[:, None] > idx[None, :]).astype(np.float32), dtype=BF16)
    per_b = lambda w: pl.BlockSpec((tn, w), lambda b, j: (b, 0))
    cache = pl.BlockSpec((None, FOX_HEADS, FOX_HEAD_DIM, tp), lambda b, j: (b, 0, 0, j))
    return pl.pallas_call(
        functools.partial(_fox_sample_kernel, tp=tp, nt=nt, tn=tn),
        grid=(batch, nt),
        in_specs=[per_b(FOX_WIDTH), per_b(FOX_WIDTH), per_b(FOX_WIDTH),
                  pl.BlockSpec((None, FOX_HEADS, tn), lambda b, j: (b, 0, 0)),
                  pl.BlockSpec((None, FOX_HEADS, past), lambda b, j: (b, 0, 0)),
                  pl.BlockSpec((blk, blk), lambda b, j: (0, 0)),
                  cache, cache],
        out_specs=per_b(FOX_WIDTH),
        out_shape=jax.ShapeDtypeStruct((batch * tn, FOX_WIDTH), BF16),
        scratch_shapes=[pltpu.VMEM((FOX_HEADS, tn, FOX_HEAD_DIM), BF16),
                        pltpu.VMEM((nt, FOX_HEADS, tp), F32),
                        pltpu.VMEM((FOX_HEADS, tn, 1), F32), pltpu.VMEM((FOX_HEADS, tn, 1), F32),
                        pltpu.VMEM((FOX_HEADS, tn, FOX_HEAD_DIM), F32)],
        compiler_params=pltpu.CompilerParams(dimension_semantics=("arbitrary", "arbitrary"),
                                             vmem_limit_bytes=VMEM_LIMIT),
        name="fox_sample",
    )(q, kn, vn, cn, lft, mlow, ck, cv)


def _gla_kernel(q_ref, k_ref, v_ref, g_ref, r_ref, s0_ref, gn_ref, w_ref, lv_ref,
                o_ref, s_ref, st_ref, *, tb, levels):
    t = pl.program_id(1)
    half = LANES // 2

    @pl.when(t == 0)
    def _():
        st_ref[...] = s0_ref[...].reshape(GLA_KW, GLA_DV).T

    r = r_ref[...]
    gate = gn_ref[...] * (r * _sigmoid(r))

    g3 = jnp.concatenate(_split3_f32(g_ref[...] * LOG2E), axis=0).astype(BF16)
    rows = w_ref.shape[0] // 2
    dec = jnp.exp2(jnp.concatenate([_dot(w_ref[0:rows, :], g3), _dot(w_ref[rows:, :], g3)], axis=0))
    from_start = dec[0:tb]
    to_end = dec[tb:2 * tb]

    q = q_ref[...] * (GLA_DK ** -0.5)
    k = k_ref[...]
    v = v_ref[...]
    row = lax.broadcasted_iota(jnp.int32, (tb, GLA_KW), 0)
    low = lax.broadcasted_iota(jnp.int32, (tb, LANES), 1) < half
    lv = lv_ref[...]

    def pair_scores(xq, xk):
        outs = []
        for p in range(GLA_HEADS // 2):
            a = xq[:, p * LANES:(p + 1) * LANES]
            zero = jnp.zeros_like(a)
            lhs = jnp.concatenate([jnp.where(low, a, zero), jnp.where(low, zero, a)], axis=0)
            outs.append(_dot_nt(lhs, xk[:, p * LANES:(p + 1) * LANES]))
        return outs

    here = lv == -1
    a = [jnp.where(here, r_, 0.0) for r_ in pair_scores(q.astype(BF16), k.astype(BF16))]
    for l in range(levels):
        upper = ((row >> l) & 1) == 1
        x = (jnp.where(upper, q, k) * dec[(l + 2) * tb:(l + 3) * tb]).astype(BF16)
        here = lv == l
        a = [jnp.where(here, r_, a_) for r_, a_ in zip(pair_scores(x, x), a)]
    o = jnp.concatenate(
        [_dot(a[h // 2][(h % 2) * tb:(h % 2 + 1) * tb].astype(BF16), v[:, h * GLA_DV:(h + 1) * GLA_DV])
         for h in range(GLA_HEADS)], axis=1)

    lane = lax.broadcasted_iota(jnp.int32, (tb, GLA_KW), 1)
    head_sel = [(lane >= h * GLA_DK) & (lane < (h + 1) * GLA_DK) for h in range(GLA_HEADS)]
    st = st_ref[...]
    qt = (q * from_start).astype(BF16)
    kt = (k * to_end).astype(BF16)
    zero = jnp.zeros_like(qt)
    q4 = jnp.concatenate([jnp.where(sel, qt, zero) for sel in head_sel], axis=0)
    oi = _dot_nt(q4, st.astype(BF16))
    o = o + jnp.concatenate([oi[h * tb:(h + 1) * tb] for h in range(GLA_HEADS)], axis=1)
    upd = None
    for h in range(GLA_HEADS):
        u = _dot_tn(v[:, h * GLA_DV:(h + 1) * GLA_DV], jnp.where(head_sel[h], kt, zero))
        upd = u if upd is None else upd + u
    st = from_start[tb - 1:tb, :] * st + upd
    st_ref[...] = st

    @pl.when(t == pl.num_programs(1) - 1)
    def _():
        s_ref[...] = st.T.reshape(s_ref.shape)

    outs = []
    for h in range(GLA_HEADS):
        oh = o[:, h * GLA_DV:(h + 1) * GLA_DV]
        outs.append(oh * lax.rsqrt(jnp.mean(oh * oh, axis=-1, keepdims=True) + EPS))
    o_ref[...] = (jnp.concatenate(outs, axis=1) * gate).astype(BF16)


def _gla(gq, gk, gv, glog, gr, s0, gn, batch, seq, tb=128):
    tb = min(tb, seq)
    nt = seq // tb
    n = batch * seq
    levels = tb.bit_length() - 1
    assert tb == 1 << levels
    ti = np.arange(tb)[:, None]
    si = np.arange(tb)[None, :]
    blocks = [si <= ti, si > ti]
    for l in range(levels):
        m = 1 << l
        mid = ti - ti % (2 * m) + m - 1
        upper = ti % (2 * m) >= m
        blocks.append(np.where(upper, (si > mid) & (si <= ti), (si > ti) & (si <= mid)))
    w = np.concatenate(blocks, axis=0).astype(np.float32)
    w = jnp.asarray(np.concatenate([w, w, w], axis=1), dtype=BF16)
    x = ti ^ si
    lv = np.where(si < ti, np.floor(np.log2(np.maximum(x, 1))).astype(np.int32),
                  np.where(si == ti, -1, -2)).astype(np.int32)
    lv = jnp.asarray(np.concatenate([lv, lv], axis=0))
    row = lambda w_: pl.BlockSpec((tb, w_), lambda b, t: (b * nt + t, 0))
    state = pl.BlockSpec((None, GLA_HEADS, GLA_DK, GLA_DV), lambda b, t: (b, 0, 0, 0))
    const = lambda shape: pl.BlockSpec(shape, lambda b, t: (0, 0))
    return pl.pallas_call(
        functools.partial(_gla_kernel, tb=tb, levels=levels),
        grid=(batch, nt),
        in_specs=[row(GLA_KW), row(GLA_KW), row(GLA_VW), row(GLA_KW), row(GLA_VW), state,
                  const((1, GLA_VW)), const(w.shape), const(lv.shape)],
        out_specs=(row(GLA_VW), state),
        out_shape=(jax.ShapeDtypeStruct((n, GLA_VW), BF16),
                   jax.ShapeDtypeStruct((batch, GLA_HEADS, GLA_DK, GLA_DV), F32)),
        scratch_shapes=[pltpu.VMEM((GLA_DV, GLA_KW), F32)],
        compiler_params=pltpu.CompilerParams(dimension_semantics=("arbitrary", "arbitrary"),
                                             vmem_limit_bytes=VMEM_LIMIT),
        name="gla",
    )(gq, gk, gv, glog, gr, s0, gn, w, lv)


def _ffn_kernel(x_ref, fo_ref, go_ref, wo_ref, g2_ref, wg_ref, wu_ref, wd_ref, gf_ref,
                y_ref, a_ref, *, chunk, fox_time_minor):
    tm = x_ref.shape[0]
    halves = [slice(0, tm // 2), slice(tm // 2, tm)] if tm % 32 == 0 else [slice(0, tm)]
    y1 = []
    for rows in halves:
        fox = (_dot_tn(fo_ref[:, rows], wo_ref[0:FOX_WIDTH, :]) if fox_time_minor
               else _dot(fo_ref[rows, :], wo_ref[0:FOX_WIDTH, :]))
        y1.append(x_ref[rows, :] + fox + _dot(go_ref[rows, :], wo_ref[FOX_WIDTH:, :]))
    h2 = jnp.concatenate([_rms(y, g2_ref[...]).astype(BF16) for y in y1], axis=0)
    for c in range(D_FF // chunk):
        cs = slice(c * chunk, (c + 1) * chunk)
        u = _dot(h2, wg_ref[:, cs])
        w = _dot(h2, wu_ref[:, cs])
        a_ref[:, cs] = (u * _sigmoid(u) * w).astype(BF16)
    y2 = [y + _dot(a_ref[rows, :], wd_ref[...]) for y, rows in zip(y1, halves)]
    for y, rows in zip(y2, halves):
        y_ref[rows, :] = _rms(y, gf_ref[...])


def _ffn(x2d, fo, go, prm, tm=512, chunk=256):
    n = x2d.shape[0]
    tm = min(tm, n)
    row = lambda w: pl.BlockSpec((tm, w), lambda i: (i, 0))
    const = lambda shape: pl.BlockSpec(shape, lambda i: (0, 0), pipeline_mode=pl.Buffered(1))
    fox_time_minor = fo.ndim == 4
    if fox_time_minor:
        tiles_per_seq = fo.shape[1]
        assert fo.shape[3] == tm
        fo_spec = pl.BlockSpec((None, None, FOX_WIDTH, tm),
                               lambda i: (i // tiles_per_seq, i % tiles_per_seq, 0, 0))
    else:
        fo_spec = row(FOX_WIDTH)
    return pl.pallas_call(
        functools.partial(_ffn_kernel, chunk=chunk, fox_time_minor=fox_time_minor),
        grid=(n // tm,),
        in_specs=[row(D_MODEL), fo_spec, row(GLA_VW), const((D_MODEL, D_MODEL)),
                  const((1, D_MODEL)), const((D_MODEL, D_FF)), const((D_MODEL, D_FF)),
                  const((D_FF, D_MODEL)), const((1, D_MODEL))],
        out_specs=row(D_MODEL),
        out_shape=jax.ShapeDtypeStruct((n, D_MODEL), F32),
        scratch_shapes=[pltpu.VMEM((tm, D_FF), BF16)],
        compiler_params=pltpu.CompilerParams(dimension_semantics=("arbitrary",),
                                             vmem_limit_bytes=VMEM_LIMIT),
        name="ffn",
    )(x2d, fo, go, prm['wo'], prm['g2'], prm['wg'], prm['wu'], prm['wd'], prm['gf'])


def _layer_params(layer, norm1_g, w_in, w_gate2, b_gate2, b_forget, gla_norm_g, w_out,
                  norm2_g, w_gate, w_up, w_down, final_norm_g):
    wt = jnp.transpose(w_in[layer])
    o_fl = 3 * FOX_WIDTH
    o_gq = o_fl + FOX_HEADS
    o_gg = o_gq + 2 * GLA_KW + GLA_VW
    o_gr = o_gg + GLA_GATE_RANK
    wc = jnp.zeros((LANES, D_MODEL), F32).at[:GLA_GATE_RANK].set(wt[o_gg:o_gr])
    wg2 = jnp.zeros((LANES, GLA_KW), F32).at[:GLA_GATE_RANK].set(w_gate2[layer])
    return dict(
        g1=norm1_g[layer].reshape(1, D_MODEL),
        wq=wt[:FOX_WIDTH].astype(BF16),
        wkv=wt[FOX_WIDTH:o_fl].astype(BF16),
        wb=jnp.concatenate([wt[o_gq:o_gg], wt[o_gr:]], axis=0).astype(BF16),
        wc=wc.astype(BF16),
        wfl=wt[o_fl:o_gq].astype(BF16),
        wg2=wg2.astype(BF16),
        bg2=b_gate2[layer].reshape(1, GLA_KW),
        bfc=b_forget[layer].reshape(FOX_HEADS, 1),
        gn=gla_norm_g[layer].reshape(1, GLA_VW),
        wo=w_out[layer].astype(BF16),
        g2=norm2_g[layer].reshape(1, D_MODEL),
        wg=w_gate[layer].astype(BF16),
        wu=w_up[layer].astype(BF16),
        wd=w_down[layer].astype(BF16),
        gf=final_norm_g.reshape(1, D_MODEL),
    )


def kernel(x_prompt, x_sample, cache_fox_k, cache_fox_v, cache_fox_logf, state_gla, norm1_g, w_in,
           w_gate2, b_gate2, b_forget, gla_norm_g, w_out, norm2_g, w_gate, w_up, w_down, final_norm_g):
    depth = w_in.shape[0]
    assert depth == 1, "the final rmsnorm is fused into the layer's ffn kernel"
    bp, tp_, _ = x_prompt.shape
    bs, ts, _ = x_sample.shape
    past = cache_fox_k.shape[2]
    layer = 0
    prm = _layer_params(layer, norm1_g, w_in, w_gate2, b_gate2, b_forget, gla_norm_g, w_out,
                        norm2_g, w_gate, w_up, w_down, final_norm_g)
    by_time = lambda a, b, t: a.reshape(FOX_HEADS, b, t).transpose(1, 2, 0)[None]

    xp = x_prompt.reshape(bp * tp_, D_MODEL)
    qt, kt_p, vt_p, ktok, lf_p, ct, gq, gk, gv, glog, gr = _proj(xp, bp, tp_, prm, True)
    fox_o = _fox_prompt(qt, ktok, vt_p, ct, bp, tp_)
    s0 = jnp.zeros((bp, GLA_HEADS, GLA_DK, GLA_DV), F32)
    gla_o, s_p = _gla(gq, gk, gv, glog, gr, s0, prm['gn'], bp, tp_)
    y_p = _ffn(xp, fox_o, gla_o, prm)

    xs = x_sample.reshape(bs * ts, D_MODEL)
    q, k_s, v_s, lf_s, ct, gq, gk, gv, glog, gr = _proj(xs, bs, ts, prm, False)
    cn = ct.reshape(FOX_HEADS, bs, ts).transpose(1, 0, 2)
    lft = cache_fox_logf[layer].astype(F32).transpose(0, 2, 1)
    ck = cache_fox_k[layer].transpose(0, 2, 3, 1)
    cv = cache_fox_v[layer].transpose(0, 2, 3, 1)
    fox_o = _fox_sample(q, k_s, v_s, cn, lft, ck, cv, bs, ts, past)
    gla_o, s_s = _gla(gq, gk, gv, glog, gr, state_gla[layer].astype(F32), prm['gn'], bs, ts)
    y_s = _ffn(xs, fox_o, gla_o, prm)

    heads = lambda a, b, t: a.reshape(1, b, t, FOX_HEADS, FOX_HEAD_DIM)
    return (y_p.reshape(bp, tp_, D_MODEL), y_s.reshape(bs, ts, D_MODEL),
            kt_p.transpose(0, 3, 1, 2)[None], vt_p.transpose(0, 3, 1, 2)[None],
            by_time(lf_p, bp, tp_), s_p[None],
            heads(k_s, bs, ts), heads(v_s, bs, ts), by_time(lf_s, bs, ts), s_s[None])
```

```python
import functools

import numpy as np
import jax
import jax.numpy as jnp
from jax import lax
from jax.experimental import pallas as pl
from jax.experimental.pallas import tpu as pltpu

D_MODEL = 1024
FOX_HEADS = 8
FOX_HEAD_DIM = 64
FOX_WIDTH = FOX_HEADS * FOX_HEAD_DIM
GLA_HEADS = 4
GLA_DK = 64
GLA_DV = 128
GLA_KW = GLA_HEADS * GLA_DK
GLA_VW = GLA_HEADS * GLA_DV
GLA_GATE_RANK = 16
GLA_GATE_TEMP = 16.0
D_FF = 2816
EPS = 1e-6

LANES = 128
LOG2E = 1.4426950408889634
VMEM_LIMIT = 56 * 1024 * 1024

F32 = jnp.float32
BF16 = jnp.bfloat16
NEG_BIG = -1e30


def _log_sigmoid(x):
    return jnp.minimum(x, 0.0) - jnp.log1p(jnp.exp(-jnp.abs(x)))


def _sigmoid(x):
    return 1.0 / (1.0 + jnp.exp(-x))


def _split3_f32(x):
    hi = x.astype(BF16).astype(F32)
    r = x - hi
    mid = r.astype(BF16).astype(F32)
    lo = (r - mid).astype(BF16).astype(F32)
    return hi, mid, lo


def _dot(a, b):
    return jnp.dot(a, b, preferred_element_type=F32)


def _dot_nt(a, b):
    return lax.dot_general(a, b, (((1,), (1,)), ((), ())), preferred_element_type=F32)


def _dot_tn(a, b):
    return lax.dot_general(a, b, (((0,), (0,)), ((), ())), preferred_element_type=F32)


def _rms(x, g):
    return x * lax.rsqrt(jnp.mean(x * x, axis=-1, keepdims=True) + EPS) * g


def _softmax_step(s, carry, pv):
    m, l, acc = carry
    m_new = jnp.maximum(m, jnp.max(s, axis=-1, keepdims=True))
    alpha = jnp.exp(m - m_new)
    pm = jnp.exp(s - m_new)
    l = alpha * l + jnp.sum(pm, axis=-1, keepdims=True)
    acc = alpha * acc + pv(pm.astype(BF16))
    return m_new, l, acc


def _proj_kernel(x_ref, g1_ref, wq_ref, wkv_ref, wb_ref, wc_ref, wfl_ref, wg2_ref, bg2_ref,
                 bfc_ref, tri_ref,
                 *rest, tiles_per_seq, time_minor):
    carry_ref = rest[-1]

    @pl.when(pl.program_id(0) % tiles_per_seq == 0)
    def _():
        carry_ref[...] = jnp.zeros_like(carry_ref)

    h = _rms(x_ref[...], g1_ref[...]).astype(BF16)
    tm = h.shape[0]
    scale = FOX_HEAD_DIM ** -0.5

    zc = _dot_nt(h, wc_ref[...])
    fl_t = _dot_nt(wfl_ref[...], h)

    if time_minor:
        (q_ref, k_ref, v_ref, ktok_ref, lft_ref, ct_ref, gq_ref, gk_ref, gv_ref, glog_ref, gr_ref,
         carry_ref) = rest
        q_ref[...] = (_dot_nt(wq_ref[...], h) * (scale * LOG2E)).astype(BF16)
    else:
        (q_ref, k_ref, v_ref, lft_ref, ct_ref, gq_ref, gk_ref, gv_ref, glog_ref, gr_ref,
         carry_ref) = rest
        q_ref[...] = (_dot_nt(h, wq_ref[...]) * scale).astype(BF16)

    logf_t = _log_sigmoid(fl_t + bfc_ref[...])
    lft_ref[...] = logf_t
    parts = jnp.concatenate(_split3_f32(logf_t), axis=0).astype(BF16)
    gg = zc.astype(BF16)

    if time_minor:
        kvt = _dot_nt(wkv_ref[...], h)
        k_ref[...] = kvt[:FOX_WIDTH].reshape(FOX_HEADS, FOX_HEAD_DIM, tm)
        v_ref[...] = kvt[FOX_WIDTH:].reshape(FOX_HEADS, FOX_HEAD_DIM, tm)
        ktok_ref[...] = _dot_nt(h, wkv_ref[0:FOX_WIDTH, :]).astype(BF16)
    else:
        kv = _dot_nt(h, wkv_ref[...])
        k_ref[...] = kv[:, :FOX_WIDTH]
        v_ref[...] = kv[:, FOX_WIDTH:]

    gpre = _dot(gg, wg2_ref[...]) + bg2_ref[...]
    cs = _dot(parts, tri_ref[...])

    zb = _dot_nt(h, wb_ref[...])
    gq_ref[...] = zb[:, :GLA_KW]
    gk_ref[...] = zb[:, GLA_KW:2 * GLA_KW]
    gv_ref[...] = zb[:, 2 * GLA_KW:2 * GLA_KW + GLA_VW].astype(BF16)
    gr_ref[...] = zb[:, 2 * GLA_KW + GLA_VW:]

    glog_ref[...] = _log_sigmoid(gpre) * (1.0 / GLA_GATE_TEMP)

    ct = cs[0:8] + cs[8:16] + cs[16:24] + carry_ref[:, 0:1]
    ct_ref[...] = ct
    carry_ref[...] = jnp.broadcast_to(ct[:, tm - 1:], carry_ref.shape)


def _proj(x2d, batch, seq_len, prm, time_minor, tm=512):
    n = x2d.shape[0]
    tm = min(tm, n)
    tiles_per_seq = max(seq_len // tm, 1)
    idx = np.arange(tm)
    tri = ((idx[:, None] <= idx[None, :]) & (idx[:, None] // seq_len == idx[None, :] // seq_len))
    tri = jnp.asarray(tri.astype(np.float32), dtype=BF16)
    const = lambda a: pl.BlockSpec(a.shape, lambda i: (0, 0))
    row = lambda w: pl.BlockSpec((tm, w), lambda i: (i, 0))
    col = pl.BlockSpec((FOX_HEADS, tm), lambda i: (0, i))
    sds = jax.ShapeDtypeStruct
    if time_minor:
        kv_shape = sds((batch, FOX_HEADS, FOX_HEAD_DIM, seq_len), F32)
        kv_spec = pl.BlockSpec((None, FOX_HEADS, FOX_HEAD_DIM, tm),
                               lambda i: (i // tiles_per_seq, 0, 0, i % tiles_per_seq))
        fox = [(sds((batch, tiles_per_seq, FOX_WIDTH, tm), BF16),
                pl.BlockSpec((None, None, FOX_WIDTH, tm), lambda i: (i // tiles_per_seq, i % tiles_per_seq, 0, 0))),
               (kv_shape, kv_spec), (kv_shape, kv_spec),
               (sds((n, FOX_WIDTH), BF16), row(FOX_WIDTH))]
    else:
        fox = [(sds((n, FOX_WIDTH), BF16), row(FOX_WIDTH)),
               (sds((n, FOX_WIDTH), F32), row(FOX_WIDTH)),
               (sds((n, FOX_WIDTH), F32), row(FOX_WIDTH))]
    outs = fox + [
        (sds((FOX_HEADS, n), F32), col),
        (sds((FOX_HEADS, n), F32), col),
        (sds((n, GLA_KW), F32), row(GLA_KW)),
        (sds((n, GLA_KW), F32), row(GLA_KW)),
        (sds((n, GLA_VW), BF16), row(GLA_VW)),
        (sds((n, GLA_KW), F32), row(GLA_KW)),
        (sds((n, GLA_VW), F32), row(GLA_VW)),
    ]
    names = ('g1', 'wq', 'wkv', 'wb', 'wc', 'wfl', 'wg2', 'bg2', 'bfc')
    return pl.pallas_call(
        functools.partial(_proj_kernel, tiles_per_seq=tiles_per_seq, time_minor=time_minor),
        grid=(n // tm,),
        in_specs=[row(D_MODEL)] + [const(prm[k]) for k in names] + [const(tri)],
        out_specs=tuple(s for _, s in outs), out_shape=tuple(s for s, _ in outs),
        scratch_shapes=[pltpu.VMEM((FOX_HEADS, LANES), F32)],
        compiler_params=pltpu.CompilerParams(dimension_semantics=("arbitrary",),
                                             vmem_limit_bytes=VMEM_LIMIT),
        name="proj",
    )(x2d, *[prm[k] for k in names], tri)


def _fox_prompt_kernel(q_ref, k_ref, v_ref, c_ref, o_ref, kb_ref, vb_ref, s_ref, p_ref, acc_ref,
                       *, tk):
    p = pl.program_id(1)
    nq = q_ref.shape[0]
    nk = vb_ref.shape[1]
    hd = FOX_HEAD_DIM
    spare = (hd, 0)

    parts = jnp.concatenate(_split3_f32(c_ref[...] * (-LOG2E)), axis=0).astype(BF16)
    r = lax.broadcasted_iota(jnp.int32, (3 * FOX_HEADS, LANES), 0)
    ln = lax.broadcasted_iota(jnp.int32, (3 * FOX_HEADS, LANES), 1)
    k = k_ref[...]
    klane = lax.broadcasted_iota(jnp.int32, k.shape, 1)
    vrow = lax.broadcasted_iota(jnp.int32, (LANES, tk), 0)
    for hh in range(2):
        place = (r % FOX_HEADS == 2 * p + hh) & (ln == spare[hh] + r // FOX_HEADS)
        extra = _dot_tn(parts, jnp.where(place, 1.0, 0.0).astype(BF16))
        own = (klane < hd) if hh == 0 else (klane >= hd)
        kb_ref[hh] = jnp.where(own, k, extra.astype(BF16))
        for jj in range(nk):
            vt = v_ref[:, :, jj * tk:(jj + 1) * tk].reshape(LANES, tk)
            vb_ref[hh, jj] = jnp.where(vrow == spare[hh], 1.0, vt).astype(BF16)

    qrow = lax.broadcasted_iota(jnp.int32, (LANES, tk), 0)
    key = lax.broadcasted_iota(jnp.int32, (tk, tk), 0)
    qry = lax.broadcasted_iota(jnp.int32, (tk, tk), 1)
    causal = key <= qry

    chains = [(hh, part) for hh in range(2) for part in range(2)]
    everyone = list(range(len(chains)))
    second = [n for n in everyone if chains[n][1] == 1]

    def q_operands(qi):
        qs = []
        for hh, part in chains:
            q = q_ref[qi, :, part * tk:(part + 1) * tk]
            own = (qrow < hd) if hh == 0 else (qrow >= hd)
            ones = (qrow >= spare[hh]) & (qrow < spare[hh] + 3)
            qs.append(jnp.where(own, q, jnp.where(ones, 1.0, 0.0).astype(BF16)))
        return qs

    steps = []
    for qi in range(nq):
        steps += [(qi, j, everyone, None) for j in range(2 * qi)]
        steps += [(qi, 2 * qi, everyone, 0), (qi, 2 * qi + 1, second, 1)]

    qs_of = {}

    def scores(step, buf):
        qi, j, live, _ = step
        if qi not in qs_of:
            qs_of.clear()
            qs_of[qi] = q_operands(qi)
        for n in live:
            s_ref[buf, n] = _dot(kb_ref[chains[n][0], j * tk:(j + 1) * tk, :], qs_of[qi][n])

    def values(step, buf):
        _, j, live, _ = step
        return {n: _dot(vb_ref[chains[n][0], j], p_ref[buf, n]) for n in live}

    def finish(qi):
        for part in range(2):
            a0 = acc_ref[qi % 2, part]
            a1 = acc_ref[qi % 2, 2 + part]
            o_ref[qi, :, part * tk:(part + 1) * tk] = jnp.concatenate(
                [a0[:hd] / a0[hd:hd + 1], a1[hd:] / a1[0:1]], axis=0).astype(BF16)

    def accumulate(step, alphas, pv):
        qi, j, live, _ = step
        for n in live:
            acc_ref[qi % 2, n] = pv[n] if j == 0 else alphas[n] * acc_ref[qi % 2, n] + pv[n]

    ms = {}
    scores(steps[0], 0)
    prev, prev_alphas = None, None
    for t, step in enumerate(steps):
        qi, j, live, masked_part = step
        if t + 1 < len(steps):
            scores(steps[t + 1], (t + 1) % 2)
        pv = values(prev, (t - 1) % 2) if prev is not None else None
        alphas = {}
        for n in live:
            s = s_ref[t % 2, n]
            if chains[n][1] == masked_part:
                s = jnp.where(causal, s, -jnp.inf)
            m_new = jnp.max(s, axis=0, keepdims=True)
            if j > 0:
                m_new = jnp.maximum(ms[n], m_new)
                alphas[n] = jnp.exp2(ms[n] - m_new)
            p_ref[t % 2, n] = jnp.exp2(s - m_new).astype(BF16)
            ms[n] = m_new
        if prev is not None:
            accumulate(prev, prev_alphas, pv)
            if prev[0] != qi:
                finish(prev[0])
        prev, prev_alphas = step, alphas
    accumulate(prev, prev_alphas, values(prev, (len(steps) - 1) % 2))
    finish(prev[0])


def _fox_prompt(qt, ktok, vt, ct, batch, seq, tk=256):
    tq = 2 * tk
    nq = seq // tq
    nk = seq // tk
    pairs = FOX_WIDTH // LANES
    assert qt.shape == (batch, nq, FOX_WIDTH, tq)
    q_spec = pl.BlockSpec((None, nq, LANES, tq), lambda b, p: (b, 0, p, 0))
    return pl.pallas_call(
        functools.partial(_fox_prompt_kernel, tk=tk),
        grid=(batch, pairs),
        in_specs=[q_spec,
                  pl.BlockSpec((seq, LANES), lambda b, p: (b, p)),
                  pl.BlockSpec((None, 2, FOX_HEAD_DIM, seq), lambda b, p: (b, p, 0, 0)),
                  pl.BlockSpec((FOX_HEADS, seq), lambda b, p: (0, b))],
        out_specs=q_spec,
        out_shape=jax.ShapeDtypeStruct((batch, nq, FOX_WIDTH, tq), BF16),
        scratch_shapes=[pltpu.VMEM((2, seq, LANES), BF16), pltpu.VMEM((2, nk, LANES, tk), BF16),
                        pltpu.VMEM((2, 4, tk, tk), F32), pltpu.VMEM((2, 4, tk, tk), BF16),
                        pltpu.VMEM((2, 4, LANES, tk), F32)],
        compiler_params=pltpu.CompilerParams(dimension_semantics=("arbitrary", "arbitrary"),
                                             vmem_limit_bytes=VMEM_LIMIT),
        name="fox_prompt",
    )(qt, ktok, vt, ct)


def _fox_sample_kernel(q_ref, kn_ref, vn_ref, cn_ref, lft_ref, mlow_ref, ck_ref, cv_ref,
                       o_ref, qh_ref, suf_ref, m_ref, l_ref, acc_ref, *, tp, nt, tn):
    j = pl.program_id(1)
    blk = 2 * LANES
    hd = FOX_HEAD_DIM

    @pl.when(j == 0)
    def _init():
        for h in range(FOX_HEADS):
            qh_ref[h] = q_ref[:, h * hd:(h + 1) * hd]
        m_ref[...] = jnp.full(m_ref.shape, NEG_BIG, F32)
        l_ref[...] = jnp.zeros(l_ref.shape, F32)
        acc_ref[...] = jnp.zeros(acc_ref.shape, F32)
        carry = jnp.zeros((FOX_HEADS, 1), F32)
        per_tile = tp // blk
        for b in reversed(range(nt * per_tile)):
            x = lft_ref[:, b * blk:(b + 1) * blk]
            parts = jnp.concatenate(_split3_f32(x), axis=0).astype(BF16)
            y = _dot(parts, mlow_ref[...])
            off = (b % per_tile) * blk
            suf_ref[b // per_tile, :, off:off + blk] = y[0:8] + y[8:16] + y[16:24] + carry
            carry = carry + jnp.sum(x, axis=1, keepdims=True)

    def update(s, pv):
        m, l, acc = _softmax_step(s, (m_ref[...], l_ref[...], acc_ref[...]), pv)
        m_ref[...] = m
        l_ref[...] = l
        acc_ref[...] = acc

    bmm = lambda a, b, ca, cb: lax.dot_general(a, b, (((ca,), (cb,)), ((0,), (0,))),
                                               preferred_element_type=F32)
    qh = qh_ref[...]

    kt = ck_ref[...].astype(BF16)
    vt = cv_ref[...].astype(BF16)
    s = bmm(qh, kt, 2, 1).reshape(FOX_HEADS * tn, tp) + jnp.repeat(suf_ref[j], tn, axis=0)
    update(s.reshape(FOX_HEADS, tn, tp), lambda pm: bmm(pm, vt, 2, 2))

    @pl.when(j == nt - 1)
    def _fin():
        per_head = lambda ref: jnp.stack([ref[:, h * hd:(h + 1) * hd] for h in range(FOX_HEADS)],
                                         axis=0).astype(BF16)
        kn = per_head(kn_ref)
        vn = per_head(vn_ref)
        r = lax.broadcasted_iota(jnp.int32, (FOX_HEADS, tn, tn), 1)
        c = lax.broadcasted_iota(jnp.int32, (FOX_HEADS, tn, tn), 2)
        s = jnp.where(c <= r, bmm(qh, kn, 2, 2) - cn_ref[...][:, None, :], -jnp.inf)
        update(s, lambda pm: bmm(pm, vn, 2, 1))
        o = acc_ref[...] / l_ref[...]
        o_ref[...] = jnp.concatenate([o[h] for h in range(FOX_HEADS)], axis=1).astype(BF16)


def _fox_sample(q, kn, vn, cn, lft, ck, cv, batch, tn, past, tp=2048):
    nt = past // tp
    blk = 2 * LANES
    idx = np.arange(blk)
    mlow = jnp.asarray((idx[:, None] > idx[None, :]).astype(np.float32), dtype=BF16)
    per_b = lambda w: pl.BlockSpec((tn, w), lambda b, j: (b, 0))
    cache = pl.BlockSpec((None, FOX_HEADS, FOX_HEAD_DIM, tp), lambda b, j: (b, 0, 0, j))
    return pl.pallas_call(
        functools.partial(_fox_sample_kernel, tp=tp, nt=nt, tn=tn),
        grid=(batch, nt),
        in_specs=[per_b(FOX_WIDTH), per_b(FOX_WIDTH), per_b(FOX_WIDTH),
                  pl.BlockSpec((None, FOX_HEADS, tn), lambda b, j: (b, 0, 0)),
                  pl.BlockSpec((None, FOX_HEADS, past), lambda b, j: (b, 0, 0)),
                  pl.BlockSpec((blk, blk), lambda b, j: (0, 0)),
                  cache, cache],
        out_specs=per_b(FOX_WIDTH),
        out_shape=jax.ShapeDtypeStruct((batch * tn, FOX_WIDTH), BF16),
        scratch_shapes=[pltpu.VMEM((FOX_HEADS, tn, FOX_HEAD_DIM), BF16),
                        pltpu.VMEM((nt, FOX_HEADS, tp), F32),
                        pltpu.VMEM((FOX_HEADS, tn, 1), F32), pltpu.VMEM((FOX_HEADS, tn, 1), F32),
                        pltpu.VMEM((FOX_HEADS, tn, FOX_HEAD_DIM), F32)],
        compiler_params=pltpu.CompilerParams(dimension_semantics=("arbitrary", "arbitrary"),
                                             vmem_limit_bytes=VMEM_LIMIT),
        name="fox_sample",
    )(q, kn, vn, cn, lft, mlow, ck, cv)


def _gla_kernel(q_ref, k_ref, v_ref, g_ref, r_ref, s0_ref, gn_ref, w_ref, lv_ref,
                o_ref, s_ref, st_ref, *, tb, levels, nb):
    t = pl.program_id(1)

    @pl.when(t == 0)
    def _():
        st_ref[...] = s0_ref[...].reshape(GLA_KW, GLA_DV).T

    st = st_ref[...]
    for blk in range(nb):
        rows = lambda ref: ref.at[pl.ds(blk * tb, tb), :]
        st = _gla_block(rows(q_ref), rows(k_ref), rows(v_ref), rows(g_ref), rows(r_ref), gn_ref,
                        w_ref, lv_ref, rows(o_ref), st, tb=tb, levels=levels)
    st_ref[...] = st

    @pl.when(t == pl.num_programs(1) - 1)
    def _():
        s_ref[...] = st.T.reshape(s_ref.shape)


def _gla_block(q_ref, k_ref, v_ref, g_ref, r_ref, gn_ref, w_ref, lv_ref, o_ref, st, *, tb, levels):
    half = LANES // 2
    r = r_ref[...]
    gate = gn_ref[...] * (r * _sigmoid(r))

    g3 = jnp.concatenate(_split3_f32(g_ref[...] * LOG2E), axis=0).astype(BF16)
    rows = w_ref.shape[0] // 2
    dec = jnp.exp2(jnp.concatenate([_dot(w_ref[0:rows, :], g3), _dot(w_ref[rows:, :], g3)], axis=0))
    from_start = dec[0:tb]
    to_end = dec[tb:2 * tb]

    q = q_ref[...] * (GLA_DK ** -0.5)
    k = k_ref[...]
    v = v_ref[...]
    row = lax.broadcasted_iota(jnp.int32, (tb, GLA_KW), 0)
    low = lax.broadcasted_iota(jnp.int32, (tb, LANES), 1) < half
    lv = lv_ref[...]

    def pair_scores(xq, xk):
        outs = []
        for p in range(GLA_HEADS // 2):
            a = xq[:, p * LANES:(p + 1) * LANES]
            zero = jnp.zeros_like(a)
            lhs = jnp.concatenate([jnp.where(low, a, zero), jnp.where(low, zero, a)], axis=0)
            outs.append(_dot_nt(lhs, xk[:, p * LANES:(p + 1) * LANES]))
        return outs

    here = lv == -1
    a = [jnp.where(here, r_, 0.0) for r_ in pair_scores(q.astype(BF16), k.astype(BF16))]
    for l in range(levels):
        upper = ((row >> l) & 1) == 1
        x = (jnp.where(upper, q, k) * dec[(l + 2) * tb:(l + 3) * tb]).astype(BF16)
        here = lv == l
        a = [jnp.where(here, r_, a_) for r_, a_ in zip(pair_scores(x, x), a)]
    o = jnp.concatenate(
        [_dot(a[h // 2][(h % 2) * tb:(h % 2 + 1) * tb].astype(BF16), v[:, h * GLA_DV:(h + 1) * GLA_DV])
         for h in range(GLA_HEADS)], axis=1)

    lane = lax.broadcasted_iota(jnp.int32, (tb, GLA_KW), 1)
    head_sel = [(lane >= h * GLA_DK) & (lane < (h + 1) * GLA_DK) for h in range(GLA_HEADS)]
    qt = (q * from_start).astype(BF16)
    kt = (k * to_end).astype(BF16)
    zero = jnp.zeros_like(qt)
    q4 = jnp.concatenate([jnp.where(sel, qt, zero) for sel in head_sel], axis=0)
    oi = _dot_nt(q4, st.astype(BF16))
    o = o + jnp.concatenate([oi[h * tb:(h + 1) * tb] for h in range(GLA_HEADS)], axis=1)
    upd = None
    for h in range(GLA_HEADS):
        u = _dot_tn(v[:, h * GLA_DV:(h + 1) * GLA_DV], jnp.where(head_sel[h], kt, zero))
        upd = u if upd is None else upd + u
    st = from_start[tb - 1:tb, :] * st + upd

    outs = []
    for h in range(GLA_HEADS):
        oh = o[:, h * GLA_DV:(h + 1) * GLA_DV]
        outs.append(oh * lax.rsqrt(jnp.mean(oh * oh, axis=-1, keepdims=True) + EPS))
    o_ref[...] = (jnp.concatenate(outs, axis=1) * gate).astype(BF16)
    return st


def _gla(gq, gk, gv, glog, gr, s0, gn, batch, seq, tb=128, rows_per_step=512):
    tb = min(tb, seq)
    nb = max(min(rows_per_step, seq) // tb, 1)
    nt = seq // (tb * nb)
    n = batch * seq
    levels = tb.bit_length() - 1
    assert tb == 1 << levels
    ti = np.arange(tb)[:, None]
    si = np.arange(tb)[None, :]
    blocks = [si <= ti, si > ti]
    for l in range(levels):
        m = 1 << l
        mid = ti - ti % (2 * m) + m - 1
        upper = ti % (2 * m) >= m
        blocks.append(np.where(upper, (si > mid) & (si <= ti), (si > ti) & (si <= mid)))
    w = np.concatenate(blocks, axis=0).astype(np.float32)
    w = jnp.asarray(np.concatenate([w, w, w], axis=1), dtype=BF16)
    x = ti ^ si
    lv = np.where(si < ti, np.floor(np.log2(np.maximum(x, 1))).astype(np.int32),
                  np.where(si == ti, -1, -2)).astype(np.int32)
    lv = jnp.asarray(np.concatenate([lv, lv], axis=0))
    row = lambda w_: pl.BlockSpec((tb * nb, w_), lambda b, t: (b * nt + t, 0))
    state = pl.BlockSpec((None, GLA_HEADS, GLA_DK, GLA_DV), lambda b, t: (b, 0, 0, 0))
    const = lambda shape: pl.BlockSpec(shape, lambda b, t: (0, 0))
    return pl.pallas_call(
        functools.partial(_gla_kernel, tb=tb, levels=levels, nb=nb),
        grid=(batch, nt),
        in_specs=[row(GLA_KW), row(GLA_KW), row(GLA_VW), row(GLA_KW), row(GLA_VW), state,
                  const((1, GLA_VW)), const(w.shape), const(lv.shape)],
        out_specs=(row(GLA_VW), state),
        out_shape=(jax.ShapeDtypeStruct((n, GLA_VW), BF16),
                   jax.ShapeDtypeStruct((batch, GLA_HEADS, GLA_DK, GLA_DV), F32)),
        scratch_shapes=[pltpu.VMEM((GLA_DV, GLA_KW), F32)],
        compiler_params=pltpu.CompilerParams(dimension_semantics=("arbitrary", "arbitrary"),
                                             vmem_limit_bytes=VMEM_LIMIT),
        name="gla",
    )(gq, gk, gv, glog, gr, s0, gn, w, lv)


def _ffn_kernel(x_ref, fo_ref, go_ref, wo_ref, g2_ref, wg_ref, wu_ref, wd_ref, gf_ref,
                y_ref, a_ref, *, chunk, fox_time_minor):
    tm = x_ref.shape[0]
    halves = [slice(0, tm // 2), slice(tm // 2, tm)] if tm % 32 == 0 else [slice(0, tm)]
    y1 = []
    for rows in halves:
        fox = (_dot_tn(fo_ref[:, rows], wo_ref[0:FOX_WIDTH, :]) if fox_time_minor
               else _dot(fo_ref[rows, :], wo_ref[0:FOX_WIDTH, :]))
        y1.append(x_ref[rows, :] + fox + _dot(go_ref[rows, :], wo_ref[FOX_WIDTH:, :]))
    h2 = jnp.concatenate([_rms(y, g2_ref[...]).astype(BF16) for y in y1], axis=0)
    for c in range(D_FF // chunk):
        cs = slice(c * chunk, (c + 1) * chunk)
        u = _dot(h2, wg_ref[:, cs])
        w = _dot(h2, wu_ref[:, cs])
        a_ref[:, cs] = (u * _sigmoid(u) * w).astype(BF16)
    y2 = [y + _dot(a_ref[rows, :], wd_ref[...]) for y, rows in zip(y1, halves)]
    for y, rows in zip(y2, halves):
        y_ref[rows, :] = _rms(y, gf_ref[...])


def _ffn(x2d, fo, go, prm, tm=512, chunk=256):
    n = x2d.shape[0]
    tm = min(tm, n)
    row = lambda w: pl.BlockSpec((tm, w), lambda i: (i, 0))
    const = lambda shape: pl.BlockSpec(shape, lambda i: (0, 0), pipeline_mode=pl.Buffered(1))
    fox_time_minor = fo.ndim == 4
    if fox_time_minor:
        tiles_per_seq = fo.shape[1]
        assert fo.shape[3] == tm
        fo_spec = pl.BlockSpec((None, None, FOX_WIDTH, tm),
                               lambda i: (i // tiles_per_seq, i % tiles_per_seq, 0, 0))
    else:
        fo_spec = row(FOX_WIDTH)
    return pl.pallas_call(
        functools.partial(_ffn_kernel, chunk=chunk, fox_time_minor=fox_time_minor),
        grid=(n // tm,),
        in_specs=[row(D_MODEL), fo_spec, row(GLA_VW), const((D_MODEL, D_MODEL)),
                  const((1, D_MODEL)), const((D_MODEL, D_FF)), const((D_MODEL, D_FF)),
                  const((D_FF, D_MODEL)), const((1, D_MODEL))],
        out_specs=row(D_MODEL),
        out_shape=jax.ShapeDtypeStruct((n, D_MODEL), F32),
        scratch_shapes=[pltpu.VMEM((tm, D_FF), BF16)],
        compiler_params=pltpu.CompilerParams(dimension_semantics=("arbitrary",),
                                             vmem_limit_bytes=VMEM_LIMIT),
        name="ffn",
    )(x2d, fo, go, prm['wo'], prm['g2'], prm['wg'], prm['wu'], prm['wd'], prm['gf'])


def _layer_params(layer, norm1_g, w_in, w_gate2, b_gate2, b_forget, gla_norm_g, w_out,
                  norm2_g, w_gate, w_up, w_down, final_norm_g):
    wt = jnp.transpose(w_in[layer])
    o_fl = 3 * FOX_WIDTH
    o_gq = o_fl + FOX_HEADS
    o_gg = o_gq + 2 * GLA_KW + GLA_VW
    o_gr = o_gg + GLA_GATE_RANK
    wc = jnp.zeros((LANES, D_MODEL), F32).at[:GLA_GATE_RANK].set(wt[o_gg:o_gr])
    wg2 = jnp.zeros((LANES, GLA_KW), F32).at[:GLA_GATE_RANK].set(w_gate2[layer])
    return dict(
        g1=norm1_g[layer].reshape(1, D_MODEL),
        wq=wt[:FOX_WIDTH].astype(BF16),
        wkv=wt[FOX_WIDTH:o_fl].astype(BF16),
        wb=jnp.concatenate([wt[o_gq:o_gg], wt[o_gr:]], axis=0).astype(BF16),
        wc=wc.astype(BF16),
        wfl=wt[o_fl:o_gq].astype(BF16),
        wg2=wg2.astype(BF16),
        bg2=b_gate2[layer].reshape(1, GLA_KW),
        bfc=b_forget[layer].reshape(FOX_HEADS, 1),
        gn=gla_norm_g[layer].reshape(1, GLA_VW),
        wo=w_out[layer].astype(BF16),
        g2=norm2_g[layer].reshape(1, D_MODEL),
        wg=w_gate[layer].astype(BF16),
        wu=w_up[layer].astype(BF16),
        wd=w_down[layer].astype(BF16),
        gf=final_norm_g.reshape(1, D_MODEL),
    )


def kernel(x_prompt, x_sample, cache_fox_k, cache_fox_v, cache_fox_logf, state_gla, norm1_g, w_in,
           w_gate2, b_gate2, b_forget, gla_norm_g, w_out, norm2_g, w_gate, w_up, w_down, final_norm_g):
    depth = w_in.shape[0]
    assert depth == 1, "the final rmsnorm is fused into the layer's ffn kernel"
    bp, tp_, _ = x_prompt.shape
    bs, ts, _ = x_sample.shape
    past = cache_fox_k.shape[2]
    layer = 0
    prm = _layer_params(layer, norm1_g, w_in, w_gate2, b_gate2, b_forget, gla_norm_g, w_out,
                        norm2_g, w_gate, w_up, w_down, final_norm_g)
    by_time = lambda a, b, t: a.reshape(FOX_HEADS, b, t).transpose(1, 2, 0)[None]

    xp = x_prompt.reshape(bp * tp_, D_MODEL)
    qt, kt_p, vt_p, ktok, lf_p, ct, gq, gk, gv, glog, gr = _proj(xp, bp, tp_, prm, True)
    fox_o = _fox_prompt(qt, ktok, vt_p, ct, bp, tp_)
    s0 = jnp.zeros((bp, GLA_HEADS, GLA_DK, GLA_DV), F32)
    gla_o, s_p = _gla(gq, gk, gv, glog, gr, s0, prm['gn'], bp, tp_)
    y_p = _ffn(xp, fox_o, gla_o, prm)

    xs = x_sample.reshape(bs * ts, D_MODEL)
    q, k_s, v_s, lf_s, ct, gq, gk, gv, glog, gr = _proj(xs, bs, ts, prm, False)
    cn = ct.reshape(FOX_HEADS, bs, ts).transpose(1, 0, 2)
    lft = cache_fox_logf[layer].astype(F32).transpose(0, 2, 1)
    ck = cache_fox_k[layer].transpose(0, 2, 3, 1)
    cv = cache_fox_v[layer].transpose(0, 2, 3, 1)
    fox_o = _fox_sample(q, k_s, v_s, cn, lft, ck, cv, bs, ts, past)
    gla_o, s_s = _gla(gq, gk, gv, glog, gr, state_gla[layer].astype(F32), prm['gn'], bs, ts)
    y_s = _ffn(xs, fox_o, gla_o, prm)

    heads = lambda a, b, t: a.reshape(1, b, t, FOX_HEADS, FOX_HEAD_DIM)
    return (y_p.reshape(bp, tp_, D_MODEL), y_s.reshape(bs, ts, D_MODEL),
            kt_p.transpose(0, 3, 1, 2)[None], vt_p.transpose(0, 3, 1, 2)[None],
            by_time(lf_p, bp, tp_), s_p[None],
            heads(k_s, bs, ts), heads(v_s, bs, ts), by_time(lf_s, bs, ts), s_s[None])
```

```python
import functools

import numpy as np
import jax
import jax.numpy as jnp
from jax import lax
from jax.experimental import pallas as pl
from jax.experimental.pallas import tpu as pltpu

D_MODEL = 1024
FOX_HEADS = 8
FOX_HEAD_DIM = 64
FOX_WIDTH = FOX_HEADS * FOX_HEAD_DIM
GLA_HEADS = 4
GLA_DK = 64
GLA_DV = 128
GLA_KW = GLA_HEADS * GLA_DK
GLA_VW = GLA_HEADS * GLA_DV
GLA_GATE_RANK = 16
GLA_GATE_TEMP = 16.0
D_FF = 2816
EPS = 1e-6

LANES = 128
LOG2E = 1.4426950408889634
VMEM_LIMIT = 56 * 1024 * 1024

F32 = jnp.float32
BF16 = jnp.bfloat16
NEG_BIG = -1e30


def _log_sigmoid(x):
    return jnp.minimum(x, 0.0) - jnp.log1p(jnp.exp(-jnp.abs(x)))


def _sigmoid(x):
    return 1.0 / (1.0 + jnp.exp(-x))


def _split3_f32(x):
    hi = x.astype(BF16).astype(F32)
    r = x - hi
    mid = r.astype(BF16).astype(F32)
    lo = (r - mid).astype(BF16).astype(F32)
    return hi, mid, lo


def _dot(a, b):
    return jnp.dot(a, b, preferred_element_type=F32)


def _dot_nt(a, b):
    return lax.dot_general(a, b, (((1,), (1,)), ((), ())), preferred_element_type=F32)


def _dot_tn(a, b):
    return lax.dot_general(a, b, (((0,), (0,)), ((), ())), preferred_element_type=F32)


def _rms(x, g):
    return x * lax.rsqrt(jnp.mean(x * x, axis=-1, keepdims=True) + EPS) * g


def _softmax_step(s, carry, pv):
    m, l, acc = carry
    m_new = jnp.maximum(m, jnp.max(s, axis=-1, keepdims=True))
    alpha = jnp.exp(m - m_new)
    pm = jnp.exp(s - m_new)
    l = alpha * l + jnp.sum(pm, axis=-1, keepdims=True)
    acc = alpha * acc + pv(pm.astype(BF16))
    return m_new, l, acc


def _proj_kernel(x_ref, g1_ref, wq_ref, wkv_ref, wb_ref, wc_ref, wfl_ref, wg2_ref, bg2_ref,
                 bfc_ref, tri_ref,
                 *rest, tiles_per_seq, time_minor):
    carry_ref = rest[-1]

    @pl.when(pl.program_id(0) % tiles_per_seq == 0)
    def _():
        carry_ref[...] = jnp.zeros_like(carry_ref)

    h = _rms(x_ref[...], g1_ref[...]).astype(BF16)
    tm = h.shape[0]
    scale = FOX_HEAD_DIM ** -0.5

    zc = _dot_nt(h, wc_ref[...])
    fl_t = _dot_nt(wfl_ref[...], h)

    if time_minor:
        (q_ref, k_ref, v_ref, lft_ref, ct_ref, gq_ref, gk_ref, gv_ref, glog_ref, gr_ref,
         carry_ref) = rest
        q_ref[...] = (_dot_nt(wq_ref[...], h) * (scale * LOG2E)).astype(BF16)
    else:
        (q_ref, k_ref, v_ref, lft_ref, ct_ref, gq_ref, gk_ref, gv_ref, glog_ref, gr_ref,
         carry_ref) = rest
        q_ref[...] = (_dot_nt(h, wq_ref[...]) * scale).astype(BF16)

    logf_t = _log_sigmoid(fl_t + bfc_ref[...])
    lft_ref[...] = logf_t
    parts = jnp.concatenate(_split3_f32(logf_t), axis=0).astype(BF16)
    gg = zc.astype(BF16)

    if time_minor:
        kvt = _dot_nt(wkv_ref[...], h)
        k_ref[...] = kvt[:FOX_WIDTH].reshape(FOX_HEADS, FOX_HEAD_DIM, tm)
        v_ref[...] = kvt[FOX_WIDTH:].reshape(FOX_HEADS, FOX_HEAD_DIM, tm)
    else:
        kv = _dot_nt(h, wkv_ref[...])
        k_ref[...] = kv[:, :FOX_WIDTH]
        v_ref[...] = kv[:, FOX_WIDTH:]

    gpre = _dot(gg, wg2_ref[...]) + bg2_ref[...]
    cs = _dot(parts, tri_ref[...])

    zb = _dot_nt(h, wb_ref[...])
    gq_ref[...] = zb[:, :GLA_KW]
    gk_ref[...] = zb[:, GLA_KW:2 * GLA_KW]
    gv_ref[...] = zb[:, 2 * GLA_KW:2 * GLA_KW + GLA_VW].astype(BF16)
    gr_ref[...] = zb[:, 2 * GLA_KW + GLA_VW:]

    glog_ref[...] = _log_sigmoid(gpre) * (1.0 / GLA_GATE_TEMP)

    ct = cs[0:8] + cs[8:16] + cs[16:24] + carry_ref[:, 0:1]
    ct_ref[...] = ct
    carry_ref[...] = jnp.broadcast_to(ct[:, tm - 1:], carry_ref.shape)


def _proj(x2d, batch, seq_len, prm, time_minor, tm=512):
    n = x2d.shape[0]
    tm = min(tm, n)
    tiles_per_seq = max(seq_len // tm, 1)
    idx = np.arange(tm)
    tri = ((idx[:, None] <= idx[None, :]) & (idx[:, None] // seq_len == idx[None, :] // seq_len))
    tri = jnp.asarray(tri.astype(np.float32), dtype=BF16)
    const = lambda a: pl.BlockSpec(a.shape, lambda i: (0, 0))
    row = lambda w: pl.BlockSpec((tm, w), lambda i: (i, 0))
    col = pl.BlockSpec((FOX_HEADS, tm), lambda i: (0, i))
    sds = jax.ShapeDtypeStruct
    if time_minor:
        kv_shape = sds((batch, FOX_HEADS, FOX_HEAD_DIM, seq_len), F32)
        kv_spec = pl.BlockSpec((None, FOX_HEADS, FOX_HEAD_DIM, tm),
                               lambda i: (i // tiles_per_seq, 0, 0, i % tiles_per_seq))
        fox = [(sds((batch, tiles_per_seq, FOX_WIDTH, tm), BF16),
                pl.BlockSpec((None, None, FOX_WIDTH, tm), lambda i: (i // tiles_per_seq, i % tiles_per_seq, 0, 0))),
               (kv_shape, kv_spec), (kv_shape, kv_spec)]
    else:
        fox = [(sds((n, FOX_WIDTH), BF16), row(FOX_WIDTH)),
               (sds((n, FOX_WIDTH), F32), row(FOX_WIDTH)),
               (sds((n, FOX_WIDTH), F32), row(FOX_WIDTH))]
    outs = fox + [
        (sds((FOX_HEADS, n), F32), col),
        (sds((FOX_HEADS, n), F32), col),
        (sds((n, GLA_KW), F32), row(GLA_KW)),
        (sds((n, GLA_KW), F32), row(GLA_KW)),
        (sds((n, GLA_VW), BF16), row(GLA_VW)),
        (sds((n, GLA_KW), F32), row(GLA_KW)),
        (sds((n, GLA_VW), F32), row(GLA_VW)),
    ]
    names = ('g1', 'wq', 'wkv', 'wb', 'wc', 'wfl', 'wg2', 'bg2', 'bfc')
    return pl.pallas_call(
        functools.partial(_proj_kernel, tiles_per_seq=tiles_per_seq, time_minor=time_minor),
        grid=(n // tm,),
        in_specs=[row(D_MODEL)] + [const(prm[k]) for k in names] + [const(tri)],
        out_specs=tuple(s for _, s in outs), out_shape=tuple(s for s, _ in outs),
        scratch_shapes=[pltpu.VMEM((FOX_HEADS, LANES), F32)],
        compiler_params=pltpu.CompilerParams(dimension_semantics=("arbitrary",),
                                             vmem_limit_bytes=VMEM_LIMIT),
        name="proj",
    )(x2d, *[prm[k] for k in names], tri)


def _fox_prompt_kernel(q_ref, k_ref, v_ref, c_ref, o_ref, kb_ref, vb_ref, s_ref, p_ref, acc_ref,
                       *, tk):
    p = pl.program_id(1)
    nq = q_ref.shape[0]
    nk = vb_ref.shape[1]
    hd = FOX_HEAD_DIM
    spare = (hd, 0)

    parts = jnp.concatenate(_split3_f32(c_ref[...] * (-LOG2E)), axis=0).astype(BF16)
    r = lax.broadcasted_iota(jnp.int32, (3 * FOX_HEADS, LANES), 0)
    ln = lax.broadcasted_iota(jnp.int32, (3 * FOX_HEADS, LANES), 1)
    k = k_ref[...].reshape(LANES, k_ref.shape[2]).T.astype(BF16)
    klane = lax.broadcasted_iota(jnp.int32, k.shape, 1)
    vrow = lax.broadcasted_iota(jnp.int32, (LANES, tk), 0)
    for hh in range(2):
        place = (r % FOX_HEADS == 2 * p + hh) & (ln == spare[hh] + r // FOX_HEADS)
        extra = _dot_tn(parts, jnp.where(place, 1.0, 0.0).astype(BF16))
        own = (klane < hd) if hh == 0 else (klane >= hd)
        kb_ref[hh] = jnp.where(own, k, extra.astype(BF16))
        for jj in range(nk):
            vt = v_ref[:, :, jj * tk:(jj + 1) * tk].reshape(LANES, tk)
            vb_ref[hh, jj] = jnp.where(vrow == spare[hh], 1.0, vt).astype(BF16)

    qrow = lax.broadcasted_iota(jnp.int32, (LANES, tk), 0)
    key = lax.broadcasted_iota(jnp.int32, (tk, tk), 0)
    qry = lax.broadcasted_iota(jnp.int32, (tk, tk), 1)
    causal = key <= qry

    chains = [(hh, part) for hh in range(2) for part in range(2)]
    everyone = list(range(len(chains)))
    second = [n for n in everyone if chains[n][1] == 1]

    def q_operands(qi):
        qs = []
        for hh, part in chains:
            q = q_ref[qi, :, part * tk:(part + 1) * tk]
            own = (qrow < hd) if hh == 0 else (qrow >= hd)
            ones = (qrow >= spare[hh]) & (qrow < spare[hh] + 3)
            qs.append(jnp.where(own, q, jnp.where(ones, 1.0, 0.0).astype(BF16)))
        return qs

    steps = []
    for qi in range(nq):
        steps += [(qi, j, everyone, None) for j in range(2 * qi)]
        steps += [(qi, 2 * qi, everyone, 0), (qi, 2 * qi + 1, second, 1)]

    qs_of = {}

    def scores(step, buf):
        qi, j, live, _ = step
        if qi not in qs_of:
            qs_of.clear()
            qs_of[qi] = q_operands(qi)
        for n in live:
            s_ref[buf, n] = _dot(kb_ref[chains[n][0], j * tk:(j + 1) * tk, :], qs_of[qi][n])

    def values(step, buf):
        _, j, live, _ = step
        return {n: _dot(vb_ref[chains[n][0], j], p_ref[buf, n]) for n in live}

    def finish(qi):
        for part in range(2):
            a0 = acc_ref[qi % 2, part]
            a1 = acc_ref[qi % 2, 2 + part]
            o_ref[qi, :, part * tk:(part + 1) * tk] = jnp.concatenate(
                [a0[:hd] / a0[hd:hd + 1], a1[hd:] / a1[0:1]], axis=0).astype(BF16)

    def accumulate(step, alphas, pv):
        qi, j, live, _ = step
        for n in live:
            acc_ref[qi % 2, n] = pv[n] if j == 0 else alphas[n] * acc_ref[qi % 2, n] + pv[n]

    ms = {}
    scores(steps[0], 0)
    prev, prev_alphas = None, None
    for t, step in enumerate(steps):
        qi, j, live, masked_part = step
        if t + 1 < len(steps):
            scores(steps[t + 1], (t + 1) % 2)
        pv = values(prev, (t - 1) % 2) if prev is not None else None
        alphas = {}
        for n in live:
            s = s_ref[t % 2, n]
            if chains[n][1] == masked_part:
                s = jnp.where(causal, s, -jnp.inf)
            m_new = jnp.max(s, axis=0, keepdims=True)
            if j > 0:
                m_new = jnp.maximum(ms[n], m_new)
                alphas[n] = jnp.exp2(ms[n] - m_new)
            p_ref[t % 2, n] = jnp.exp2(s - m_new).astype(BF16)
            ms[n] = m_new
        if prev is not None:
            accumulate(prev, prev_alphas, pv)
            if prev[0] != qi:
                finish(prev[0])
        prev, prev_alphas = step, alphas
    accumulate(prev, prev_alphas, values(prev, (len(steps) - 1) % 2))
    finish(prev[0])


def _fox_prompt(qt, kt, vt, ct, batch, seq, tk=256):
    tq = 2 * tk
    nq = seq // tq
    nk = seq // tk
    pairs = FOX_WIDTH // LANES
    assert qt.shape == (batch, nq, FOX_WIDTH, tq)
    q_spec = pl.BlockSpec((None, nq, LANES, tq), lambda b, p: (b, 0, p, 0))
    return pl.pallas_call(
        functools.partial(_fox_prompt_kernel, tk=tk),
        grid=(batch, pairs),
        in_specs=[q_spec,
                  pl.BlockSpec((None, 2, FOX_HEAD_DIM, seq), lambda b, p: (b, p, 0, 0)),
                  pl.BlockSpec((None, 2, FOX_HEAD_DIM, seq), lambda b, p: (b, p, 0, 0)),
                  pl.BlockSpec((FOX_HEADS, seq), lambda b, p: (0, b))],
        out_specs=q_spec,
        out_shape=jax.ShapeDtypeStruct((batch, nq, FOX_WIDTH, tq), BF16),
        scratch_shapes=[pltpu.VMEM((2, seq, LANES), BF16), pltpu.VMEM((2, nk, LANES, tk), BF16),
                        pltpu.VMEM((2, 4, tk, tk), F32), pltpu.VMEM((2, 4, tk, tk), BF16),
                        pltpu.VMEM((2, 4, LANES, tk), F32)],
        compiler_params=pltpu.CompilerParams(dimension_semantics=("arbitrary", "arbitrary"),
                                             vmem_limit_bytes=VMEM_LIMIT),
        name="fox_prompt",
    )(qt, kt, vt, ct)


def _fox_sample_kernel(q_ref, kn_ref, vn_ref, cn_ref, lft_ref, mlow_ref, ck_ref, cv_ref,
                       o_ref, qh_ref, suf_ref, m_ref, l_ref, acc_ref, *, tp, nt, tn):
    j = pl.program_id(1)
    blk = 2 * LANES
    hd = FOX_HEAD_DIM

    @pl.when(j == 0)
    def _init():
        for h in range(FOX_HEADS):
            qh_ref[h] = q_ref[:, h * hd:(h + 1) * hd]
        m_ref[...] = jnp.full(m_ref.shape, NEG_BIG, F32)
        l_ref[...] = jnp.zeros(l_ref.shape, F32)
        acc_ref[...] = jnp.zeros(acc_ref.shape, F32)
        carry = jnp.zeros((FOX_HEADS, 1), F32)
        per_tile = tp // blk
        for b in reversed(range(nt * per_tile)):
            x = lft_ref[:, b * blk:(b + 1) * blk]
            parts = jnp.concatenate(_split3_f32(x), axis=0).astype(BF16)
            y = _dot(parts, mlow_ref[...])
            off = (b % per_tile) * blk
            suf_ref[b // per_tile, :, off:off + blk] = y[0:8] + y[8:16] + y[16:24] + carry
            carry = carry + jnp.sum(x, axis=1, keepdims=True)

    def update(s, pv):
        m, l, acc = _softmax_step(s, (m_ref[...], l_ref[...], acc_ref[...]), pv)
        m_ref[...] = m
        l_ref[...] = l
        acc_ref[...] = acc

    bmm = lambda a, b, ca, cb: lax.dot_general(a, b, (((ca,), (cb,)), ((0,), (0,))),
                                               preferred_element_type=F32)
    qh = qh_ref[...]

    kt = ck_ref[...].astype(BF16)
    vt = cv_ref[...].astype(BF16)
    s = bmm(qh, kt, 2, 1).reshape(FOX_HEADS * tn, tp) + jnp.repeat(suf_ref[j], tn, axis=0)
    update(s.reshape(FOX_HEADS, tn, tp), lambda pm: bmm(pm, vt, 2, 2))

    @pl.when(j == nt - 1)
    def _fin():
        per_head = lambda ref: jnp.stack([ref[:, h * hd:(h + 1) * hd] for h in range(FOX_HEADS)],
                                         axis=0).astype(BF16)
        kn = per_head(kn_ref)
        vn = per_head(vn_ref)
        r = lax.broadcasted_iota(jnp.int32, (FOX_HEADS, tn, tn), 1)
        c = lax.broadcasted_iota(jnp.int32, (FOX_HEADS, tn, tn), 2)
        s = jnp.where(c <= r, bmm(qh, kn, 2, 2) - cn_ref[...][:, None, :], -jnp.inf)
        update(s, lambda pm: bmm(pm, vn, 2, 1))
        o = acc_ref[...] / l_ref[...]
        o_ref[...] = jnp.concatenate([o[h] for h in range(FOX_HEADS)], axis=1).astype(BF16)


def _fox_sample(q, kn, vn, cn, lft, ck, cv, batch, tn, past, tp=4096):
    nt = past // tp
    blk = 2 * LANES
    idx = np.arange(blk)
    mlow = jnp.asarray((idx[:, None] > idx[None, :]).astype(np.float32), dtype=BF16)
    per_b = lambda w: pl.BlockSpec((tn, w), lambda b, j: (b, 0))
    cache = pl.BlockSpec((None, FOX_HEADS, FOX_HEAD_DIM, tp), lambda b, j: (b, 0, 0, j))
    return pl.pallas_call(
        functools.partial(_fox_sample_kernel, tp=tp, nt=nt, tn=tn),
        grid=(batch, nt),
        in_specs=[per_b(FOX_WIDTH), per_b(FOX_WIDTH), per_b(FOX_WIDTH),
                  pl.BlockSpec((None, FOX_HEADS, tn), lambda b, j: (b, 0, 0)),
                  pl.BlockSpec((None, FOX_HEADS, past), lambda b, j: (b, 0, 0)),
                  pl.BlockSpec((blk, blk), lambda b, j: (0, 0)),
                  cache, cache],
        out_specs=per_b(FOX_WIDTH),
        out_shape=jax.ShapeDtypeStruct((batch * tn, FOX_WIDTH), BF16),
        scratch_shapes=[pltpu.VMEM((FOX_HEADS, tn, FOX_HEAD_DIM), BF16),
                        pltpu.VMEM((nt, FOX_HEADS, tp), F32),
                        pltpu.VMEM((FOX_HEADS, tn, 1), F32), pltpu.VMEM((FOX_HEADS, tn, 1), F32),
                        pltpu.VMEM((FOX_HEADS, tn, FOX_HEAD_DIM), F32)],
        compiler_params=pltpu.CompilerParams(dimension_semantics=("arbitrary", "arbitrary"),
                                             vmem_limit_bytes=VMEM_LIMIT),
        name="fox_sample",
    )(q, kn, vn, cn, lft, mlow, ck, cv)


def _gla_kernel(q_ref, k_ref, v_ref, g_ref, r_ref, s0_ref, gn_ref, w_ref, lv_ref,
                o_ref, s_ref, st_ref, *, tb, levels, blocks, seqs, carried):
    to_work = lambda s0: s0.reshape(GLA_KW, GLA_DV).T
    from_work = lambda st: st.T.reshape(GLA_HEADS, GLA_DK, GLA_DV)
    if carried:
        @pl.when(pl.program_id(1) == 0)
        def _():
            st_ref[...] = to_work(s0_ref[0])

    for sq in range(seqs):
        st = st_ref[...] if carried else to_work(s0_ref[sq])
        for blk in range(blocks):
            rows = lambda ref: ref.at[pl.ds((sq * blocks + blk) * tb, tb), :]
            st = _gla_block(rows(q_ref), rows(k_ref), rows(v_ref), rows(g_ref), rows(r_ref), gn_ref,
                            w_ref, lv_ref, rows(o_ref), st, tb=tb, levels=levels)
        if carried:
            st_ref[...] = st

            @pl.when(pl.program_id(1) == pl.num_programs(1) - 1)
            def _():
                s_ref[0] = from_work(st)
        else:
            s_ref[sq] = from_work(st)


def _gla_block(q_ref, k_ref, v_ref, g_ref, r_ref, gn_ref, w_ref, lv_ref, o_ref, st, *, tb, levels):
    half = LANES // 2
    r = r_ref[...]
    gate = gn_ref[...] * (r * _sigmoid(r))

    g3 = jnp.concatenate(_split3_f32(g_ref[...] * LOG2E), axis=0).astype(BF16)
    rows = w_ref.shape[0] // 2
    dec = jnp.exp2(jnp.concatenate([_dot(w_ref[0:rows, :], g3), _dot(w_ref[rows:, :], g3)], axis=0))
    from_start = dec[0:tb]
    to_end = dec[tb:2 * tb]

    q = q_ref[...] * (GLA_DK ** -0.5)
    k = k_ref[...]
    v = v_ref[...]
    row = lax.broadcasted_iota(jnp.int32, (tb, GLA_KW), 0)
    low = lax.broadcasted_iota(jnp.int32, (tb, LANES), 1) < half
    lv = lv_ref[...]

    def pair_scores(xq, xk):
        outs = []
        for p in range(GLA_HEADS // 2):
            a = xq[:, p * LANES:(p + 1) * LANES]
            zero = jnp.zeros_like(a)
            lhs = jnp.concatenate([jnp.where(low, a, zero), jnp.where(low, zero, a)], axis=0)
            outs.append(_dot_nt(lhs, xk[:, p * LANES:(p + 1) * LANES]))
        return outs

    here = lv == -1
    a = [jnp.where(here, r_, 0.0) for r_ in pair_scores(q.astype(BF16), k.astype(BF16))]
    for l in range(levels):
        upper = ((row >> l) & 1) == 1
        x = (jnp.where(upper, q, k) * dec[(l + 2) * tb:(l + 3) * tb]).astype(BF16)
        here = lv == l
        a = [jnp.where(here, r_, a_) for r_, a_ in zip(pair_scores(x, x), a)]
    o = jnp.concatenate(
        [_dot(a[h // 2][(h % 2) * tb:(h % 2 + 1) * tb].astype(BF16), v[:, h * GLA_DV:(h + 1) * GLA_DV])
         for h in range(GLA_HEADS)], axis=1)

    lane = lax.broadcasted_iota(jnp.int32, (tb, GLA_KW), 1)
    head_sel = [(lane >= h * GLA_DK) & (lane < (h + 1) * GLA_DK) for h in range(GLA_HEADS)]
    qt = (q * from_start).astype(BF16)
    kt = (k * to_end).astype(BF16)
    zero = jnp.zeros_like(qt)
    q4 = jnp.concatenate([jnp.where(sel, qt, zero) for sel in head_sel], axis=0)
    oi = _dot_nt(q4, st.astype(BF16))
    o = o + jnp.concatenate([oi[h * tb:(h + 1) * tb] for h in range(GLA_HEADS)], axis=1)
    upd = None
    for h in range(GLA_HEADS):
        u = _dot_tn(v[:, h * GLA_DV:(h + 1) * GLA_DV], jnp.where(head_sel[h], kt, zero))
        upd = u if upd is None else upd + u
    st = from_start[tb - 1:tb, :] * st + upd

    outs = []
    for h in range(GLA_HEADS):
        oh = o[:, h * GLA_DV:(h + 1) * GLA_DV]
        outs.append(oh * lax.rsqrt(jnp.mean(oh * oh, axis=-1, keepdims=True) + EPS))
    o_ref[...] = (jnp.concatenate(outs, axis=1) * gate).astype(BF16)
    return st


def _gla(gq, gk, gv, glog, gr, s0, gn, batch, seq, tb=128, rows_per_step=512):
    tb = min(tb, seq)
    step_rows = min(rows_per_step, batch * seq)
    carried = seq > step_rows
    seqs = 1 if carried else step_rows // seq
    per_seq = (step_rows if carried else seq) // tb
    nt = seq // (tb * per_seq)
    n = batch * seq
    levels = tb.bit_length() - 1
    assert tb == 1 << levels
    ti = np.arange(tb)[:, None]
    si = np.arange(tb)[None, :]
    blocks = [si <= ti, si > ti]
    for l in range(levels):
        m = 1 << l
        mid = ti - ti % (2 * m) + m - 1
        upper = ti % (2 * m) >= m
        blocks.append(np.where(upper, (si > mid) & (si <= ti), (si > ti) & (si <= mid)))
    w = np.concatenate(blocks, axis=0).astype(np.float32)
    w = jnp.asarray(np.concatenate([w, w, w], axis=1), dtype=BF16)
    x = ti ^ si
    lv = np.where(si < ti, np.floor(np.log2(np.maximum(x, 1))).astype(np.int32),
                  np.where(si == ti, -1, -2)).astype(np.int32)
    lv = jnp.asarray(np.concatenate([lv, lv], axis=0))
    row = lambda w_: pl.BlockSpec((step_rows, w_), lambda b, t: (b * nt + t, 0))
    state = pl.BlockSpec((seqs, GLA_HEADS, GLA_DK, GLA_DV), lambda b, t: (b, 0, 0, 0))
    const = lambda shape: pl.BlockSpec(shape, lambda b, t: (0, 0))
    return pl.pallas_call(
        functools.partial(_gla_kernel, tb=tb, levels=levels, blocks=per_seq, seqs=seqs, carried=carried),
        grid=(batch // seqs, nt),
        in_specs=[row(GLA_KW), row(GLA_KW), row(GLA_VW), row(GLA_KW), row(GLA_VW), state,
                  const((1, GLA_VW)), const(w.shape), const(lv.shape)],
        out_specs=(row(GLA_VW), state),
        out_shape=(jax.ShapeDtypeStruct((n, GLA_VW), BF16),
                   jax.ShapeDtypeStruct((batch, GLA_HEADS, GLA_DK, GLA_DV), F32)),
        scratch_shapes=[pltpu.VMEM((GLA_DV, GLA_KW), F32)],
        compiler_params=pltpu.CompilerParams(dimension_semantics=("arbitrary", "arbitrary"),
                                             vmem_limit_bytes=VMEM_LIMIT),
        name="gla",
    )(gq, gk, gv, glog, gr, s0, gn, w, lv)


def _ffn_kernel(x_ref, fo_ref, go_ref, wo_ref, g2_ref, wg_ref, wu_ref, wd_ref, gf_ref,
                y_ref, a_ref, *, chunk, fox_time_minor):
    tm = x_ref.shape[0]
    halves = [slice(0, tm // 2), slice(tm // 2, tm)] if tm % 32 == 0 else [slice(0, tm)]
    y1 = []
    for rows in halves:
        fox = (_dot_tn(fo_ref[:, rows], wo_ref[0:FOX_WIDTH, :]) if fox_time_minor
               else _dot(fo_ref[rows, :], wo_ref[0:FOX_WIDTH, :]))
        y1.append(x_ref[rows, :] + fox + _dot(go_ref[rows, :], wo_ref[FOX_WIDTH:, :]))
    h2 = jnp.concatenate([_rms(y, g2_ref[...]).astype(BF16) for y in y1], axis=0)
    for c in range(D_FF // chunk):
        cs = slice(c * chunk, (c + 1) * chunk)
        u = _dot(h2, wg_ref[:, cs])
        w = _dot(h2, wu_ref[:, cs])
        a_ref[:, cs] = (u * _sigmoid(u) * w).astype(BF16)
    y2 = [y + _dot(a_ref[rows, :], wd_ref[...]) for y, rows in zip(y1, halves)]
    for y, rows in zip(y2, halves):
        y_ref[rows, :] = _rms(y, gf_ref[...])


def _ffn(x2d, fo, go, prm, tm=512, chunk=256):
    n = x2d.shape[0]
    tm = min(tm, n)
    row = lambda w: pl.BlockSpec((tm, w), lambda i: (i, 0))
    const = lambda shape: pl.BlockSpec(shape, lambda i: (0, 0), pipeline_mode=pl.Buffered(1))
    fox_time_minor = fo.ndim == 4
    if fox_time_minor:
        tiles_per_seq = fo.shape[1]
        assert fo.shape[3] == tm
        fo_spec = pl.BlockSpec((None, None, FOX_WIDTH, tm),
                               lambda i: (i // tiles_per_seq, i % tiles_per_seq, 0, 0))
    else:
        fo_spec = row(FOX_WIDTH)
    return pl.pallas_call(
        functools.partial(_ffn_kernel, chunk=chunk, fox_time_minor=fox_time_minor),
        grid=(n // tm,),
        in_specs=[row(D_MODEL), fo_spec, row(GLA_VW), const((D_MODEL, D_MODEL)),
                  const((1, D_MODEL)), const((D_MODEL, D_FF)), const((D_MODEL, D_FF)),
                  const((D_FF, D_MODEL)), const((1, D_MODEL))],
        out_specs=row(D_MODEL),
        out_shape=jax.ShapeDtypeStruct((n, D_MODEL), F32),
        scratch_shapes=[pltpu.VMEM((tm, D_FF), BF16)],
        compiler_params=pltpu.CompilerParams(dimension_semantics=("arbitrary",),
                                             vmem_limit_bytes=VMEM_LIMIT),
        name="ffn",
    )(x2d, fo, go, prm['wo'], prm['g2'], prm['wg'], prm['wu'], prm['wd'], prm['gf'])


def _layer_params(layer, norm1_g, w_in, w_gate2, b_gate2, b_forget, gla_norm_g, w_out,
                  norm2_g, w_gate, w_up, w_down, final_norm_g):
    wt = jnp.transpose(w_in[layer])
    o_fl = 3 * FOX_WIDTH
    o_gq = o_fl + FOX_HEADS
    o_gg = o_gq + 2 * GLA_KW + GLA_VW
    o_gr = o_gg + GLA_GATE_RANK
    wc = jnp.zeros((LANES, D_MODEL), F32).at[:GLA_GATE_RANK].set(wt[o_gg:o_gr])
    wg2 = jnp.zeros((LANES, GLA_KW), F32).at[:GLA_GATE_RANK].set(w_gate2[layer])
    return dict(
        g1=norm1_g[layer].reshape(1, D_MODEL),
        wq=wt[:FOX_WIDTH].astype(BF16),
        wkv=wt[FOX_WIDTH:o_fl].astype(BF16),
        wb=jnp.concatenate([wt[o_gq:o_gg], wt[o_gr:]], axis=0).astype(BF16),
        wc=wc.astype(BF16),
        wfl=wt[o_fl:o_gq].astype(BF16),
        wg2=wg2.astype(BF16),
        bg2=b_gate2[layer].reshape(1, GLA_KW),
        bfc=b_forget[layer].reshape(FOX_HEADS, 1),
        gn=gla_norm_g[layer].reshape(1, GLA_VW),
        wo=w_out[layer].astype(BF16),
        g2=norm2_g[layer].reshape(1, D_MODEL),
        wg=w_gate[layer].astype(BF16),
        wu=w_up[layer].astype(BF16),
        wd=w_down[layer].astype(BF16),
        gf=final_norm_g.reshape(1, D_MODEL),
    )


def kernel(x_prompt, x_sample, cache_fox_k, cache_fox_v, cache_fox_logf, state_gla, norm1_g, w_in,
           w_gate2, b_gate2, b_forget, gla_norm_g, w_out, norm2_g, w_gate, w_up, w_down, final_norm_g):
    depth = w_in.shape[0]
    assert depth == 1, "the final rmsnorm is fused into the layer's ffn kernel"
    bp, tp_, _ = x_prompt.shape
    bs, ts, _ = x_sample.shape
    past = cache_fox_k.shape[2]
    layer = 0
    prm = _layer_params(layer, norm1_g, w_in, w_gate2, b_gate2, b_forget, gla_norm_g, w_out,
                        norm2_g, w_gate, w_up, w_down, final_norm_g)
    by_time = lambda a, b, t: a.reshape(FOX_HEADS, b, t).transpose(1, 2, 0)[None]

    xp = x_prompt.reshape(bp * tp_, D_MODEL)
    qt, kt_p, vt_p, lf_p, ct, gq, gk, gv, glog, gr = _proj(xp, bp, tp_, prm, True)
    fox_o = _fox_prompt(qt, kt_p, vt_p, ct, bp, tp_)
    s0 = jnp.zeros((bp, GLA_HEADS, GLA_DK, GLA_DV), F32)
    gla_o, s_p = _gla(gq, gk, gv, glog, gr, s0, prm['gn'], bp, tp_)
    y_p = _ffn(xp, fox_o, gla_o, prm)

    xs = x_sample.reshape(bs * ts, D_MODEL)
    q, k_s, v_s, lf_s, ct, gq, gk, gv, glog, gr = _proj(xs, bs, ts, prm, False)
    cn = ct.reshape(FOX_HEADS, bs, ts).transpose(1, 0, 2)
    lft = cache_fox_logf[layer].astype(F32).transpose(0, 2, 1)
    ck = cache_fox_k[layer].transpose(0, 2, 3, 1)
    cv = cache_fox_v[layer].transpose(0, 2, 3, 1)
    fox_o = _fox_sample(q, k_s, v_s, cn, lft, ck, cv, bs, ts, past)
    gla_o, s_s = _gla(gq, gk, gv, glog, gr, state_gla[layer].astype(F32), prm['gn'], bs, ts)
    y_s = _ffn(xs, fox_o, gla_o, prm)

    heads = lambda a, b, t: a.reshape(1, b, t, FOX_HEADS, FOX_HEAD_DIM)
    return (y_p.reshape(bp, tp_, D_MODEL), y_s.reshape(bs, ts, D_MODEL),
            kt_p.transpose(0, 3, 1, 2)[None], vt_p.transpose(0, 3, 1, 2)[None],
            by_time(lf_p, bp, tp_), s_p[None],
            heads(k_s, bs, ts), heads(v_s, bs, ts), by_time(lf_s, bs, ts), s_s[None])
```

```python
import functools

import numpy as np
import jax
import jax.numpy as jnp
from jax import lax
from jax.experimental import pallas as pl
from jax.experimental.pallas import tpu as pltpu

D_MODEL = 1024
FOX_HEADS = 8
FOX_HEAD_DIM = 64
FOX_WIDTH = FOX_HEADS * FOX_HEAD_DIM
GLA_HEADS = 4
GLA_DK = 64
GLA_DV = 128
GLA_KW = GLA_HEADS * GLA_DK
GLA_VW = GLA_HEADS * GLA_DV
GLA_GATE_RANK = 16
GLA_GATE_TEMP = 16.0
D_FF = 2816
EPS = 1e-6

LANES = 128
LOG2E = 1.4426950408889634
VMEM_LIMIT = 56 * 1024 * 1024

F32 = jnp.float32
BF16 = jnp.bfloat16
NEG_BIG = -1e30


def _log_sigmoid(x):
    return jnp.minimum(x, 0.0) - jnp.log1p(jnp.exp(-jnp.abs(x)))


def _sigmoid(x):
    return 1.0 / (1.0 + jnp.exp(-x))


def _split3_f32(x):
    hi = x.astype(BF16).astype(F32)
    r = x - hi
    mid = r.astype(BF16).astype(F32)
    lo = (r - mid).astype(BF16).astype(F32)
    return hi, mid, lo


def _dot(a, b):
    return jnp.dot(a, b, preferred_element_type=F32)


def _dot_nt(a, b):
    return lax.dot_general(a, b, (((1,), (1,)), ((), ())), preferred_element_type=F32)


def _dot_tn(a, b):
    return lax.dot_general(a, b, (((0,), (0,)), ((), ())), preferred_element_type=F32)


def _rms(x, g):
    return x * lax.rsqrt(jnp.mean(x * x, axis=-1, keepdims=True) + EPS) * g


def _softmax_step(s, carry, pv):
    m, l, acc = carry
    m_new = jnp.maximum(m, jnp.max(s, axis=-1, keepdims=True))
    alpha = jnp.exp(m - m_new)
    pm = jnp.exp(s - m_new)
    l = alpha * l + jnp.sum(pm, axis=-1, keepdims=True)
    acc = alpha * acc + pv(pm.astype(BF16))
    return m_new, l, acc


def _proj_kernel(x_ref, g1_ref, wq_ref, wkv_ref, wb_ref, wc_ref, wfl_ref, wg2_ref, bg2_ref,
                 bfc_ref, tri_ref,
                 *rest, tiles_per_seq, time_minor):
    carry_ref = rest[-1]

    @pl.when(pl.program_id(0) % tiles_per_seq == 0)
    def _():
        carry_ref[...] = jnp.zeros_like(carry_ref)

    h = _rms(x_ref[...], g1_ref[...]).astype(BF16)
    tm = h.shape[0]
    scale = FOX_HEAD_DIM ** -0.5

    zc = _dot_nt(h, wc_ref[...])
    fl_t = _dot_nt(wfl_ref[...], h)

    if time_minor:
        (q_ref, k_ref, v_ref, lft_ref, ct_ref, gq_ref, gk_ref, gv_ref, glog_ref, gr_ref,
         carry_ref) = rest
        q_ref[...] = (_dot_nt(wq_ref[...], h) * (scale * LOG2E)).astype(BF16)
    else:
        (q_ref, k_ref, v_ref, lft_ref, ct_ref, gq_ref, gk_ref, gv_ref, glog_ref, gr_ref,
         carry_ref) = rest
        q_ref[...] = (_dot_nt(h, wq_ref[...]) * scale).astype(BF16)

    logf_t = _log_sigmoid(fl_t + bfc_ref[...])
    lft_ref[...] = logf_t
    parts = jnp.concatenate(_split3_f32(logf_t), axis=0).astype(BF16)
    gg = zc.astype(BF16)

    if time_minor:
        kvt = _dot_nt(wkv_ref[...], h)
        k_ref[...] = kvt[:FOX_WIDTH].reshape(FOX_HEADS, FOX_HEAD_DIM, tm)
        v_ref[...] = kvt[FOX_WIDTH:].reshape(FOX_HEADS, FOX_HEAD_DIM, tm)
    else:
        kv = _dot_nt(h, wkv_ref[...])
        k_ref[...] = kv[:, :FOX_WIDTH]
        v_ref[...] = kv[:, FOX_WIDTH:]

    gpre = _dot(gg, wg2_ref[...]) + bg2_ref[...]
    cs = _dot(parts, tri_ref[...])

    zb = _dot_nt(h, wb_ref[...])
    gq_ref[...] = zb[:, :GLA_KW]
    gk_ref[...] = zb[:, GLA_KW:2 * GLA_KW]
    gv_ref[...] = zb[:, 2 * GLA_KW:2 * GLA_KW + GLA_VW].astype(BF16)
    gr_ref[...] = zb[:, 2 * GLA_KW + GLA_VW:]

    glog_ref[...] = _log_sigmoid(gpre) * (1.0 / GLA_GATE_TEMP)

    ct = cs[0:8] + cs[8:16] + cs[16:24] + carry_ref[:, 0:1]
    ct_ref[...] = ct
    carry_ref[...] = jnp.broadcast_to(ct[:, tm - 1:], carry_ref.shape)


def _proj(x2d, batch, seq_len, prm, time_minor, tm=512):
    n = x2d.shape[0]
    tm = min(tm, n)
    tiles_per_seq = max(seq_len // tm, 1)
    idx = np.arange(tm)
    tri = ((idx[:, None] <= idx[None, :]) & (idx[:, None] // seq_len == idx[None, :] // seq_len))
    tri = jnp.asarray(tri.astype(np.float32), dtype=BF16)
    const = lambda a: pl.BlockSpec(a.shape, lambda i: (0, 0))
    row = lambda w: pl.BlockSpec((tm, w), lambda i: (i, 0))
    col = pl.BlockSpec((FOX_HEADS, tm), lambda i: (0, i))
    sds = jax.ShapeDtypeStruct
    if time_minor:
        kv_shape = sds((batch, FOX_HEADS, FOX_HEAD_DIM, seq_len), F32)
        kv_spec = pl.BlockSpec((None, FOX_HEADS, FOX_HEAD_DIM, tm),
                               lambda i: (i // tiles_per_seq, 0, 0, i % tiles_per_seq))
        fox = [(sds((batch, tiles_per_seq, FOX_WIDTH, tm), BF16),
                pl.BlockSpec((None, None, FOX_WIDTH, tm), lambda i: (i // tiles_per_seq, i % tiles_per_seq, 0, 0))),
               (kv_shape, kv_spec), (kv_shape, kv_spec)]
    else:
        fox = [(sds((n, FOX_WIDTH), BF16), row(FOX_WIDTH)),
               (sds((n, FOX_WIDTH), F32), row(FOX_WIDTH)),
               (sds((n, FOX_WIDTH), F32), row(FOX_WIDTH))]
    outs = fox + [
        (sds((FOX_HEADS, n), F32), col),
        (sds((FOX_HEADS, n), F32), col),
        (sds((n, GLA_KW), F32), row(GLA_KW)),
        (sds((n, GLA_KW), F32), row(GLA_KW)),
        (sds((n, GLA_VW), BF16), row(GLA_VW)),
        (sds((n, GLA_KW), F32), row(GLA_KW)),
        (sds((n, GLA_VW), F32), row(GLA_VW)),
    ]
    names = ('g1', 'wq', 'wkv', 'wb', 'wc', 'wfl', 'wg2', 'bg2', 'bfc')
    return pl.pallas_call(
        functools.partial(_proj_kernel, tiles_per_seq=tiles_per_seq, time_minor=time_minor),
        grid=(n // tm,),
        in_specs=[row(D_MODEL)] + [const(prm[k]) for k in names] + [const(tri)],
        out_specs=tuple(s for _, s in outs), out_shape=tuple(s for s, _ in outs),
        scratch_shapes=[pltpu.VMEM((FOX_HEADS, LANES), F32)],
        compiler_params=pltpu.CompilerParams(dimension_semantics=("arbitrary",),
                                             vmem_limit_bytes=VMEM_LIMIT),
        name="proj",
    )(x2d, *[prm[k] for k in names], tri)


def _fox_prompt_kernel(q_ref, k_ref, v_ref, c_ref, o_ref, kb_ref, vb_ref, s_ref, p_ref, acc_ref,
                       *, tk):
    p = pl.program_id(1)
    nq = q_ref.shape[0]
    nk = vb_ref.shape[1]
    hd = FOX_HEAD_DIM
    spare = (hd, 0)

    parts = jnp.concatenate(_split3_f32(c_ref[...] * (-LOG2E)), axis=0).astype(BF16)
    r = lax.broadcasted_iota(jnp.int32, (3 * FOX_HEADS, LANES), 0)
    ln = lax.broadcasted_iota(jnp.int32, (3 * FOX_HEADS, LANES), 1)
    k = k_ref[...].reshape(LANES, k_ref.shape[2]).T.astype(BF16)
    klane = lax.broadcasted_iota(jnp.int32, k.shape, 1)
    vrow = lax.broadcasted_iota(jnp.int32, (LANES, tk), 0)
    for hh in range(2):
        place = (r % FOX_HEADS == 2 * p + hh) & (ln == spare[hh] + r // FOX_HEADS)
        extra = _dot_tn(parts, jnp.where(place, 1.0, 0.0).astype(BF16))
        own = (klane < hd) if hh == 0 else (klane >= hd)
        kb_ref[hh] = jnp.where(own, k, extra.astype(BF16))
        for jj in range(nk):
            vt = v_ref[:, :, jj * tk:(jj + 1) * tk].reshape(LANES, tk)
            vb_ref[hh, jj] = jnp.where(vrow == spare[hh], 1.0, vt).astype(BF16)

    qrow = lax.broadcasted_iota(jnp.int32, (LANES, tk), 0)
    key = lax.broadcasted_iota(jnp.int32, (tk, tk), 0)
    qry = lax.broadcasted_iota(jnp.int32, (tk, tk), 1)
    causal = key <= qry

    chains = [(hh, part) for hh in range(2) for part in range(2)]
    everyone = list(range(len(chains)))
    second = [n for n in everyone if chains[n][1] == 1]

    def q_operands(qi):
        qs = []
        for hh, part in chains:
            q = q_ref[qi, :, part * tk:(part + 1) * tk]
            own = (qrow < hd) if hh == 0 else (qrow >= hd)
            ones = (qrow >= spare[hh]) & (qrow < spare[hh] + 3)
            qs.append(jnp.where(own, q, jnp.where(ones, 1.0, 0.0).astype(BF16)))
        return qs

    steps = []
    for qi in range(nq):
        steps += [(qi, j, everyone, None) for j in range(2 * qi)]
        steps += [(qi, 2 * qi, everyone, 0), (qi, 2 * qi + 1, second, 1)]

    qs_of = {}

    def scores(step, buf):
        qi, j, live, _ = step
        if qi not in qs_of:
            qs_of.clear()
            qs_of[qi] = q_operands(qi)
        for n in live:
            s_ref[buf, n] = _dot(kb_ref[chains[n][0], j * tk:(j + 1) * tk, :], qs_of[qi][n])

    def values(step, buf):
        _, j, live, _ = step
        return {n: _dot(vb_ref[chains[n][0], j], p_ref[buf, n]) for n in live}

    def finish(qi):
        for part in range(2):
            a0 = acc_ref[qi % 2, part]
            a1 = acc_ref[qi % 2, 2 + part]
            o_ref[qi, :, part * tk:(part + 1) * tk] = jnp.concatenate(
                [a0[:hd] / a0[hd:hd + 1], a1[hd:] / a1[0:1]], axis=0).astype(BF16)

    def accumulate(step, alphas, pv):
        qi, j, live, _ = step
        for n in live:
            acc_ref[qi % 2, n] = pv[n] if j == 0 else alphas[n] * acc_ref[qi % 2, n] + pv[n]

    ms = {}
    scores(steps[0], 0)
    prev, prev_alphas = None, None
    for t, step in enumerate(steps):
        qi, j, live, masked_part = step
        if t + 1 < len(steps):
            scores(steps[t + 1], (t + 1) % 2)
        pv = values(prev, (t - 1) % 2) if prev is not None else None
        alphas = {}
        for n in live:
            s = s_ref[t % 2, n]
            if chains[n][1] == masked_part:
                s = jnp.where(causal, s, -jnp.inf)
            m_new = jnp.max(s, axis=0, keepdims=True)
            if j > 0:
                m_new = jnp.maximum(ms[n], m_new)
                alphas[n] = jnp.exp2(ms[n] - m_new)
            p_ref[t % 2, n] = jnp.exp2(s - m_new).astype(BF16)
            ms[n] = m_new
        if prev is not None:
            accumulate(prev, prev_alphas, pv)
            if prev[0] != qi:
                finish(prev[0])
        prev, prev_alphas = step, alphas
    accumulate(prev, prev_alphas, values(prev, (len(steps) - 1) % 2))
    finish(prev[0])


def _fox_prompt(qt, kt, vt, ct, batch, seq, tk=256):
    tq = 2 * tk
    nq = seq // tq
    nk = seq // tk
    pairs = FOX_WIDTH // LANES
    assert qt.shape == (batch, nq, FOX_WIDTH, tq)
    q_spec = pl.BlockSpec((None, nq, LANES, tq), lambda b, p: (b, 0, p, 0))
    return pl.pallas_call(
        functools.partial(_fox_prompt_kernel, tk=tk),
        grid=(batch, pairs),
        in_specs=[q_spec,
                  pl.BlockSpec((None, 2, FOX_HEAD_DIM, seq), lambda b, p: (b, p, 0, 0)),
                  pl.BlockSpec((None, 2, FOX_HEAD_DIM, seq), lambda b, p: (b, p, 0, 0)),
                  pl.BlockSpec((FOX_HEADS, seq), lambda b, p: (0, b))],
        out_specs=q_spec,
        out_shape=jax.ShapeDtypeStruct((batch, nq, FOX_WIDTH, tq), BF16),
        scratch_shapes=[pltpu.VMEM((2, seq, LANES), BF16), pltpu.VMEM((2, nk, LANES, tk), BF16),
                        pltpu.VMEM((2, 4, tk, tk), F32), pltpu.VMEM((2, 4, tk, tk), BF16),
                        pltpu.VMEM((2, 4, LANES, tk), F32)],
        compiler_params=pltpu.CompilerParams(dimension_semantics=("arbitrary", "arbitrary"),
                                             vmem_limit_bytes=VMEM_LIMIT),
        name="fox_prompt",
    )(qt, kt, vt, ct)


def _fox_sample_kernel(q_ref, kn_ref, vn_ref, cn_ref, lft_ref, mlow_ref, ck_ref, cv_ref,
                       o_ref, qh_ref, suf_ref, m_ref, l_ref, acc_ref, *, tp, nt, tn):
    j = pl.program_id(1)
    blk = 2 * LANES
    hd = FOX_HEAD_DIM

    @pl.when(j == 0)
    def _init():
        for h in range(FOX_HEADS):
            qh_ref[h] = q_ref[:, h * hd:(h + 1) * hd]
        m_ref[...] = jnp.full(m_ref.shape, NEG_BIG, F32)
        l_ref[...] = jnp.zeros(l_ref.shape, F32)
        acc_ref[...] = jnp.zeros(acc_ref.shape, F32)
        carry = jnp.zeros((FOX_HEADS, 1), F32)
        per_tile = tp // blk
        for b in reversed(range(nt * per_tile)):
            x = lft_ref[:, b * blk:(b + 1) * blk]
            parts = jnp.concatenate(_split3_f32(x), axis=0).astype(BF16)
            y = _dot(parts, mlow_ref[...])
            off = (b % per_tile) * blk
            suf_ref[b // per_tile, :, off:off + blk] = y[0:8] + y[8:16] + y[16:24] + carry
            carry = carry + jnp.sum(x, axis=1, keepdims=True)

    def update(s, pv):
        m, l, acc = _softmax_step(s, (m_ref[...], l_ref[...], acc_ref[...]), pv)
        m_ref[...] = m
        l_ref[...] = l
        acc_ref[...] = acc

    bmm = lambda a, b, ca, cb: lax.dot_general(a, b, (((ca,), (cb,)), ((0,), (0,))),
                                               preferred_element_type=F32)
    qh = qh_ref[...]

    kt = ck_ref[...].astype(BF16)
    vt = cv_ref[...].astype(BF16)
    s = bmm(qh, kt, 2, 1).reshape(FOX_HEADS * tn, tp) + jnp.repeat(suf_ref[j], tn, axis=0)
    update(s.reshape(FOX_HEADS, tn, tp), lambda pm: bmm(pm, vt, 2, 2))

    @pl.when(j == nt - 1)
    def _fin():
        per_head = lambda ref: jnp.stack([ref[:, h * hd:(h + 1) * hd] for h in range(FOX_HEADS)],
                                         axis=0).astype(BF16)
        kn = per_head(kn_ref)
        vn = per_head(vn_ref)
        r = lax.broadcasted_iota(jnp.int32, (FOX_HEADS, tn, tn), 1)
        c = lax.broadcasted_iota(jnp.int32, (FOX_HEADS, tn, tn), 2)
        s = jnp.where(c <= r, bmm(qh, kn, 2, 2) - cn_ref[...][:, None, :], -jnp.inf)
        update(s, lambda pm: bmm(pm, vn, 2, 1))
        o = acc_ref[...] / l_ref[...]
        o_ref[...] = jnp.concatenate([o[h] for h in range(FOX_HEADS)], axis=1).astype(BF16)


def _fox_sample(q, kn, vn, cn, lft, ck, cv, batch, tn, past, tp=4096):
    nt = past // tp
    blk = 2 * LANES
    idx = np.arange(blk)
    mlow = jnp.asarray((idx[:, None] > idx[None, :]).astype(np.float32), dtype=BF16)
    per_b = lambda w: pl.BlockSpec((tn, w), lambda b, j: (b, 0))
    cache = pl.BlockSpec((None, FOX_HEADS, FOX_HEAD_DIM, tp), lambda b, j: (b, 0, 0, j))
    return pl.pallas_call(
        functools.partial(_fox_sample_kernel, tp=tp, nt=nt, tn=tn),
        grid=(batch, nt),
        in_specs=[per_b(FOX_WIDTH), per_b(FOX_WIDTH), per_b(FOX_WIDTH),
                  pl.BlockSpec((None, FOX_HEADS, tn), lambda b, j: (b, 0, 0)),
                  pl.BlockSpec((None, FOX_HEADS, past), lambda b, j: (b, 0, 0)),
                  pl.BlockSpec((blk, blk), lambda b, j: (0, 0)),
                  cache, cache],
        out_specs=per_b(FOX_WIDTH),
        out_shape=jax.ShapeDtypeStruct((batch * tn, FOX_WIDTH), BF16),
        scratch_shapes=[pltpu.VMEM((FOX_HEADS, tn, FOX_HEAD_DIM), BF16),
                        pltpu.VMEM((nt, FOX_HEADS, tp), F32),
                        pltpu.VMEM((FOX_HEADS, tn, 1), F32), pltpu.VMEM((FOX_HEADS, tn, 1), F32),
                        pltpu.VMEM((FOX_HEADS, tn, FOX_HEAD_DIM), F32)],
        compiler_params=pltpu.CompilerParams(dimension_semantics=("arbitrary", "arbitrary"),
                                             vmem_limit_bytes=VMEM_LIMIT),
        name="fox_sample",
    )(q, kn, vn, cn, lft, mlow, ck, cv)


def _gla_kernel(q_ref, k_ref, v_ref, g_ref, r_ref, s0_ref, gn_ref, w_ref, lv_ref,
                o_ref, s_ref, st_ref, *, tb, levels, blocks, seqs, carried):
    to_work = lambda s0: s0.reshape(GLA_KW, GLA_DV).T
    from_work = lambda st: st.T.reshape(GLA_HEADS, GLA_DK, GLA_DV)
    if carried:
        @pl.when(pl.program_id(1) == 0)
        def _():
            st_ref[...] = to_work(s0_ref[0])

    for sq in range(seqs):
        st = st_ref[...] if carried else to_work(s0_ref[sq])
        for blk in range(blocks):
            rows = lambda ref: ref.at[pl.ds((sq * blocks + blk) * tb, tb), :]
            st = _gla_block(rows(q_ref), rows(k_ref), rows(v_ref), rows(g_ref), rows(r_ref), gn_ref,
                            w_ref, lv_ref, rows(o_ref), st, tb=tb, levels=levels)
        if carried:
            st_ref[...] = st

            @pl.when(pl.program_id(1) == pl.num_programs(1) - 1)
            def _():
                s_ref[0] = from_work(st)
        else:
            s_ref[sq] = from_work(st)


def _gla_block(q_ref, k_ref, v_ref, g_ref, r_ref, gn_ref, w_ref, lv_ref, o_ref, st, *, tb, levels):
    half = LANES // 2
    r = r_ref[...]
    gate = gn_ref[...] * (r * _sigmoid(r))

    g3 = jnp.concatenate(_split3_f32(g_ref[...] * LOG2E), axis=0).astype(BF16)
    rows = w_ref.shape[0] // 2
    dec = jnp.exp2(jnp.concatenate([_dot(w_ref[0:rows, :], g3), _dot(w_ref[rows:, :], g3)], axis=0))
    from_start = dec[0:tb]
    to_end = dec[tb:2 * tb]

    q = q_ref[...] * (GLA_DK ** -0.5)
    k = k_ref[...]
    v = v_ref[...]
    row = lax.broadcasted_iota(jnp.int32, (tb, GLA_KW), 0)
    low = lax.broadcasted_iota(jnp.int32, (tb, LANES), 1) < half
    lv = lv_ref[...]

    def pair_scores(xq, xk):
        outs = []
        for p in range(GLA_HEADS // 2):
            a = xq[:, p * LANES:(p + 1) * LANES]
            zero = jnp.zeros_like(a)
            lhs = jnp.concatenate([jnp.where(low, a, zero), jnp.where(low, zero, a)], axis=0)
            outs.append(_dot_nt(lhs, xk[:, p * LANES:(p + 1) * LANES]))
        return outs

    here = lv == -1
    a = [jnp.where(here, r_, 0.0) for r_ in pair_scores(q.astype(BF16), k.astype(BF16))]
    for l in range(levels):
        upper = ((row >> l) & 1) == 1
        x = (jnp.where(upper, q, k) * dec[(l + 2) * tb:(l + 3) * tb]).astype(BF16)
        here = lv == l
        a = [jnp.where(here, r_, a_) for r_, a_ in zip(pair_scores(x, x), a)]
    o = jnp.concatenate(
        [_dot(a[h // 2][(h % 2) * tb:(h % 2 + 1) * tb].astype(BF16), v[:, h * GLA_DV:(h + 1) * GLA_DV])
         for h in range(GLA_HEADS)], axis=1)

    lane = lax.broadcasted_iota(jnp.int32, (tb, GLA_KW), 1)
    head_sel = [(lane >= h * GLA_DK) & (lane < (h + 1) * GLA_DK) for h in range(GLA_HEADS)]
    qt = (q * from_start).astype(BF16)
    kt = (k * to_end).astype(BF16)
    zero = jnp.zeros_like(qt)
    q4 = jnp.concatenate([jnp.where(sel, qt, zero) for sel in head_sel], axis=0)
    oi = _dot_nt(q4, st.astype(BF16))
    o = o + jnp.concatenate([oi[h * tb:(h + 1) * tb] for h in range(GLA_HEADS)], axis=1)
    upd = None
    for h in range(GLA_HEADS):
        u = _dot_tn(v[:, h * GLA_DV:(h + 1) * GLA_DV], jnp.where(head_sel[h], kt, zero))
        upd = u if upd is None else upd + u
    st = from_start[tb - 1:tb, :] * st + upd

    outs = []
    for h in range(GLA_HEADS):
        oh = o[:, h * GLA_DV:(h + 1) * GLA_DV]
        outs.append(oh * lax.rsqrt(jnp.mean(oh * oh, axis=-1, keepdims=True) + EPS))
    o_ref[...] = (jnp.concatenate(outs, axis=1) * gate).astype(BF16)
    return st


def _gla(gq, gk, gv, glog, gr, s0, gn, batch, seq, tb=128, rows_per_step=1024):
    tb = min(tb, seq)
    step_rows = min(rows_per_step, batch * seq)
    carried = seq > step_rows
    seqs = 1 if carried else step_rows // seq
    per_seq = (step_rows if carried else seq) // tb
    nt = seq // (tb * per_seq)
    n = batch * seq
    levels = tb.bit_length() - 1
    assert tb == 1 << levels
    ti = np.arange(tb)[:, None]
    si = np.arange(tb)[None, :]
    blocks = [si <= ti, si > ti]
    for l in range(levels):
        m = 1 << l
        mid = ti - ti % (2 * m) + m - 1
        upper = ti % (2 * m) >= m
        blocks.append(np.where(upper, (si > mid) & (si <= ti), (si > ti) & (si <= mid)))
    w = np.concatenate(blocks, axis=0).astype(np.float32)
    w = jnp.asarray(np.concatenate([w, w, w], axis=1), dtype=BF16)
    x = ti ^ si
    lv = np.where(si < ti, np.floor(np.log2(np.maximum(x, 1))).astype(np.int32),
                  np.where(si == ti, -1, -2)).astype(np.int32)
    lv = jnp.asarray(np.concatenate([lv, lv], axis=0))
    row = lambda w_: pl.BlockSpec((step_rows, w_), lambda b, t: (b * nt + t, 0))
    state = pl.BlockSpec((seqs, GLA_HEADS, GLA_DK, GLA_DV), lambda b, t: (b, 0, 0, 0))
    const = lambda shape: pl.BlockSpec(shape, lambda b, t: (0, 0))
    return pl.pallas_call(
        functools.partial(_gla_kernel, tb=tb, levels=levels, blocks=per_seq, seqs=seqs, carried=carried),
        grid=(batch // seqs, nt),
        in_specs=[row(GLA_KW), row(GLA_KW), row(GLA_VW), row(GLA_KW), row(GLA_VW), state,
                  const((1, GLA_VW)), const(w.shape), const(lv.shape)],
        out_specs=(row(GLA_VW), state),
        out_shape=(jax.ShapeDtypeStruct((n, GLA_VW), BF16),
                   jax.ShapeDtypeStruct((batch, GLA_HEADS, GLA_DK, GLA_DV), F32)),
        scratch_shapes=[pltpu.VMEM((GLA_DV, GLA_KW), F32)],
        compiler_params=pltpu.CompilerParams(dimension_semantics=("arbitrary", "arbitrary"),
                                             vmem_limit_bytes=VMEM_LIMIT),
        name="gla",
    )(gq, gk, gv, glog, gr, s0, gn, w, lv)


def _ffn_kernel(x_ref, fo_ref, go_ref, wo_ref, g2_ref, wg_ref, wu_ref, wd_ref, gf_ref,
                y_ref, a_ref, *, chunk, fox_time_minor):
    tm = x_ref.shape[0]
    parts = fo_ref.shape[0] if fox_time_minor else 2
    halves = [slice(i * (tm // parts), (i + 1) * (tm // parts)) for i in range(parts)]
    y1 = []
    for i, rows in enumerate(halves):
        fox = (_dot_tn(fo_ref[i], wo_ref[0:FOX_WIDTH, :]) if fox_time_minor
               else _dot(fo_ref[rows, :], wo_ref[0:FOX_WIDTH, :]))
        y1.append(x_ref[rows, :] + fox + _dot(go_ref[rows, :], wo_ref[FOX_WIDTH:, :]))
    h2 = jnp.concatenate([_rms(y, g2_ref[...]).astype(BF16) for y in y1], axis=0)
    for c in range(D_FF // chunk):
        cs = slice(c * chunk, (c + 1) * chunk)
        u = _dot(h2, wg_ref[:, cs])
        w = _dot(h2, wu_ref[:, cs])
        a_ref[:, cs] = (u * _sigmoid(u) * w).astype(BF16)
    y2 = [y + _dot(a_ref[rows, :], wd_ref[...]) for y, rows in zip(y1, halves)]
    for y, rows in zip(y2, halves):
        y_ref[rows, :] = _rms(y, gf_ref[...])


def _ffn(x2d, fo, go, prm, tm=1024, chunk=256):
    n = x2d.shape[0]
    tm = min(tm, n)
    row = lambda w: pl.BlockSpec((tm, w), lambda i: (i, 0))
    const = lambda shape: pl.BlockSpec(shape, lambda i: (0, 0), pipeline_mode=pl.Buffered(1))
    fox_time_minor = fo.ndim == 4
    if fox_time_minor:
        per_tile = tm // fo.shape[3]
        tiles_per_seq = fo.shape[1] // per_tile
        assert per_tile * fo.shape[3] == tm and tiles_per_seq * per_tile == fo.shape[1]
        fo_spec = pl.BlockSpec((None, per_tile, FOX_WIDTH, fo.shape[3]),
                               lambda i: (i // tiles_per_seq, i % tiles_per_seq, 0, 0))
    else:
        fo_spec = row(FOX_WIDTH)
    return pl.pallas_call(
        functools.partial(_ffn_kernel, chunk=chunk, fox_time_minor=fox_time_minor),
        grid=(n // tm,),
        in_specs=[row(D_MODEL), fo_spec, row(GLA_VW), const((D_MODEL, D_MODEL)),
                  const((1, D_MODEL)), const((D_MODEL, D_FF)), const((D_MODEL, D_FF)),
                  const((D_FF, D_MODEL)), const((1, D_MODEL))],
        out_specs=row(D_MODEL),
        out_shape=jax.ShapeDtypeStruct((n, D_MODEL), F32),
        scratch_shapes=[pltpu.VMEM((tm, D_FF), BF16)],
        compiler_params=pltpu.CompilerParams(dimension_semantics=("arbitrary",),
                                             vmem_limit_bytes=VMEM_LIMIT),
        name="ffn",
    )(x2d, fo, go, prm['wo'], prm['g2'], prm['wg'], prm['wu'], prm['wd'], prm['gf'])


def _layer_params(layer, norm1_g, w_in, w_gate2, b_gate2, b_forget, gla_norm_g, w_out,
                  norm2_g, w_gate, w_up, w_down, final_norm_g):
    wt = jnp.transpose(w_in[layer])
    o_fl = 3 * FOX_WIDTH
    o_gq = o_fl + FOX_HEADS
    o_gg = o_gq + 2 * GLA_KW + GLA_VW
    o_gr = o_gg + GLA_GATE_RANK
    wc = jnp.zeros((LANES, D_MODEL), F32).at[:GLA_GATE_RANK].set(wt[o_gg:o_gr])
    wg2 = jnp.zeros((LANES, GLA_KW), F32).at[:GLA_GATE_RANK].set(w_gate2[layer])
    return dict(
        g1=norm1_g[layer].reshape(1, D_MODEL),
        wq=wt[:FOX_WIDTH].astype(BF16),
        wkv=wt[FOX_WIDTH:o_fl].astype(BF16),
        wb=jnp.concatenate([wt[o_gq:o_gg], wt[o_gr:]], axis=0).astype(BF16),
        wc=wc.astype(BF16),
        wfl=wt[o_fl:o_gq].astype(BF16),
        wg2=wg2.astype(BF16),
        bg2=b_gate2[layer].reshape(1, GLA_KW),
        bfc=b_forget[layer].reshape(FOX_HEADS, 1),
        gn=gla_norm_g[layer].reshape(1, GLA_VW),
        wo=w_out[layer].astype(BF16),
        g2=norm2_g[layer].reshape(1, D_MODEL),
        wg=w_gate[layer].astype(BF16),
        wu=w_up[layer].astype(BF16),
        wd=w_down[layer].astype(BF16),
        gf=final_norm_g.reshape(1, D_MODEL),
    )


def kernel(x_prompt, x_sample, cache_fox_k, cache_fox_v, cache_fox_logf, state_gla, norm1_g, w_in,
           w_gate2, b_gate2, b_forget, gla_norm_g, w_out, norm2_g, w_gate, w_up, w_down, final_norm_g):
    depth = w_in.shape[0]
    assert depth == 1, "the final rmsnorm is fused into the layer's ffn kernel"
    bp, tp_, _ = x_prompt.shape
    bs, ts, _ = x_sample.shape
    past = cache_fox_k.shape[2]
    layer = 0
    prm = _layer_params(layer, norm1_g, w_in, w_gate2, b_gate2, b_forget, gla_norm_g, w_out,
                        norm2_g, w_gate, w_up, w_down, final_norm_g)
    by_time = lambda a, b, t: a.reshape(FOX_HEADS, b, t).transpose(1, 2, 0)[None]

    xp = x_prompt.reshape(bp * tp_, D_MODEL)
    qt, kt_p, vt_p, lf_p, ct, gq, gk, gv, glog, gr = _proj(xp, bp, tp_, prm, True)
    fox_o = _fox_prompt(qt, kt_p, vt_p, ct, bp, tp_)
    s0 = jnp.zeros((bp, GLA_HEADS, GLA_DK, GLA_DV), F32)
    gla_o, s_p = _gla(gq, gk, gv, glog, gr, s0, prm['gn'], bp, tp_)
    y_p = _ffn(xp, fox_o, gla_o, prm)

    xs = x_sample.reshape(bs * ts, D_MODEL)
    q, k_s, v_s, lf_s, ct, gq, gk, gv, glog, gr = _proj(xs, bs, ts, prm, False)
    cn = ct.reshape(FOX_HEADS, bs, ts).transpose(1, 0, 2)
    lft = cache_fox_logf[layer].astype(F32).transpose(0, 2, 1)
    ck = cache_fox_k[layer].transpose(0, 2, 3, 1)
    cv = cache_fox_v[layer].transpose(0, 2, 3, 1)
    fox_o = _fox_sample(q, k_s, v_s, cn, lft, ck, cv, bs, ts, past)
    gla_o, s_s = _gla(gq, gk, gv, glog, gr, state_gla[layer].astype(F32), prm['gn'], bs, ts)
    y_s = _ffn(xs, fox_o, gla_o, prm)

    heads = lambda a, b, t: a.reshape(1, b, t, FOX_HEADS, FOX_HEAD_DIM)
    return (y_p.reshape(bp, tp_, D_MODEL), y_s.reshape(bs, ts, D_MODEL),
            kt_p.transpose(0, 3, 1, 2)[None], vt_p.transpose(0, 3, 1, 2)[None],
            by_time(lf_p, bp, tp_), s_p[None],
            heads(k_s, bs, ts), heads(v_s, bs, ts), by_time(lf_s, bs, ts), s_s[None])
```

```python
import functools

import numpy as np
import jax
import jax.numpy as jnp
from jax import lax
from jax.experimental import pallas as pl
from jax.experimental.pallas import tpu as pltpu

D_MODEL = 1024
FOX_HEADS = 8
FOX_HEAD_DIM = 64
FOX_WIDTH = FOX_HEADS * FOX_HEAD_DIM
GLA_HEADS = 4
GLA_DK = 64
GLA_DV = 128
GLA_KW = GLA_HEADS * GLA_DK
GLA_VW = GLA_HEADS * GLA_DV
GLA_GATE_RANK = 16
GLA_GATE_TEMP = 16.0
D_FF = 2816
EPS = 1e-6

LANES = 128
LOG2E = 1.4426950408889634
VMEM_LIMIT = 56 * 1024 * 1024

F32 = jnp.float32
BF16 = jnp.bfloat16
NEG_BIG = -1e30


def _log_sigmoid(x):
    return jnp.minimum(x, 0.0) - jnp.log1p(jnp.exp(-jnp.abs(x)))


def _sigmoid(x):
    return 1.0 / (1.0 + jnp.exp(-x))


def _split3_f32(x):
    hi = x.astype(BF16).astype(F32)
    r = x - hi
    mid = r.astype(BF16).astype(F32)
    lo = (r - mid).astype(BF16).astype(F32)
    return hi, mid, lo


def _dot(a, b):
    return jnp.dot(a, b, preferred_element_type=F32)


def _dot_nt(a, b):
    return lax.dot_general(a, b, (((1,), (1,)), ((), ())), preferred_element_type=F32)


def _dot_tn(a, b):
    return lax.dot_general(a, b, (((0,), (0,)), ((), ())), preferred_element_type=F32)


def _rms(x, g):
    return x * lax.rsqrt(jnp.mean(x * x, axis=-1, keepdims=True) + EPS) * g


def _softmax_step(s, carry, pv):
    m, l, acc = carry
    m_new = jnp.maximum(m, jnp.max(s, axis=-1, keepdims=True))
    alpha = jnp.exp(m - m_new)
    pm = jnp.exp(s - m_new)
    l = alpha * l + jnp.sum(pm, axis=-1, keepdims=True)
    acc = alpha * acc + pv(pm.astype(BF16))
    return m_new, l, acc


def _proj_kernel(x_ref, g1_ref, wq_ref, wkv_ref, wb_ref, wc_ref, wfl_ref, wg2_ref, bg2_ref,
                 bfc_ref, tri_ref,
                 *rest, tiles_per_seq, time_minor):
    carry_ref = rest[-1]

    @pl.when(pl.program_id(0) % tiles_per_seq == 0)
    def _():
        carry_ref[...] = jnp.zeros_like(carry_ref)

    h = _rms(x_ref[...], g1_ref[...]).astype(BF16)
    tm = h.shape[0]
    scale = FOX_HEAD_DIM ** -0.5

    zc = _dot_nt(h, wc_ref[...])
    fl_t = _dot_nt(wfl_ref[...], h)

    if time_minor:
        (q_ref, k_ref, v_ref, lft_ref, ct_ref, gq_ref, gk_ref, gv_ref, glog_ref, gr_ref,
         carry_ref) = rest
        q_ref[...] = (_dot_nt(wq_ref[...], h) * (scale * LOG2E)).astype(BF16)
    else:
        (q_ref, k_ref, v_ref, lft_ref, ct_ref, gq_ref, gk_ref, gv_ref, glog_ref, gr_ref,
         carry_ref) = rest
        q_ref[...] = (_dot_nt(h, wq_ref[...]) * scale).astype(BF16)

    logf_t = _log_sigmoid(fl_t + bfc_ref[...])
    lft_ref[...] = logf_t
    parts = jnp.concatenate(_split3_f32(logf_t), axis=0).astype(BF16)
    gg = zc.astype(BF16)

    if time_minor:
        kvt = _dot_nt(wkv_ref[...], h)
        k_ref[...] = kvt[:FOX_WIDTH].reshape(FOX_HEADS, FOX_HEAD_DIM, tm)
        v_ref[...] = kvt[FOX_WIDTH:].reshape(FOX_HEADS, FOX_HEAD_DIM, tm)
    else:
        kv = _dot_nt(h, wkv_ref[...])
        k_ref[...] = kv[:, :FOX_WIDTH]
        v_ref[...] = kv[:, FOX_WIDTH:]

    gpre = _dot(gg, wg2_ref[...]) + bg2_ref[...]
    cs = _dot(parts, tri_ref[...])

    zb = _dot_nt(h, wb_ref[...])
    gq_ref[...] = zb[:, :GLA_KW]
    gk_ref[...] = zb[:, GLA_KW:2 * GLA_KW]
    gv_ref[...] = zb[:, 2 * GLA_KW:2 * GLA_KW + GLA_VW].astype(BF16)
    gr_ref[...] = zb[:, 2 * GLA_KW + GLA_VW:]

    glog_ref[...] = _log_sigmoid(gpre) * (1.0 / GLA_GATE_TEMP)

    ct = cs[0:8] + cs[8:16] + cs[16:24] + carry_ref[:, 0:1]
    ct_ref[...] = ct
    carry_ref[...] = jnp.broadcast_to(ct[:, tm - 1:], carry_ref.shape)


def _proj(x2d, batch, seq_len, prm, time_minor, tm=512):
    n = x2d.shape[0]
    tm = min(tm, n)
    tiles_per_seq = max(seq_len // tm, 1)
    idx = np.arange(tm)
    tri = ((idx[:, None] <= idx[None, :]) & (idx[:, None] // seq_len == idx[None, :] // seq_len))
    tri = jnp.asarray(tri.astype(np.float32), dtype=BF16)
    const = lambda a: pl.BlockSpec(a.shape, lambda i: (0, 0))
    row = lambda w: pl.BlockSpec((tm, w), lambda i: (i, 0))
    col = pl.BlockSpec((FOX_HEADS, tm), lambda i: (0, i))
    sds = jax.ShapeDtypeStruct
    if time_minor:
        kv_shape = sds((batch, FOX_HEADS, FOX_HEAD_DIM, seq_len), F32)
        kv_spec = pl.BlockSpec((None, FOX_HEADS, FOX_HEAD_DIM, tm),
                               lambda i: (i // tiles_per_seq, 0, 0, i % tiles_per_seq))
        fox = [(sds((batch, tiles_per_seq, FOX_WIDTH, tm), BF16),
                pl.BlockSpec((None, None, FOX_WIDTH, tm), lambda i: (i // tiles_per_seq, i % tiles_per_seq, 0, 0))),
               (kv_shape, kv_spec), (kv_shape, kv_spec)]
    else:
        fox = [(sds((n, FOX_WIDTH), BF16), row(FOX_WIDTH)),
               (sds((n, FOX_WIDTH), F32), row(FOX_WIDTH)),
               (sds((n, FOX_WIDTH), F32), row(FOX_WIDTH))]
    outs = fox + [
        (sds((FOX_HEADS, n), F32), col),
        (sds((FOX_HEADS, n), F32), col),
        (sds((n, GLA_KW), F32), row(GLA_KW)),
        (sds((n, GLA_KW), F32), row(GLA_KW)),
        (sds((n, GLA_VW), BF16), row(GLA_VW)),
        (sds((n, GLA_KW), F32), row(GLA_KW)),
        (sds((n, GLA_VW), F32), row(GLA_VW)),
    ]
    names = ('g1', 'wq', 'wkv', 'wb', 'wc', 'wfl', 'wg2', 'bg2', 'bfc')
    return pl.pallas_call(
        functools.partial(_proj_kernel, tiles_per_seq=tiles_per_seq, time_minor=time_minor),
        grid=(n // tm,),
        in_specs=[row(D_MODEL)] + [const(prm[k]) for k in names] + [const(tri)],
        out_specs=tuple(s for _, s in outs), out_shape=tuple(s for s, _ in outs),
        scratch_shapes=[pltpu.VMEM((FOX_HEADS, LANES), F32)],
        compiler_params=pltpu.CompilerParams(dimension_semantics=("arbitrary",),
                                             vmem_limit_bytes=VMEM_LIMIT),
        name="proj",
    )(x2d, *[prm[k] for k in names], tri)


def _fox_prompt_kernel(q_ref, k_ref, v_ref, c_ref, o_ref, kb_ref, vb_ref, s_ref, p_ref, acc_ref,
                       *, tk):
    g = pl.program_id(1)
    nq = q_ref.shape[0]
    nk = vb_ref.shape[1]
    pairs = q_ref.shape[1] // LANES
    hd = FOX_HEAD_DIM
    spare = (hd, 0)
    vrows = vb_ref.shape[2]

    parts = jnp.concatenate(_split3_f32(c_ref[...] * (-LOG2E)), axis=0).astype(BF16)
    r = lax.broadcasted_iota(jnp.int32, (3 * FOX_HEADS, LANES), 0)
    ln = lax.broadcasted_iota(jnp.int32, (3 * FOX_HEADS, LANES), 1)
    klane = lax.broadcasted_iota(jnp.int32, (k_ref.shape[2], LANES), 1)
    tail = jnp.where(lax.broadcasted_iota(jnp.int32, (vrows - hd, tk), 0) == 0, 1.0, 0.0)
    for pp in range(pairs):
        k = k_ref[2 * pp:2 * pp + 2].reshape(LANES, k_ref.shape[2]).T.astype(BF16)
        for hh in range(2):
            head = 2 * (pairs * g + pp) + hh
            place = (r % FOX_HEADS == head) & (ln == spare[hh] + r // FOX_HEADS)
            extra = _dot_tn(parts, jnp.where(place, 1.0, 0.0).astype(BF16))
            own = (klane < hd) if hh == 0 else (klane >= hd)
            kb_ref[2 * pp + hh] = jnp.where(own, k, extra.astype(BF16))
            for jj in range(nk):
                vt = v_ref[2 * pp + hh, :, jj * tk:(jj + 1) * tk]
                vb_ref[2 * pp + hh, jj] = jnp.concatenate([vt, tail], axis=0).astype(BF16)

    qrow = lax.broadcasted_iota(jnp.int32, (LANES, tk), 0)
    key = lax.broadcasted_iota(jnp.int32, (tk, tk), 0)
    qry = lax.broadcasted_iota(jnp.int32, (tk, tk), 1)
    causal = key <= qry

    chains = [(slot, part) for slot in range(2 * pairs) for part in range(2)]
    everyone = list(range(len(chains)))
    second = [n for n in everyone if chains[n][1] == 1]

    def q_operands(qi):
        qs = []
        for slot, part in chains:
            pp, hh = divmod(slot, 2)
            q = q_ref[qi, pp * LANES:(pp + 1) * LANES, part * tk:(part + 1) * tk]
            own = (qrow < hd) if hh == 0 else (qrow >= hd)
            ones = (qrow >= spare[hh]) & (qrow < spare[hh] + 3)
            qs.append(jnp.where(own, q, jnp.where(ones, 1.0, 0.0).astype(BF16)))
        return qs

    steps = []
    for qi in range(nq):
        steps += [(qi, j, everyone, None) for j in range(2 * qi)]
        steps += [(qi, 2 * qi, everyone, 0), (qi, 2 * qi + 1, second, 1)]

    qs_of = {}

    def scores(step, buf):
        qi, j, live, _ = step
        if qi not in qs_of:
            qs_of.clear()
            qs_of[qi] = q_operands(qi)
        for n in live:
            s_ref[buf, n] = _dot(kb_ref[chains[n][0], j * tk:(j + 1) * tk, :], qs_of[qi][n])

    def values(step, buf):
        _, j, live, _ = step
        return {n: _dot(vb_ref[chains[n][0], j], p_ref[buf, n]) for n in live}

    def finish(qi):
        for part in range(2):
            heads = [acc_ref[qi % 2, 2 * slot + part] for slot in range(2 * pairs)]
            o_ref[qi, :, part * tk:(part + 1) * tk] = jnp.concatenate(
                [a[:hd] / a[hd:hd + 1] for a in heads], axis=0).astype(BF16)

    def accumulate(step, alphas, pv):
        qi, j, live, _ = step
        for n in live:
            acc_ref[qi % 2, n] = pv[n] if j == 0 else alphas[n] * acc_ref[qi % 2, n] + pv[n]

    ms = {}
    scores(steps[0], 0)
    prev, prev_alphas = None, None
    for t, step in enumerate(steps):
        qi, j, live, masked_part = step
        if t + 1 < len(steps):
            scores(steps[t + 1], (t + 1) % 2)
        pv = values(prev, (t - 1) % 2) if prev is not None else None
        alphas = {}
        for n in live:
            s = s_ref[t % 2, n]
            if chains[n][1] == masked_part:
                s = jnp.where(causal, s, -jnp.inf)
            m_new = jnp.max(s, axis=0, keepdims=True)
            if j > 0:
                m_new = jnp.maximum(ms[n], m_new)
                alphas[n] = jnp.exp2(ms[n] - m_new)
            p_ref[t % 2, n] = jnp.exp2(s - m_new).astype(BF16)
            ms[n] = m_new
        if prev is not None:
            accumulate(prev, prev_alphas, pv)
            if prev[0] != qi:
                finish(prev[0])
        prev, prev_alphas = step, alphas
    accumulate(prev, prev_alphas, values(prev, (len(steps) - 1) % 2))
    finish(prev[0])


def _fox_prompt(qt, kt, vt, ct, batch, seq, tk=256, pairs=2):
    tq = 2 * tk
    nq = seq // tq
    nk = seq // tk
    heads = 2 * pairs
    chains = 2 * heads
    vrows = FOX_HEAD_DIM + 16
    assert qt.shape == (batch, nq, FOX_WIDTH, tq) and FOX_HEADS % heads == 0
    q_spec = pl.BlockSpec((None, nq, pairs * LANES, tq), lambda b, p: (b, 0, p, 0))
    kv_spec = pl.BlockSpec((None, heads, FOX_HEAD_DIM, seq), lambda b, p: (b, p, 0, 0))
    return pl.pallas_call(
        functools.partial(_fox_prompt_kernel, tk=tk),
        grid=(batch, FOX_HEADS // heads),
        in_specs=[q_spec, kv_spec, kv_spec,
                  pl.BlockSpec((FOX_HEADS, seq), lambda b, p: (0, b))],
        out_specs=q_spec,
        out_shape=jax.ShapeDtypeStruct((batch, nq, FOX_WIDTH, tq), BF16),
        scratch_shapes=[pltpu.VMEM((heads, seq, LANES), BF16), pltpu.VMEM((heads, nk, vrows, tk), BF16),
                        pltpu.VMEM((2, chains, tk, tk), F32), pltpu.VMEM((2, chains, tk, tk), BF16),
                        pltpu.VMEM((2, chains, vrows, tk), F32)],
        compiler_params=pltpu.CompilerParams(dimension_semantics=("arbitrary", "arbitrary"),
                                             vmem_limit_bytes=VMEM_LIMIT),
        name="fox_prompt",
    )(qt, kt, vt, ct)


def _fox_sample_kernel(q_ref, kn_ref, vn_ref, cn_ref, lft_ref, mlow_ref, ck_ref, cv_ref,
                       o_ref, qh_ref, suf_ref, m_ref, l_ref, acc_ref, *, tp, nt, tn):
    j = pl.program_id(1)
    blk = 2 * LANES
    hd = FOX_HEAD_DIM

    @pl.when(j == 0)
    def _init():
        for h in range(FOX_HEADS):
            qh_ref[h] = q_ref[:, h * hd:(h + 1) * hd]
        m_ref[...] = jnp.full(m_ref.shape, NEG_BIG, F32)
        l_ref[...] = jnp.zeros(l_ref.shape, F32)
        acc_ref[...] = jnp.zeros(acc_ref.shape, F32)
        carry = jnp.zeros((FOX_HEADS, 1), F32)
        per_tile = tp // blk
        for b in reversed(range(nt * per_tile)):
            x = lft_ref[:, b * blk:(b + 1) * blk]
            parts = jnp.concatenate(_split3_f32(x), axis=0).astype(BF16)
            y = _dot(parts, mlow_ref[...])
            off = (b % per_tile) * blk
            suf_ref[b // per_tile, :, off:off + blk] = y[0:8] + y[8:16] + y[16:24] + carry
            carry = carry + jnp.sum(x, axis=1, keepdims=True)

    def update(s, pv):
        m, l, acc = _softmax_step(s, (m_ref[...], l_ref[...], acc_ref[...]), pv)
        m_ref[...] = m
        l_ref[...] = l
        acc_ref[...] = acc

    bmm = lambda a, b, ca, cb: lax.dot_general(a, b, (((ca,), (cb,)), ((0,), (0,))),
                                               preferred_element_type=F32)
    qh = qh_ref[...]

    kt = ck_ref[...].astype(BF16)
    vt = cv_ref[...].astype(BF16)
    s = bmm(qh, kt, 2, 1).reshape(FOX_HEADS * tn, tp) + jnp.repeat(suf_ref[j], tn, axis=0)
    update(s.reshape(FOX_HEADS, tn, tp), lambda pm: bmm(pm, vt, 2, 2))

    @pl.when(j == nt - 1)
    def _fin():
        per_head = lambda ref: jnp.stack([ref[:, h * hd:(h + 1) * hd] for h in range(FOX_HEADS)],
                                         axis=0).astype(BF16)
        kn = per_head(kn_ref)
        vn = per_head(vn_ref)
        r = lax.broadcasted_iota(jnp.int32, (FOX_HEADS, tn, tn), 1)
        c = lax.broadcasted_iota(jnp.int32, (FOX_HEADS, tn, tn), 2)
        s = jnp.where(c <= r, bmm(qh, kn, 2, 2) - cn_ref[...][:, None, :], -jnp.inf)
        update(s, lambda pm: bmm(pm, vn, 2, 1))
        o = acc_ref[...] / l_ref[...]
        o_ref[...] = jnp.concatenate([o[h] for h in range(FOX_HEADS)], axis=1).astype(BF16)


def _fox_sample(q, kn, vn, cn, lft, ck, cv, batch, tn, past, tp=4096):
    nt = past // tp
    blk = 2 * LANES
    idx = np.arange(blk)
    mlow = jnp.asarray((idx[:, None] > idx[None, :]).astype(np.float32), dtype=BF16)
    per_b = lambda w: pl.BlockSpec((tn, w), lambda b, j: (b, 0))
    cache = pl.BlockSpec((None, FOX_HEADS, FOX_HEAD_DIM, tp), lambda b, j: (b, 0, 0, j))
    return pl.pallas_call(
        functools.partial(_fox_sample_kernel, tp=tp, nt=nt, tn=tn),
        grid=(batch, nt),
        in_specs=[per_b(FOX_WIDTH), per_b(FOX_WIDTH), per_b(FOX_WIDTH),
                  pl.BlockSpec((None, FOX_HEADS, tn), lambda b, j: (b, 0, 0)),
                  pl.BlockSpec((None, FOX_HEADS, past), lambda b, j: (b, 0, 0)),
                  pl.BlockSpec((blk, blk), lambda b, j: (0, 0)),
                  cache, cache],
        out_specs=per_b(FOX_WIDTH),
        out_shape=jax.ShapeDtypeStruct((batch * tn, FOX_WIDTH), BF16),
        scratch_shapes=[pltpu.VMEM((FOX_HEADS, tn, FOX_HEAD_DIM), BF16),
                        pltpu.VMEM((nt, FOX_HEADS, tp), F32),
                        pltpu.VMEM((FOX_HEADS, tn, 1), F32), pltpu.VMEM((FOX_HEADS, tn, 1), F32),
                        pltpu.VMEM((FOX_HEADS, tn, FOX_HEAD_DIM), F32)],
        compiler_params=pltpu.CompilerParams(dimension_semantics=("arbitrary", "arbitrary"),
                                             vmem_limit_bytes=VMEM_LIMIT),
        name="fox_sample",
    )(q, kn, vn, cn, lft, mlow, ck, cv)


def _gla_kernel(q_ref, k_ref, v_ref, g_ref, r_ref, s0_ref, gn_ref, w_ref, lv_ref,
                o_ref, s_ref, st_ref, *, tb, levels, blocks, seqs, carried):
    to_work = lambda s0: s0.reshape(GLA_KW, GLA_DV).T
    from_work = lambda st: st.T.reshape(GLA_HEADS, GLA_DK, GLA_DV)
    if carried:
        @pl.when(pl.program_id(1) == 0)
        def _():
            st_ref[...] = to_work(s0_ref[0])

    for sq in range(seqs):
        st = st_ref[...] if carried else to_work(s0_ref[sq])
        for blk in range(blocks):
            rows = lambda ref: ref.at[pl.ds((sq * blocks + blk) * tb, tb), :]
            st = _gla_block(rows(q_ref), rows(k_ref), rows(v_ref), rows(g_ref), rows(r_ref), gn_ref,
                            w_ref, lv_ref, rows(o_ref), st, tb=tb, levels=levels)
        if carried:
            st_ref[...] = st

            @pl.when(pl.program_id(1) == pl.num_programs(1) - 1)
            def _():
                s_ref[0] = from_work(st)
        else:
            s_ref[sq] = from_work(st)


def _gla_block(q_ref, k_ref, v_ref, g_ref, r_ref, gn_ref, w_ref, lv_ref, o_ref, st, *, tb, levels):
    half = LANES // 2
    r = r_ref[...]
    gate = gn_ref[...] * (r * _sigmoid(r))

    g3 = jnp.concatenate(_split3_f32(g_ref[...] * LOG2E), axis=0).astype(BF16)
    rows = w_ref.shape[0] // 2
    dec = jnp.exp2(jnp.concatenate([_dot(w_ref[0:rows, :], g3), _dot(w_ref[rows:, :], g3)], axis=0))
    from_start = dec[0:tb]
    to_end = dec[tb:2 * tb]

    q = q_ref[...] * (GLA_DK ** -0.5)
    k = k_ref[...]
    v = v_ref[...]
    row = lax.broadcasted_iota(jnp.int32, (tb, GLA_KW), 0)
    low = lax.broadcasted_iota(jnp.int32, (tb, LANES), 1) < half
    lv = lv_ref[...]

    def pair_scores(xq, xk):
        outs = []
        for p in range(GLA_HEADS // 2):
            a = xq[:, p * LANES:(p + 1) * LANES]
            zero = jnp.zeros_like(a)
            lhs = jnp.concatenate([jnp.where(low, a, zero), jnp.where(low, zero, a)], axis=0)
            outs.append(_dot_nt(lhs, xk[:, p * LANES:(p + 1) * LANES]))
        return outs

    here = lv == -1
    a = [jnp.where(here, r_, 0.0) for r_ in pair_scores(q.astype(BF16), k.astype(BF16))]
    for l in range(levels):
        upper = ((row >> l) & 1) == 1
        x = (jnp.where(upper, q, k) * dec[(l + 2) * tb:(l + 3) * tb]).astype(BF16)
        here = lv == l
        a = [jnp.where(here, r_, a_) for r_, a_ in zip(pair_scores(x, x), a)]
    o = jnp.concatenate(
        [_dot(a[h // 2][(h % 2) * tb:(h % 2 + 1) * tb].astype(BF16), v[:, h * GLA_DV:(h + 1) * GLA_DV])
         for h in range(GLA_HEADS)], axis=1)

    lane = lax.broadcasted_iota(jnp.int32, (tb, GLA_KW), 1)
    head_sel = [(lane >= h * GLA_DK) & (lane < (h + 1) * GLA_DK) for h in range(GLA_HEADS)]
    qt = (q * from_start).astype(BF16)
    kt = (k * to_end).astype(BF16)
    zero = jnp.zeros_like(qt)
    q4 = jnp.concatenate([jnp.where(sel, qt, zero) for sel in head_sel], axis=0)
    oi = _dot_nt(q4, st.astype(BF16))
    o = o + jnp.concatenate([oi[h * tb:(h + 1) * tb] for h in range(GLA_HEADS)], axis=1)
    upd = None
    for h in range(GLA_HEADS):
        u = _dot_tn(v[:, h * GLA_DV:(h + 1) * GLA_DV], jnp.where(head_sel[h], kt, zero))
        upd = u if upd is None else upd + u
    st = from_start[tb - 1:tb, :] * st + upd

    outs = []
    for h in range(GLA_HEADS):
        oh = o[:, h * GLA_DV:(h + 1) * GLA_DV]
        outs.append(oh * lax.rsqrt(jnp.mean(oh * oh, axis=-1, keepdims=True) + EPS))
    o_ref[...] = (jnp.concatenate(outs, axis=1) * gate).astype(BF16)
    return st


def _gla(gq, gk, gv, glog, gr, s0, gn, batch, seq, tb=128, rows_per_step=1024):
    tb = min(tb, seq)
    step_rows = min(rows_per_step, batch * seq)
    carried = seq > step_rows
    seqs = 1 if carried else step_rows // seq
    per_seq = (step_rows if carried else seq) // tb
    nt = seq // (tb * per_seq)
    n = batch * seq
    levels = tb.bit_length() - 1
    assert tb == 1 << levels
    ti = np.arange(tb)[:, None]
    si = np.arange(tb)[None, :]
    blocks = [si <= ti, si > ti]
    for l in range(levels):
        m = 1 << l
        mid = ti - ti % (2 * m) + m - 1
        upper = ti % (2 * m) >= m
        blocks.append(np.where(upper, (si > mid) & (si <= ti), (si > ti) & (si <= mid)))
    w = np.concatenate(blocks, axis=0).astype(np.float32)
    w = jnp.asarray(np.concatenate([w, w, w], axis=1), dtype=BF16)
    x = ti ^ si
    lv = np.where(si < ti, np.floor(np.log2(np.maximum(x, 1))).astype(np.int32),
                  np.where(si == ti, -1, -2)).astype(np.int32)
    lv = jnp.asarray(np.concatenate([lv, lv], axis=0))
    row = lambda w_: pl.BlockSpec((step_rows, w_), lambda b, t: (b * nt + t, 0))
    state = pl.BlockSpec((seqs, GLA_HEADS, GLA_DK, GLA_DV), lambda b, t: (b, 0, 0, 0))
    const = lambda shape: pl.BlockSpec(shape, lambda b, t: (0, 0))
    return pl.pallas_call(
        functools.partial(_gla_kernel, tb=tb, levels=levels, blocks=per_seq, seqs=seqs, carried=carried),
        grid=(batch // seqs, nt),
        in_specs=[row(GLA_KW), row(GLA_KW), row(GLA_VW), row(GLA_KW), row(GLA_VW), state,
                  const((1, GLA_VW)), const(w.shape), const(lv.shape)],
        out_specs=(row(GLA_VW), state),
        out_shape=(jax.ShapeDtypeStruct((n, GLA_VW), BF16),
                   jax.ShapeDtypeStruct((batch, GLA_HEADS, GLA_DK, GLA_DV), F32)),
        scratch_shapes=[pltpu.VMEM((GLA_DV, GLA_KW), F32)],
        compiler_params=pltpu.CompilerParams(dimension_semantics=("arbitrary", "arbitrary"),
                                             vmem_limit_bytes=VMEM_LIMIT),
        name="gla",
    )(gq, gk, gv, glog, gr, s0, gn, w, lv)


def _ffn_kernel(x_ref, fo_ref, go_ref, wo_ref, g2_ref, wg_ref, wu_ref, wd_ref, gf_ref,
                y_ref, a_ref, *, chunk, fox_time_minor):
    tm = x_ref.shape[0]
    parts = fo_ref.shape[0] if fox_time_minor else 2
    halves = [slice(i * (tm // parts), (i + 1) * (tm // parts)) for i in range(parts)]
    y1 = []
    for i, rows in enumerate(halves):
        fox = (_dot_tn(fo_ref[i], wo_ref[0:FOX_WIDTH, :]) if fox_time_minor
               else _dot(fo_ref[rows, :], wo_ref[0:FOX_WIDTH, :]))
        y1.append(x_ref[rows, :] + fox + _dot(go_ref[rows, :], wo_ref[FOX_WIDTH:, :]))
    h2 = jnp.concatenate([_rms(y, g2_ref[...]).astype(BF16) for y in y1], axis=0)
    for c in range(D_FF // chunk):
        cs = slice(c * chunk, (c + 1) * chunk)
        u = _dot(h2, wg_ref[:, cs])
        w = _dot(h2, wu_ref[:, cs])
        a_ref[:, cs] = (u * _sigmoid(u) * w).astype(BF16)
    y2 = [y + _dot(a_ref[rows, :], wd_ref[...]) for y, rows in zip(y1, halves)]
    for y, rows in zip(y2, halves):
        y_ref[rows, :] = _rms(y, gf_ref[...])


def _ffn(x2d, fo, go, prm, tm=1024, chunk=256):
    n = x2d.shape[0]
    tm = min(tm, n)
    row = lambda w: pl.BlockSpec((tm, w), lambda i: (i, 0))
    const = lambda shape: pl.BlockSpec(shape, lambda i: (0, 0), pipeline_mode=pl.Buffered(1))
    fox_time_minor = fo.ndim == 4
    if fox_time_minor:
        per_tile = tm // fo.shape[3]
        tiles_per_seq = fo.shape[1] // per_tile
        assert per_tile * fo.shape[3] == tm and tiles_per_seq * per_tile == fo.shape[1]
        fo_spec = pl.BlockSpec((None, per_tile, FOX_WIDTH, fo.shape[3]),
                               lambda i: (i // tiles_per_seq, i % tiles_per_seq, 0, 0))
    else:
        fo_spec = row(FOX_WIDTH)
    return pl.pallas_call(
        functools.partial(_ffn_kernel, chunk=chunk, fox_time_minor=fox_time_minor),
        grid=(n // tm,),
        in_specs=[row(D_MODEL), fo_spec, row(GLA_VW), const((D_MODEL, D_MODEL)),
                  const((1, D_MODEL)), const((D_MODEL, D_FF)), const((D_MODEL, D_FF)),
                  const((D_FF, D_MODEL)), const((1, D_MODEL))],
        out_specs=row(D_MODEL),
        out_shape=jax.ShapeDtypeStruct((n, D_MODEL), F32),
        scratch_shapes=[pltpu.VMEM((tm, D_FF), BF16)],
        compiler_params=pltpu.CompilerParams(dimension_semantics=("arbitrary",),
                                             vmem_limit_bytes=VMEM_LIMIT),
        name="ffn",
    )(x2d, fo, go, prm['wo'], prm['g2'], prm['wg'], prm['wu'], prm['wd'], prm['gf'])


def _layer_params(layer, norm1_g, w_in, w_gate2, b_gate2, b_forget, gla_norm_g, w_out,
                  norm2_g, w_gate, w_up, w_down, final_norm_g):
    wt = jnp.transpose(w_in[layer])
    o_fl = 3 * FOX_WIDTH
    o_gq = o_fl + FOX_HEADS
    o_gg = o_gq + 2 * GLA_KW + GLA_VW
    o_gr = o_gg + GLA_GATE_RANK
    wc = jnp.zeros((LANES, D_MODEL), F32).at[:GLA_GATE_RANK].set(wt[o_gg:o_gr])
    wg2 = jnp.zeros((LANES, GLA_KW), F32).at[:GLA_GATE_RANK].set(w_gate2[layer])
    return dict(
        g1=norm1_g[layer].reshape(1, D_MODEL),
        wq=wt[:FOX_WIDTH].astype(BF16),
        wkv=wt[FOX_WIDTH:o_fl].astype(BF16),
        wb=jnp.concatenate([wt[o_gq:o_gg], wt[o_gr:]], axis=0).astype(BF16),
        wc=wc.astype(BF16),
        wfl=wt[o_fl:o_gq].astype(BF16),
        wg2=wg2.astype(BF16),
        bg2=b_gate2[layer].reshape(1, GLA_KW),
        bfc=b_forget[layer].reshape(FOX_HEADS, 1),
        gn=gla_norm_g[layer].reshape(1, GLA_VW),
        wo=w_out[layer].astype(BF16),
        g2=norm2_g[layer].reshape(1, D_MODEL),
        wg=w_gate[layer].astype(BF16),
        wu=w_up[layer].astype(BF16),
        wd=w_down[layer].astype(BF16),
        gf=final_norm_g.reshape(1, D_MODEL),
    )


def kernel(x_prompt, x_sample, cache_fox_k, cache_fox_v, cache_fox_logf, state_gla, norm1_g, w_in,
           w_gate2, b_gate2, b_forget, gla_norm_g, w_out, norm2_g, w_gate, w_up, w_down, final_norm_g):
    depth = w_in.shape[0]
    assert depth == 1, "the final rmsnorm is fused into the layer's ffn kernel"
    bp, tp_, _ = x_prompt.shape
    bs, ts, _ = x_sample.shape
    past = cache_fox_k.shape[2]
    layer = 0
    prm = _layer_params(layer, norm1_g, w_in, w_gate2, b_gate2, b_forget, gla_norm_g, w_out,
                        norm2_g, w_gate, w_up, w_down, final_norm_g)
    by_time = lambda a, b, t: a.reshape(FOX_HEADS, b, t).transpose(1, 2, 0)[None]

    xp = x_prompt.reshape(bp * tp_, D_MODEL)
    qt, kt_p, vt_p, lf_p, ct, gq, gk, gv, glog, gr = _proj(xp, bp, tp_, prm, True)
    fox_o = _fox_prompt(qt, kt_p, vt_p, ct, bp, tp_)
    s0 = jnp.zeros((bp, GLA_HEADS, GLA_DK, GLA_DV), F32)
    gla_o, s_p = _gla(gq, gk, gv, glog, gr, s0, prm['gn'], bp, tp_)
    y_p = _ffn(xp, fox_o, gla_o, prm)

    xs = x_sample.reshape(bs * ts, D_MODEL)
    q, k_s, v_s, lf_s, ct, gq, gk, gv, glog, gr = _proj(xs, bs, ts, prm, False)
    cn = ct.reshape(FOX_HEADS, bs, ts).transpose(1, 0, 2)
    lft = cache_fox_logf[layer].astype(F32).transpose(0, 2, 1)
    ck = cache_fox_k[layer].transpose(0, 2, 3, 1)
    cv = cache_fox_v[layer].transpose(0, 2, 3, 1)
    fox_o = _fox_sample(q, k_s, v_s, cn, lft, ck, cv, bs, ts, past)
    gla_o, s_s = _gla(gq, gk, gv, glog, gr, state_gla[layer].astype(F32), prm['gn'], bs, ts)
    y_s = _ffn(xs, fox_o, gla_o, prm)

    heads = lambda a, b, t: a.reshape(1, b, t, FOX_HEADS, FOX_HEAD_DIM)
    return (y_p.reshape(bp, tp_, D_MODEL), y_s.reshape(bs, ts, D_MODEL),
            kt_p.transpose(0, 3, 1, 2)[None], vt_p.transpose(0, 3, 1, 2)[None],
            by_time(lf_p, bp, tp_), s_p[None],
            heads(k_s, bs, ts), heads(v_s, bs, ts), by_time(lf_s, bs, ts), s_s[None])
```

```python
import functools

import numpy as np
import jax
import jax.numpy as jnp
from jax import lax
from jax.experimental import pallas as pl
from jax.experimental.pallas import tpu as pltpu

D_MODEL = 1024
FOX_HEADS = 8
FOX_HEAD_DIM = 64
FOX_WIDTH = FOX_HEADS * FOX_HEAD_DIM
GLA_HEADS = 4
GLA_DK = 64
GLA_DV = 128
GLA_KW = GLA_HEADS * GLA_DK
GLA_VW = GLA_HEADS * GLA_DV
GLA_GATE_RANK = 16
GLA_GATE_TEMP = 16.0
D_FF = 2816
EPS = 1e-6

LANES = 128
LOG2E = 1.4426950408889634
VMEM_LIMIT = 56 * 1024 * 1024

F32 = jnp.float32
BF16 = jnp.bfloat16
NEG_BIG = -1e30


def _log_sigmoid(x):
    return jnp.minimum(x, 0.0) - jnp.log1p(jnp.exp(-jnp.abs(x)))


def _sigmoid(x):
    return 1.0 / (1.0 + jnp.exp(-x))


def _split3_f32(x):
    hi = x.astype(BF16).astype(F32)
    r = x - hi
    mid = r.astype(BF16).astype(F32)
    lo = (r - mid).astype(BF16).astype(F32)
    return hi, mid, lo


def _split2_f32(x):
    hi = x.astype(BF16).astype(F32)
    return hi, (x - hi).astype(BF16).astype(F32)


def _dot(a, b):
    return jnp.dot(a, b, preferred_element_type=F32)


def _dot_nt(a, b):
    return lax.dot_general(a, b, (((1,), (1,)), ((), ())), preferred_element_type=F32)


def _dot_tn(a, b):
    return lax.dot_general(a, b, (((0,), (0,)), ((), ())), preferred_element_type=F32)


def _rms(x, g):
    return x * lax.rsqrt(jnp.mean(x * x, axis=-1, keepdims=True) + EPS) * g


def _softmax_step(s, carry, pv):
    m, l, acc = carry
    m_new = jnp.maximum(m, jnp.max(s, axis=-1, keepdims=True))
    alpha = jnp.exp(m - m_new)
    pm = jnp.exp(s - m_new)
    l = alpha * l + jnp.sum(pm, axis=-1, keepdims=True)
    acc = alpha * acc + pv(pm.astype(BF16))
    return m_new, l, acc


def _proj_kernel(x_ref, g1_ref, w_ref, wg2_ref, bg2_ref, bfc_ref, tri_ref,
                 *rest, tiles_per_seq, time_minor):
    o_kv, o_b = FOX_WIDTH, 3 * FOX_WIDTH
    o_tail = o_b + 2 * GLA_KW + 2 * GLA_VW
    carry_ref = rest[-1]

    @pl.when(pl.program_id(0) % tiles_per_seq == 0)
    def _():
        carry_ref[...] = jnp.zeros_like(carry_ref)

    h = _rms(x_ref[...], g1_ref[...]).astype(BF16)
    tm = h.shape[0]
    scale = FOX_HEAD_DIM ** -0.5

    zc = _dot_nt(h, w_ref[o_tail:, :])
    fl_t = _dot_nt(w_ref[o_tail:, :], h)[0:FOX_HEADS]

    if time_minor:
        (q_ref, k_ref, v_ref, lft_ref, ct_ref, gq_ref, gk_ref, gv_ref, glog_ref, gr_ref,
         carry_ref) = rest
        q_ref[...] = (_dot_nt(w_ref[0:o_kv, :], h) * (scale * LOG2E)).astype(BF16)
    else:
        (q_ref, k_ref, v_ref, lft_ref, ct_ref, gq_ref, gk_ref, gv_ref, glog_ref, gr_ref,
         carry_ref) = rest
        q_ref[...] = (_dot_nt(h, w_ref[0:o_kv, :]) * scale).astype(BF16)

    logf_t = _log_sigmoid(fl_t + bfc_ref[...])
    lft_ref[...] = logf_t
    parts = jnp.concatenate(_split3_f32(logf_t), axis=0).astype(BF16)
    gg = zc.astype(BF16)

    if time_minor:
        kvt = _dot_nt(w_ref[o_kv:o_b, :], h)
        k_ref[...] = kvt[:FOX_WIDTH].reshape(FOX_HEADS, FOX_HEAD_DIM, tm)
        v_ref[...] = kvt[FOX_WIDTH:].reshape(FOX_HEADS, FOX_HEAD_DIM, tm)
    else:
        kv = _dot_nt(h, w_ref[o_kv:o_b, :])
        k_ref[...] = kv[:, :FOX_WIDTH]
        v_ref[...] = kv[:, FOX_WIDTH:]

    gpre = _dot(gg, wg2_ref[...]) + bg2_ref[...]
    cs = _dot(parts, tri_ref[...])

    zb = _dot_nt(h, w_ref[o_b:o_tail, :])
    gq_ref[...] = zb[:, :GLA_KW]
    gk_ref[...] = zb[:, GLA_KW:2 * GLA_KW]
    gv_ref[...] = zb[:, 2 * GLA_KW:2 * GLA_KW + GLA_VW].astype(BF16)
    gr_ref[...] = zb[:, 2 * GLA_KW + GLA_VW:]

    glog_ref[...] = _log_sigmoid(gpre) * (1.0 / GLA_GATE_TEMP)

    ct = cs[0:8] + cs[8:16] + cs[16:24] + carry_ref[:, 0:1]
    ct_ref[...] = ct
    carry_ref[...] = jnp.broadcast_to(ct[:, tm - 1:], carry_ref.shape)


def _proj(x2d, batch, seq_len, prm, time_minor, tm=512):
    n = x2d.shape[0]
    tm = min(tm, n)
    tiles_per_seq = max(seq_len // tm, 1)
    idx = np.arange(tm)
    tri = ((idx[:, None] <= idx[None, :]) & (idx[:, None] // seq_len == idx[None, :] // seq_len))
    tri = jnp.asarray(tri.astype(np.float32), dtype=BF16)
    const = lambda a: pl.BlockSpec(a.shape, lambda i: (0, 0))
    row = lambda w: pl.BlockSpec((tm, w), lambda i: (i, 0))
    col = pl.BlockSpec((FOX_HEADS, tm), lambda i: (0, i))
    sds = jax.ShapeDtypeStruct
    if time_minor:
        kv_shape = sds((batch, FOX_HEADS, FOX_HEAD_DIM, seq_len), F32)
        kv_spec = pl.BlockSpec((None, FOX_HEADS, FOX_HEAD_DIM, tm),
                               lambda i: (i // tiles_per_seq, 0, 0, i % tiles_per_seq))
        fox = [(sds((batch, tiles_per_seq, FOX_WIDTH, tm), BF16),
                pl.BlockSpec((None, None, FOX_WIDTH, tm), lambda i: (i // tiles_per_seq, i % tiles_per_seq, 0, 0))),
               (kv_shape, kv_spec), (kv_shape, kv_spec)]
    else:
        fox = [(sds((n, FOX_WIDTH), BF16), row(FOX_WIDTH)),
               (sds((n, FOX_WIDTH), F32), row(FOX_WIDTH)),
               (sds((n, FOX_WIDTH), F32), row(FOX_WIDTH))]
    outs = fox + [
        (sds((FOX_HEADS, n), F32), col),
        (sds((FOX_HEADS, n), F32), col),
        (sds((n, GLA_KW), F32), row(GLA_KW)),
        (sds((n, GLA_KW), F32), row(GLA_KW)),
        (sds((n, GLA_VW), BF16), row(GLA_VW)),
        (sds((n, GLA_KW), F32), row(GLA_KW)),
        (sds((n, GLA_VW), F32), row(GLA_VW)),
    ]
    names = ('g1', 'w', 'wg2', 'bg2', 'bfc')
    return pl.pallas_call(
        functools.partial(_proj_kernel, tiles_per_seq=tiles_per_seq, time_minor=time_minor),
        grid=(n // tm,),
        in_specs=[row(D_MODEL)] + [const(prm[k]) for k in names] + [const(tri)],
        out_specs=tuple(s for _, s in outs), out_shape=tuple(s for s, _ in outs),
        scratch_shapes=[pltpu.VMEM((FOX_HEADS, LANES), F32)],
        compiler_params=pltpu.CompilerParams(dimension_semantics=("arbitrary",),
                                             vmem_limit_bytes=VMEM_LIMIT),
        name="proj",
    )(x2d, *[prm[k] for k in names], tri)


def _fox_prompt_kernel(q_ref, k_ref, v_ref, c_ref, o_ref, kb_ref, vb_ref, s_ref, p_ref, acc_ref,
                       *, tk):
    g = pl.program_id(1)
    nq = q_ref.shape[0]
    nk = vb_ref.shape[1]
    pairs = q_ref.shape[1] // LANES
    hd = FOX_HEAD_DIM
    spare = (hd, 0)
    vrows = vb_ref.shape[2]

    parts = jnp.concatenate(_split3_f32(c_ref[...] * (-LOG2E)), axis=0).astype(BF16)
    r = lax.broadcasted_iota(jnp.int32, (3 * FOX_HEADS, LANES), 0)
    ln = lax.broadcasted_iota(jnp.int32, (3 * FOX_HEADS, LANES), 1)
    klane = lax.broadcasted_iota(jnp.int32, (k_ref.shape[2], LANES), 1)
    tail = jnp.where(lax.broadcasted_iota(jnp.int32, (vrows - hd, tk), 0) == 0, 1.0, 0.0)
    for pp in range(pairs):
        k = k_ref[2 * pp:2 * pp + 2].reshape(LANES, k_ref.shape[2]).T.astype(BF16)
        for hh in range(2):
            head = 2 * (pairs * g + pp) + hh
            place = (r % FOX_HEADS == head) & (ln == spare[hh] + r // FOX_HEADS)
            extra = _dot_tn(parts, jnp.where(place, 1.0, 0.0).astype(BF16))
            own = (klane < hd) if hh == 0 else (klane >= hd)
            kb_ref[2 * pp + hh] = jnp.where(own, k, extra.astype(BF16))
            for jj in range(nk):
                vt = v_ref[2 * pp + hh, :, jj * tk:(jj + 1) * tk]
                vb_ref[2 * pp + hh, jj] = jnp.concatenate([vt, tail], axis=0).astype(BF16)

    qrow = lax.broadcasted_iota(jnp.int32, (LANES, tk), 0)
    key = lax.broadcasted_iota(jnp.int32, (tk, tk), 0)
    qry = lax.broadcasted_iota(jnp.int32, (tk, tk), 1)
    causal = key <= qry

    chains = [(slot, part) for slot in range(2 * pairs) for part in range(2)]
    everyone = list(range(len(chains)))
    second = [n for n in everyone if chains[n][1] == 1]

    def q_operands(qi):
        qs = []
        for slot, part in chains:
            pp, hh = divmod(slot, 2)
            q = q_ref[qi, pp * LANES:(pp + 1) * LANES, part * tk:(part + 1) * tk]
            own = (qrow < hd) if hh == 0 else (qrow >= hd)
            ones = (qrow >= spare[hh]) & (qrow < spare[hh] + 3)
            qs.append(jnp.where(own, q, jnp.where(ones, 1.0, 0.0).astype(BF16)))
        return qs

    steps = []
    for qi in range(nq):
        steps += [(qi, j, everyone, None) for j in range(2 * qi)]
        steps += [(qi, 2 * qi, everyone, 0), (qi, 2 * qi + 1, second, 1)]

    qs_of = {}

    def scores(step, buf):
        qi, j, live, _ = step
        if qi not in qs_of:
            qs_of.clear()
            qs_of[qi] = q_operands(qi)
        for n in live:
            s_ref[buf, n] = _dot(kb_ref[chains[n][0], j * tk:(j + 1) * tk, :], qs_of[qi][n])

    def values(step, buf):
        _, j, live, _ = step
        return {n: _dot(vb_ref[chains[n][0], j], p_ref[buf, n]) for n in live}

    def finish(qi):
        for part in range(2):
            heads = [acc_ref[qi % 2, 2 * slot + part] for slot in range(2 * pairs)]
            o_ref[qi, :, part * tk:(part + 1) * tk] = jnp.concatenate(
                [a[:hd] / a[hd:hd + 1] for a in heads], axis=0).astype(BF16)

    def accumulate(step, alphas, pv):
        qi, j, live, _ = step
        for n in live:
            acc_ref[qi % 2, n] = pv[n] if j == 0 else alphas[n] * acc_ref[qi % 2, n] + pv[n]

    ms = {}
    scores(steps[0], 0)
    prev, prev_alphas = None, None
    for t, step in enumerate(steps):
        qi, j, live, masked_part = step
        if t + 1 < len(steps):
            scores(steps[t + 1], (t + 1) % 2)
        pv = values(prev, (t - 1) % 2) if prev is not None else None
        alphas = {}
        for n in live:
            s = s_ref[t % 2, n]
            if chains[n][1] == masked_part:
                s = jnp.where(causal, s, -jnp.inf)
            m_new = jnp.max(s, axis=0, keepdims=True)
            if j > 0:
                m_new = jnp.maximum(ms[n], m_new)
                alphas[n] = jnp.exp2(ms[n] - m_new)
            p_ref[t % 2, n] = jnp.exp2(s - m_new).astype(BF16)
            ms[n] = m_new
        if prev is not None:
            accumulate(prev, prev_alphas, pv)
            if prev[0] != qi:
                finish(prev[0])
        prev, prev_alphas = step, alphas
    accumulate(prev, prev_alphas, values(prev, (len(steps) - 1) % 2))
    finish(prev[0])


def _fox_prompt(qt, kt, vt, ct, batch, seq, tk=256, pairs=2):
    tq = 2 * tk
    nq = seq // tq
    nk = seq // tk
    heads = 2 * pairs
    chains = 2 * heads
    vrows = FOX_HEAD_DIM + 16
    assert qt.shape == (batch, nq, FOX_WIDTH, tq) and FOX_HEADS % heads == 0
    q_spec = pl.BlockSpec((None, nq, pairs * LANES, tq), lambda b, p: (b, 0, p, 0))
    kv_spec = pl.BlockSpec((None, heads, FOX_HEAD_DIM, seq), lambda b, p: (b, p, 0, 0))
    return pl.pallas_call(
        functools.partial(_fox_prompt_kernel, tk=tk),
        grid=(batch, FOX_HEADS // heads),
        in_specs=[q_spec, kv_spec, kv_spec,
                  pl.BlockSpec((FOX_HEADS, seq), lambda b, p: (0, b))],
        out_specs=q_spec,
        out_shape=jax.ShapeDtypeStruct((batch, nq, FOX_WIDTH, tq), BF16),
        scratch_shapes=[pltpu.VMEM((heads, seq, LANES), BF16), pltpu.VMEM((heads, nk, vrows, tk), BF16),
                        pltpu.VMEM((2, chains, tk, tk), F32), pltpu.VMEM((2, chains, tk, tk), BF16),
                        pltpu.VMEM((2, chains, vrows, tk), F32)],
        compiler_params=pltpu.CompilerParams(dimension_semantics=("arbitrary", "arbitrary"),
                                             vmem_limit_bytes=VMEM_LIMIT),
        name="fox_prompt",
    )(qt, kt, vt, ct)


def _fox_sample_kernel(q_ref, kn_ref, vn_ref, cn_ref, lft_ref, mlow_ref, ck_ref, cv_ref,
                       o_ref, qh_ref, suf_ref, m_ref, l_ref, acc_ref, *, tp, nt, tn):
    j = pl.program_id(1)
    blk = 2 * LANES
    hd = FOX_HEAD_DIM

    @pl.when(j == 0)
    def _init():
        for h in range(FOX_HEADS):
            qh_ref[h] = q_ref[:, h * hd:(h + 1) * hd]
        m_ref[...] = jnp.full(m_ref.shape, NEG_BIG, F32)
        l_ref[...] = jnp.zeros(l_ref.shape, F32)
        acc_ref[...] = jnp.zeros(acc_ref.shape, F32)
        carry = jnp.zeros((FOX_HEADS, 1), F32)
        per_tile = tp // blk
        for b in reversed(range(nt * per_tile)):
            x = lft_ref[:, b * blk:(b + 1) * blk]
            parts = jnp.concatenate(_split3_f32(x), axis=0).astype(BF16)
            y = _dot(parts, mlow_ref[...])
            off = (b % per_tile) * blk
            suf_ref[b // per_tile, :, off:off + blk] = y[0:8] + y[8:16] + y[16:24] + carry
            carry = carry + jnp.sum(x, axis=1, keepdims=True)

    def update(s, pv):
        m, l, acc = _softmax_step(s, (m_ref[...], l_ref[...], acc_ref[...]), pv)
        m_ref[...] = m
        l_ref[...] = l
        acc_ref[...] = acc

    bmm = lambda a, b, ca, cb: lax.dot_general(a, b, (((ca,), (cb,)), ((0,), (0,))),
                                               preferred_element_type=F32)
    qh = qh_ref[...]

    kt = ck_ref[...].astype(BF16)
    vt = cv_ref[...].astype(BF16)
    s = bmm(qh, kt, 2, 1).reshape(FOX_HEADS * tn, tp) + jnp.repeat(suf_ref[j], tn, axis=0)
    update(s.reshape(FOX_HEADS, tn, tp), lambda pm: bmm(pm, vt, 2, 2))

    @pl.when(j == nt - 1)
    def _fin():
        per_head = lambda ref: jnp.stack([ref[:, h * hd:(h + 1) * hd] for h in range(FOX_HEADS)],
                                         axis=0).astype(BF16)
        kn = per_head(kn_ref)
        vn = per_head(vn_ref)
        r = lax.broadcasted_iota(jnp.int32, (FOX_HEADS, tn, tn), 1)
        c = lax.broadcasted_iota(jnp.int32, (FOX_HEADS, tn, tn), 2)
        s = jnp.where(c <= r, bmm(qh, kn, 2, 2) - cn_ref[...][:, None, :], -jnp.inf)
        update(s, lambda pm: bmm(pm, vn, 2, 1))
        o = acc_ref[...] / l_ref[...]
        o_ref[...] = jnp.concatenate([o[h] for h in range(FOX_HEADS)], axis=1).astype(BF16)


def _fox_sample(q, kn, vn, cn, lft, ck, cv, batch, tn, past, tp=4096):
    nt = past // tp
    blk = 2 * LANES
    idx = np.arange(blk)
    mlow = jnp.asarray((idx[:, None] > idx[None, :]).astype(np.float32), dtype=BF16)
    per_b = lambda w: pl.BlockSpec((tn, w), lambda b, j: (b, 0))
    cache = pl.BlockSpec((None, FOX_HEADS, FOX_HEAD_DIM, tp), lambda b, j: (b, 0, 0, j))
    return pl.pallas_call(
        functools.partial(_fox_sample_kernel, tp=tp, nt=nt, tn=tn),
        grid=(batch, nt),
        in_specs=[per_b(FOX_WIDTH), per_b(FOX_WIDTH), per_b(FOX_WIDTH),
                  pl.BlockSpec((None, FOX_HEADS, tn), lambda b, j: (b, 0, 0)),
                  pl.BlockSpec((None, FOX_HEADS, past), lambda b, j: (b, 0, 0)),
                  pl.BlockSpec((blk, blk), lambda b, j: (0, 0)),
                  cache, cache],
        out_specs=per_b(FOX_WIDTH),
        out_shape=jax.ShapeDtypeStruct((batch * tn, FOX_WIDTH), BF16),
        scratch_shapes=[pltpu.VMEM((FOX_HEADS, tn, FOX_HEAD_DIM), BF16),
                        pltpu.VMEM((nt, FOX_HEADS, tp), F32),
                        pltpu.VMEM((FOX_HEADS, tn, 1), F32), pltpu.VMEM((FOX_HEADS, tn, 1), F32),
                        pltpu.VMEM((FOX_HEADS, tn, FOX_HEAD_DIM), F32)],
        compiler_params=pltpu.CompilerParams(dimension_semantics=("arbitrary", "arbitrary"),
                                             vmem_limit_bytes=VMEM_LIMIT),
        name="fox_sample",
    )(q, kn, vn, cn, lft, mlow, ck, cv)


def _gla_kernel(q_ref, k_ref, v_ref, g_ref, r_ref, s0_ref, gn_ref, w_ref, lv_ref,
                o_ref, s_ref, st_ref, *, tb, levels, blocks, seqs, carried):
    to_work = lambda s0: s0.reshape(GLA_KW, GLA_DV).T
    from_work = lambda st: st.T.reshape(GLA_HEADS, GLA_DK, GLA_DV)
    if carried:
        @pl.when(pl.program_id(1) == 0)
        def _():
            st_ref[...] = to_work(s0_ref[0])

    for sq in range(seqs):
        st = st_ref[...] if carried else to_work(s0_ref[sq])
        for blk in range(blocks):
            rows = lambda ref: ref.at[pl.ds((sq * blocks + blk) * tb, tb), :]
            st = _gla_block(rows(q_ref), rows(k_ref), rows(v_ref), rows(g_ref), rows(r_ref), gn_ref,
                            w_ref, lv_ref, rows(o_ref), st, tb=tb, levels=levels)
        if carried:
            st_ref[...] = st

            @pl.when(pl.program_id(1) == pl.num_programs(1) - 1)
            def _():
                s_ref[0] = from_work(st)
        else:
            s_ref[sq] = from_work(st)


def _gla_block(q_ref, k_ref, v_ref, g_ref, r_ref, gn_ref, w_ref, lv_ref, o_ref, st, *, tb, levels):
    half = LANES // 2
    r = r_ref[...]
    gate = gn_ref[...] * (r * _sigmoid(r))

    g2 = g_ref[...] * LOG2E
    gp = jnp.concatenate(_split2_f32(g2), axis=0).astype(BF16)
    rows = w_ref.shape[0] // 2
    cum = jnp.concatenate([_dot(w_ref[0:rows, :], gp), _dot(w_ref[rows:, :], gp)], axis=0)
    dec = jnp.exp2(cum)
    from_start = dec[0:tb]
    to_end = jnp.exp2(cum[tb - 1:tb] - cum[0:tb])

    q = q_ref[...] * (GLA_DK ** -0.5)
    k = k_ref[...]
    v = v_ref[...]
    row = lax.broadcasted_iota(jnp.int32, (tb, GLA_KW), 0)
    low = lax.broadcasted_iota(jnp.int32, (tb, LANES), 1) < half
    lv = lv_ref[...]

    def pair_scores(xq, xk):
        outs = []
        for p in range(GLA_HEADS // 2):
            a = xq[:, p * LANES:(p + 1) * LANES]
            zero = jnp.zeros_like(a)
            lhs = jnp.concatenate([jnp.where(low, a, zero), jnp.where(low, zero, a)], axis=0)
            outs.append(_dot_nt(lhs, xk[:, p * LANES:(p + 1) * LANES]))
        return outs

    here = lv == -1
    a = [jnp.where(here, r_, 0.0) for r_ in pair_scores(q.astype(BF16), k.astype(BF16))]
    for l in range(levels):
        upper = ((row >> l) & 1) == 1
        e = jnp.where(upper, jnp.exp2(g2), 1.0) if l == 0 else dec[l * tb:(l + 1) * tb]
        x = (jnp.where(upper, q, k) * e).astype(BF16)
        here = lv == l
        a = [jnp.where(here, r_, a_) for r_, a_ in zip(pair_scores(x, x), a)]
    o = jnp.concatenate(
        [_dot(a[h // 2][(h % 2) * tb:(h % 2 + 1) * tb].astype(BF16), v[:, h * GLA_DV:(h + 1) * GLA_DV])
         for h in range(GLA_HEADS)], axis=1)

    lane = lax.broadcasted_iota(jnp.int32, (tb, GLA_KW), 1)
    head_sel = [(lane >= h * GLA_DK) & (lane < (h + 1) * GLA_DK) for h in range(GLA_HEADS)]
    qt = (q * from_start).astype(BF16)
    kt = (k * to_end).astype(BF16)
    zero = jnp.zeros_like(qt)
    q4 = jnp.concatenate([jnp.where(sel, qt, zero) for sel in head_sel], axis=0)
    oi = _dot_nt(q4, st.astype(BF16))
    o = o + jnp.concatenate([oi[h * tb:(h + 1) * tb] for h in range(GLA_HEADS)], axis=1)
    upd = None
    for h in range(GLA_HEADS):
        u = _dot_tn(v[:, h * GLA_DV:(h + 1) * GLA_DV], jnp.where(head_sel[h], kt, zero))
        upd = u if upd is None else upd + u
    st = from_start[tb - 1:tb, :] * st + upd

    outs = []
    for h in range(GLA_HEADS):
        oh = o[:, h * GLA_DV:(h + 1) * GLA_DV]
        outs.append(oh * lax.rsqrt(jnp.mean(oh * oh, axis=-1, keepdims=True) + EPS))
    o_ref[...] = (jnp.concatenate(outs, axis=1) * gate).astype(BF16)
    return st


def _gla(gq, gk, gv, glog, gr, s0, gn, batch, seq, tb=128, rows_per_step=1024):
    tb = min(tb, seq)
    step_rows = min(rows_per_step, batch * seq)
    carried = seq > step_rows
    seqs = 1 if carried else step_rows // seq
    per_seq = (step_rows if carried else seq) // tb
    nt = seq // (tb * per_seq)
    n = batch * seq
    levels = tb.bit_length() - 1
    assert tb == 1 << levels
    ti = np.arange(tb)[:, None]
    si = np.arange(tb)[None, :]
    blocks = [si <= ti]
    for l in range(1, levels):
        m = 1 << l
        mid = ti - ti % (2 * m) + m - 1
        upper = ti % (2 * m) >= m
        blocks.append(np.where(upper, (si > mid) & (si <= ti), (si > ti) & (si <= mid)))
    w = np.concatenate(blocks, axis=0).astype(np.float32)
    w = jnp.asarray(np.concatenate([w, w], axis=1), dtype=BF16)
    x = ti ^ si
    lv = np.where(si < ti, np.floor(np.log2(np.maximum(x, 1))).astype(np.int32),
                  np.where(si == ti, -1, -2)).astype(np.int32)
    lv = jnp.asarray(np.concatenate([lv, lv], axis=0))
    row = lambda w_: pl.BlockSpec((step_rows, w_), lambda b, t: (b * nt + t, 0))
    state = pl.BlockSpec((seqs, GLA_HEADS, GLA_DK, GLA_DV), lambda b, t: (b, 0, 0, 0))
    const = lambda shape: pl.BlockSpec(shape, lambda b, t: (0, 0))
    return pl.pallas_call(
        functools.partial(_gla_kernel, tb=tb, levels=levels, blocks=per_seq, seqs=seqs, carried=carried),
        grid=(batch // seqs, nt),
        in_specs=[row(GLA_KW), row(GLA_KW), row(GLA_VW), row(GLA_KW), row(GLA_VW), state,
                  const((1, GLA_VW)), const(w.shape), const(lv.shape)],
        out_specs=(row(GLA_VW), state),
        out_shape=(jax.ShapeDtypeStruct((n, GLA_VW), BF16),
                   jax.ShapeDtypeStruct((batch, GLA_HEADS, GLA_DK, GLA_DV), F32)),
        scratch_shapes=[pltpu.VMEM((GLA_DV, GLA_KW), F32)],
        compiler_params=pltpu.CompilerParams(dimension_semantics=("arbitrary", "arbitrary"),
                                             vmem_limit_bytes=VMEM_LIMIT),
        name="gla",
    )(gq, gk, gv, glog, gr, s0, gn, w, lv)


def _ffn_kernel(x_ref, fo_ref, go_ref, wo_ref, g2_ref, wg_ref, wu_ref, wd_ref, gf_ref,
                y_ref, a_ref, *, chunk, fox_time_minor):
    tm = x_ref.shape[0]
    parts = fo_ref.shape[0] if fox_time_minor else 2
    halves = [slice(i * (tm // parts), (i + 1) * (tm // parts)) for i in range(parts)]
    y1 = []
    for i, rows in enumerate(halves):
        fox = (_dot_tn(fo_ref[i], wo_ref[0:FOX_WIDTH, :]) if fox_time_minor
               else _dot(fo_ref[rows, :], wo_ref[0:FOX_WIDTH, :]))
        y1.append(x_ref[rows, :] + fox + _dot(go_ref[rows, :], wo_ref[FOX_WIDTH:, :]))
    h2 = jnp.concatenate([_rms(y, g2_ref[...]).astype(BF16) for y in y1], axis=0)
    for c in range(D_FF // chunk):
        cs = slice(c * chunk, (c + 1) * chunk)
        u = _dot(h2, wg_ref[:, cs])
        w = _dot(h2, wu_ref[:, cs])
        a_ref[:, cs] = (u * _sigmoid(u) * w).astype(BF16)
    y2 = [y + _dot(a_ref[rows, :], wd_ref[...]) for y, rows in zip(y1, halves)]
    for y, rows in zip(y2, halves):
        y_ref[rows, :] = _rms(y, gf_ref[...])


def _ffn(x2d, fo, go, prm, tm=1024, chunk=256):
    n = x2d.shape[0]
    tm = min(tm, n)
    row = lambda w: pl.BlockSpec((tm, w), lambda i: (i, 0))
    const = lambda shape: pl.BlockSpec(shape, lambda i: (0, 0), pipeline_mode=pl.Buffered(1))
    fox_time_minor = fo.ndim == 4
    if fox_time_minor:
        per_tile = tm // fo.shape[3]
        tiles_per_seq = fo.shape[1] // per_tile
        assert per_tile * fo.shape[3] == tm and tiles_per_seq * per_tile == fo.shape[1]
        fo_spec = pl.BlockSpec((None, per_tile, FOX_WIDTH, fo.shape[3]),
                               lambda i: (i // tiles_per_seq, i % tiles_per_seq, 0, 0))
    else:
        fo_spec = row(FOX_WIDTH)
    return pl.pallas_call(
        functools.partial(_ffn_kernel, chunk=chunk, fox_time_minor=fox_time_minor),
        grid=(n // tm,),
        in_specs=[row(D_MODEL), fo_spec, row(GLA_VW), const((D_MODEL, D_MODEL)),
                  const((1, D_MODEL)), const((D_MODEL, D_FF)), const((D_MODEL, D_FF)),
                  const((D_FF, D_MODEL)), const((1, D_MODEL))],
        out_specs=row(D_MODEL),
        out_shape=jax.ShapeDtypeStruct((n, D_MODEL), F32),
        scratch_shapes=[pltpu.VMEM((tm, D_FF), BF16)],
        compiler_params=pltpu.CompilerParams(dimension_semantics=("arbitrary",),
                                             vmem_limit_bytes=VMEM_LIMIT),
        name="ffn",
    )(x2d, fo, go, prm['wo'], prm['g2'], prm['wg'], prm['wu'], prm['wd'], prm['gf'])


def _layer_params(layer, norm1_g, w_in, w_gate2, b_gate2, b_forget, gla_norm_g, w_out,
                  norm2_g, w_gate, w_up, w_down, final_norm_g):
    wt = jnp.transpose(w_in[layer])
    o_fl = 3 * FOX_WIDTH
    o_gq = o_fl + FOX_HEADS
    o_gg = o_gq + 2 * GLA_KW + GLA_VW
    o_gr = o_gg + GLA_GATE_RANK
    tail = 2 * GLA_GATE_RANK
    pad = jnp.zeros((tail - FOX_HEADS - GLA_GATE_RANK, D_MODEL), F32)
    w = jnp.concatenate([wt[:o_fl], wt[o_gq:o_gg], wt[o_gr:], wt[o_fl:o_gq], wt[o_gg:o_gr], pad], axis=0)
    wg2 = jnp.zeros((tail, GLA_KW), F32).at[FOX_HEADS:FOX_HEADS + GLA_GATE_RANK].set(w_gate2[layer])
    return dict(
        g1=norm1_g[layer].reshape(1, D_MODEL),
        w=w.astype(BF16),
        wg2=wg2.astype(BF16),
        bg2=b_gate2[layer].reshape(1, GLA_KW),
        bfc=b_forget[layer].reshape(FOX_HEADS, 1),
        gn=gla_norm_g[layer].reshape(1, GLA_VW),
        wo=w_out[layer].astype(BF16),
        g2=norm2_g[layer].reshape(1, D_MODEL),
        wg=w_gate[layer].astype(BF16),
        wu=w_up[layer].astype(BF16),
        wd=w_down[layer].astype(BF16),
        gf=final_norm_g.reshape(1, D_MODEL),
    )


def kernel(x_prompt, x_sample, cache_fox_k, cache_fox_v, cache_fox_logf, state_gla, norm1_g, w_in,
           w_gate2, b_gate2, b_forget, gla_norm_g, w_out, norm2_g, w_gate, w_up, w_down, final_norm_g):
    depth = w_in.shape[0]
    assert depth == 1, "the final rmsnorm is fused into the layer's ffn kernel"
    bp, tp_, _ = x_prompt.shape
    bs, ts, _ = x_sample.shape
    past = cache_fox_k.shape[2]
    layer = 0
    prm = _layer_params(layer, norm1_g, w_in, w_gate2, b_gate2, b_forget, gla_norm_g, w_out,
                        norm2_g, w_gate, w_up, w_down, final_norm_g)
    by_time = lambda a, b, t: a.reshape(FOX_HEADS, b, t).transpose(1, 2, 0)[None]

    xp = x_prompt.reshape(bp * tp_, D_MODEL)
    qt, kt_p, vt_p, lf_p, ct, gq, gk, gv, glog, gr = _proj(xp, bp, tp_, prm, True)
    fox_o = _fox_prompt(qt, kt_p, vt_p, ct, bp, tp_)
    s0 = jnp.zeros((bp, GLA_HEADS, GLA_DK, GLA_DV), F32)
    gla_o, s_p = _gla(gq, gk, gv, glog, gr, s0, prm['gn'], bp, tp_)
    y_p = _ffn(xp, fox_o, gla_o, prm)

    xs = x_sample.reshape(bs * ts, D_MODEL)
    q, k_s, v_s, lf_s, ct, gq, gk, gv, glog, gr = _proj(xs, bs, ts, prm, False)
    cn = ct.reshape(FOX_HEADS, bs, ts).transpose(1, 0, 2)
    lft = cache_fox_logf[layer].astype(F32).transpose(0, 2, 1)
    ck = cache_fox_k[layer].transpose(0, 2, 3, 1)
    cv = cache_fox_v[layer].transpose(0, 2, 3, 1)
    fox_o = _fox_sample(q, k_s, v_s, cn, lft, ck, cv, bs, ts, past)
    gla_o, s_s = _gla(gq, gk, gv, glog, gr, state_gla[layer].astype(F32), prm['gn'], bs, ts)
    y_s = _ffn(xs, fox_o, gla_o, prm)

    heads = lambda a, b, t: a.reshape(1, b, t, FOX_HEADS, FOX_HEAD_DIM)
    return (y_p.reshape(bp, tp_, D_MODEL), y_s.reshape(bs, ts, D_MODEL),
            kt_p.transpose(0, 3, 1, 2)[None], vt_p.transpose(0, 3, 1, 2)[None],
            by_time(lf_p, bp, tp_), s_p[None],
            heads(k_s, bs, ts), heads(v_s, bs, ts), by_time(lf_s, bs, ts), s_s[None])
```

```python
import functools

import numpy as np
import jax
import jax.numpy as jnp
from jax import lax
from jax.experimental import pallas as pl
from jax.experimental.pallas import tpu as pltpu

D_MODEL = 1024
FOX_HEADS = 8
FOX_HEAD_DIM = 64
FOX_WIDTH = FOX_HEADS * FOX_HEAD_DIM
GLA_HEADS = 4
GLA_DK = 64
GLA_DV = 128
GLA_KW = GLA_HEADS * GLA_DK
GLA_VW = GLA_HEADS * GLA_DV
GLA_GATE_RANK = 16
GLA_GATE_TEMP = 16.0
D_FF = 2816
EPS = 1e-6

LANES = 128
LOG2E = 1.4426950408889634
VMEM_LIMIT = 56 * 1024 * 1024

F32 = jnp.float32
BF16 = jnp.bfloat16
NEG_BIG = -1e30


def _log_sigmoid(x):
    return jnp.minimum(x, 0.0) - jnp.log1p(jnp.exp(-jnp.abs(x)))


def _sigmoid(x):
    return 1.0 / (1.0 + jnp.exp(-x))


def _split3_f32(x):
    hi = x.astype(BF16).astype(F32)
    r = x - hi
    mid = r.astype(BF16).astype(F32)
    lo = (r - mid).astype(BF16).astype(F32)
    return hi, mid, lo


def _split2_f32(x):
    hi = x.astype(BF16).astype(F32)
    return hi, (x - hi).astype(BF16).astype(F32)


def _dot(a, b):
    return jnp.dot(a, b, preferred_element_type=F32)


def _dot_nt(a, b):
    return lax.dot_general(a, b, (((1,), (1,)), ((), ())), preferred_element_type=F32)


def _dot_tn(a, b):
    return lax.dot_general(a, b, (((0,), (0,)), ((), ())), preferred_element_type=F32)


def _rms(x, g):
    return x * lax.rsqrt(jnp.mean(x * x, axis=-1, keepdims=True) + EPS) * g


def _softmax_step(s, carry, pv):
    m, l, acc = carry
    m_new = jnp.maximum(m, jnp.max(s, axis=-1, keepdims=True))
    alpha = jnp.exp(m - m_new)
    pm = jnp.exp(s - m_new)
    l = alpha * l + jnp.sum(pm, axis=-1, keepdims=True)
    acc = alpha * acc + pv(pm.astype(BF16))
    return m_new, l, acc


def _proj_kernel(x_ref, g1_ref, w_ref, wg2_ref, bg2_ref, bfc_ref, tri_ref,
                 *rest, tiles_per_seq, time_minor):
    o_kv, o_b = FOX_WIDTH, 3 * FOX_WIDTH
    o_tail = o_b + 2 * GLA_KW + 2 * GLA_VW
    carry_ref = rest[-1]

    @pl.when(pl.program_id(0) % tiles_per_seq == 0)
    def _():
        carry_ref[...] = jnp.zeros_like(carry_ref)

    h = _rms(x_ref[...], g1_ref[...]).astype(BF16)
    tm = h.shape[0]
    scale = FOX_HEAD_DIM ** -0.5

    zc = _dot_nt(h, w_ref[o_tail:, :])
    fl_t = _dot_nt(w_ref[o_tail:, :], h)[0:FOX_HEADS]

    if time_minor:
        (q_ref, k_ref, v_ref, lft_ref, ct_ref, gq_ref, gk_ref, gv_ref, glog_ref, gr_ref,
         carry_ref) = rest
        qt = (_dot_nt(w_ref[0:o_kv, :], h) * (scale * LOG2E)).astype(BF16)
        tq = q_ref.shape[2]
        for i in range(q_ref.shape[0]):
            q_ref[i] = qt[:, i * tq:(i + 1) * tq]
    else:
        (q_ref, k_ref, v_ref, lft_ref, ct_ref, gq_ref, gk_ref, gv_ref, glog_ref, gr_ref,
         carry_ref) = rest
        q_ref[...] = (_dot_nt(h, w_ref[0:o_kv, :]) * scale).astype(BF16)

    logf_t = _log_sigmoid(fl_t + bfc_ref[...])
    lft_ref[...] = logf_t
    parts = jnp.concatenate(_split3_f32(logf_t), axis=0).astype(BF16)
    gg = zc.astype(BF16)

    if time_minor:
        kvt = _dot_nt(w_ref[o_kv:o_b, :], h)
        k_ref[...] = kvt[:FOX_WIDTH].reshape(FOX_HEADS, FOX_HEAD_DIM, tm)
        v_ref[...] = kvt[FOX_WIDTH:].reshape(FOX_HEADS, FOX_HEAD_DIM, tm)
    else:
        kv = _dot_nt(h, w_ref[o_kv:o_b, :])
        k_ref[...] = kv[:, :FOX_WIDTH]
        v_ref[...] = kv[:, FOX_WIDTH:]

    gpre = _dot(gg, wg2_ref[...]) + bg2_ref[...]
    cs = _dot(parts, tri_ref[...])

    zb = _dot_nt(h, w_ref[o_b:o_tail, :])
    gq_ref[...] = zb[:, :GLA_KW]
    gk_ref[...] = zb[:, GLA_KW:2 * GLA_KW]
    gv_ref[...] = zb[:, 2 * GLA_KW:2 * GLA_KW + GLA_VW].astype(BF16)
    gr_ref[...] = zb[:, 2 * GLA_KW + GLA_VW:]

    glog_ref[...] = _log_sigmoid(gpre) * (1.0 / GLA_GATE_TEMP)

    ct = cs[0:8] + cs[8:16] + cs[16:24] + carry_ref[:, 0:1]
    ct_ref[...] = ct
    carry_ref[...] = jnp.broadcast_to(ct[:, tm - 1:], carry_ref.shape)


def _proj(x2d, batch, seq_len, prm, time_minor, tm=1024, tq=512):
    n = x2d.shape[0]
    tm = min(tm, n)
    tiles_per_seq = max(seq_len // tm, 1)
    per_tile = tm // tq
    idx = np.arange(tm)
    tri = ((idx[:, None] <= idx[None, :]) & (idx[:, None] // seq_len == idx[None, :] // seq_len))
    tri = jnp.asarray(tri.astype(np.float32), dtype=BF16)
    const = lambda a: pl.BlockSpec(a.shape, lambda i: (0, 0), pipeline_mode=pl.Buffered(1))
    row = lambda w: pl.BlockSpec((tm, w), lambda i: (i, 0))
    col = pl.BlockSpec((FOX_HEADS, tm), lambda i: (0, i))
    sds = jax.ShapeDtypeStruct
    if time_minor:
        kv_shape = sds((batch, FOX_HEADS, FOX_HEAD_DIM, seq_len), F32)
        kv_spec = pl.BlockSpec((None, FOX_HEADS, FOX_HEAD_DIM, tm),
                               lambda i: (i // tiles_per_seq, 0, 0, i % tiles_per_seq))
        fox = [(sds((batch, seq_len // tq, FOX_WIDTH, tq), BF16),
                pl.BlockSpec((None, per_tile, FOX_WIDTH, tq),
                             lambda i: (i // tiles_per_seq, i % tiles_per_seq, 0, 0))),
               (kv_shape, kv_spec), (kv_shape, kv_spec)]
    else:
        fox = [(sds((n, FOX_WIDTH), BF16), row(FOX_WIDTH)),
               (sds((n, FOX_WIDTH), F32), row(FOX_WIDTH)),
               (sds((n, FOX_WIDTH), F32), row(FOX_WIDTH))]
    outs = fox + [
        (sds((FOX_HEADS, n), F32), col),
        (sds((FOX_HEADS, n), F32), col),
        (sds((n, GLA_KW), F32), row(GLA_KW)),
        (sds((n, GLA_KW), F32), row(GLA_KW)),
        (sds((n, GLA_VW), BF16), row(GLA_VW)),
        (sds((n, GLA_KW), F32), row(GLA_KW)),
        (sds((n, GLA_VW), F32), row(GLA_VW)),
    ]
    names = ('g1', 'w', 'wg2', 'bg2', 'bfc')
    return pl.pallas_call(
        functools.partial(_proj_kernel, tiles_per_seq=tiles_per_seq, time_minor=time_minor),
        grid=(n // tm,),
        in_specs=[row(D_MODEL)] + [const(prm[k]) for k in names] + [const(tri)],
        out_specs=tuple(s for _, s in outs), out_shape=tuple(s for s, _ in outs),
        scratch_shapes=[pltpu.VMEM((FOX_HEADS, LANES), F32)],
        compiler_params=pltpu.CompilerParams(dimension_semantics=("arbitrary",),
                                             vmem_limit_bytes=VMEM_LIMIT),
        name="proj",
    )(x2d, *[prm[k] for k in names], tri)


def _fox_prompt_kernel(q_ref, k_ref, v_ref, c_ref, o_ref, kb_ref, vb_ref, s_ref, p_ref, acc_ref,
                       *, tk):
    g = pl.program_id(1)
    nq = q_ref.shape[0]
    nk = vb_ref.shape[1]
    pairs = q_ref.shape[1] // LANES
    hd = FOX_HEAD_DIM
    spare = (hd, 0)
    vrows = vb_ref.shape[2]

    parts = jnp.concatenate(_split3_f32(c_ref[...] * (-LOG2E)), axis=0).astype(BF16)
    r = lax.broadcasted_iota(jnp.int32, (3 * FOX_HEADS, LANES), 0)
    ln = lax.broadcasted_iota(jnp.int32, (3 * FOX_HEADS, LANES), 1)
    klane = lax.broadcasted_iota(jnp.int32, (k_ref.shape[2], LANES), 1)
    tail = jnp.where(lax.broadcasted_iota(jnp.int32, (vrows - hd, tk), 0) == 0, 1.0, 0.0)
    for pp in range(pairs):
        k = k_ref[2 * pp:2 * pp + 2].reshape(LANES, k_ref.shape[2]).T.astype(BF16)
        for hh in range(2):
            head = 2 * (pairs * g + pp) + hh
            place = (r % FOX_HEADS == head) & (ln == spare[hh] + r // FOX_HEADS)
            extra = _dot_tn(parts, jnp.where(place, 1.0, 0.0).astype(BF16))
            own = (klane < hd) if hh == 0 else (klane >= hd)
            kb_ref[2 * pp + hh] = jnp.where(own, k, extra.astype(BF16))
            for jj in range(nk):
                vt = v_ref[2 * pp + hh, :, jj * tk:(jj + 1) * tk]
                vb_ref[2 * pp + hh, jj] = jnp.concatenate([vt, tail], axis=0).astype(BF16)

    qrow = lax.broadcasted_iota(jnp.int32, (LANES, tk), 0)
    key = lax.broadcasted_iota(jnp.int32, (tk, tk), 0)
    qry = lax.broadcasted_iota(jnp.int32, (tk, tk), 1)
    causal = key <= qry

    chains = [(slot, part) for slot in range(2 * pairs) for part in range(2)]
    everyone = list(range(len(chains)))
    second = [n for n in everyone if chains[n][1] == 1]

    def q_operands(qi):
        qs = []
        for slot, part in chains:
            pp, hh = divmod(slot, 2)
            q = q_ref[qi, pp * LANES:(pp + 1) * LANES, part * tk:(part + 1) * tk]
            own = (qrow < hd) if hh == 0 else (qrow >= hd)
            ones = (qrow >= spare[hh]) & (qrow < spare[hh] + 3)
            qs.append(jnp.where(own, q, jnp.where(ones, 1.0, 0.0).astype(BF16)))
        return qs

    steps = []
    for qi in range(nq):
        steps += [(qi, j, everyone, None) for j in range(2 * qi)]
        steps += [(qi, 2 * qi, everyone, 0), (qi, 2 * qi + 1, second, 1)]

    qs_of = {}

    def scores(step, buf):
        qi, j, live, _ = step
        if qi not in qs_of:
            qs_of.clear()
            qs_of[qi] = q_operands(qi)
        for n in live:
            s_ref[buf, n] = _dot(kb_ref[chains[n][0], j * tk:(j + 1) * tk, :], qs_of[qi][n])

    def values(step, buf):
        _, j, live, _ = step
        return {n: _dot(vb_ref[chains[n][0], j], p_ref[buf, n]) for n in live}

    def finish(qi):
        for part in range(2):
            heads = [acc_ref[qi % 2, 2 * slot + part] for slot in range(2 * pairs)]
            o_ref[qi, :, part * tk:(part + 1) * tk] = jnp.concatenate(
                [a[:hd] / a[hd:hd + 1] for a in heads], axis=0).astype(BF16)

    def accumulate(step, alphas, pv):
        qi, j, live, _ = step
        for n in live:
            acc_ref[qi % 2, n] = pv[n] if j == 0 else alphas[n] * acc_ref[qi % 2, n] + pv[n]

    ms = {}
    scores(steps[0], 0)
    prev, prev_alphas = None, None
    for t, step in enumerate(steps):
        qi, j, live, masked_part = step
        if t + 1 < len(steps):
            scores(steps[t + 1], (t + 1) % 2)
        pv = values(prev, (t - 1) % 2) if prev is not None else None
        alphas = {}
        for n in live:
            s = s_ref[t % 2, n]
            if chains[n][1] == masked_part:
                s = jnp.where(causal, s, -jnp.inf)
            m_new = jnp.max(s, axis=0, keepdims=True)
            if j > 0:
                m_new = jnp.maximum(ms[n], m_new)
                alphas[n] = jnp.exp2(ms[n] - m_new)
            p_ref[t % 2, n] = jnp.exp2(s - m_new).astype(BF16)
            ms[n] = m_new
        if prev is not None:
            accumulate(prev, prev_alphas, pv)
            if prev[0] != qi:
                finish(prev[0])
        prev, prev_alphas = step, alphas
    accumulate(prev, prev_alphas, values(prev, (len(steps) - 1) % 2))
    finish(prev[0])


def _fox_prompt(qt, kt, vt, ct, batch, seq, tk=256, pairs=2):
    tq = 2 * tk
    nq = seq // tq
    nk = seq // tk
    heads = 2 * pairs
    chains = 2 * heads
    vrows = FOX_HEAD_DIM + 16
    assert qt.shape == (batch, nq, FOX_WIDTH, tq) and FOX_HEADS % heads == 0
    q_spec = pl.BlockSpec((None, nq, pairs * LANES, tq), lambda b, p: (b, 0, p, 0))
    kv_spec = pl.BlockSpec((None, heads, FOX_HEAD_DIM, seq), lambda b, p: (b, p, 0, 0))
    return pl.pallas_call(
        functools.partial(_fox_prompt_kernel, tk=tk),
        grid=(batch, FOX_HEADS // heads),
        in_specs=[q_spec, kv_spec, kv_spec,
                  pl.BlockSpec((FOX_HEADS, seq), lambda b, p: (0, b))],
        out_specs=q_spec,
        out_shape=jax.ShapeDtypeStruct((batch, nq, FOX_WIDTH, tq), BF16),
        scratch_shapes=[pltpu.VMEM((heads, seq, LANES), BF16), pltpu.VMEM((heads, nk, vrows, tk), BF16),
                        pltpu.VMEM((2, chains, tk, tk), F32), pltpu.VMEM((2, chains, tk, tk), BF16),
                        pltpu.VMEM((2, chains, vrows, tk), F32)],
        compiler_params=pltpu.CompilerParams(dimension_semantics=("arbitrary", "arbitrary"),
                                             vmem_limit_bytes=VMEM_LIMIT),
        name="fox_prompt",
    )(qt, kt, vt, ct)


def _fox_sample_kernel(q_ref, kn_ref, vn_ref, cn_ref, lft_ref, mlow_ref, ck_ref, cv_ref,
                       o_ref, qh_ref, suf_ref, m_ref, l_ref, acc_ref, *, tp, nt, tn):
    j = pl.program_id(1)
    blk = 2 * LANES
    hd = FOX_HEAD_DIM

    @pl.when(j == 0)
    def _init():
        for h in range(FOX_HEADS):
            qh_ref[h] = q_ref[:, h * hd:(h + 1) * hd]
        m_ref[...] = jnp.full(m_ref.shape, NEG_BIG, F32)
        l_ref[...] = jnp.zeros(l_ref.shape, F32)
        acc_ref[...] = jnp.zeros(acc_ref.shape, F32)
        carry = jnp.zeros((FOX_HEADS, 1), F32)
        per_tile = tp // blk
        for b in reversed(range(nt * per_tile)):
            x = lft_ref[:, b * blk:(b + 1) * blk]
            parts = jnp.concatenate(_split3_f32(x), axis=0).astype(BF16)
            y = _dot(parts, mlow_ref[...])
            off = (b % per_tile) * blk
            suf_ref[b // per_tile, :, off:off + blk] = y[0:8] + y[8:16] + y[16:24] + carry
            carry = carry + jnp.sum(x, axis=1, keepdims=True)

    def update(s, pv):
        m, l, acc = _softmax_step(s, (m_ref[...], l_ref[...], acc_ref[...]), pv)
        m_ref[...] = m
        l_ref[...] = l
        acc_ref[...] = acc

    bmm = lambda a, b, ca, cb: lax.dot_general(a, b, (((ca,), (cb,)), ((0,), (0,))),
                                               preferred_element_type=F32)
    qh = qh_ref[...]

    kt = ck_ref[...].astype(BF16)
    vt = cv_ref[...].astype(BF16)
    s = bmm(qh, kt, 2, 1).reshape(FOX_HEADS * tn, tp) + jnp.repeat(suf_ref[j], tn, axis=0)
    update(s.reshape(FOX_HEADS, tn, tp), lambda pm: bmm(pm, vt, 2, 2))

    @pl.when(j == nt - 1)
    def _fin():
        per_head = lambda ref: jnp.stack([ref[:, h * hd:(h + 1) * hd] for h in range(FOX_HEADS)],
                                         axis=0).astype(BF16)
        kn = per_head(kn_ref)
        vn = per_head(vn_ref)
        r = lax.broadcasted_iota(jnp.int32, (FOX_HEADS, tn, tn), 1)
        c = lax.broadcasted_iota(jnp.int32, (FOX_HEADS, tn, tn), 2)
        s = jnp.where(c <= r, bmm(qh, kn, 2, 2) - cn_ref[...][:, None, :], -jnp.inf)
        update(s, lambda pm: bmm(pm, vn, 2, 1))
        o = acc_ref[...] / l_ref[...]
        o_ref[...] = jnp.concatenate([o[h] for h in range(FOX_HEADS)], axis=1).astype(BF16)


def _fox_sample(q, kn, vn, cn, lft, ck, cv, batch, tn, past, tp=4096):
    nt = past // tp
    blk = 2 * LANES
    idx = np.arange(blk)
    mlow = jnp.asarray((idx[:, None] > idx[None, :]).astype(np.float32), dtype=BF16)
    per_b = lambda w: pl.BlockSpec((tn, w), lambda b, j: (b, 0))
    cache = pl.BlockSpec((None, FOX_HEADS, FOX_HEAD_DIM, tp), lambda b, j: (b, 0, 0, j))
    return pl.pallas_call(
        functools.partial(_fox_sample_kernel, tp=tp, nt=nt, tn=tn),
        grid=(batch, nt),
        in_specs=[per_b(FOX_WIDTH), per_b(FOX_WIDTH), per_b(FOX_WIDTH),
                  pl.BlockSpec((None, FOX_HEADS, tn), lambda b, j: (b, 0, 0)),
                  pl.BlockSpec((None, FOX_HEADS, past), lambda b, j: (b, 0, 0)),
                  pl.BlockSpec((blk, blk), lambda b, j: (0, 0)),
                  cache, cache],
        out_specs=per_b(FOX_WIDTH),
        out_shape=jax.ShapeDtypeStruct((batch * tn, FOX_WIDTH), BF16),
        scratch_shapes=[pltpu.VMEM((FOX_HEADS, tn, FOX_HEAD_DIM), BF16),
                        pltpu.VMEM((nt, FOX_HEADS, tp), F32),
                        pltpu.VMEM((FOX_HEADS, tn, 1), F32), pltpu.VMEM((FOX_HEADS, tn, 1), F32),
                        pltpu.VMEM((FOX_HEADS, tn, FOX_HEAD_DIM), F32)],
        compiler_params=pltpu.CompilerParams(dimension_semantics=("arbitrary", "arbitrary"),
                                             vmem_limit_bytes=VMEM_LIMIT),
        name="fox_sample",
    )(q, kn, vn, cn, lft, mlow, ck, cv)


def _gla_kernel(q_ref, k_ref, v_ref, g_ref, r_ref, s0_ref, gn_ref, w_ref, lv_ref,
                o_ref, s_ref, st_ref, *, tb, levels, blocks, seqs, carried):
    to_work = lambda s0: s0.reshape(GLA_KW, GLA_DV).T
    from_work = lambda st: st.T.reshape(GLA_HEADS, GLA_DK, GLA_DV)
    if carried:
        @pl.when(pl.program_id(1) == 0)
        def _():
            st_ref[...] = to_work(s0_ref[0])

    for sq in range(seqs):
        st = st_ref[...] if carried else to_work(s0_ref[sq])
        for blk in range(blocks):
            rows = lambda ref: ref.at[pl.ds((sq * blocks + blk) * tb, tb), :]
            st = _gla_block(rows(q_ref), rows(k_ref), rows(v_ref), rows(g_ref), rows(r_ref), gn_ref,
                            w_ref, lv_ref, rows(o_ref), st, tb=tb, levels=levels)
        if carried:
            st_ref[...] = st

            @pl.when(pl.program_id(1) == pl.num_programs(1) - 1)
            def _():
                s_ref[0] = from_work(st)
        else:
            s_ref[sq] = from_work(st)


def _gla_block(q_ref, k_ref, v_ref, g_ref, r_ref, gn_ref, w_ref, lv_ref, o_ref, st, *, tb, levels):
    half = LANES // 2
    r = r_ref[...]
    gate = gn_ref[...] * (r * _sigmoid(r))

    g2 = g_ref[...] * LOG2E
    gp = jnp.concatenate(_split2_f32(g2), axis=0).astype(BF16)
    rows = w_ref.shape[0] // 2
    cum = jnp.concatenate([_dot(w_ref[0:rows, :], gp), _dot(w_ref[rows:, :], gp)], axis=0)
    dec = jnp.exp2(cum)
    from_start = dec[0:tb]
    to_end = jnp.exp2(cum[tb - 1:tb] - cum[0:tb])

    q = q_ref[...] * (GLA_DK ** -0.5)
    k = k_ref[...]
    v = v_ref[...]
    row = lax.broadcasted_iota(jnp.int32, (tb, GLA_KW), 0)
    low = lax.broadcasted_iota(jnp.int32, (tb, LANES), 1) < half
    lv = lv_ref[...]

    def pair_scores(xq, xk):
        outs = []
        for p in range(GLA_HEADS // 2):
            a = xq[:, p * LANES:(p + 1) * LANES]
            zero = jnp.zeros_like(a)
            lhs = jnp.concatenate([jnp.where(low, a, zero), jnp.where(low, zero, a)], axis=0)
            outs.append(_dot_nt(lhs, xk[:, p * LANES:(p + 1) * LANES]))
        return outs

    here = lv == -1
    a = [jnp.where(here, r_, 0.0) for r_ in pair_scores(q.astype(BF16), k.astype(BF16))]
    for l in range(levels):
        upper = ((row >> l) & 1) == 1
        e = jnp.where(upper, jnp.exp2(g2), 1.0) if l == 0 else dec[l * tb:(l + 1) * tb]
        x = (jnp.where(upper, q, k) * e).astype(BF16)
        here = lv == l
        a = [jnp.where(here, r_, a_) for r_, a_ in zip(pair_scores(x, x), a)]
    o = jnp.concatenate(
        [_dot(a[h // 2][(h % 2) * tb:(h % 2 + 1) * tb].astype(BF16), v[:, h * GLA_DV:(h + 1) * GLA_DV])
         for h in range(GLA_HEADS)], axis=1)

    lane = lax.broadcasted_iota(jnp.int32, (tb, GLA_KW), 1)
    head_sel = [(lane >= h * GLA_DK) & (lane < (h + 1) * GLA_DK) for h in range(GLA_HEADS)]
    qt = (q * from_start).astype(BF16)
    kt = (k * to_end).astype(BF16)
    zero = jnp.zeros_like(qt)
    q4 = jnp.concatenate([jnp.where(sel, qt, zero) for sel in head_sel], axis=0)
    oi = _dot_nt(q4, st.astype(BF16))
    o = o + jnp.concatenate([oi[h * tb:(h + 1) * tb] for h in range(GLA_HEADS)], axis=1)
    upd = None
    for h in range(GLA_HEADS):
        u = _dot_tn(v[:, h * GLA_DV:(h + 1) * GLA_DV], jnp.where(head_sel[h], kt, zero))
        upd = u if upd is None else upd + u
    st = from_start[tb - 1:tb, :] * st + upd

    outs = []
    for h in range(GLA_HEADS):
        oh = o[:, h * GLA_DV:(h + 1) * GLA_DV]
        outs.append(oh * lax.rsqrt(jnp.mean(oh * oh, axis=-1, keepdims=True) + EPS))
    o_ref[...] = (jnp.concatenate(outs, axis=1) * gate).astype(BF16)
    return st


def _gla(gq, gk, gv, glog, gr, s0, gn, batch, seq, tb=128, rows_per_step=1024):
    tb = min(tb, seq)
    step_rows = min(rows_per_step, batch * seq)
    carried = seq > step_rows
    seqs = 1 if carried else step_rows // seq
    per_seq = (step_rows if carried else seq) // tb
    nt = seq // (tb * per_seq)
    n = batch * seq
    levels = tb.bit_length() - 1
    assert tb == 1 << levels
    ti = np.arange(tb)[:, None]
    si = np.arange(tb)[None, :]
    blocks = [si <= ti]
    for l in range(1, levels):
        m = 1 << l
        mid = ti - ti % (2 * m) + m - 1
        upper = ti % (2 * m) >= m
        blocks.append(np.where(upper, (si > mid) & (si <= ti), (si > ti) & (si <= mid)))
    w = np.concatenate(blocks, axis=0).astype(np.float32)
    w = jnp.asarray(np.concatenate([w, w], axis=1), dtype=BF16)
    x = ti ^ si
    lv = np.where(si < ti, np.floor(np.log2(np.maximum(x, 1))).astype(np.int32),
                  np.where(si == ti, -1, -2)).astype(np.int32)
    lv = jnp.asarray(np.concatenate([lv, lv], axis=0))
    row = lambda w_: pl.BlockSpec((step_rows, w_), lambda b, t: (b * nt + t, 0))
    state = pl.BlockSpec((seqs, GLA_HEADS, GLA_DK, GLA_DV), lambda b, t: (b, 0, 0, 0))
    const = lambda shape: pl.BlockSpec(shape, lambda b, t: (0, 0))
    return pl.pallas_call(
        functools.partial(_gla_kernel, tb=tb, levels=levels, blocks=per_seq, seqs=seqs, carried=carried),
        grid=(batch // seqs, nt),
        in_specs=[row(GLA_KW), row(GLA_KW), row(GLA_VW), row(GLA_KW), row(GLA_VW), state,
                  const((1, GLA_VW)), const(w.shape), const(lv.shape)],
        out_specs=(row(GLA_VW), state),
        out_shape=(jax.ShapeDtypeStruct((n, GLA_VW), BF16),
                   jax.ShapeDtypeStruct((batch, GLA_HEADS, GLA_DK, GLA_DV), F32)),
        scratch_shapes=[pltpu.VMEM((GLA_DV, GLA_KW), F32)],
        compiler_params=pltpu.CompilerParams(dimension_semantics=("arbitrary", "arbitrary"),
                                             vmem_limit_bytes=VMEM_LIMIT),
        name="gla",
    )(gq, gk, gv, glog, gr, s0, gn, w, lv)


def _ffn_kernel(x_ref, fo_ref, go_ref, wo_ref, g2_ref, wg_ref, wu_ref, wd_ref, gf_ref,
                y_ref, a_ref, *, chunk, fox_time_minor):
    tm = x_ref.shape[0]
    parts = fo_ref.shape[0] if fox_time_minor else 2
    halves = [slice(i * (tm // parts), (i + 1) * (tm // parts)) for i in range(parts)]
    y1 = []
    for i, rows in enumerate(halves):
        fox = (_dot_tn(fo_ref[i], wo_ref[0:FOX_WIDTH, :]) if fox_time_minor
               else _dot(fo_ref[rows, :], wo_ref[0:FOX_WIDTH, :]))
        y1.append(x_ref[rows, :] + fox + _dot(go_ref[rows, :], wo_ref[FOX_WIDTH:, :]))
    h2 = jnp.concatenate([_rms(y, g2_ref[...]).astype(BF16) for y in y1], axis=0)
    for c in range(D_FF // chunk):
        cs = slice(c * chunk, (c + 1) * chunk)
        u = _dot(h2, wg_ref[:, cs])
        w = _dot(h2, wu_ref[:, cs])
        a_ref[:, cs] = (u * _sigmoid(u) * w).astype(BF16)
    y2 = [y + _dot(a_ref[rows, :], wd_ref[...]) for y, rows in zip(y1, halves)]
    for y, rows in zip(y2, halves):
        y_ref[rows, :] = _rms(y, gf_ref[...])


def _ffn(x2d, fo, go, prm, tm=1024, chunk=256):
    n = x2d.shape[0]
    tm = min(tm, n)
    row = lambda w: pl.BlockSpec((tm, w), lambda i: (i, 0))
    const = lambda shape: pl.BlockSpec(shape, lambda i: (0, 0), pipeline_mode=pl.Buffered(1))
    fox_time_minor = fo.ndim == 4
    if fox_time_minor:
        per_tile = tm // fo.shape[3]
        tiles_per_seq = fo.shape[1] // per_tile
        assert per_tile * fo.shape[3] == tm and tiles_per_seq * per_tile == fo.shape[1]
        fo_spec = pl.BlockSpec((None, per_tile, FOX_WIDTH, fo.shape[3]),
                               lambda i: (i // tiles_per_seq, i % tiles_per_seq, 0, 0))
    else:
        fo_spec = row(FOX_WIDTH)
    return pl.pallas_call(
        functools.partial(_ffn_kernel, chunk=chunk, fox_time_minor=fox_time_minor),
        grid=(n // tm,),
        in_specs=[row(D_MODEL), fo_spec, row(GLA_VW), const((D_MODEL, D_MODEL)),
                  const((1, D_MODEL)), const((D_MODEL, D_FF)), const((D_MODEL, D_FF)),
                  const((D_FF, D_MODEL)), const((1, D_MODEL))],
        out_specs=row(D_MODEL),
        out_shape=jax.ShapeDtypeStruct((n, D_MODEL), F32),
        scratch_shapes=[pltpu.VMEM((tm, D_FF), BF16)],
        compiler_params=pltpu.CompilerParams(dimension_semantics=("arbitrary",),
                                             vmem_limit_bytes=VMEM_LIMIT),
        name="ffn",
    )(x2d, fo, go, prm['wo'], prm['g2'], prm['wg'], prm['wu'], prm['wd'], prm['gf'])


def _layer_params(layer, norm1_g, w_in, w_gate2, b_gate2, b_forget, gla_norm_g, w_out,
                  norm2_g, w_gate, w_up, w_down, final_norm_g):
    wt = jnp.transpose(w_in[layer])
    o_fl = 3 * FOX_WIDTH
    o_gq = o_fl + FOX_HEADS
    o_gg = o_gq + 2 * GLA_KW + GLA_VW
    o_gr = o_gg + GLA_GATE_RANK
    tail = 2 * GLA_GATE_RANK
    pad = jnp.zeros((tail - FOX_HEADS - GLA_GATE_RANK, D_MODEL), F32)
    w = jnp.concatenate([wt[:o_fl], wt[o_gq:o_gg], wt[o_gr:], wt[o_fl:o_gq], wt[o_gg:o_gr], pad], axis=0)
    wg2 = jnp.zeros((tail, GLA_KW), F32).at[FOX_HEADS:FOX_HEADS + GLA_GATE_RANK].set(w_gate2[layer])
    return dict(
        g1=norm1_g[layer].reshape(1, D_MODEL),
        w=w.astype(BF16),
        wg2=wg2.astype(BF16),
        bg2=b_gate2[layer].reshape(1, GLA_KW),
        bfc=b_forget[layer].reshape(FOX_HEADS, 1),
        gn=gla_norm_g[layer].reshape(1, GLA_VW),
        wo=w_out[layer].astype(BF16),
        g2=norm2_g[layer].reshape(1, D_MODEL),
        wg=w_gate[layer].astype(BF16),
        wu=w_up[layer].astype(BF16),
        wd=w_down[layer].astype(BF16),
        gf=final_norm_g.reshape(1, D_MODEL),
    )


def kernel(x_prompt, x_sample, cache_fox_k, cache_fox_v, cache_fox_logf, state_gla, norm1_g, w_in,
           w_gate2, b_gate2, b_forget, gla_norm_g, w_out, norm2_g, w_gate, w_up, w_down, final_norm_g):
    depth = w_in.shape[0]
    assert depth == 1, "the final rmsnorm is fused into the layer's ffn kernel"
    bp, tp_, _ = x_prompt.shape
    bs, ts, _ = x_sample.shape
    past = cache_fox_k.shape[2]
    layer = 0
    prm = _layer_params(layer, norm1_g, w_in, w_gate2, b_gate2, b_forget, gla_norm_g, w_out,
                        norm2_g, w_gate, w_up, w_down, final_norm_g)
    by_time = lambda a, b, t: a.reshape(FOX_HEADS, b, t).transpose(1, 2, 0)[None]

    xp = x_prompt.reshape(bp * tp_, D_MODEL)
    qt, kt_p, vt_p, lf_p, ct, gq, gk, gv, glog, gr = _proj(xp, bp, tp_, prm, True)
    fox_o = _fox_prompt(qt, kt_p, vt_p, ct, bp, tp_)
    s0 = jnp.zeros((bp, GLA_HEADS, GLA_DK, GLA_DV), F32)
    gla_o, s_p = _gla(gq, gk, gv, glog, gr, s0, prm['gn'], bp, tp_)
    y_p = _ffn(xp, fox_o, gla_o, prm)

    xs = x_sample.reshape(bs * ts, D_MODEL)
    q, k_s, v_s, lf_s, ct, gq, gk, gv, glog, gr = _proj(xs, bs, ts, prm, False)
    cn = ct.reshape(FOX_HEADS, bs, ts).transpose(1, 0, 2)
    lft = cache_fox_logf[layer].astype(F32).transpose(0, 2, 1)
    ck = cache_fox_k[layer].transpose(0, 2, 3, 1)
    cv = cache_fox_v[layer].transpose(0, 2, 3, 1)
    fox_o = _fox_sample(q, k_s, v_s, cn, lft, ck, cv, bs, ts, past)
    gla_o, s_s = _gla(gq, gk, gv, glog, gr, state_gla[layer].astype(F32), prm['gn'], bs, ts)
    y_s = _ffn(xs, fox_o, gla_o, prm)

    heads = lambda a, b, t: a.reshape(1, b, t, FOX_HEADS, FOX_HEAD_DIM)
    return (y_p.reshape(bp, tp_, D_MODEL), y_s.reshape(bs, ts, D_MODEL),
            kt_p.transpose(0, 3, 1, 2)[None], vt_p.transpose(0, 3, 1, 2)[None],
            by_time(lf_p, bp, tp_), s_p[None],
            heads(k_s, bs, ts), heads(v_s, bs, ts), by_time(lf_s, bs, ts), s_s[None])
```

```python
import functools

import numpy as np
import jax
import jax.numpy as jnp
from jax import lax
from jax.experimental import pallas as pl
from jax.experimental.pallas import tpu as pltpu

D_MODEL = 1024
FOX_HEADS = 8
FOX_HEAD_DIM = 64
FOX_WIDTH = FOX_HEADS * FOX_HEAD_DIM
GLA_HEADS = 4
GLA_DK = 64
GLA_DV = 128
GLA_KW = GLA_HEADS * GLA_DK
GLA_VW = GLA_HEADS * GLA_DV
GLA_GATE_RANK = 16
GLA_GATE_TEMP = 16.0
D_FF = 2816
EPS = 1e-6

LANES = 128
LOG2E = 1.4426950408889634
VMEM_LIMIT = 56 * 1024 * 1024

F32 = jnp.float32
BF16 = jnp.bfloat16
NEG_BIG = -1e30


def _log_sigmoid(x):
    return jnp.minimum(x, 0.0) - jnp.log1p(jnp.exp(-jnp.abs(x)))


def _sigmoid(x):
    return 1.0 / (1.0 + jnp.exp(-x))


def _split3_f32(x):
    hi = x.astype(BF16).astype(F32)
    r = x - hi
    mid = r.astype(BF16).astype(F32)
    lo = (r - mid).astype(BF16).astype(F32)
    return hi, mid, lo


def _split2_f32(x):
    hi = x.astype(BF16).astype(F32)
    return hi, (x - hi).astype(BF16).astype(F32)


def _dot(a, b):
    return jnp.dot(a, b, preferred_element_type=F32)


def _dot_nt(a, b):
    return lax.dot_general(a, b, (((1,), (1,)), ((), ())), preferred_element_type=F32)


def _dot_tn(a, b):
    return lax.dot_general(a, b, (((0,), (0,)), ((), ())), preferred_element_type=F32)


def _rms(x, g):
    return x * lax.rsqrt(jnp.mean(x * x, axis=-1, keepdims=True) + EPS) * g


def _softmax_step(s, carry, pv):
    m, l, acc = carry
    m_new = jnp.maximum(m, jnp.max(s, axis=-1, keepdims=True))
    alpha = jnp.exp(m - m_new)
    pm = jnp.exp(s - m_new)
    l = alpha * l + jnp.sum(pm, axis=-1, keepdims=True)
    acc = alpha * acc + pv(pm.astype(BF16))
    return m_new, l, acc


def _proj_kernel(x_ref, g1_ref, w_ref, wg2_ref, bg2_ref, bfc_ref, tri_ref,
                 *rest, tiles_per_seq, time_minor):
    o_kv, o_b = FOX_WIDTH, 3 * FOX_WIDTH
    o_tail = o_b + 2 * GLA_KW + 2 * GLA_VW
    carry_ref = rest[-1]

    @pl.when(pl.program_id(0) % tiles_per_seq == 0)
    def _():
        carry_ref[...] = jnp.zeros_like(carry_ref)

    h = _rms(x_ref[...], g1_ref[...]).astype(BF16)
    tm = h.shape[0]
    scale = FOX_HEAD_DIM ** -0.5

    zc = _dot_nt(h, w_ref[o_tail:, :])
    fl_t = _dot_nt(w_ref[o_tail:, :], h)[0:FOX_HEADS]

    if time_minor:
        (q_ref, k_ref, v_ref, lft_ref, ct_ref, gq_ref, gk_ref, gv_ref, glog_ref, gr_ref,
         carry_ref) = rest
        qt = (_dot_nt(w_ref[0:o_kv, :], h) * (scale * LOG2E)).astype(BF16)
        tq = q_ref.shape[2]
        for i in range(q_ref.shape[0]):
            q_ref[i] = qt[:, i * tq:(i + 1) * tq]
    else:
        (q_ref, k_ref, v_ref, lft_ref, ct_ref, gq_ref, gk_ref, gv_ref, glog_ref, gr_ref,
         carry_ref) = rest
        q_ref[...] = (_dot_nt(h, w_ref[0:o_kv, :]) * scale).astype(BF16)

    logf_t = _log_sigmoid(fl_t + bfc_ref[...])
    lft_ref[...] = logf_t
    parts = jnp.concatenate(_split3_f32(logf_t), axis=0).astype(BF16)
    gg = zc.astype(BF16)

    if time_minor:
        kvt = _dot_nt(w_ref[o_kv:o_b, :], h)
        k_ref[...] = kvt[:FOX_WIDTH].reshape(FOX_HEADS, FOX_HEAD_DIM, tm)
        v_ref[...] = kvt[FOX_WIDTH:].reshape(FOX_HEADS, FOX_HEAD_DIM, tm)
    else:
        kv = _dot_nt(h, w_ref[o_kv:o_b, :])
        k_ref[...] = kv[:, :FOX_WIDTH]
        v_ref[...] = kv[:, FOX_WIDTH:]

    gpre = _dot(gg, wg2_ref[...]) + bg2_ref[...]
    cs = _dot(parts, tri_ref[...])

    zb = _dot_nt(h, w_ref[o_b:o_tail, :])
    gq_ref[...] = zb[:, :GLA_KW]
    gk_ref[...] = zb[:, GLA_KW:2 * GLA_KW]
    gv_ref[...] = zb[:, 2 * GLA_KW:2 * GLA_KW + GLA_VW].astype(BF16)
    gr_ref[...] = zb[:, 2 * GLA_KW + GLA_VW:]

    glog_ref[...] = _log_sigmoid(gpre) * (1.0 / GLA_GATE_TEMP)

    ct = cs[0:8] + cs[8:16] + cs[16:24] + carry_ref[:, 0:1]
    ct_ref[...] = ct
    carry_ref[...] = jnp.broadcast_to(ct[:, tm - 1:], carry_ref.shape)


def _proj(x2d, batch, seq_len, prm, time_minor, tm=1024, tq=512):
    n = x2d.shape[0]
    tm = min(tm, n)
    tiles_per_seq = max(seq_len // tm, 1)
    per_tile = tm // tq
    idx = np.arange(tm)
    tri = ((idx[:, None] <= idx[None, :]) & (idx[:, None] // seq_len == idx[None, :] // seq_len))
    tri = jnp.asarray(tri.astype(np.float32), dtype=BF16)
    const = lambda a: pl.BlockSpec(a.shape, lambda i: (0, 0), pipeline_mode=pl.Buffered(1))
    row = lambda w: pl.BlockSpec((tm, w), lambda i: (i, 0))
    col = pl.BlockSpec((FOX_HEADS, tm), lambda i: (0, i))
    sds = jax.ShapeDtypeStruct
    if time_minor:
        kv_shape = sds((batch, FOX_HEADS, FOX_HEAD_DIM, seq_len), F32)
        kv_spec = pl.BlockSpec((None, FOX_HEADS, FOX_HEAD_DIM, tm),
                               lambda i: (i // tiles_per_seq, 0, 0, i % tiles_per_seq))
        fox = [(sds((batch, seq_len // tq, FOX_WIDTH, tq), BF16),
                pl.BlockSpec((None, per_tile, FOX_WIDTH, tq),
                             lambda i: (i // tiles_per_seq, i % tiles_per_seq, 0, 0))),
               (kv_shape, kv_spec), (kv_shape, kv_spec)]
    else:
        fox = [(sds((n, FOX_WIDTH), BF16), row(FOX_WIDTH)),
               (sds((n, FOX_WIDTH), F32), row(FOX_WIDTH)),
               (sds((n, FOX_WIDTH), F32), row(FOX_WIDTH))]
    outs = fox + [
        (sds((FOX_HEADS, n), F32), col),
        (sds((FOX_HEADS, n), F32), col),
        (sds((n, GLA_KW), F32), row(GLA_KW)),
        (sds((n, GLA_KW), F32), row(GLA_KW)),
        (sds((n, GLA_VW), BF16), row(GLA_VW)),
        (sds((n, GLA_KW), F32), row(GLA_KW)),
        (sds((n, GLA_VW), F32), row(GLA_VW)),
    ]
    names = ('g1', 'w', 'wg2', 'bg2', 'bfc')
    return pl.pallas_call(
        functools.partial(_proj_kernel, tiles_per_seq=tiles_per_seq, time_minor=time_minor),
        grid=(n // tm,),
        in_specs=[row(D_MODEL)] + [const(prm[k]) for k in names] + [const(tri)],
        out_specs=tuple(s for _, s in outs), out_shape=tuple(s for s, _ in outs),
        scratch_shapes=[pltpu.VMEM((FOX_HEADS, LANES), F32)],
        compiler_params=pltpu.CompilerParams(dimension_semantics=("arbitrary",),
                                             vmem_limit_bytes=VMEM_LIMIT),
        name="proj",
    )(x2d, *[prm[k] for k in names], tri)


def _fox_prompt_kernel(q_ref, k_ref, v_ref, c_ref, o_ref, kb_ref, vb_ref, s_ref, p_ref, acc_ref,
                       *, tk):
    g = pl.program_id(1)
    nq = q_ref.shape[0]
    nk = vb_ref.shape[1]
    pairs = q_ref.shape[1] // LANES
    hd = FOX_HEAD_DIM
    spare = (hd, 0)
    vrows = vb_ref.shape[2]

    parts = jnp.concatenate(_split3_f32(c_ref[...] * (-LOG2E)), axis=0).astype(BF16)
    r = lax.broadcasted_iota(jnp.int32, (3 * FOX_HEADS, LANES), 0)
    ln = lax.broadcasted_iota(jnp.int32, (3 * FOX_HEADS, LANES), 1)
    klane = lax.broadcasted_iota(jnp.int32, (k_ref.shape[2], LANES), 1)
    tail = jnp.where(lax.broadcasted_iota(jnp.int32, (vrows - hd, tk), 0) == 0, 1.0, 0.0)
    for pp in range(pairs):
        k = k_ref[2 * pp:2 * pp + 2].reshape(LANES, k_ref.shape[2]).T.astype(BF16)
        for hh in range(2):
            head = 2 * (pairs * g + pp) + hh
            place = (r % FOX_HEADS == head) & (ln == spare[hh] + r // FOX_HEADS)
            extra = _dot_tn(parts, jnp.where(place, 1.0, 0.0).astype(BF16))
            own = (klane < hd) if hh == 0 else (klane >= hd)
            kb_ref[2 * pp + hh] = jnp.where(own, k, extra.astype(BF16))
            for jj in range(nk):
                vt = v_ref[2 * pp + hh, :, jj * tk:(jj + 1) * tk]
                vb_ref[2 * pp + hh, jj] = jnp.concatenate([vt, tail], axis=0).astype(BF16)

    qrow = lax.broadcasted_iota(jnp.int32, (LANES, tk), 0)
    key = lax.broadcasted_iota(jnp.int32, (tk, tk), 0)
    qry = lax.broadcasted_iota(jnp.int32, (tk, tk), 1)
    causal = key <= qry

    chains = [(slot, part) for slot in range(2 * pairs) for part in range(2)]
    everyone = list(range(len(chains)))
    second = [n for n in everyone if chains[n][1] == 1]

    def q_operands(qi):
        qs = []
        for slot, part in chains:
            pp, hh = divmod(slot, 2)
            q = q_ref[qi, pp * LANES:(pp + 1) * LANES, part * tk:(part + 1) * tk]
            own = (qrow < hd) if hh == 0 else (qrow >= hd)
            ones = (qrow >= spare[hh]) & (qrow < spare[hh] + 3)
            qs.append(jnp.where(own, q, jnp.where(ones, 1.0, 0.0).astype(BF16)))
        return qs

    steps = []
    for qi in range(nq):
        steps += [(qi, j, everyone, None) for j in range(2 * qi)]
        steps += [(qi, 2 * qi, everyone, 0), (qi, 2 * qi + 1, second, 1)]

    qs_of = {}

    def scores(step, buf):
        qi, j, live, _ = step
        if qi not in qs_of:
            qs_of.clear()
            qs_of[qi] = q_operands(qi)
        for n in live:
            s_ref[buf, n] = _dot(kb_ref[chains[n][0], j * tk:(j + 1) * tk, :], qs_of[qi][n])

    def values(step, buf):
        _, j, live, _ = step
        return {n: _dot(vb_ref[chains[n][0], j], p_ref[buf, n]) for n in live}

    def finish(qi):
        for part in range(2):
            heads = [acc_ref[qi % 2, 2 * slot + part] for slot in range(2 * pairs)]
            o_ref[qi, :, part * tk:(part + 1) * tk] = jnp.concatenate(
                [a[:hd] / a[hd:hd + 1] for a in heads], axis=0).astype(BF16)

    def accumulate(step, alphas, pv):
        qi, j, live, _ = step
        for n in live:
            acc_ref[qi % 2, n] = pv[n] if j == 0 else alphas[n] * acc_ref[qi % 2, n] + pv[n]

    ms = {}
    scores(steps[0], 0)
    prev, prev_alphas = None, None
    for t, step in enumerate(steps):
        qi, j, live, masked_part = step
        if t + 1 < len(steps):
            scores(steps[t + 1], (t + 1) % 2)
        pv = values(prev, (t - 1) % 2) if prev is not None else None
        alphas = {}
        for n in live:
            s = s_ref[t % 2, n]
            if chains[n][1] == masked_part:
                s = jnp.where(causal, s, -jnp.inf)
            m_new = jnp.max(s, axis=0, keepdims=True)
            if j > 0:
                m_new = jnp.maximum(ms[n], m_new)
                alphas[n] = jnp.exp2(ms[n] - m_new)
            p_ref[t % 2, n] = jnp.exp2(s - m_new).astype(BF16)
            ms[n] = m_new
        if prev is not None:
            accumulate(prev, prev_alphas, pv)
            if prev[0] != qi:
                finish(prev[0])
        prev, prev_alphas = step, alphas
    accumulate(prev, prev_alphas, values(prev, (len(steps) - 1) % 2))
    finish(prev[0])


def _fox_prompt(qt, kt, vt, ct, batch, seq, tk=256, pairs=2):
    tq = 2 * tk
    nq = seq // tq
    nk = seq // tk
    heads = 2 * pairs
    chains = 2 * heads
    vrows = FOX_HEAD_DIM + 16
    assert qt.shape == (batch, nq, FOX_WIDTH, tq) and FOX_HEADS % heads == 0
    q_spec = pl.BlockSpec((None, nq, pairs * LANES, tq), lambda b, p: (b, 0, p, 0))
    kv_spec = pl.BlockSpec((None, heads, FOX_HEAD_DIM, seq), lambda b, p: (b, p, 0, 0))
    return pl.pallas_call(
        functools.partial(_fox_prompt_kernel, tk=tk),
        grid=(batch, FOX_HEADS // heads),
        in_specs=[q_spec, kv_spec, kv_spec,
                  pl.BlockSpec((FOX_HEADS, seq), lambda b, p: (0, b))],
        out_specs=q_spec,
        out_shape=jax.ShapeDtypeStruct((batch, nq, FOX_WIDTH, tq), BF16),
        scratch_shapes=[pltpu.VMEM((heads, seq, LANES), BF16), pltpu.VMEM((heads, nk, vrows, tk), BF16),
                        pltpu.VMEM((2, chains, tk, tk), F32), pltpu.VMEM((2, chains, tk, tk), BF16),
                        pltpu.VMEM((2, chains, vrows, tk), F32)],
        compiler_params=pltpu.CompilerParams(dimension_semantics=("arbitrary", "arbitrary"),
                                             vmem_limit_bytes=VMEM_LIMIT),
        name="fox_prompt",
    )(qt, kt, vt, ct)


def _fox_sample_kernel(q_ref, kn_ref, vn_ref, cn_ref, lft_ref, mlow_ref, ck_ref, cv_ref,
                       o_ref, qh_ref, suf_ref, m_ref, l_ref, acc_ref, *, tp, nt, tn, j=None):
    j = pl.program_id(1) if j is None else j
    blk = 2 * LANES
    hd = FOX_HEAD_DIM

    @pl.when(j == 0)
    def _init():
        for h in range(FOX_HEADS):
            qh_ref[h] = q_ref[:, h * hd:(h + 1) * hd]
        m_ref[...] = jnp.full(m_ref.shape, NEG_BIG, F32)
        l_ref[...] = jnp.zeros(l_ref.shape, F32)
        acc_ref[...] = jnp.zeros(acc_ref.shape, F32)
        carry = jnp.zeros((FOX_HEADS, 1), F32)
        per_tile = tp // blk
        for b in reversed(range(nt * per_tile)):
            x = lft_ref[:, b * blk:(b + 1) * blk]
            parts = jnp.concatenate(_split3_f32(x), axis=0).astype(BF16)
            y = _dot(parts, mlow_ref[...])
            off = (b % per_tile) * blk
            suf_ref[b // per_tile, :, off:off + blk] = y[0:8] + y[8:16] + y[16:24] + carry
            carry = carry + jnp.sum(x, axis=1, keepdims=True)

    def update(s, pv):
        m, l, acc = _softmax_step(s, (m_ref[...], l_ref[...], acc_ref[...]), pv)
        m_ref[...] = m
        l_ref[...] = l
        acc_ref[...] = acc

    bmm = lambda a, b, ca, cb: lax.dot_general(a, b, (((ca,), (cb,)), ((0,), (0,))),
                                               preferred_element_type=F32)
    qh = qh_ref[...]

    kt = ck_ref[...].astype(BF16)
    vt = cv_ref[...].astype(BF16)
    s = bmm(qh, kt, 2, 1).reshape(FOX_HEADS * tn, tp) + jnp.repeat(suf_ref[j], tn, axis=0)
    update(s.reshape(FOX_HEADS, tn, tp), lambda pm: bmm(pm, vt, 2, 2))

    @pl.when(j == nt - 1)
    def _fin():
        per_head = lambda ref: jnp.stack([ref[:, h * hd:(h + 1) * hd] for h in range(FOX_HEADS)],
                                         axis=0).astype(BF16)
        kn = per_head(kn_ref)
        vn = per_head(vn_ref)
        r = lax.broadcasted_iota(jnp.int32, (FOX_HEADS, tn, tn), 1)
        c = lax.broadcasted_iota(jnp.int32, (FOX_HEADS, tn, tn), 2)
        s = jnp.where(c <= r, bmm(qh, kn, 2, 2) - cn_ref[...][:, None, :], -jnp.inf)
        update(s, lambda pm: bmm(pm, vn, 2, 1))
        o = acc_ref[...] / l_ref[...]
        o_ref[...] = jnp.concatenate([o[h] for h in range(FOX_HEADS)], axis=1).astype(BF16)


def _fox_sample_call(batch, tn, past, tp, ix):
    nt = past // tp
    blk = 2 * LANES
    idx = np.arange(blk)
    mlow = jnp.asarray((idx[:, None] > idx[None, :]).astype(np.float32), dtype=BF16)
    per_b = lambda w: pl.BlockSpec((tn, w), lambda *g: (ix(*g)[0], 0))
    cache = pl.BlockSpec((None, FOX_HEADS, FOX_HEAD_DIM, tp), lambda *g: (ix(*g)[0], 0, 0, ix(*g)[1]))
    return dict(
        kernel=functools.partial(_fox_sample_kernel, tp=tp, nt=nt, tn=tn), nt=nt, consts=(mlow,),
        in_specs=[per_b(FOX_WIDTH), per_b(FOX_WIDTH), per_b(FOX_WIDTH),
                  pl.BlockSpec((None, FOX_HEADS, tn), lambda *g: (ix(*g)[0], 0, 0)),
                  pl.BlockSpec((None, FOX_HEADS, past), lambda *g: (ix(*g)[0], 0, 0)),
                  pl.BlockSpec((blk, blk), lambda *g: (0, 0)),
                  cache, cache],
        out_specs=[per_b(FOX_WIDTH)],
        out_shape=[jax.ShapeDtypeStruct((batch * tn, FOX_WIDTH), BF16)],
        scratch_shapes=[pltpu.VMEM((FOX_HEADS, tn, FOX_HEAD_DIM), BF16),
                        pltpu.VMEM((nt, FOX_HEADS, tp), F32),
                        pltpu.VMEM((FOX_HEADS, tn, 1), F32), pltpu.VMEM((FOX_HEADS, tn, 1), F32),
                        pltpu.VMEM((FOX_HEADS, tn, FOX_HEAD_DIM), F32)])


def _gla_kernel(q_ref, k_ref, v_ref, g_ref, r_ref, s0_ref, gn_ref, w_ref, lv_ref,
                o_ref, s_ref, st_ref, *, tb, levels, blocks, seqs, carried, t=None, nt=None):
    to_work = lambda s0: s0.reshape(GLA_KW, GLA_DV).T
    from_work = lambda st: st.T.reshape(GLA_HEADS, GLA_DK, GLA_DV)
    if carried:
        t = pl.program_id(1) if t is None else t
        nt = pl.num_programs(1) if nt is None else nt

        @pl.when(t == 0)
        def _():
            st_ref[...] = to_work(s0_ref[0])

    for sq in range(seqs):
        st = st_ref[...] if carried else to_work(s0_ref[sq])
        for blk in range(blocks):
            rows = lambda ref: ref.at[pl.ds((sq * blocks + blk) * tb, tb), :]
            st = _gla_block(rows(q_ref), rows(k_ref), rows(v_ref), rows(g_ref), rows(r_ref), gn_ref,
                            w_ref, lv_ref, rows(o_ref), st, tb=tb, levels=levels)
        if carried:
            st_ref[...] = st

            @pl.when(t == nt - 1)
            def _():
                s_ref[0] = from_work(st)
        else:
            s_ref[sq] = from_work(st)


def _gla_block(q_ref, k_ref, v_ref, g_ref, r_ref, gn_ref, w_ref, lv_ref, o_ref, st, *, tb, levels):
    half = LANES // 2
    r = r_ref[...]
    gate = gn_ref[...] * (r * _sigmoid(r))

    g2 = g_ref[...] * LOG2E
    gp = jnp.concatenate(_split2_f32(g2), axis=0).astype(BF16)
    rows = w_ref.shape[0] // 2
    cum = jnp.concatenate([_dot(w_ref[0:rows, :], gp), _dot(w_ref[rows:, :], gp)], axis=0)
    dec = jnp.exp2(cum)
    from_start = dec[0:tb]
    to_end = jnp.exp2(cum[tb - 1:tb] - cum[0:tb])

    q = q_ref[...] * (GLA_DK ** -0.5)
    k = k_ref[...]
    v = v_ref[...]
    row = lax.broadcasted_iota(jnp.int32, (tb, GLA_KW), 0)
    low = lax.broadcasted_iota(jnp.int32, (tb, LANES), 1) < half
    lv = lv_ref[...]

    def pair_scores(xq, xk):
        outs = []
        for p in range(GLA_HEADS // 2):
            a = xq[:, p * LANES:(p + 1) * LANES]
            zero = jnp.zeros_like(a)
            lhs = jnp.concatenate([jnp.where(low, a, zero), jnp.where(low, zero, a)], axis=0)
            outs.append(_dot_nt(lhs, xk[:, p * LANES:(p + 1) * LANES]))
        return outs

    here = lv == -1
    a = [jnp.where(here, r_, 0.0) for r_ in pair_scores(q.astype(BF16), k.astype(BF16))]
    for l in range(levels):
        upper = ((row >> l) & 1) == 1
        e = jnp.where(upper, jnp.exp2(g2), 1.0) if l == 0 else dec[l * tb:(l + 1) * tb]
        x = (jnp.where(upper, q, k) * e).astype(BF16)
        here = lv == l
        a = [jnp.where(here, r_, a_) for r_, a_ in zip(pair_scores(x, x), a)]
    o = jnp.concatenate(
        [_dot(a[h // 2][(h % 2) * tb:(h % 2 + 1) * tb].astype(BF16), v[:, h * GLA_DV:(h + 1) * GLA_DV])
         for h in range(GLA_HEADS)], axis=1)

    lane = lax.broadcasted_iota(jnp.int32, (tb, GLA_KW), 1)
    head_sel = [(lane >= h * GLA_DK) & (lane < (h + 1) * GLA_DK) for h in range(GLA_HEADS)]
    qt = (q * from_start).astype(BF16)
    kt = (k * to_end).astype(BF16)
    zero = jnp.zeros_like(qt)
    q4 = jnp.concatenate([jnp.where(sel, qt, zero) for sel in head_sel], axis=0)
    oi = _dot_nt(q4, st.astype(BF16))
    o = o + jnp.concatenate([oi[h * tb:(h + 1) * tb] for h in range(GLA_HEADS)], axis=1)
    upd = None
    for h in range(GLA_HEADS):
        u = _dot_tn(v[:, h * GLA_DV:(h + 1) * GLA_DV], jnp.where(head_sel[h], kt, zero))
        upd = u if upd is None else upd + u
    st = from_start[tb - 1:tb, :] * st + upd

    outs = []
    for h in range(GLA_HEADS):
        oh = o[:, h * GLA_DV:(h + 1) * GLA_DV]
        outs.append(oh * lax.rsqrt(jnp.mean(oh * oh, axis=-1, keepdims=True) + EPS))
    o_ref[...] = (jnp.concatenate(outs, axis=1) * gate).astype(BF16)
    return st


def _gla_call(batch, seq, ix, tb=128, rows_per_step=1024):
    tb = min(tb, seq)
    step_rows = min(rows_per_step, batch * seq)
    carried = seq > step_rows
    seqs = 1 if carried else step_rows // seq
    per_seq = (step_rows if carried else seq) // tb
    nt = seq // (tb * per_seq)
    n = batch * seq
    levels = tb.bit_length() - 1
    assert tb == 1 << levels
    ti = np.arange(tb)[:, None]
    si = np.arange(tb)[None, :]
    blocks = [si <= ti]
    for l in range(1, levels):
        m = 1 << l
        mid = ti - ti % (2 * m) + m - 1
        upper = ti % (2 * m) >= m
        blocks.append(np.where(upper, (si > mid) & (si <= ti), (si > ti) & (si <= mid)))
    w = np.concatenate(blocks, axis=0).astype(np.float32)
    w = jnp.asarray(np.concatenate([w, w], axis=1), dtype=BF16)
    x = ti ^ si
    lv = np.where(si < ti, np.floor(np.log2(np.maximum(x, 1))).astype(np.int32),
                  np.where(si == ti, -1, -2)).astype(np.int32)
    lv = jnp.asarray(np.concatenate([lv, lv], axis=0))
    row = lambda w_: pl.BlockSpec((step_rows, w_), lambda *g: (ix(*g)[0] * nt + ix(*g)[1], 0))
    state = pl.BlockSpec((seqs, GLA_HEADS, GLA_DK, GLA_DV), lambda *g: (ix(*g)[0], 0, 0, 0))
    const = lambda shape: pl.BlockSpec(shape, lambda *g: (0, 0))
    return dict(
        kernel=functools.partial(_gla_kernel, tb=tb, levels=levels, blocks=per_seq, seqs=seqs,
                                 carried=carried),
        groups=batch // seqs, nt=nt, consts=(w, lv),
        in_specs=[row(GLA_KW), row(GLA_KW), row(GLA_VW), row(GLA_KW), row(GLA_VW), state,
                  const((1, GLA_VW)), const(w.shape), const(lv.shape)],
        out_specs=[row(GLA_VW), state],
        out_shape=[jax.ShapeDtypeStruct((n, GLA_VW), BF16),
                   jax.ShapeDtypeStruct((batch, GLA_HEADS, GLA_DK, GLA_DV), F32)],
        scratch_shapes=[pltpu.VMEM((GLA_DV, GLA_KW), F32)])


def _gla(gq, gk, gv, glog, gr, s0, gn, batch, seq):
    c = _gla_call(batch, seq, lambda b, t: (b, t))
    return pl.pallas_call(
        c['kernel'], grid=(c['groups'], c['nt']), in_specs=c['in_specs'], out_specs=c['out_specs'],
        out_shape=c['out_shape'], scratch_shapes=c['scratch_shapes'],
        compiler_params=pltpu.CompilerParams(dimension_semantics=("arbitrary", "arbitrary"),
                                             vmem_limit_bytes=VMEM_LIMIT),
        name="gla",
    )(gq, gk, gv, glog, gr, s0, gn, *c['consts'])


def _gla_and_fox_sample(gla_args, fox_args, gla_batch, gla_seq, batch, tn, past,
                        gla_rows=256, tp=2048):
    g = _gla_call(gla_batch, gla_seq, lambda i: (i // (gla_seq // gla_rows), i % (gla_seq // gla_rows)),
                  rows_per_step=gla_rows)
    f = _fox_sample_call(batch, tn, past, tp, lambda i: (i // (past // tp), i % (past // tp)))
    steps = g['groups'] * g['nt']
    assert steps == batch * f['nt'] and g['nt'] == gla_seq // gla_rows
    n_in = (len(g['in_specs']), len(f['in_specs']))
    n_out = (len(g['out_specs']), len(f['out_specs']))

    def kernel(*refs):
        i = pl.program_id(0)
        refs = list(refs)
        take = lambda k: [refs.pop(0) for _ in range(k)]
        g_in, f_in = take(n_in[0]), take(n_in[1])
        g_out, f_out = take(n_out[0]), take(n_out[1])
        g_scr, f_scr = take(len(g['scratch_shapes'])), take(len(f['scratch_shapes']))
        g['kernel'](*g_in, *g_out, *g_scr, t=i % g['nt'], nt=g['nt'])
        f['kernel'](*f_in, *f_out, *f_scr, j=i % f['nt'])

    outs = pl.pallas_call(
        kernel, grid=(steps,), in_specs=g['in_specs'] + f['in_specs'],
        out_specs=g['out_specs'] + f['out_specs'], out_shape=g['out_shape'] + f['out_shape'],
        scratch_shapes=g['scratch_shapes'] + f['scratch_shapes'],
        compiler_params=pltpu.CompilerParams(dimension_semantics=("arbitrary",),
                                             vmem_limit_bytes=VMEM_LIMIT),
        name="gla_fox_sample",
    )(*gla_args, *g['consts'], *fox_args[:5], *f['consts'], *fox_args[5:])
    return outs


def _ffn_kernel(x_ref, fo_ref, go_ref, wo_ref, g2_ref, wg_ref, wu_ref, wd_ref, gf_ref,
                y_ref, a_ref, *, chunk, fox_time_minor):
    tm = x_ref.shape[0]
    parts = fo_ref.shape[0] if fox_time_minor else 2
    halves = [slice(i * (tm // parts), (i + 1) * (tm // parts)) for i in range(parts)]
    y1 = []
    for i, rows in enumerate(halves):
        fox = (_dot_tn(fo_ref[i], wo_ref[0:FOX_WIDTH, :]) if fox_time_minor
               else _dot(fo_ref[rows, :], wo_ref[0:FOX_WIDTH, :]))
        y1.append(x_ref[rows, :] + fox + _dot(go_ref[rows, :], wo_ref[FOX_WIDTH:, :]))
    h2 = jnp.concatenate([_rms(y, g2_ref[...]).astype(BF16) for y in y1], axis=0)
    for c in range(D_FF // chunk):
        cs = slice(c * chunk, (c + 1) * chunk)
        u = _dot(h2, wg_ref[:, cs])
        w = _dot(h2, wu_ref[:, cs])
        a_ref[:, cs] = (u * _sigmoid(u) * w).astype(BF16)
    y2 = [y + _dot(a_ref[rows, :], wd_ref[...]) for y, rows in zip(y1, halves)]
    for y, rows in zip(y2, halves):
        y_ref[rows, :] = _rms(y, gf_ref[...])


def _ffn(x2d, fo, go, prm, tm=1024, chunk=256):
    n = x2d.shape[0]
    tm = min(tm, n)
    row = lambda w: pl.BlockSpec((tm, w), lambda i: (i, 0))
    const = lambda shape: pl.BlockSpec(shape, lambda i: (0, 0), pipeline_mode=pl.Buffered(1))
    fox_time_minor = fo.ndim == 4
    if fox_time_minor:
        per_tile = tm // fo.shape[3]
        tiles_per_seq = fo.shape[1] // per_tile
        assert per_tile * fo.shape[3] == tm and tiles_per_seq * per_tile == fo.shape[1]
        fo_spec = pl.BlockSpec((None, per_tile, FOX_WIDTH, fo.shape[3]),
                               lambda i: (i // tiles_per_seq, i % tiles_per_seq, 0, 0))
    else:
        fo_spec = row(FOX_WIDTH)
    return pl.pallas_call(
        functools.partial(_ffn_kernel, chunk=chunk, fox_time_minor=fox_time_minor),
        grid=(n // tm,),
        in_specs=[row(D_MODEL), fo_spec, row(GLA_VW), const((D_MODEL, D_MODEL)),
                  const((1, D_MODEL)), const((D_MODEL, D_FF)), const((D_MODEL, D_FF)),
                  const((D_FF, D_MODEL)), const((1, D_MODEL))],
        out_specs=row(D_MODEL),
        out_shape=jax.ShapeDtypeStruct((n, D_MODEL), F32),
        scratch_shapes=[pltpu.VMEM((tm, D_FF), BF16)],
        compiler_params=pltpu.CompilerParams(dimension_semantics=("arbitrary",),
                                             vmem_limit_bytes=VMEM_LIMIT),
        name="ffn",
    )(x2d, fo, go, prm['wo'], prm['g2'], prm['wg'], prm['wu'], prm['wd'], prm['gf'])


def _layer_params(layer, norm1_g, w_in, w_gate2, b_gate2, b_forget, gla_norm_g, w_out,
                  norm2_g, w_gate, w_up, w_down, final_norm_g):
    wt = jnp.transpose(w_in[layer])
    o_fl = 3 * FOX_WIDTH
    o_gq = o_fl + FOX_HEADS
    o_gg = o_gq + 2 * GLA_KW + GLA_VW
    o_gr = o_gg + GLA_GATE_RANK
    tail = 2 * GLA_GATE_RANK
    pad = jnp.zeros((tail - FOX_HEADS - GLA_GATE_RANK, D_MODEL), F32)
    w = jnp.concatenate([wt[:o_fl], wt[o_gq:o_gg], wt[o_gr:], wt[o_fl:o_gq], wt[o_gg:o_gr], pad], axis=0)
    wg2 = jnp.zeros((tail, GLA_KW), F32).at[FOX_HEADS:FOX_HEADS + GLA_GATE_RANK].set(w_gate2[layer])
    return dict(
        g1=norm1_g[layer].reshape(1, D_MODEL),
        w=w.astype(BF16),
        wg2=wg2.astype(BF16),
        bg2=b_gate2[layer].reshape(1, GLA_KW),
        bfc=b_forget[layer].reshape(FOX_HEADS, 1),
        gn=gla_norm_g[layer].reshape(1, GLA_VW),
        wo=w_out[layer].astype(BF16),
        g2=norm2_g[layer].reshape(1, D_MODEL),
        wg=w_gate[layer].astype(BF16),
        wu=w_up[layer].astype(BF16),
        wd=w_down[layer].astype(BF16),
        gf=final_norm_g.reshape(1, D_MODEL),
    )


def kernel(x_prompt, x_sample, cache_fox_k, cache_fox_v, cache_fox_logf, state_gla, norm1_g, w_in,
           w_gate2, b_gate2, b_forget, gla_norm_g, w_out, norm2_g, w_gate, w_up, w_down, final_norm_g):
    depth = w_in.shape[0]
    assert depth == 1, "the final rmsnorm is fused into the layer's ffn kernel"
    bp, tp_, _ = x_prompt.shape
    bs, ts, _ = x_sample.shape
    past = cache_fox_k.shape[2]
    layer = 0
    prm = _layer_params(layer, norm1_g, w_in, w_gate2, b_gate2, b_forget, gla_norm_g, w_out,
                        norm2_g, w_gate, w_up, w_down, final_norm_g)
    by_time = lambda a, b, t: a.reshape(FOX_HEADS, b, t).transpose(1, 2, 0)[None]

    xp = x_prompt.reshape(bp * tp_, D_MODEL)
    qt, kt_p, vt_p, lf_p, ct, gq, gk, gv, glog, gr = _proj(xp, bp, tp_, prm, True)
    fox_p = _fox_prompt(qt, kt_p, vt_p, ct, bp, tp_)
    s0 = jnp.zeros((bp, GLA_HEADS, GLA_DK, GLA_DV), F32)
    gla_p_args = (gq, gk, gv, glog, gr, s0, prm['gn'])

    xs = x_sample.reshape(bs * ts, D_MODEL)
    q, k_s, v_s, lf_s, ct, gq, gk, gv, glog, gr = _proj(xs, bs, ts, prm, False)
    cn = ct.reshape(FOX_HEADS, bs, ts).transpose(1, 0, 2)
    lft = cache_fox_logf[layer].astype(F32).transpose(0, 2, 1)
    ck = cache_fox_k[layer].transpose(0, 2, 3, 1)
    cv = cache_fox_v[layer].transpose(0, 2, 3, 1)
    gla_p, s_p, fox_s = _gla_and_fox_sample(gla_p_args, (q, k_s, v_s, cn, lft, ck, cv),
                                            bp, tp_, bs, ts, past)
    y_p = _ffn(xp, fox_p, gla_p, prm)
    gla_s, s_s = _gla(gq, gk, gv, glog, gr, state_gla[layer].astype(F32), prm['gn'], bs, ts)
    y_s = _ffn(xs, fox_s, gla_s, prm)

    heads = lambda a, b, t: a.reshape(1, b, t, FOX_HEADS, FOX_HEAD_DIM)
    return (y_p.reshape(bp, tp_, D_MODEL), y_s.reshape(bs, ts, D_MODEL),
            kt_p.transpose(0, 3, 1, 2)[None], vt_p.transpose(0, 3, 1, 2)[None],
            by_time(lf_p, bp, tp_), s_p[None],
            heads(k_s, bs, ts), heads(v_s, bs, ts), by_time(lf_s, bs, ts), s_s[None])
```

```python
import functools

import numpy as np
import jax
import jax.numpy as jnp
from jax import lax
from jax.experimental import pallas as pl
from jax.experimental.pallas import tpu as pltpu

D_MODEL = 1024
FOX_HEADS = 8
FOX_HEAD_DIM = 64
FOX_WIDTH = FOX_HEADS * FOX_HEAD_DIM
GLA_HEADS = 4
GLA_DK = 64
GLA_DV = 128
GLA_KW = GLA_HEADS * GLA_DK
GLA_VW = GLA_HEADS * GLA_DV
GLA_GATE_RANK = 16
GLA_GATE_TEMP = 16.0
D_FF = 2816
EPS = 1e-6

LANES = 128
LOG2E = 1.4426950408889634
VMEM_LIMIT = 56 * 1024 * 1024

F32 = jnp.float32
BF16 = jnp.bfloat16
NEG_BIG = -1e30


def _log_sigmoid(x):
    return jnp.minimum(x, 0.0) - jnp.log1p(jnp.exp(-jnp.abs(x)))


def _sigmoid(x):
    return 1.0 / (1.0 + jnp.exp(-x))


def _split3_f32(x):
    hi = x.astype(BF16).astype(F32)
    r = x - hi
    mid = r.astype(BF16).astype(F32)
    lo = (r - mid).astype(BF16).astype(F32)
    return hi, mid, lo


def _split2_f32(x):
    hi = x.astype(BF16).astype(F32)
    return hi, (x - hi).astype(BF16).astype(F32)


def _dot(a, b):
    return jnp.dot(a, b, preferred_element_type=F32)


def _dot_nt(a, b):
    return lax.dot_general(a, b, (((1,), (1,)), ((), ())), preferred_element_type=F32)


def _dot_tn(a, b):
    return lax.dot_general(a, b, (((0,), (0,)), ((), ())), preferred_element_type=F32)


def _rms(x, g):
    return x * lax.rsqrt(jnp.mean(x * x, axis=-1, keepdims=True) + EPS) * g


def _proj_kernel(x_ref, g1_ref, w_ref, wg2_ref, bg2_ref, bfc_ref, tri_ref,
                 *rest, tiles_per_seq, time_minor):
    o_kv, o_b = FOX_WIDTH, 3 * FOX_WIDTH
    o_tail = o_b + 2 * GLA_KW + 2 * GLA_VW
    carry_ref = rest[-1]

    @pl.when(pl.program_id(0) % tiles_per_seq == 0)
    def _():
        carry_ref[...] = jnp.zeros_like(carry_ref)

    h = _rms(x_ref[...], g1_ref[...]).astype(BF16)
    tm = h.shape[0]
    scale = FOX_HEAD_DIM ** -0.5

    zc = _dot_nt(h, w_ref[o_tail:, :])
    fl_t = _dot_nt(w_ref[o_tail:, :], h)[0:FOX_HEADS]

    if time_minor:
        (q_ref, k_ref, v_ref, lft_ref, ct_ref, gq_ref, gk_ref, gv_ref, glog_ref, gr_ref,
         carry_ref) = rest
        qt = (_dot_nt(w_ref[0:o_kv, :], h) * (scale * LOG2E)).astype(BF16)
        tq = q_ref.shape[2]
        for i in range(q_ref.shape[0]):
            q_ref[i] = qt[:, i * tq:(i + 1) * tq]
    else:
        (q_ref, k_ref, v_ref, lft_ref, ct_ref, gq_ref, gk_ref, gv_ref, glog_ref, gr_ref,
         carry_ref) = rest
        q_ref[...] = (_dot_nt(h, w_ref[0:o_kv, :]) * scale).astype(BF16)

    logf_t = _log_sigmoid(fl_t + bfc_ref[...])
    lft_ref[...] = logf_t
    parts = jnp.concatenate(_split3_f32(logf_t), axis=0).astype(BF16)
    gg = zc.astype(BF16)

    if time_minor:
        kvt = _dot_nt(w_ref[o_kv:o_b, :], h)
        k_ref[...] = kvt[:FOX_WIDTH].reshape(FOX_HEADS, FOX_HEAD_DIM, tm)
        v_ref[...] = kvt[FOX_WIDTH:].reshape(FOX_HEADS, FOX_HEAD_DIM, tm)
    else:
        kv = _dot_nt(h, w_ref[o_kv:o_b, :])
        k_ref[...] = kv[:, :FOX_WIDTH]
        v_ref[...] = kv[:, FOX_WIDTH:]

    gpre = _dot(gg, wg2_ref[...]) + bg2_ref[...]
    cs = _dot(parts, tri_ref[...])

    zb = _dot_nt(h, w_ref[o_b:o_tail, :])
    gq_ref[...] = zb[:, :GLA_KW]
    gk_ref[...] = zb[:, GLA_KW:2 * GLA_KW]
    gv_ref[...] = zb[:, 2 * GLA_KW:2 * GLA_KW + GLA_VW].astype(BF16)
    gr_ref[...] = zb[:, 2 * GLA_KW + GLA_VW:]

    glog_ref[...] = _log_sigmoid(gpre) * (1.0 / GLA_GATE_TEMP)

    ct = cs[0:8] + cs[8:16] + cs[16:24] + carry_ref[:, 0:1]
    ct_ref[...] = ct
    carry_ref[...] = jnp.broadcast_to(ct[:, tm - 1:], carry_ref.shape)


def _proj(x2d, batch, seq_len, prm, time_minor, tm=1024, tq=512):
    n = x2d.shape[0]
    tm = min(tm, n)
    tiles_per_seq = max(seq_len // tm, 1)
    per_tile = tm // tq
    idx = np.arange(tm)
    tri = ((idx[:, None] <= idx[None, :]) & (idx[:, None] // seq_len == idx[None, :] // seq_len))
    tri = jnp.asarray(tri.astype(np.float32), dtype=BF16)
    const = lambda a: pl.BlockSpec(a.shape, lambda i: (0, 0), pipeline_mode=pl.Buffered(1))
    row = lambda w: pl.BlockSpec((tm, w), lambda i: (i, 0))
    col = pl.BlockSpec((FOX_HEADS, tm), lambda i: (0, i))
    sds = jax.ShapeDtypeStruct
    if time_minor:
        kv_shape = sds((batch, FOX_HEADS, FOX_HEAD_DIM, seq_len), F32)
        kv_spec = pl.BlockSpec((None, FOX_HEADS, FOX_HEAD_DIM, tm),
                               lambda i: (i // tiles_per_seq, 0, 0, i % tiles_per_seq))
        fox = [(sds((batch, seq_len // tq, FOX_WIDTH, tq), BF16),
                pl.BlockSpec((None, per_tile, FOX_WIDTH, tq),
                             lambda i: (i // tiles_per_seq, i % tiles_per_seq, 0, 0))),
               (kv_shape, kv_spec), (kv_shape, kv_spec)]
    else:
        fox = [(sds((n, FOX_WIDTH), BF16), row(FOX_WIDTH)),
               (sds((n, FOX_WIDTH), F32), row(FOX_WIDTH)),
               (sds((n, FOX_WIDTH), F32), row(FOX_WIDTH))]
    outs = fox + [
        (sds((FOX_HEADS, n), F32), col),
        (sds((FOX_HEADS, n), F32), col),
        (sds((n, GLA_KW), F32), row(GLA_KW)),
        (sds((n, GLA_KW), F32), row(GLA_KW)),
        (sds((n, GLA_VW), BF16), row(GLA_VW)),
        (sds((n, GLA_KW), F32), row(GLA_KW)),
        (sds((n, GLA_VW), F32), row(GLA_VW)),
    ]
    names = ('g1', 'w', 'wg2', 'bg2', 'bfc')
    return pl.pallas_call(
        functools.partial(_proj_kernel, tiles_per_seq=tiles_per_seq, time_minor=time_minor),
        grid=(n // tm,),
        in_specs=[row(D_MODEL)] + [const(prm[k]) for k in names] + [const(tri)],
        out_specs=tuple(s for _, s in outs), out_shape=tuple(s for s, _ in outs),
        scratch_shapes=[pltpu.VMEM((FOX_HEADS, LANES), F32)],
        compiler_params=pltpu.CompilerParams(dimension_semantics=("arbitrary",),
                                             vmem_limit_bytes=VMEM_LIMIT),
        name="proj",
    )(x2d, *[prm[k] for k in names], tri)


def _fox_prompt_kernel(q_ref, k_ref, v_ref, c_ref, o_ref, kb_ref, vb_ref, s_ref, p_ref, acc_ref,
                       *, tk):
    g = pl.program_id(1)
    nq = q_ref.shape[0]
    nk = vb_ref.shape[1]
    pairs = q_ref.shape[1] // LANES
    hd = FOX_HEAD_DIM
    spare = (hd, 0)
    vrows = vb_ref.shape[2]

    parts = jnp.concatenate(_split3_f32(c_ref[...] * (-LOG2E)), axis=0).astype(BF16)
    r = lax.broadcasted_iota(jnp.int32, (3 * FOX_HEADS, LANES), 0)
    ln = lax.broadcasted_iota(jnp.int32, (3 * FOX_HEADS, LANES), 1)
    klane = lax.broadcasted_iota(jnp.int32, (k_ref.shape[2], LANES), 1)
    tail = jnp.where(lax.broadcasted_iota(jnp.int32, (vrows - hd, tk), 0) == 0, 1.0, 0.0)
    for pp in range(pairs):
        k = k_ref[2 * pp:2 * pp + 2].reshape(LANES, k_ref.shape[2]).T.astype(BF16)
        for hh in range(2):
            head = 2 * (pairs * g + pp) + hh
            place = (r % FOX_HEADS == head) & (ln == spare[hh] + r // FOX_HEADS)
            extra = _dot_tn(parts, jnp.where(place, 1.0, 0.0).astype(BF16))
            own = (klane < hd) if hh == 0 else (klane >= hd)
            kb_ref[2 * pp + hh] = jnp.where(own, k, extra.astype(BF16))
            for jj in range(nk):
                vt = v_ref[2 * pp + hh, :, jj * tk:(jj + 1) * tk]
                vb_ref[2 * pp + hh, jj] = jnp.concatenate([vt, tail], axis=0).astype(BF16)

    qrow = lax.broadcasted_iota(jnp.int32, (LANES, tk), 0)
    key = lax.broadcasted_iota(jnp.int32, (tk, tk), 0)
    qry = lax.broadcasted_iota(jnp.int32, (tk, tk), 1)
    causal = key <= qry

    chains = [(slot, part) for slot in range(2 * pairs) for part in range(2)]
    everyone = list(range(len(chains)))
    second = [n for n in everyone if chains[n][1] == 1]

    def q_operands(qi):
        qs = []
        for slot, part in chains:
            pp, hh = divmod(slot, 2)
            q = q_ref[qi, pp * LANES:(pp + 1) * LANES, part * tk:(part + 1) * tk]
            own = (qrow < hd) if hh == 0 else (qrow >= hd)
            ones = (qrow >= spare[hh]) & (qrow < spare[hh] + 3)
            qs.append(jnp.where(own, q, jnp.where(ones, 1.0, 0.0).astype(BF16)))
        return qs

    steps = []
    for qi in range(nq):
        steps += [(qi, j, everyone, None) for j in range(2 * qi)]
        steps += [(qi, 2 * qi, everyone, 0), (qi, 2 * qi + 1, second, 1)]

    qs_of = {}

    def scores(step, buf):
        qi, j, live, _ = step
        if qi not in qs_of:
            qs_of.clear()
            qs_of[qi] = q_operands(qi)
        for n in live:
            s_ref[buf, n] = _dot(kb_ref[chains[n][0], j * tk:(j + 1) * tk, :], qs_of[qi][n])

    def values(step, buf):
        _, j, live, _ = step
        return {n: _dot(vb_ref[chains[n][0], j], p_ref[buf, n]) for n in live}

    def finish(qi):
        for part in range(2):
            heads = [acc_ref[qi % 2, 2 * slot + part] for slot in range(2 * pairs)]
            o_ref[qi, :, part * tk:(part + 1) * tk] = jnp.concatenate(
                [a[:hd] / a[hd:hd + 1] for a in heads], axis=0).astype(BF16)

    def accumulate(step, alphas, pv):
        qi, j, live, _ = step
        for n in live:
            acc_ref[qi % 2, n] = pv[n] if j == 0 else alphas[n] * acc_ref[qi % 2, n] + pv[n]

    ms = {}
    scores(steps[0], 0)
    prev, prev_alphas = None, None
    for t, step in enumerate(steps):
        qi, j, live, masked_part = step
        if t + 1 < len(steps):
            scores(steps[t + 1], (t + 1) % 2)
        pv = values(prev, (t - 1) % 2) if prev is not None else None
        alphas = {}
        for n in live:
            s = s_ref[t % 2, n]
            if chains[n][1] == masked_part:
                s = jnp.where(causal, s, -jnp.inf)
            m_new = jnp.max(s, axis=0, keepdims=True)
            if j > 0:
                m_new = jnp.maximum(ms[n], m_new)
                alphas[n] = jnp.exp2(ms[n] - m_new)
            p_ref[t % 2, n] = jnp.exp2(s - m_new).astype(BF16)
            ms[n] = m_new
        if prev is not None:
            accumulate(prev, prev_alphas, pv)
            if prev[0] != qi:
                finish(prev[0])
        prev, prev_alphas = step, alphas
    accumulate(prev, prev_alphas, values(prev, (len(steps) - 1) % 2))
    finish(prev[0])


def _fox_prompt(qt, kt, vt, ct, batch, seq, tk=256, pairs=2):
    tq = 2 * tk
    nq = seq // tq
    nk = seq // tk
    heads = 2 * pairs
    chains = 2 * heads
    vrows = FOX_HEAD_DIM + 16
    assert qt.shape == (batch, nq, FOX_WIDTH, tq) and FOX_HEADS % heads == 0
    q_spec = pl.BlockSpec((None, nq, pairs * LANES, tq), lambda b, p: (b, 0, p, 0))
    kv_spec = pl.BlockSpec((None, heads, FOX_HEAD_DIM, seq), lambda b, p: (b, p, 0, 0))
    return pl.pallas_call(
        functools.partial(_fox_prompt_kernel, tk=tk),
        grid=(batch, FOX_HEADS // heads),
        in_specs=[q_spec, kv_spec, kv_spec,
                  pl.BlockSpec((FOX_HEADS, seq), lambda b, p: (0, b))],
        out_specs=q_spec,
        out_shape=jax.ShapeDtypeStruct((batch, nq, FOX_WIDTH, tq), BF16),
        scratch_shapes=[pltpu.VMEM((heads, seq, LANES), BF16), pltpu.VMEM((heads, nk, vrows, tk), BF16),
                        pltpu.VMEM((2, chains, tk, tk), F32), pltpu.VMEM((2, chains, tk, tk), BF16),
                        pltpu.VMEM((2, chains, vrows, tk), F32)],
        compiler_params=pltpu.CompilerParams(dimension_semantics=("arbitrary", "arbitrary"),
                                             vmem_limit_bytes=VMEM_LIMIT),
        name="fox_prompt",
    )(qt, kt, vt, ct)


def _fox_sample_kernel(q_ref, kn_ref, vn_ref, cn_ref, lft_ref, mlow_ref, ck_ref, cv_ref,
                       o_ref, qh_ref, suf_ref, m_ref, l_ref, acc_ref, *, tp, nt, tn, j=None):
    j = pl.program_id(1) if j is None else j
    blk = 2 * LANES
    hd = FOX_HEAD_DIM

    @pl.when(j == 0)
    def _init():
        for h in range(FOX_HEADS):
            qh_ref[h] = q_ref[:, h * hd:(h + 1) * hd]
        m_ref[...] = jnp.full(m_ref.shape, NEG_BIG, F32)
        l_ref[...] = jnp.zeros(l_ref.shape, F32)
        acc_ref[...] = jnp.zeros(acc_ref.shape, F32)
        x = lft_ref[...]
        parts = _split3_f32(x)
        per_tile = tp // blk
        nblk = nt * per_tile
        rows = jnp.concatenate([p_[:, b * blk:(b + 1) * blk] for b in range(nblk) for p_ in parts], axis=0)
        y = _dot(rows.astype(BF16), mlow_ref[...])
        carry = jnp.zeros((FOX_HEADS, 1), F32)
        for b in reversed(range(nblk)):
            yb = y[24 * b:24 * b + 8] + y[24 * b + 8:24 * b + 16] + y[24 * b + 16:24 * b + 24]
            off = (b % per_tile) * blk
            suf_ref[b // per_tile, :, off:off + blk] = yb + carry
            carry = carry + yb[:, 0:1] + x[:, b * blk:b * blk + 1]

    def update(blocks):
        m_old = m_ref[...]
        m_new = m_old
        for s, _ in blocks:
            m_new = jnp.maximum(m_new, jnp.max(s, axis=-1, keepdims=True))
        alpha = jnp.exp(m_old - m_new)
        l = alpha * l_ref[...]
        acc = alpha * acc_ref[...]
        for s, pv in blocks:
            pm = jnp.exp(s - m_new)
            l = l + jnp.sum(pm, axis=-1, keepdims=True)
            acc = acc + pv(pm.astype(BF16))
        return m_new, l, acc

    bmm = lambda a, b, ca, cb: lax.dot_general(a, b, (((ca,), (cb,)), ((0,), (0,))),
                                               preferred_element_type=F32)
    qh = qh_ref[...]

    def cache_block():
        kt = ck_ref[...].astype(BF16)
        vt = cv_ref[...].astype(BF16)
        s = bmm(qh, kt, 2, 1).reshape(FOX_HEADS * tn, tp) + jnp.repeat(suf_ref[j], tn, axis=0)
        return s.reshape(FOX_HEADS, tn, tp), lambda pm: bmm(pm, vt, 2, 2)

    def new_block():
        per_head = lambda ref: jnp.stack([ref[:, h * hd:(h + 1) * hd] for h in range(FOX_HEADS)],
                                         axis=0).astype(BF16)
        kn = per_head(kn_ref)
        vn = per_head(vn_ref)
        r = lax.broadcasted_iota(jnp.int32, (FOX_HEADS, tn, tn), 1)
        c = lax.broadcasted_iota(jnp.int32, (FOX_HEADS, tn, tn), 2)
        s = jnp.where(c <= r, bmm(qh, kn, 2, 2) - cn_ref[...][:, None, :], -jnp.inf)
        return s, lambda pm: bmm(pm, vn, 2, 1)

    @pl.when(j < nt - 1)
    def _():
        m, l, acc = update([cache_block()])
        m_ref[...] = m
        l_ref[...] = l
        acc_ref[...] = acc

    @pl.when(j == nt - 1)
    def _():
        _, l, acc = update([cache_block(), new_block()])
        o = acc / l
        o_ref[...] = jnp.concatenate([o[h] for h in range(FOX_HEADS)], axis=1).astype(BF16)


def _fox_sample_call(batch, tn, past, tp, ix):
    nt = past // tp
    blk = 2 * LANES
    idx = np.arange(blk)
    mlow = jnp.asarray((idx[:, None] > idx[None, :]).astype(np.float32), dtype=BF16)
    per_b = lambda w: pl.BlockSpec((tn, w), lambda *g: (ix(*g)[0], 0))
    cache = pl.BlockSpec((None, FOX_HEADS, FOX_HEAD_DIM, tp), lambda *g: (ix(*g)[0], 0, 0, ix(*g)[1]))
    return dict(
        kernel=functools.partial(_fox_sample_kernel, tp=tp, nt=nt, tn=tn), nt=nt, consts=(mlow,),
        in_specs=[per_b(FOX_WIDTH), per_b(FOX_WIDTH), per_b(FOX_WIDTH),
                  pl.BlockSpec((None, FOX_HEADS, tn), lambda *g: (ix(*g)[0], 0, 0)),
                  pl.BlockSpec((None, FOX_HEADS, past), lambda *g: (ix(*g)[0], 0, 0)),
                  pl.BlockSpec((blk, blk), lambda *g: (0, 0)),
                  cache, cache],
        out_specs=[per_b(FOX_WIDTH)],
        out_shape=[jax.ShapeDtypeStruct((batch * tn, FOX_WIDTH), BF16)],
        scratch_shapes=[pltpu.VMEM((FOX_HEADS, tn, FOX_HEAD_DIM), BF16),
                        pltpu.VMEM((nt, FOX_HEADS, tp), F32),
                        pltpu.VMEM((FOX_HEADS, tn, 1), F32), pltpu.VMEM((FOX_HEADS, tn, 1), F32),
                        pltpu.VMEM((FOX_HEADS, tn, FOX_HEAD_DIM), F32)])


def _gla_kernel(q_ref, k_ref, v_ref, g_ref, r_ref, s0_ref, gn_ref, w_ref, lv_ref,
                o_ref, s_ref, st_ref, *, tb, levels, blocks, seqs, carried, t=None, nt=None):
    to_work = lambda s0: s0.reshape(GLA_KW, GLA_DV).T
    from_work = lambda st: st.T.reshape(GLA_HEADS, GLA_DK, GLA_DV)
    if carried:
        t = pl.program_id(1) if t is None else t
        nt = pl.num_programs(1) if nt is None else nt

        @pl.when(t == 0)
        def _():
            st_ref[...] = to_work(s0_ref[0])

    for sq in range(seqs):
        st = st_ref[...] if carried else to_work(s0_ref[sq])
        for blk in range(blocks):
            rows = lambda ref: ref.at[pl.ds((sq * blocks + blk) * tb, tb), :]
            st = _gla_block(rows(q_ref), rows(k_ref), rows(v_ref), rows(g_ref), rows(r_ref), gn_ref,
                            w_ref, lv_ref, rows(o_ref), st, tb=tb, levels=levels)
        if carried:
            st_ref[...] = st

            @pl.when(t == nt - 1)
            def _():
                s_ref[0] = from_work(st)
        else:
            s_ref[sq] = from_work(st)


def _gla_block(q_ref, k_ref, v_ref, g_ref, r_ref, gn_ref, w_ref, lv_ref, o_ref, st, *, tb, levels):
    half = LANES // 2
    r = r_ref[...]
    gate = gn_ref[...] * (r * _sigmoid(r))

    g2 = g_ref[...] * LOG2E
    gp = jnp.concatenate(_split2_f32(g2), axis=0).astype(BF16)
    rows = w_ref.shape[0] // 2
    cum = jnp.concatenate([_dot(w_ref[0:rows, :], gp), _dot(w_ref[rows:, :], gp)], axis=0)
    dec = jnp.exp2(cum)
    from_start = dec[0:tb]
    to_end = jnp.exp2(cum[tb - 1:tb] - cum[0:tb])

    q = q_ref[...] * (GLA_DK ** -0.5)
    k = k_ref[...]
    v = v_ref[...]
    row = lax.broadcasted_iota(jnp.int32, (tb, GLA_KW), 0)
    low = lax.broadcasted_iota(jnp.int32, (tb, LANES), 1) < half
    lv = lv_ref[...]

    def pair_scores(xq, xk):
        outs = []
        for p in range(GLA_HEADS // 2):
            a = xq[:, p * LANES:(p + 1) * LANES]
            zero = jnp.zeros_like(a)
            lhs = jnp.concatenate([jnp.where(low, a, zero), jnp.where(low, zero, a)], axis=0)
            outs.append(_dot_nt(lhs, xk[:, p * LANES:(p + 1) * LANES]))
        return outs

    here = lv == -1
    a = [jnp.where(here, r_, 0.0) for r_ in pair_scores(q.astype(BF16), k.astype(BF16))]
    for l in range(levels):
        upper = ((row >> l) & 1) == 1
        e = jnp.where(upper, jnp.exp2(g2), 1.0) if l == 0 else dec[l * tb:(l + 1) * tb]
        x = (jnp.where(upper, q, k) * e).astype(BF16)
        here = lv == l
        a = [jnp.where(here, r_, a_) for r_, a_ in zip(pair_scores(x, x), a)]
    o = jnp.concatenate(
        [_dot(a[h // 2][(h % 2) * tb:(h % 2 + 1) * tb].astype(BF16), v[:, h * GLA_DV:(h + 1) * GLA_DV])
         for h in range(GLA_HEADS)], axis=1)

    lane = lax.broadcasted_iota(jnp.int32, (tb, GLA_KW), 1)
    head_sel = [(lane >= h * GLA_DK) & (lane < (h + 1) * GLA_DK) for h in range(GLA_HEADS)]
    qt = (q * from_start).astype(BF16)
    kt = (k * to_end).astype(BF16)
    zero = jnp.zeros_like(qt)
    q4 = jnp.concatenate([jnp.where(sel, qt, zero) for sel in head_sel], axis=0)
    oi = _dot_nt(q4, st.astype(BF16))
    o = o + jnp.concatenate([oi[h * tb:(h + 1) * tb] for h in range(GLA_HEADS)], axis=1)
    upd = None
    for h in range(GLA_HEADS):
        u = _dot_tn(v[:, h * GLA_DV:(h + 1) * GLA_DV], jnp.where(head_sel[h], kt, zero))
        upd = u if upd is None else upd + u
    st = from_start[tb - 1:tb, :] * st + upd

    outs = []
    for h in range(GLA_HEADS):
        oh = o[:, h * GLA_DV:(h + 1) * GLA_DV]
        outs.append(oh * lax.rsqrt(jnp.mean(oh * oh, axis=-1, keepdims=True) + EPS))
    o_ref[...] = (jnp.concatenate(outs, axis=1) * gate).astype(BF16)
    return st


def _gla_call(batch, seq, ix, tb=128, rows_per_step=1024):
    tb = min(tb, seq)
    step_rows = min(rows_per_step, batch * seq)
    carried = seq > step_rows
    seqs = 1 if carried else step_rows // seq
    per_seq = (step_rows if carried else seq) // tb
    nt = seq // (tb * per_seq)
    n = batch * seq
    levels = tb.bit_length() - 1
    assert tb == 1 << levels
    ti = np.arange(tb)[:, None]
    si = np.arange(tb)[None, :]
    blocks = [si <= ti]
    for l in range(1, levels):
        m = 1 << l
        mid = ti - ti % (2 * m) + m - 1
        upper = ti % (2 * m) >= m
        blocks.append(np.where(upper, (si > mid) & (si <= ti), (si > ti) & (si <= mid)))
    w = np.concatenate(blocks, axis=0).astype(np.float32)
    w = jnp.asarray(np.concatenate([w, w], axis=1), dtype=BF16)
    x = ti ^ si
    lv = np.where(si < ti, np.floor(np.log2(np.maximum(x, 1))).astype(np.int32),
                  np.where(si == ti, -1, -2)).astype(np.int32)
    lv = jnp.asarray(np.concatenate([lv, lv], axis=0))
    row = lambda w_: pl.BlockSpec((step_rows, w_), lambda *g: (ix(*g)[0] * nt + ix(*g)[1], 0))
    state = pl.BlockSpec((seqs, GLA_HEADS, GLA_DK, GLA_DV), lambda *g: (ix(*g)[0], 0, 0, 0))
    const = lambda shape: pl.BlockSpec(shape, lambda *g: (0, 0))
    return dict(
        kernel=functools.partial(_gla_kernel, tb=tb, levels=levels, blocks=per_seq, seqs=seqs,
                                 carried=carried),
        groups=batch // seqs, nt=nt, consts=(w, lv),
        in_specs=[row(GLA_KW), row(GLA_KW), row(GLA_VW), row(GLA_KW), row(GLA_VW), state,
                  const((1, GLA_VW)), const(w.shape), const(lv.shape)],
        out_specs=[row(GLA_VW), state],
        out_shape=[jax.ShapeDtypeStruct((n, GLA_VW), BF16),
                   jax.ShapeDtypeStruct((batch, GLA_HEADS, GLA_DK, GLA_DV), F32)],
        scratch_shapes=[pltpu.VMEM((GLA_DV, GLA_KW), F32)])


def _gla(gq, gk, gv, glog, gr, s0, gn, batch, seq):
    c = _gla_call(batch, seq, lambda b, t: (b, t))
    return pl.pallas_call(
        c['kernel'], grid=(c['groups'], c['nt']), in_specs=c['in_specs'], out_specs=c['out_specs'],
        out_shape=c['out_shape'], scratch_shapes=c['scratch_shapes'],
        compiler_params=pltpu.CompilerParams(dimension_semantics=("arbitrary", "arbitrary"),
                                             vmem_limit_bytes=VMEM_LIMIT),
        name="gla",
    )(gq, gk, gv, glog, gr, s0, gn, *c['consts'])


def _gla_and_fox_sample(gla_args, fox_args, gla_batch, gla_seq, batch, tn, past,
                        gla_rows=256, tp=2048):
    g = _gla_call(gla_batch, gla_seq, lambda i: (i // (gla_seq // gla_rows), i % (gla_seq // gla_rows)),
                  rows_per_step=gla_rows)
    f = _fox_sample_call(batch, tn, past, tp, lambda i: (i // (past // tp), i % (past // tp)))
    steps = g['groups'] * g['nt']
    assert steps == batch * f['nt'] and g['nt'] == gla_seq // gla_rows
    n_in = (len(g['in_specs']), len(f['in_specs']))
    n_out = (len(g['out_specs']), len(f['out_specs']))

    def kernel(*refs):
        i = pl.program_id(0)
        refs = list(refs)
        take = lambda k: [refs.pop(0) for _ in range(k)]
        g_in, f_in = take(n_in[0]), take(n_in[1])
        g_out, f_out = take(n_out[0]), take(n_out[1])
        g_scr, f_scr = take(len(g['scratch_shapes'])), take(len(f['scratch_shapes']))
        g['kernel'](*g_in, *g_out, *g_scr, t=i % g['nt'], nt=g['nt'])
        f['kernel'](*f_in, *f_out, *f_scr, j=i % f['nt'])

    outs = pl.pallas_call(
        kernel, grid=(steps,), in_specs=g['in_specs'] + f['in_specs'],
        out_specs=g['out_specs'] + f['out_specs'], out_shape=g['out_shape'] + f['out_shape'],
        scratch_shapes=g['scratch_shapes'] + f['scratch_shapes'],
        compiler_params=pltpu.CompilerParams(dimension_semantics=("arbitrary",),
                                             vmem_limit_bytes=VMEM_LIMIT),
        name="gla_fox_sample",
    )(*gla_args, *g['consts'], *fox_args[:5], *f['consts'], *fox_args[5:])
    return outs


def _ffn_kernel(x_ref, fo_ref, go_ref, wo_ref, g2_ref, wg_ref, wu_ref, wd_ref, gf_ref,
                y_ref, a_ref, *, chunk, fox_time_minor):
    tm = x_ref.shape[0]
    parts = fo_ref.shape[0] if fox_time_minor else 2
    halves = [slice(i * (tm // parts), (i + 1) * (tm // parts)) for i in range(parts)]
    y1 = []
    for i, rows in enumerate(halves):
        fox = (_dot_tn(fo_ref[i], wo_ref[0:FOX_WIDTH, :]) if fox_time_minor
               else _dot(fo_ref[rows, :], wo_ref[0:FOX_WIDTH, :]))
        y1.append(x_ref[rows, :] + fox + _dot(go_ref[rows, :], wo_ref[FOX_WIDTH:, :]))
    h2 = jnp.concatenate([_rms(y, g2_ref[...]).astype(BF16) for y in y1], axis=0)
    for c in range(D_FF // chunk):
        cs = slice(c * chunk, (c + 1) * chunk)
        u = _dot(h2, wg_ref[:, cs])
        w = _dot(h2, wu_ref[:, cs])
        a_ref[:, cs] = (u * _sigmoid(u) * w).astype(BF16)
    y2 = [y + _dot(a_ref[rows, :], wd_ref[...]) for y, rows in zip(y1, halves)]
    for y, rows in zip(y2, halves):
        y_ref[rows, :] = _rms(y, gf_ref[...])


def _ffn(x2d, fo, go, prm, tm=1024, chunk=256):
    n = x2d.shape[0]
    tm = min(tm, n)
    row = lambda w: pl.BlockSpec((tm, w), lambda i: (i, 0))
    const = lambda shape: pl.BlockSpec(shape, lambda i: (0, 0), pipeline_mode=pl.Buffered(1))
    fox_time_minor = fo.ndim == 4
    if fox_time_minor:
        per_tile = tm // fo.shape[3]
        tiles_per_seq = fo.shape[1] // per_tile
        assert per_tile * fo.shape[3] == tm and tiles_per_seq * per_tile == fo.shape[1]
        fo_spec = pl.BlockSpec((None, per_tile, FOX_WIDTH, fo.shape[3]),
                               lambda i: (i // tiles_per_seq, i % tiles_per_seq, 0, 0))
    else:
        fo_spec = row(FOX_WIDTH)
    return pl.pallas_call(
        functools.partial(_ffn_kernel, chunk=chunk, fox_time_minor=fox_time_minor),
        grid=(n // tm,),
        in_specs=[row(D_MODEL), fo_spec, row(GLA_VW), const((D_MODEL, D_MODEL)),
                  const((1, D_MODEL)), const((D_MODEL, D_FF)), const((D_MODEL, D_FF)),
                  const((D_FF, D_MODEL)), const((1, D_MODEL))],
        out_specs=row(D_MODEL),
        out_shape=jax.ShapeDtypeStruct((n, D_MODEL), F32),
        scratch_shapes=[pltpu.VMEM((tm, D_FF), BF16)],
        compiler_params=pltpu.CompilerParams(dimension_semantics=("arbitrary",),
                                             vmem_limit_bytes=VMEM_LIMIT),
        name="ffn",
    )(x2d, fo, go, prm['wo'], prm['g2'], prm['wg'], prm['wu'], prm['wd'], prm['gf'])


def _layer_params(layer, norm1_g, w_in, w_gate2, b_gate2, b_forget, gla_norm_g, w_out,
                  norm2_g, w_gate, w_up, w_down, final_norm_g):
    wt = jnp.transpose(w_in[layer])
    o_fl = 3 * FOX_WIDTH
    o_gq = o_fl + FOX_HEADS
    o_gg = o_gq + 2 * GLA_KW + GLA_VW
    o_gr = o_gg + GLA_GATE_RANK
    tail = 2 * GLA_GATE_RANK
    pad = jnp.zeros((tail - FOX_HEADS - GLA_GATE_RANK, D_MODEL), F32)
    w = jnp.concatenate([wt[:o_fl], wt[o_gq:o_gg], wt[o_gr:], wt[o_fl:o_gq], wt[o_gg:o_gr], pad], axis=0)
    wg2 = jnp.zeros((tail, GLA_KW), F32).at[FOX_HEADS:FOX_HEADS + GLA_GATE_RANK].set(w_gate2[layer])
    return dict(
        g1=norm1_g[layer].reshape(1, D_MODEL),
        w=w.astype(BF16),
        wg2=wg2.astype(BF16),
        bg2=b_gate2[layer].reshape(1, GLA_KW),
        bfc=b_forget[layer].reshape(FOX_HEADS, 1),
        gn=gla_norm_g[layer].reshape(1, GLA_VW),
        wo=w_out[layer].astype(BF16),
        g2=norm2_g[layer].reshape(1, D_MODEL),
        wg=w_gate[layer].astype(BF16),
        wu=w_up[layer].astype(BF16),
        wd=w_down[layer].astype(BF16),
        gf=final_norm_g.reshape(1, D_MODEL),
    )


def kernel(x_prompt, x_sample, cache_fox_k, cache_fox_v, cache_fox_logf, state_gla, norm1_g, w_in,
           w_gate2, b_gate2, b_forget, gla_norm_g, w_out, norm2_g, w_gate, w_up, w_down, final_norm_g):
    depth = w_in.shape[0]
    assert depth == 1, "the final rmsnorm is fused into the layer's ffn kernel"
    bp, tp_, _ = x_prompt.shape
    bs, ts, _ = x_sample.shape
    past = cache_fox_k.shape[2]
    layer = 0
    prm = _layer_params(layer, norm1_g, w_in, w_gate2, b_gate2, b_forget, gla_norm_g, w_out,
                        norm2_g, w_gate, w_up, w_down, final_norm_g)
    by_time = lambda a, b, t: a.reshape(FOX_HEADS, b, t).transpose(1, 2, 0)[None]

    xp = x_prompt.reshape(bp * tp_, D_MODEL)
    qt, kt_p, vt_p, lf_p, ct, gq, gk, gv, glog, gr = _proj(xp, bp, tp_, prm, True)
    fox_p = _fox_prompt(qt, kt_p, vt_p, ct, bp, tp_)
    s0 = jnp.zeros((bp, GLA_HEADS, GLA_DK, GLA_DV), F32)
    gla_p_args = (gq, gk, gv, glog, gr, s0, prm['gn'])

    xs = x_sample.reshape(bs * ts, D_MODEL)
    q, k_s, v_s, lf_s, ct, gq, gk, gv, glog, gr = _proj(xs, bs, ts, prm, False)
    cn = ct.reshape(FOX_HEADS, bs, ts).transpose(1, 0, 2)
    lft = cache_fox_logf[layer].astype(F32).transpose(0, 2, 1)
    ck = cache_fox_k[layer].transpose(0, 2, 3, 1)
    cv = cache_fox_v[layer].transpose(0, 2, 3, 1)
    gla_p, s_p, fox_s = _gla_and_fox_sample(gla_p_args, (q, k_s, v_s, cn, lft, ck, cv),
                                            bp, tp_, bs, ts, past)
    y_p = _ffn(xp, fox_p, gla_p, prm)
    gla_s, s_s = _gla(gq, gk, gv, glog, gr, state_gla[layer].astype(F32), prm['gn'], bs, ts)
    y_s = _ffn(xs, fox_s, gla_s, prm)

    heads = lambda a, b, t: a.reshape(1, b, t, FOX_HEADS, FOX_HEAD_DIM)
    return (y_p.reshape(bp, tp_, D_MODEL), y_s.reshape(bs, ts, D_MODEL),
            kt_p.transpose(0, 3, 1, 2)[None], vt_p.transpose(0, 3, 1, 2)[None],
            by_time(lf_p, bp, tp_), s_p[None],
            heads(k_s, bs, ts), heads(v_s, bs, ts), by_time(lf_s, bs, ts), s_s[None])
```

```python
import functools

import numpy as np
import jax
import jax.numpy as jnp
from jax import lax
from jax.experimental import pallas as pl
from jax.experimental.pallas import tpu as pltpu

D_MODEL = 1024
FOX_HEADS = 8
FOX_HEAD_DIM = 64
FOX_WIDTH = FOX_HEADS * FOX_HEAD_DIM
GLA_HEADS = 4
GLA_DK = 64
GLA_DV = 128
GLA_KW = GLA_HEADS * GLA_DK
GLA_VW = GLA_HEADS * GLA_DV
GLA_GATE_RANK = 16
GLA_GATE_TEMP = 16.0
D_FF = 2816
EPS = 1e-6

LANES = 128
LOG2E = 1.4426950408889634
VMEM_LIMIT = 56 * 1024 * 1024

F32 = jnp.float32
BF16 = jnp.bfloat16
NEG_BIG = -1e30


def _log_sigmoid(x):
    return jnp.minimum(x, 0.0) - jnp.log1p(jnp.exp(-jnp.abs(x)))


def _sigmoid(x):
    return 1.0 / (1.0 + jnp.exp(-x))


def _split3_f32(x):
    hi = x.astype(BF16).astype(F32)
    r = x - hi
    mid = r.astype(BF16).astype(F32)
    lo = (r - mid).astype(BF16).astype(F32)
    return hi, mid, lo


def _split2_f32(x):
    hi = x.astype(BF16).astype(F32)
    return hi, (x - hi).astype(BF16).astype(F32)


def _dot(a, b):
    return jnp.dot(a, b, preferred_element_type=F32)


def _dot_nt(a, b):
    return lax.dot_general(a, b, (((1,), (1,)), ((), ())), preferred_element_type=F32)


def _dot_tn(a, b):
    return lax.dot_general(a, b, (((0,), (0,)), ((), ())), preferred_element_type=F32)


def _rms(x, g):
    return x * lax.rsqrt(jnp.mean(x * x, axis=-1, keepdims=True) + EPS) * g


def _proj_kernel(x_ref, g1_ref, w_ref, wg2_ref, bg2_ref, bfc_ref, tri_ref,
                 *rest, tiles_per_seq, time_minor):
    o_kv, o_b = FOX_WIDTH, 3 * FOX_WIDTH
    o_tail = o_b + 2 * GLA_KW + 2 * GLA_VW
    carry_ref = rest[-1]

    @pl.when(pl.program_id(0) % tiles_per_seq == 0)
    def _():
        carry_ref[...] = jnp.zeros_like(carry_ref)

    h = _rms(x_ref[...], g1_ref[...]).astype(BF16)
    tm = h.shape[0]
    scale = FOX_HEAD_DIM ** -0.5

    zc = _dot_nt(h, w_ref[o_tail:, :])
    fl_t = _dot_nt(w_ref[o_tail:, :], h)[0:FOX_HEADS]

    if time_minor:
        (q_ref, k_ref, v_ref, lft_ref, ct_ref, gq_ref, gk_ref, gv_ref, glog_ref, gr_ref,
         carry_ref) = rest
        qt = (_dot_nt(w_ref[0:o_kv, :], h) * (scale * LOG2E)).astype(BF16)
        tq = q_ref.shape[2]
        for i in range(q_ref.shape[0]):
            q_ref[i] = qt[:, i * tq:(i + 1) * tq]
    else:
        (q_ref, k_ref, v_ref, lft_ref, ct_ref, gq_ref, gk_ref, gv_ref, glog_ref, gr_ref,
         carry_ref) = rest
        q_ref[...] = (_dot_nt(h, w_ref[0:o_kv, :]) * scale).astype(BF16)

    logf_t = _log_sigmoid(fl_t + bfc_ref[...])
    lft_ref[...] = logf_t
    parts = jnp.concatenate(_split3_f32(logf_t), axis=0).astype(BF16)
    gg = zc.astype(BF16)

    if time_minor:
        kvt = _dot_nt(w_ref[o_kv:o_b, :], h)
        k_ref[...] = kvt[:FOX_WIDTH].reshape(FOX_HEADS, FOX_HEAD_DIM, tm)
        v_ref[...] = kvt[FOX_WIDTH:].reshape(FOX_HEADS, FOX_HEAD_DIM, tm)
    else:
        kv = _dot_nt(h, w_ref[o_kv:o_b, :])
        k_ref[...] = kv[:, :FOX_WIDTH]
        v_ref[...] = kv[:, FOX_WIDTH:]

    gpre = _dot(gg, wg2_ref[...]) + bg2_ref[...]
    cs = _dot(parts, tri_ref[...])

    zb = _dot_nt(h, w_ref[o_b:o_tail, :])
    gq_ref[...] = zb[:, :GLA_KW]
    gk_ref[...] = zb[:, GLA_KW:2 * GLA_KW]
    gv_ref[...] = zb[:, 2 * GLA_KW:2 * GLA_KW + GLA_VW].astype(BF16)
    gr_ref[...] = zb[:, 2 * GLA_KW + GLA_VW:]

    glog_ref[...] = _log_sigmoid(gpre) * (1.0 / GLA_GATE_TEMP)

    ct = cs[0:8] + cs[8:16] + cs[16:24] + carry_ref[:, 0:1]
    ct_ref[...] = ct
    carry_ref[...] = jnp.broadcast_to(ct[:, tm - 1:], carry_ref.shape)


def _proj(x2d, batch, seq_len, prm, time_minor, tm=1024, tq=512):
    n = x2d.shape[0]
    tm = min(tm, n)
    tiles_per_seq = max(seq_len // tm, 1)
    per_tile = tm // tq
    idx = np.arange(tm)
    tri = ((idx[:, None] <= idx[None, :]) & (idx[:, None] // seq_len == idx[None, :] // seq_len))
    tri = jnp.asarray(tri.astype(np.float32), dtype=BF16)
    const = lambda a: pl.BlockSpec(a.shape, lambda i: (0, 0), pipeline_mode=pl.Buffered(1))
    row = lambda w: pl.BlockSpec((tm, w), lambda i: (i, 0))
    col = pl.BlockSpec((FOX_HEADS, tm), lambda i: (0, i))
    sds = jax.ShapeDtypeStruct
    if time_minor:
        kv_shape = sds((batch, FOX_HEADS, FOX_HEAD_DIM, seq_len), F32)
        kv_spec = pl.BlockSpec((None, FOX_HEADS, FOX_HEAD_DIM, tm),
                               lambda i: (i // tiles_per_seq, 0, 0, i % tiles_per_seq))
        fox = [(sds((batch, seq_len // tq, FOX_WIDTH, tq), BF16),
                pl.BlockSpec((None, per_tile, FOX_WIDTH, tq),
                             lambda i: (i // tiles_per_seq, i % tiles_per_seq, 0, 0))),
               (kv_shape, kv_spec), (kv_shape, kv_spec)]
    else:
        fox = [(sds((n, FOX_WIDTH), BF16), row(FOX_WIDTH)),
               (sds((n, FOX_WIDTH), F32), row(FOX_WIDTH)),
               (sds((n, FOX_WIDTH), F32), row(FOX_WIDTH))]
    outs = fox + [
        (sds((FOX_HEADS, n), F32), col),
        (sds((FOX_HEADS, n), F32), col),
        (sds((n, GLA_KW), F32), row(GLA_KW)),
        (sds((n, GLA_KW), F32), row(GLA_KW)),
        (sds((n, GLA_VW), BF16), row(GLA_VW)),
        (sds((n, GLA_KW), F32), row(GLA_KW)),
        (sds((n, GLA_VW), F32), row(GLA_VW)),
    ]
    names = ('g1', 'w', 'wg2', 'bg2', 'bfc')
    return pl.pallas_call(
        functools.partial(_proj_kernel, tiles_per_seq=tiles_per_seq, time_minor=time_minor),
        grid=(n // tm,),
        in_specs=[row(D_MODEL)] + [const(prm[k]) for k in names] + [const(tri)],
        out_specs=tuple(s for _, s in outs), out_shape=tuple(s for s, _ in outs),
        scratch_shapes=[pltpu.VMEM((FOX_HEADS, LANES), F32)],
        compiler_params=pltpu.CompilerParams(dimension_semantics=("arbitrary",),
                                             vmem_limit_bytes=VMEM_LIMIT),
        name="proj",
    )(x2d, *[prm[k] for k in names], tri)


def _fox_prompt_kernel(q_ref, k_ref, v_ref, c_ref, o_ref, kb_ref, vb_ref, s_ref, p_ref, acc_ref,
                       *, tk):
    g = pl.program_id(1)
    nq = q_ref.shape[0]
    nk = vb_ref.shape[1]
    pairs = q_ref.shape[1] // LANES
    hd = FOX_HEAD_DIM
    spare = (hd, 0)
    vrows = vb_ref.shape[2]

    parts = jnp.concatenate(_split3_f32(c_ref[...] * (-LOG2E)), axis=0).astype(BF16)
    r = lax.broadcasted_iota(jnp.int32, (3 * FOX_HEADS, LANES), 0)
    ln = lax.broadcasted_iota(jnp.int32, (3 * FOX_HEADS, LANES), 1)
    klane = lax.broadcasted_iota(jnp.int32, (k_ref.shape[2], LANES), 1)
    tail = jnp.where(lax.broadcasted_iota(jnp.int32, (vrows - hd, tk), 0) == 0, 1.0, 0.0)
    for pp in range(pairs):
        k = k_ref[2 * pp:2 * pp + 2].reshape(LANES, k_ref.shape[2]).T.astype(BF16)
        for hh in range(2):
            head = 2 * (pairs * g + pp) + hh
            place = (r % FOX_HEADS == head) & (ln == spare[hh] + r // FOX_HEADS)
            extra = _dot_tn(parts, jnp.where(place, 1.0, 0.0).astype(BF16))
            own = (klane < hd) if hh == 0 else (klane >= hd)
            kb_ref[2 * pp + hh] = jnp.where(own, k, extra.astype(BF16))
            for jj in range(nk):
                vt = v_ref[2 * pp + hh, :, jj * tk:(jj + 1) * tk]
                vb_ref[2 * pp + hh, jj] = jnp.concatenate([vt, tail], axis=0).astype(BF16)

    qrow = lax.broadcasted_iota(jnp.int32, (LANES, tk), 0)
    key = lax.broadcasted_iota(jnp.int32, (tk, tk), 0)
    qry = lax.broadcasted_iota(jnp.int32, (tk, tk), 1)
    causal = key <= qry

    chains = [(slot, part) for slot in range(2 * pairs) for part in range(2)]
    everyone = list(range(len(chains)))
    second = [n for n in everyone if chains[n][1] == 1]

    def q_operands(qi):
        qs = []
        for slot, part in chains:
            pp, hh = divmod(slot, 2)
            q = q_ref[qi, pp * LANES:(pp + 1) * LANES, part * tk:(part + 1) * tk]
            own = (qrow < hd) if hh == 0 else (qrow >= hd)
            ones = (qrow >= spare[hh]) & (qrow < spare[hh] + 3)
            qs.append(jnp.where(own, q, jnp.where(ones, 1.0, 0.0).astype(BF16)))
        return qs

    steps = []
    for qi in range(nq):
        steps += [(qi, j, everyone, None) for j in range(2 * qi)]
        steps += [(qi, 2 * qi, everyone, 0), (qi, 2 * qi + 1, second, 1)]

    qs_of = {}

    def scores(step, buf):
        qi, j, live, _ = step
        if qi not in qs_of:
            qs_of.clear()
            qs_of[qi] = q_operands(qi)
        for n in live:
            s_ref[buf, n] = _dot(kb_ref[chains[n][0], j * tk:(j + 1) * tk, :], qs_of[qi][n])

    def values(step, buf):
        _, j, live, _ = step
        return {n: _dot(vb_ref[chains[n][0], j], p_ref[buf, n]) for n in live}

    def finish(qi):
        for part in range(2):
            heads = [acc_ref[qi % 2, 2 * slot + part] for slot in range(2 * pairs)]
            o_ref[qi, :, part * tk:(part + 1) * tk] = jnp.concatenate(
                [a[:hd] / a[hd:hd + 1] for a in heads], axis=0).astype(BF16)

    def accumulate(step, alphas, pv):
        qi, j, live, _ = step
        for n in live:
            acc_ref[qi % 2, n] = pv[n] if j == 0 else alphas[n] * acc_ref[qi % 2, n] + pv[n]

    ms = {}
    scores(steps[0], 0)
    prev, prev_alphas = None, None
    for t, step in enumerate(steps):
        qi, j, live, masked_part = step
        if t + 1 < len(steps):
            scores(steps[t + 1], (t + 1) % 2)
        pv = values(prev, (t - 1) % 2) if prev is not None else None
        alphas = {}
        for n in live:
            s = s_ref[t % 2, n]
            if chains[n][1] == masked_part:
                s = jnp.where(causal, s, -jnp.inf)
            m_new = jnp.max(s, axis=0, keepdims=True)
            if j > 0:
                m_new = jnp.maximum(ms[n], m_new)
                alphas[n] = jnp.exp2(ms[n] - m_new)
            p_ref[t % 2, n] = jnp.exp2(s - m_new).astype(BF16)
            ms[n] = m_new
        if prev is not None:
            accumulate(prev, prev_alphas, pv)
            if prev[0] != qi:
                finish(prev[0])
        prev, prev_alphas = step, alphas
    accumulate(prev, prev_alphas, values(prev, (len(steps) - 1) % 2))
    finish(prev[0])


def _fox_prompt(qt, kt, vt, ct, batch, seq, tk=256, pairs=2):
    tq = 2 * tk
    nq = seq // tq
    nk = seq // tk
    heads = 2 * pairs
    chains = 2 * heads
    vrows = FOX_HEAD_DIM + 16
    assert qt.shape == (batch, nq, FOX_WIDTH, tq) and FOX_HEADS % heads == 0
    q_spec = pl.BlockSpec((None, nq, pairs * LANES, tq), lambda b, p: (b, 0, p, 0))
    kv_spec = pl.BlockSpec((None, heads, FOX_HEAD_DIM, seq), lambda b, p: (b, p, 0, 0))
    return pl.pallas_call(
        functools.partial(_fox_prompt_kernel, tk=tk),
        grid=(batch, FOX_HEADS // heads),
        in_specs=[q_spec, kv_spec, kv_spec,
                  pl.BlockSpec((FOX_HEADS, seq), lambda b, p: (0, b))],
        out_specs=q_spec,
        out_shape=jax.ShapeDtypeStruct((batch, nq, FOX_WIDTH, tq), BF16),
        scratch_shapes=[pltpu.VMEM((heads, seq, LANES), BF16), pltpu.VMEM((heads, nk, vrows, tk), BF16),
                        pltpu.VMEM((2, chains, tk, tk), F32), pltpu.VMEM((2, chains, tk, tk), BF16),
                        pltpu.VMEM((2, chains, vrows, tk), F32)],
        compiler_params=pltpu.CompilerParams(dimension_semantics=("arbitrary", "arbitrary"),
                                             vmem_limit_bytes=VMEM_LIMIT),
        name="fox_prompt",
    )(qt, kt, vt, ct)


def _fox_sample_kernel(q_ref, kn_ref, vn_ref, cn_ref, lft_ref, mlow_ref, ck_ref, cv_ref,
                       o_ref, qh_ref, suf_ref, m_ref, l_ref, acc_ref, *, tp, nt, tn, j=None):
    j = pl.program_id(1) if j is None else j
    blk = 2 * LANES
    hd = FOX_HEAD_DIM

    @pl.when(j == 0)
    def _init():
        for h in range(FOX_HEADS):
            qh_ref[h] = q_ref[:, h * hd:(h + 1) * hd]
        m_ref[...] = jnp.full(m_ref.shape, NEG_BIG, F32)
        l_ref[...] = jnp.zeros(l_ref.shape, F32)
        acc_ref[...] = jnp.zeros(acc_ref.shape, F32)
        x = lft_ref[...]
        parts = _split3_f32(x)
        per_tile = tp // blk
        nblk = nt * per_tile
        rows = jnp.concatenate([p_[:, b * blk:(b + 1) * blk] for b in range(nblk) for p_ in parts], axis=0)
        y = _dot(rows.astype(BF16), mlow_ref[...])
        carry = jnp.zeros((FOX_HEADS, 1), F32)
        for b in reversed(range(nblk)):
            yb = y[24 * b:24 * b + 8] + y[24 * b + 8:24 * b + 16] + y[24 * b + 16:24 * b + 24]
            off = (b % per_tile) * blk
            suf_ref[b // per_tile, :, off:off + blk] = yb + carry
            carry = carry + yb[:, 0:1] + x[:, b * blk:b * blk + 1]

    def update(blocks):
        m_old = m_ref[...]
        m_new = m_old
        for s, _ in blocks:
            m_new = jnp.maximum(m_new, jnp.max(s, axis=-1, keepdims=True))
        alpha = jnp.exp(m_old - m_new)
        l = alpha * l_ref[...]
        acc = alpha * acc_ref[...]
        for s, pv in blocks:
            pm = jnp.exp(s - m_new)
            l = l + jnp.sum(pm, axis=-1, keepdims=True)
            acc = acc + pv(pm.astype(BF16))
        return m_new, l, acc

    bmm = lambda a, b, ca, cb: lax.dot_general(a, b, (((ca,), (cb,)), ((0,), (0,))),
                                               preferred_element_type=F32)
    qh = qh_ref[...]

    def cache_block():
        kt = ck_ref[...].astype(BF16)
        vt = cv_ref[...].astype(BF16)
        s = bmm(qh, kt, 2, 1).reshape(FOX_HEADS * tn, tp) + jnp.repeat(suf_ref[j], tn, axis=0)
        return s.reshape(FOX_HEADS, tn, tp), lambda pm: bmm(pm, vt, 2, 2)

    def new_block():
        per_head = lambda ref: jnp.stack([ref[:, h * hd:(h + 1) * hd] for h in range(FOX_HEADS)],
                                         axis=0).astype(BF16)
        kn = per_head(kn_ref)
        vn = per_head(vn_ref)
        r = lax.broadcasted_iota(jnp.int32, (FOX_HEADS, tn, tn), 1)
        c = lax.broadcasted_iota(jnp.int32, (FOX_HEADS, tn, tn), 2)
        s = jnp.where(c <= r, bmm(qh, kn, 2, 2) - cn_ref[...][:, None, :], -jnp.inf)
        return s, lambda pm: bmm(pm, vn, 2, 1)

    @pl.when(j < nt - 1)
    def _():
        m, l, acc = update([cache_block()])
        m_ref[...] = m
        l_ref[...] = l
        acc_ref[...] = acc

    @pl.when(j == nt - 1)
    def _():
        _, l, acc = update([cache_block(), new_block()])
        o = acc / l
        o_ref[...] = jnp.concatenate([o[h] for h in range(FOX_HEADS)], axis=1).astype(BF16)


def _fox_sample_call(batch, tn, past, tp, ix):
    nt = past // tp
    blk = 2 * LANES
    idx = np.arange(blk)
    mlow = jnp.asarray((idx[:, None] > idx[None, :]).astype(np.float32), dtype=BF16)
    per_b = lambda w: pl.BlockSpec((tn, w), lambda *g: (ix(*g)[0], 0))
    cache = pl.BlockSpec((None, FOX_HEADS, FOX_HEAD_DIM, tp), lambda *g: (ix(*g)[0], 0, 0, ix(*g)[1]))
    return dict(
        kernel=functools.partial(_fox_sample_kernel, tp=tp, nt=nt, tn=tn), nt=nt, consts=(mlow,),
        in_specs=[per_b(FOX_WIDTH), per_b(FOX_WIDTH), per_b(FOX_WIDTH),
                  pl.BlockSpec((None, FOX_HEADS, tn), lambda *g: (ix(*g)[0], 0, 0)),
                  pl.BlockSpec((None, FOX_HEADS, past), lambda *g: (ix(*g)[0], 0, 0)),
                  pl.BlockSpec((blk, blk), lambda *g: (0, 0)),
                  cache, cache],
        out_specs=[per_b(FOX_WIDTH)],
        out_shape=[jax.ShapeDtypeStruct((batch * tn, FOX_WIDTH), BF16)],
        scratch_shapes=[pltpu.VMEM((FOX_HEADS, tn, FOX_HEAD_DIM), BF16),
                        pltpu.VMEM((nt, FOX_HEADS, tp), F32),
                        pltpu.VMEM((FOX_HEADS, tn, 1), F32), pltpu.VMEM((FOX_HEADS, tn, 1), F32),
                        pltpu.VMEM((FOX_HEADS, tn, FOX_HEAD_DIM), F32)])


def _gla_kernel(q_ref, k_ref, v_ref, g_ref, r_ref, s0_ref, gn_ref, w_ref, lv_ref,
                o_ref, s_ref, st_ref, *, tb, levels, blocks, seqs, carried, t=None, nt=None):
    to_work = lambda s0: s0.reshape(GLA_KW, GLA_DV).T
    from_work = lambda st: st.T.reshape(GLA_HEADS, GLA_DK, GLA_DV)
    if carried:
        t = pl.program_id(1) if t is None else t
        nt = pl.num_programs(1) if nt is None else nt

        @pl.when(t == 0)
        def _():
            st_ref[...] = to_work(s0_ref[0])

    for sq in range(seqs):
        st = st_ref[...] if carried else to_work(s0_ref[sq])
        for blk in range(blocks):
            rows = lambda ref: ref.at[pl.ds((sq * blocks + blk) * tb, tb), :]
            st = _gla_block(rows(q_ref), rows(k_ref), rows(v_ref), rows(g_ref), rows(r_ref), gn_ref,
                            w_ref, lv_ref, rows(o_ref), st, tb=tb, levels=levels)
        if carried:
            st_ref[...] = st

            @pl.when(t == nt - 1)
            def _():
                s_ref[0] = from_work(st)
        else:
            s_ref[sq] = from_work(st)


def _gla_block(q_ref, k_ref, v_ref, g_ref, r_ref, gn_ref, w_ref, lv_ref, o_ref, st, *, tb, levels):
    half = LANES // 2
    r = r_ref[...]
    gate = gn_ref[...] * (r * _sigmoid(r))

    g2 = g_ref[...] * LOG2E
    gp = jnp.concatenate(_split2_f32(g2), axis=0).astype(BF16)
    rows = w_ref.shape[0] // 2
    cum = jnp.concatenate([_dot(w_ref[0:rows, :], gp), _dot(w_ref[rows:, :], gp)], axis=0)
    dec = jnp.exp2(cum)
    from_start = dec[0:tb]
    to_end = jnp.exp2(cum[tb - 1:tb] - cum[0:tb])

    q = q_ref[...] * (GLA_DK ** -0.5)
    k = k_ref[...]
    v = v_ref[...]
    row = lax.broadcasted_iota(jnp.int32, (tb, GLA_KW), 0)
    low = lax.broadcasted_iota(jnp.int32, (tb, LANES), 1) < half
    lv = lv_ref[...]

    def pair_scores(xq, xk):
        outs = []
        for p in range(GLA_HEADS // 2):
            a = xq[:, p * LANES:(p + 1) * LANES]
            zero = jnp.zeros_like(a)
            lhs = jnp.concatenate([jnp.where(low, a, zero), jnp.where(low, zero, a)], axis=0)
            outs.append(_dot_nt(lhs, xk[:, p * LANES:(p + 1) * LANES]))
        return outs

    here = lv == -1
    a = [jnp.where(here, r_, 0.0) for r_ in pair_scores(q.astype(BF16), k.astype(BF16))]
    for l in range(levels):
        upper = ((row >> l) & 1) == 1
        e = jnp.where(upper, jnp.exp2(g2), 1.0) if l == 0 else dec[l * tb:(l + 1) * tb]
        x = (jnp.where(upper, q, k) * e).astype(BF16)
        here = lv == l
        a = [jnp.where(here, r_, a_) for r_, a_ in zip(pair_scores(x, x), a)]
    o = jnp.concatenate(
        [_dot(a[h // 2][(h % 2) * tb:(h % 2 + 1) * tb].astype(BF16), v[:, h * GLA_DV:(h + 1) * GLA_DV])
         for h in range(GLA_HEADS)], axis=1)

    lane = lax.broadcasted_iota(jnp.int32, (tb, GLA_KW), 1)
    head_sel = [(lane >= h * GLA_DK) & (lane < (h + 1) * GLA_DK) for h in range(GLA_HEADS)]
    qt = (q * from_start).astype(BF16)
    kt = (k * to_end).astype(BF16)
    zero = jnp.zeros_like(qt)
    q4 = jnp.concatenate([jnp.where(sel, qt, zero) for sel in head_sel], axis=0)
    oi = _dot_nt(q4, st.astype(BF16))
    o = o + jnp.concatenate([oi[h * tb:(h + 1) * tb] for h in range(GLA_HEADS)], axis=1)
    upd = None
    for h in range(GLA_HEADS):
        u = _dot_tn(v[:, h * GLA_DV:(h + 1) * GLA_DV], jnp.where(head_sel[h], kt, zero))
        upd = u if upd is None else upd + u
    st = from_start[tb - 1:tb, :] * st + upd

    outs = []
    for h in range(GLA_HEADS):
        oh = o[:, h * GLA_DV:(h + 1) * GLA_DV]
        outs.append(oh * lax.rsqrt(jnp.mean(oh * oh, axis=-1, keepdims=True) + EPS))
    o_ref[...] = (jnp.concatenate(outs, axis=1) * gate).astype(BF16)
    return st


def _gla_call(batch, seq, ix, tb=128, rows_per_step=1024):
    tb = min(tb, seq)
    step_rows = min(rows_per_step, batch * seq)
    carried = seq > step_rows
    seqs = 1 if carried else step_rows // seq
    per_seq = (step_rows if carried else seq) // tb
    nt = seq // (tb * per_seq)
    n = batch * seq
    levels = tb.bit_length() - 1
    assert tb == 1 << levels
    ti = np.arange(tb)[:, None]
    si = np.arange(tb)[None, :]
    blocks = [si <= ti]
    for l in range(1, levels):
        m = 1 << l
        mid = ti - ti % (2 * m) + m - 1
        upper = ti % (2 * m) >= m
        blocks.append(np.where(upper, (si > mid) & (si <= ti), (si > ti) & (si <= mid)))
    w = np.concatenate(blocks, axis=0).astype(np.float32)
    w = jnp.asarray(np.concatenate([w, w], axis=1), dtype=BF16)
    x = ti ^ si
    lv = np.where(si < ti, np.floor(np.log2(np.maximum(x, 1))).astype(np.int32),
                  np.where(si == ti, -1, -2)).astype(np.int32)
    lv = jnp.asarray(np.concatenate([lv, lv], axis=0))
    row = lambda w_: pl.BlockSpec((step_rows, w_), lambda *g: (ix(*g)[0] * nt + ix(*g)[1], 0))
    state = pl.BlockSpec((seqs, GLA_HEADS, GLA_DK, GLA_DV), lambda *g: (ix(*g)[0], 0, 0, 0))
    const = lambda shape: pl.BlockSpec(shape, lambda *g: (0, 0))
    return dict(
        kernel=functools.partial(_gla_kernel, tb=tb, levels=levels, blocks=per_seq, seqs=seqs,
                                 carried=carried),
        groups=batch // seqs, nt=nt, consts=(w, lv),
        in_specs=[row(GLA_KW), row(GLA_KW), row(GLA_VW), row(GLA_KW), row(GLA_VW), state,
                  const((1, GLA_VW)), const(w.shape), const(lv.shape)],
        out_specs=[row(GLA_VW), state],
        out_shape=[jax.ShapeDtypeStruct((n, GLA_VW), BF16),
                   jax.ShapeDtypeStruct((batch, GLA_HEADS, GLA_DK, GLA_DV), F32)],
        scratch_shapes=[pltpu.VMEM((GLA_DV, GLA_KW), F32)])


def _mixers_under_cache_stream(gla_p_args, fox_args, gla_s_args, gla_batch, gla_seq, batch, tn, past,
                               gla_rows=256, tp=2048):
    nt_g, nt_f = gla_seq // gla_rows, past // tp
    g = _gla_call(gla_batch, gla_seq, lambda i: (i // nt_g, i % nt_g), rows_per_step=gla_rows)
    f = _fox_sample_call(batch, tn, past, tp, lambda i: (i // nt_f, i % nt_f))
    h = _gla_call(batch, tn, lambda i: (i // nt_f, 0), rows_per_step=tn)
    steps = g['groups'] * nt_g
    assert steps == batch * nt_f and g['nt'] == nt_g and f['nt'] == nt_f and h['nt'] == 1
    calls = (g, f, h)

    def kernel(*refs):
        i = pl.program_id(0)
        refs = list(refs)
        take = lambda key: [[refs.pop(0) for _ in c[key]] for c in calls]
        (g_in, f_in, h_in), (g_out, f_out, h_out) = take('in_specs'), take('out_specs')
        g_scr, f_scr, h_scr = take('scratch_shapes')
        g['kernel'](*g_in, *g_out, *g_scr, t=i % nt_g, nt=nt_g)
        f['kernel'](*f_in, *f_out, *f_scr, j=i % nt_f)

        @pl.when(i % nt_f == nt_f - 1)
        def _():
            h['kernel'](*h_in, *h_out, *h_scr)

    cat = lambda key: [x for c in calls for x in c[key]]
    return pl.pallas_call(
        kernel, grid=(steps,), in_specs=cat('in_specs'), out_specs=cat('out_specs'),
        out_shape=cat('out_shape'), scratch_shapes=cat('scratch_shapes'),
        compiler_params=pltpu.CompilerParams(dimension_semantics=("arbitrary",),
                                             vmem_limit_bytes=VMEM_LIMIT),
        name="mixers",
    )(*gla_p_args, *g['consts'], *fox_args[:5], *f['consts'], *fox_args[5:], *gla_s_args, *h['consts'])


def _ffn_kernel(x_ref, fo_ref, go_ref, wo_ref, g2_ref, wg_ref, wu_ref, wd_ref, gf_ref,
                y_ref, a_ref, *, chunk, fox_time_minor):
    tm = x_ref.shape[0]
    parts = fo_ref.shape[0] if fox_time_minor else 2
    halves = [slice(i * (tm // parts), (i + 1) * (tm // parts)) for i in range(parts)]
    y1 = []
    for i, rows in enumerate(halves):
        fox = (_dot_tn(fo_ref[i], wo_ref[0:FOX_WIDTH, :]) if fox_time_minor
               else _dot(fo_ref[rows, :], wo_ref[0:FOX_WIDTH, :]))
        y1.append(x_ref[rows, :] + fox + _dot(go_ref[rows, :], wo_ref[FOX_WIDTH:, :]))
    h2 = jnp.concatenate([_rms(y, g2_ref[...]).astype(BF16) for y in y1], axis=0)
    for c in range(D_FF // chunk):
        cs = slice(c * chunk, (c + 1) * chunk)
        u = _dot(h2, wg_ref[:, cs])
        w = _dot(h2, wu_ref[:, cs])
        a_ref[:, cs] = (u * _sigmoid(u) * w).astype(BF16)
    y2 = [y + _dot(a_ref[rows, :], wd_ref[...]) for y, rows in zip(y1, halves)]
    for y, rows in zip(y2, halves):
        y_ref[rows, :] = _rms(y, gf_ref[...])


def _ffn(x2d, fo, go, prm, tm=1024, chunk=256):
    n = x2d.shape[0]
    tm = min(tm, n)
    row = lambda w: pl.BlockSpec((tm, w), lambda i: (i, 0))
    const = lambda shape: pl.BlockSpec(shape, lambda i: (0, 0), pipeline_mode=pl.Buffered(1))
    fox_time_minor = fo.ndim == 4
    if fox_time_minor:
        per_tile = tm // fo.shape[3]
        tiles_per_seq = fo.shape[1] // per_tile
        assert per_tile * fo.shape[3] == tm and tiles_per_seq * per_tile == fo.shape[1]
        fo_spec = pl.BlockSpec((None, per_tile, FOX_WIDTH, fo.shape[3]),
                               lambda i: (i // tiles_per_seq, i % tiles_per_seq, 0, 0))
    else:
        fo_spec = row(FOX_WIDTH)
    return pl.pallas_call(
        functools.partial(_ffn_kernel, chunk=chunk, fox_time_minor=fox_time_minor),
        grid=(n // tm,),
        in_specs=[row(D_MODEL), fo_spec, row(GLA_VW), const((D_MODEL, D_MODEL)),
                  const((1, D_MODEL)), const((D_MODEL, D_FF)), const((D_MODEL, D_FF)),
                  const((D_FF, D_MODEL)), const((1, D_MODEL))],
        out_specs=row(D_MODEL),
        out_shape=jax.ShapeDtypeStruct((n, D_MODEL), F32),
        scratch_shapes=[pltpu.VMEM((tm, D_FF), BF16)],
        compiler_params=pltpu.CompilerParams(dimension_semantics=("arbitrary",),
                                             vmem_limit_bytes=VMEM_LIMIT),
        name="ffn",
    )(x2d, fo, go, prm['wo'], prm['g2'], prm['wg'], prm['wu'], prm['wd'], prm['gf'])


def _layer_params(layer, norm1_g, w_in, w_gate2, b_gate2, b_forget, gla_norm_g, w_out,
                  norm2_g, w_gate, w_up, w_down, final_norm_g):
    wt = jnp.transpose(w_in[layer])
    o_fl = 3 * FOX_WIDTH
    o_gq = o_fl + FOX_HEADS
    o_gg = o_gq + 2 * GLA_KW + GLA_VW
    o_gr = o_gg + GLA_GATE_RANK
    tail = 2 * GLA_GATE_RANK
    pad = jnp.zeros((tail - FOX_HEADS - GLA_GATE_RANK, D_MODEL), F32)
    w = jnp.concatenate([wt[:o_fl], wt[o_gq:o_gg], wt[o_gr:], wt[o_fl:o_gq], wt[o_gg:o_gr], pad], axis=0)
    wg2 = jnp.zeros((tail, GLA_KW), F32).at[FOX_HEADS:FOX_HEADS + GLA_GATE_RANK].set(w_gate2[layer])
    return dict(
        g1=norm1_g[layer].reshape(1, D_MODEL),
        w=w.astype(BF16),
        wg2=wg2.astype(BF16),
        bg2=b_gate2[layer].reshape(1, GLA_KW),
        bfc=b_forget[layer].reshape(FOX_HEADS, 1),
        gn=gla_norm_g[layer].reshape(1, GLA_VW),
        wo=w_out[layer].astype(BF16),
        g2=norm2_g[layer].reshape(1, D_MODEL),
        wg=w_gate[layer].astype(BF16),
        wu=w_up[layer].astype(BF16),
        wd=w_down[layer].astype(BF16),
        gf=final_norm_g.reshape(1, D_MODEL),
    )


def kernel(x_prompt, x_sample, cache_fox_k, cache_fox_v, cache_fox_logf, state_gla, norm1_g, w_in,
           w_gate2, b_gate2, b_forget, gla_norm_g, w_out, norm2_g, w_gate, w_up, w_down, final_norm_g):
    depth = w_in.shape[0]
    assert depth == 1, "the final rmsnorm is fused into the layer's ffn kernel"
    bp, tp_, _ = x_prompt.shape
    bs, ts, _ = x_sample.shape
    past = cache_fox_k.shape[2]
    layer = 0
    prm = _layer_params(layer, norm1_g, w_in, w_gate2, b_gate2, b_forget, gla_norm_g, w_out,
                        norm2_g, w_gate, w_up, w_down, final_norm_g)
    by_time = lambda a, b, t: a.reshape(FOX_HEADS, b, t).transpose(1, 2, 0)[None]

    xp = x_prompt.reshape(bp * tp_, D_MODEL)
    qt, kt_p, vt_p, lf_p, ct, gq, gk, gv, glog, gr = _proj(xp, bp, tp_, prm, True)
    fox_p = _fox_prompt(qt, kt_p, vt_p, ct, bp, tp_)
    s0 = jnp.zeros((bp, GLA_HEADS, GLA_DK, GLA_DV), F32)
    gla_p_args = (gq, gk, gv, glog, gr, s0, prm['gn'])

    xs = x_sample.reshape(bs * ts, D_MODEL)
    q, k_s, v_s, lf_s, ct, gq, gk, gv, glog, gr = _proj(xs, bs, ts, prm, False)
    cn = ct.reshape(FOX_HEADS, bs, ts).transpose(1, 0, 2)
    lft = cache_fox_logf[layer].astype(F32).transpose(0, 2, 1)
    ck = cache_fox_k[layer].transpose(0, 2, 3, 1)
    cv = cache_fox_v[layer].transpose(0, 2, 3, 1)
    gla_s_args = (gq, gk, gv, glog, gr, state_gla[layer].astype(F32), prm['gn'])
    gla_p, s_p, fox_s, gla_s, s_s = _mixers_under_cache_stream(
        gla_p_args, (q, k_s, v_s, cn, lft, ck, cv), gla_s_args, bp, tp_, bs, ts, past)
    y_p = _ffn(xp, fox_p, gla_p, prm)
    y_s = _ffn(xs, fox_s, gla_s, prm)

    heads = lambda a, b, t: a.reshape(1, b, t, FOX_HEADS, FOX_HEAD_DIM)
    return (y_p.reshape(bp, tp_, D_MODEL), y_s.reshape(bs, ts, D_MODEL),
            kt_p.transpose(0, 3, 1, 2)[None], vt_p.transpose(0, 3, 1, 2)[None],
            by_time(lf_p, bp, tp_), s_p[None],
            heads(k_s, bs, ts), heads(v_s, bs, ts), by_time(lf_s, bs, ts), s_s[None])
```

```python
import functools

import numpy as np
import jax
import jax.numpy as jnp
from jax import lax
from jax.experimental import pallas as pl
from jax.experimental.pallas import tpu as pltpu

D_MODEL = 1024
FOX_HEADS = 8
FOX_HEAD_DIM = 64
FOX_WIDTH = FOX_HEADS * FOX_HEAD_DIM
GLA_HEADS = 4
GLA_DK = 64
GLA_DV = 128
GLA_KW = GLA_HEADS * GLA_DK
GLA_VW = GLA_HEADS * GLA_DV
GLA_GATE_RANK = 16
GLA_GATE_TEMP = 16.0
D_FF = 2816
EPS = 1e-6

LANES = 128
LOG2E = 1.4426950408889634
VMEM_LIMIT = 56 * 1024 * 1024

F32 = jnp.float32
BF16 = jnp.bfloat16
NEG_BIG = -1e30


def _log_sigmoid(x):
    return jnp.minimum(x, 0.0) - jnp.log1p(jnp.exp(-jnp.abs(x)))


def _sigmoid(x):
    return 1.0 / (1.0 + jnp.exp(-x))


def _split3_f32(x):
    hi = x.astype(BF16).astype(F32)
    r = x - hi
    mid = r.astype(BF16).astype(F32)
    lo = (r - mid).astype(BF16).astype(F32)
    return hi, mid, lo


def _split2_f32(x):
    hi = x.astype(BF16).astype(F32)
    return hi, (x - hi).astype(BF16).astype(F32)


def _dot(a, b):
    return jnp.dot(a, b, preferred_element_type=F32)


def _dot_nt(a, b):
    return lax.dot_general(a, b, (((1,), (1,)), ((), ())), preferred_element_type=F32)


def _dot_tn(a, b):
    return lax.dot_general(a, b, (((0,), (0,)), ((), ())), preferred_element_type=F32)


def _rms(x, g):
    return x * lax.rsqrt(jnp.mean(x * x, axis=-1, keepdims=True) + EPS) * g


def _proj_kernel(x_ref, g1_ref, w_ref, wg2_ref, bg2_ref, bfc_ref, tri_ref,
                 *rest, tiles_per_seq, time_minor):
    o_kv, o_b = FOX_WIDTH, 3 * FOX_WIDTH
    o_tail = o_b + 2 * GLA_KW + 2 * GLA_VW
    carry_ref = rest[-1]

    @pl.when(pl.program_id(0) % tiles_per_seq == 0)
    def _():
        carry_ref[...] = jnp.zeros_like(carry_ref)

    h = _rms(x_ref[...], g1_ref[...]).astype(BF16)
    tm = h.shape[0]
    scale = FOX_HEAD_DIM ** -0.5

    zc = _dot_nt(h, w_ref[o_tail:, :])
    fl_t = _dot_nt(w_ref[o_tail:, :], h)[0:FOX_HEADS]

    if time_minor:
        (q_ref, k_ref, v_ref, lft_ref, ct_ref, gq_ref, gk_ref, gv_ref, glog_ref, gr_ref,
         carry_ref) = rest
        qt = (_dot_nt(w_ref[0:o_kv, :], h) * (scale * LOG2E)).astype(BF16)
        tq = q_ref.shape[2]
        for i in range(q_ref.shape[0]):
            q_ref[i] = qt[:, i * tq:(i + 1) * tq]
    else:
        (q_ref, k_ref, v_ref, lft_ref, ct_ref, gq_ref, gk_ref, gv_ref, glog_ref, gr_ref,
         carry_ref) = rest
        q_ref[...] = (_dot_nt(h, w_ref[0:o_kv, :]) * scale).astype(BF16)

    logf_t = _log_sigmoid(fl_t + bfc_ref[...])
    lft_ref[...] = logf_t
    parts = jnp.concatenate(_split3_f32(logf_t), axis=0).astype(BF16)
    gg = zc.astype(BF16)

    if time_minor:
        kvt = _dot_nt(w_ref[o_kv:o_b, :], h)
        k_ref[...] = kvt[:FOX_WIDTH].reshape(FOX_HEADS, FOX_HEAD_DIM, tm)
        v_ref[...] = kvt[FOX_WIDTH:].reshape(FOX_HEADS, FOX_HEAD_DIM, tm)
    else:
        kv = _dot_nt(h, w_ref[o_kv:o_b, :])
        k_ref[...] = kv[:, :FOX_WIDTH]
        v_ref[...] = kv[:, FOX_WIDTH:]

    gpre = _dot(gg, wg2_ref[...]) + bg2_ref[...]
    cs = _dot(parts, tri_ref[...])

    zb = _dot_nt(h, w_ref[o_b:o_tail, :])
    gq_ref[...] = zb[:, :GLA_KW]
    gk_ref[...] = zb[:, GLA_KW:2 * GLA_KW]
    gv_ref[...] = zb[:, 2 * GLA_KW:2 * GLA_KW + GLA_VW].astype(BF16)
    gr_ref[...] = zb[:, 2 * GLA_KW + GLA_VW:]

    glog_ref[...] = _log_sigmoid(gpre) * (1.0 / GLA_GATE_TEMP)

    ct = cs[0:8] + cs[8:16] + cs[16:24] + carry_ref[:, 0:1]
    ct_ref[...] = ct
    carry_ref[...] = jnp.broadcast_to(ct[:, tm - 1:], carry_ref.shape)


def _proj(x2d, batch, seq_len, prm, time_minor, tm=1024, tq=512):
    n = x2d.shape[0]
    tm = min(tm, n)
    tiles_per_seq = max(seq_len // tm, 1)
    per_tile = tm // tq
    idx = np.arange(tm)
    tri = ((idx[:, None] <= idx[None, :]) & (idx[:, None] // seq_len == idx[None, :] // seq_len))
    tri = jnp.asarray(tri.astype(np.float32), dtype=BF16)
    const = lambda a: pl.BlockSpec(a.shape, lambda i: (0, 0), pipeline_mode=pl.Buffered(1))
    row = lambda w: pl.BlockSpec((tm, w), lambda i: (i, 0))
    col = pl.BlockSpec((FOX_HEADS, tm), lambda i: (0, i))
    sds = jax.ShapeDtypeStruct
    if time_minor:
        kv_shape = sds((batch, FOX_HEADS, FOX_HEAD_DIM, seq_len), F32)
        kv_spec = pl.BlockSpec((None, FOX_HEADS, FOX_HEAD_DIM, tm),
                               lambda i: (i // tiles_per_seq, 0, 0, i % tiles_per_seq))
        fox = [(sds((batch, seq_len // tq, FOX_WIDTH, tq), BF16),
                pl.BlockSpec((None, per_tile, FOX_WIDTH, tq),
                             lambda i: (i // tiles_per_seq, i % tiles_per_seq, 0, 0))),
               (kv_shape, kv_spec), (kv_shape, kv_spec)]
    else:
        fox = [(sds((n, FOX_WIDTH), BF16), row(FOX_WIDTH)),
               (sds((n, FOX_WIDTH), F32), row(FOX_WIDTH)),
               (sds((n, FOX_WIDTH), F32), row(FOX_WIDTH))]
    outs = fox + [
        (sds((FOX_HEADS, n), F32), col),
        (sds((FOX_HEADS, n), F32), col),
        (sds((n, GLA_KW), F32), row(GLA_KW)),
        (sds((n, GLA_KW), F32), row(GLA_KW)),
        (sds((n, GLA_VW), BF16), row(GLA_VW)),
        (sds((n, GLA_KW), F32), row(GLA_KW)),
        (sds((n, GLA_VW), F32), row(GLA_VW)),
    ]
    names = ('g1', 'w', 'wg2', 'bg2', 'bfc')
    return pl.pallas_call(
        functools.partial(_proj_kernel, tiles_per_seq=tiles_per_seq, time_minor=time_minor),
        grid=(n // tm,),
        in_specs=[row(D_MODEL)] + [const(prm[k]) for k in names] + [const(tri)],
        out_specs=tuple(s for _, s in outs), out_shape=tuple(s for s, _ in outs),
        scratch_shapes=[pltpu.VMEM((FOX_HEADS, LANES), F32)],
        compiler_params=pltpu.CompilerParams(dimension_semantics=("arbitrary",),
                                             vmem_limit_bytes=VMEM_LIMIT),
        name="proj",
    )(x2d, *[prm[k] for k in names], tri)


def _fox_prompt_kernel(q_ref, k_ref, v_ref, c_ref, o_ref, kb_ref, vb_ref, s_ref, p_ref, acc_ref,
                       *, tk):
    g = pl.program_id(1)
    nq = q_ref.shape[0]
    nk = vb_ref.shape[1]
    pairs = q_ref.shape[1] // LANES
    hd = FOX_HEAD_DIM
    spare = (hd, 0)
    vrows = vb_ref.shape[2]

    parts = jnp.concatenate(_split3_f32(c_ref[...] * (-LOG2E)), axis=0).astype(BF16)
    r = lax.broadcasted_iota(jnp.int32, (3 * FOX_HEADS, LANES), 0)
    ln = lax.broadcasted_iota(jnp.int32, (3 * FOX_HEADS, LANES), 1)
    klane = lax.broadcasted_iota(jnp.int32, (k_ref.shape[2], LANES), 1)
    tail = jnp.where(lax.broadcasted_iota(jnp.int32, (vrows - hd, tk), 0) == 0, 1.0, 0.0)
    for pp in range(pairs):
        k = k_ref[2 * pp:2 * pp + 2].reshape(LANES, k_ref.shape[2]).T.astype(BF16)
        for hh in range(2):
            head = 2 * (pairs * g + pp) + hh
            place = (r % FOX_HEADS == head) & (ln == spare[hh] + r // FOX_HEADS)
            extra = _dot_tn(parts, jnp.where(place, 1.0, 0.0).astype(BF16))
            own = (klane < hd) if hh == 0 else (klane >= hd)
            kb_ref[2 * pp + hh] = jnp.where(own, k, extra.astype(BF16))
            for jj in range(nk):
                vt = v_ref[2 * pp + hh, :, jj * tk:(jj + 1) * tk]
                vb_ref[2 * pp + hh, jj] = jnp.concatenate([vt, tail], axis=0).astype(BF16)

    qrow = lax.broadcasted_iota(jnp.int32, (LANES, tk), 0)
    key = lax.broadcasted_iota(jnp.int32, (tk, tk), 0)
    qry = lax.broadcasted_iota(jnp.int32, (tk, tk), 1)
    causal = key <= qry

    chains = [(slot, part) for slot in range(2 * pairs) for part in range(2)]
    everyone = list(range(len(chains)))
    second = [n for n in everyone if chains[n][1] == 1]

    def q_operands(qi):
        qs = []
        for slot, part in chains:
            pp, hh = divmod(slot, 2)
            q = q_ref[qi, pp * LANES:(pp + 1) * LANES, part * tk:(part + 1) * tk]
            own = (qrow < hd) if hh == 0 else (qrow >= hd)
            ones = (qrow >= spare[hh]) & (qrow < spare[hh] + 3)
            qs.append(jnp.where(own, q, jnp.where(ones, 1.0, 0.0).astype(BF16)))
        return qs

    steps = []
    for qi in range(nq):
        steps += [(qi, j, everyone, None) for j in range(2 * qi)]
        steps += [(qi, 2 * qi, everyone, 0), (qi, 2 * qi + 1, second, 1)]

    qs_of = {}

    def scores(step, buf):
        qi, j, live, _ = step
        if qi not in qs_of:
            qs_of.clear()
            qs_of[qi] = q_operands(qi)
        for n in live:
            s_ref[buf, n] = _dot(kb_ref[chains[n][0], j * tk:(j + 1) * tk, :], qs_of[qi][n])

    def values(step, buf):
        _, j, live, _ = step
        return {n: _dot(vb_ref[chains[n][0], j], p_ref[buf, n]) for n in live}

    def finish(qi):
        for part in range(2):
            heads = [acc_ref[qi % 2, 2 * slot + part] for slot in range(2 * pairs)]
            o_ref[qi, :, part * tk:(part + 1) * tk] = jnp.concatenate(
                [a[:hd] / a[hd:hd + 1] for a in heads], axis=0).astype(BF16)

    def accumulate(step, alphas, pv):
        qi, j, live, _ = step
        for n in live:
            acc_ref[qi % 2, n] = pv[n] if j == 0 else alphas[n] * acc_ref[qi % 2, n] + pv[n]

    ms = {}
    scores(steps[0], 0)
    prev, prev_alphas = None, None
    for t, step in enumerate(steps):
        qi, j, live, masked_part = step
        if t + 1 < len(steps):
            scores(steps[t + 1], (t + 1) % 2)
        pv = values(prev, (t - 1) % 2) if prev is not None else None
        alphas = {}
        for n in live:
            s = s_ref[t % 2, n]
            if chains[n][1] == masked_part:
                s = jnp.where(causal, s, -jnp.inf)
            m_new = jnp.max(s, axis=0, keepdims=True)
            if j > 0:
                m_new = jnp.maximum(ms[n], m_new)
                alphas[n] = jnp.exp2(ms[n] - m_new)
            p_ref[t % 2, n] = jnp.exp2(s - m_new).astype(BF16)
            ms[n] = m_new
        if prev is not None:
            accumulate(prev, prev_alphas, pv)
            if prev[0] != qi:
                finish(prev[0])
        prev, prev_alphas = step, alphas
    accumulate(prev, prev_alphas, values(prev, (len(steps) - 1) % 2))
    finish(prev[0])


def _fox_prompt(qt, kt, vt, ct, batch, seq, tk=256, pairs=2):
    tq = 2 * tk
    nq = seq // tq
    nk = seq // tk
    heads = 2 * pairs
    chains = 2 * heads
    vrows = FOX_HEAD_DIM + 16
    assert qt.shape == (batch, nq, FOX_WIDTH, tq) and FOX_HEADS % heads == 0
    q_spec = pl.BlockSpec((None, nq, pairs * LANES, tq), lambda b, p: (b, 0, p, 0))
    kv_spec = pl.BlockSpec((None, heads, FOX_HEAD_DIM, seq), lambda b, p: (b, p, 0, 0))
    return pl.pallas_call(
        functools.partial(_fox_prompt_kernel, tk=tk),
        grid=(batch, FOX_HEADS // heads),
        in_specs=[q_spec, kv_spec, kv_spec,
                  pl.BlockSpec((FOX_HEADS, seq), lambda b, p: (0, b))],
        out_specs=q_spec,
        out_shape=jax.ShapeDtypeStruct((batch, nq, FOX_WIDTH, tq), BF16),
        scratch_shapes=[pltpu.VMEM((heads, seq, LANES), BF16), pltpu.VMEM((heads, nk, vrows, tk), BF16),
                        pltpu.VMEM((2, chains, tk, tk), F32), pltpu.VMEM((2, chains, tk, tk), BF16),
                        pltpu.VMEM((2, chains, vrows, tk), F32)],
        compiler_params=pltpu.CompilerParams(dimension_semantics=("arbitrary", "arbitrary"),
                                             vmem_limit_bytes=VMEM_LIMIT),
        name="fox_prompt",
    )(qt, kt, vt, ct)


def _fox_sample_kernel(q_ref, kn_ref, vn_ref, cn_ref, lft_ref, mlow_ref, ck_ref, cv_ref,
                       o_ref, qh_ref, suf_ref, m_ref, l_ref, acc_ref, *, tp, nt, tn, j=None):
    j = pl.program_id(1) if j is None else j
    blk = 2 * LANES
    hd = FOX_HEAD_DIM

    @pl.when(j == 0)
    def _init():
        for h in range(FOX_HEADS):
            qh_ref[h] = q_ref[:, h * hd:(h + 1) * hd]
        m_ref[...] = jnp.full(m_ref.shape, NEG_BIG, F32)
        l_ref[...] = jnp.zeros(l_ref.shape, F32)
        acc_ref[...] = jnp.zeros(acc_ref.shape, F32)
        x = lft_ref[...]
        parts = _split3_f32(x)
        per_tile = tp // blk
        nblk = nt * per_tile
        rows = jnp.concatenate([p_[:, b * blk:(b + 1) * blk] for b in range(nblk) for p_ in parts], axis=0)
        y = _dot(rows.astype(BF16), mlow_ref[...])
        carry = jnp.zeros((FOX_HEADS, 1), F32)
        for b in reversed(range(nblk)):
            yb = y[24 * b:24 * b + 8] + y[24 * b + 8:24 * b + 16] + y[24 * b + 16:24 * b + 24]
            off = (b % per_tile) * blk
            suf_ref[b // per_tile, :, off:off + blk] = yb + carry
            carry = carry + yb[:, 0:1] + x[:, b * blk:b * blk + 1]

    def update(blocks):
        m_old = m_ref[...]
        m_new = m_old
        for s, _ in blocks:
            m_new = jnp.maximum(m_new, jnp.max(s, axis=-1, keepdims=True))
        alpha = jnp.exp(m_old - m_new)
        l = alpha * l_ref[...]
        acc = alpha * acc_ref[...]
        for s, pv in blocks:
            pm = jnp.exp(s - m_new)
            l = l + jnp.sum(pm, axis=-1, keepdims=True)
            acc = acc + pv(pm.astype(BF16))
        return m_new, l, acc

    bmm = lambda a, b, ca, cb: lax.dot_general(a, b, (((ca,), (cb,)), ((0,), (0,))),
                                               preferred_element_type=F32)
    qh = qh_ref[...]

    def cache_block():
        kt = ck_ref[...].astype(BF16)
        vt = cv_ref[...].astype(BF16)
        s = bmm(qh, kt, 2, 1).reshape(FOX_HEADS * tn, tp) + jnp.repeat(suf_ref[j], tn, axis=0)
        return s.reshape(FOX_HEADS, tn, tp), lambda pm: bmm(pm, vt, 2, 2)

    def new_block():
        per_head = lambda ref: jnp.stack([ref[:, h * hd:(h + 1) * hd] for h in range(FOX_HEADS)],
                                         axis=0).astype(BF16)
        kn = per_head(kn_ref)
        vn = per_head(vn_ref)
        r = lax.broadcasted_iota(jnp.int32, (FOX_HEADS, tn, tn), 1)
        c = lax.broadcasted_iota(jnp.int32, (FOX_HEADS, tn, tn), 2)
        s = jnp.where(c <= r, bmm(qh, kn, 2, 2) - cn_ref[...][:, None, :], -jnp.inf)
        return s, lambda pm: bmm(pm, vn, 2, 1)

    @pl.when(j < nt - 1)
    def _():
        m, l, acc = update([cache_block()])
        m_ref[...] = m
        l_ref[...] = l
        acc_ref[...] = acc

    @pl.when(j == nt - 1)
    def _():
        _, l, acc = update([cache_block(), new_block()])
        o = acc / l
        o_ref[...] = jnp.concatenate([o[h] for h in range(FOX_HEADS)], axis=1).astype(BF16)


def _fox_sample_call(batch, tn, past, tp, ix):
    nt = past // tp
    blk = 2 * LANES
    idx = np.arange(blk)
    mlow = jnp.asarray((idx[:, None] > idx[None, :]).astype(np.float32), dtype=BF16)
    per_b = lambda w: pl.BlockSpec((tn, w), lambda *g: (ix(*g)[0], 0))
    cache = pl.BlockSpec((None, FOX_HEADS, FOX_HEAD_DIM, tp), lambda *g: (ix(*g)[0], 0, 0, ix(*g)[1]))
    return dict(
        kernel=functools.partial(_fox_sample_kernel, tp=tp, nt=nt, tn=tn), nt=nt, consts=(mlow,),
        in_specs=[per_b(FOX_WIDTH), per_b(FOX_WIDTH), per_b(FOX_WIDTH),
                  pl.BlockSpec((None, FOX_HEADS, tn), lambda *g: (ix(*g)[0], 0, 0)),
                  pl.BlockSpec((None, FOX_HEADS, past), lambda *g: (ix(*g)[0], 0, 0)),
                  pl.BlockSpec((blk, blk), lambda *g: (0, 0)),
                  cache, cache],
        out_specs=[per_b(FOX_WIDTH)],
        out_shape=[jax.ShapeDtypeStruct((batch * tn, FOX_WIDTH), BF16)],
        scratch_shapes=[pltpu.VMEM((FOX_HEADS, tn, FOX_HEAD_DIM), BF16),
                        pltpu.VMEM((nt, FOX_HEADS, tp), F32),
                        pltpu.VMEM((FOX_HEADS, tn, 1), F32), pltpu.VMEM((FOX_HEADS, tn, 1), F32),
                        pltpu.VMEM((FOX_HEADS, tn, FOX_HEAD_DIM), F32)])


def _gla_kernel(q_ref, k_ref, v_ref, g_ref, r_ref, s0_ref, gn_ref, w_ref, lv_ref,
                o_ref, s_ref, st_ref, *, tb, levels, blocks, seqs, carried, t=None, nt=None):
    to_work = lambda s0: s0.reshape(GLA_KW, GLA_DV).T
    from_work = lambda st: st.T.reshape(GLA_HEADS, GLA_DK, GLA_DV)
    if carried:
        t = pl.program_id(1) if t is None else t
        nt = pl.num_programs(1) if nt is None else nt

        @pl.when(t == 0)
        def _():
            st_ref[...] = to_work(s0_ref[0])

    for sq in range(seqs):
        st = st_ref[...] if carried else to_work(s0_ref[sq])
        for blk in range(blocks):
            rows = lambda ref: ref.at[pl.ds((sq * blocks + blk) * tb, tb), :]
            st = _gla_block(rows(q_ref), rows(k_ref), rows(v_ref), rows(g_ref), rows(r_ref), gn_ref,
                            w_ref, lv_ref, rows(o_ref), st, tb=tb, levels=levels)
        if carried:
            st_ref[...] = st

            @pl.when(t == nt - 1)
            def _():
                s_ref[0] = from_work(st)
        else:
            s_ref[sq] = from_work(st)


def _gla_block(q_ref, k_ref, v_ref, g_ref, r_ref, gn_ref, w_ref, lv_ref, o_ref, st, *, tb, levels):
    half = LANES // 2
    r = r_ref[...]
    gate = gn_ref[...] * (r * _sigmoid(r))

    g2 = g_ref[...] * LOG2E
    gp = jnp.concatenate(_split2_f32(g2), axis=0).astype(BF16)
    rows = w_ref.shape[0] // 2
    cum = jnp.concatenate([_dot(w_ref[0:rows, :], gp), _dot(w_ref[rows:, :], gp)], axis=0)
    dec = jnp.exp2(cum)
    from_start = dec[0:tb]
    to_end = jnp.exp2(cum[tb - 1:tb] - cum[0:tb])

    q = q_ref[...] * (GLA_DK ** -0.5)
    k = k_ref[...]
    v = v_ref[...]
    row = lax.broadcasted_iota(jnp.int32, (tb, GLA_KW), 0)
    low = lax.broadcasted_iota(jnp.int32, (tb, LANES), 1) < half
    lv = lv_ref[...]

    def pair_scores(xq, xk):
        outs = []
        for p in range(GLA_HEADS // 2):
            a = xq[:, p * LANES:(p + 1) * LANES]
            zero = jnp.zeros_like(a)
            lhs = jnp.concatenate([jnp.where(low, a, zero), jnp.where(low, zero, a)], axis=0)
            outs.append(_dot_nt(lhs, xk[:, p * LANES:(p + 1) * LANES]))
        return outs

    here = lv == -1
    a = [jnp.where(here, r_, 0.0) for r_ in pair_scores(q.astype(BF16), k.astype(BF16))]
    for l in range(levels):
        upper = ((row >> l) & 1) == 1
        e = jnp.where(upper, jnp.exp2(g2), 1.0) if l == 0 else dec[l * tb:(l + 1) * tb]
        x = (jnp.where(upper, q, k) * e).astype(BF16)
        here = lv == l
        a = [jnp.where(here, r_, a_) for r_, a_ in zip(pair_scores(x, x), a)]
    o = jnp.concatenate(
        [_dot(a[h // 2][(h % 2) * tb:(h % 2 + 1) * tb].astype(BF16), v[:, h * GLA_DV:(h + 1) * GLA_DV])
         for h in range(GLA_HEADS)], axis=1)

    lane = lax.broadcasted_iota(jnp.int32, (tb, GLA_KW), 1)
    head_sel = [(lane >= h * GLA_DK) & (lane < (h + 1) * GLA_DK) for h in range(GLA_HEADS)]
    qt = (q * from_start).astype(BF16)
    kt = (k * to_end).astype(BF16)
    zero = jnp.zeros_like(qt)
    q4 = jnp.concatenate([jnp.where(sel, qt, zero) for sel in head_sel], axis=0)
    oi = _dot_nt(q4, st.astype(BF16))
    o = o + jnp.concatenate([oi[h * tb:(h + 1) * tb] for h in range(GLA_HEADS)], axis=1)
    upd = None
    for h in range(GLA_HEADS):
        u = _dot_tn(v[:, h * GLA_DV:(h + 1) * GLA_DV], jnp.where(head_sel[h], kt, zero))
        upd = u if upd is None else upd + u
    st = from_start[tb - 1:tb, :] * st + upd

    outs = []
    for h in range(GLA_HEADS):
        oh = o[:, h * GLA_DV:(h + 1) * GLA_DV]
        outs.append(oh * lax.rsqrt(jnp.mean(oh * oh, axis=-1, keepdims=True) + EPS))
    o_ref[...] = (jnp.concatenate(outs, axis=1) * gate).astype(BF16)
    return st


def _gla_call(batch, seq, ix, tb=128, rows_per_step=1024):
    tb = min(tb, seq)
    step_rows = min(rows_per_step, batch * seq)
    carried = seq > step_rows
    seqs = 1 if carried else step_rows // seq
    per_seq = (step_rows if carried else seq) // tb
    nt = seq // (tb * per_seq)
    n = batch * seq
    levels = tb.bit_length() - 1
    assert tb == 1 << levels
    ti = np.arange(tb)[:, None]
    si = np.arange(tb)[None, :]
    blocks = [si <= ti]
    for l in range(1, levels):
        m = 1 << l
        mid = ti - ti % (2 * m) + m - 1
        upper = ti % (2 * m) >= m
        blocks.append(np.where(upper, (si > mid) & (si <= ti), (si > ti) & (si <= mid)))
    w = np.concatenate(blocks, axis=0).astype(np.float32)
    w = jnp.asarray(np.concatenate([w, w], axis=1), dtype=BF16)
    x = ti ^ si
    lv = np.where(si < ti, np.floor(np.log2(np.maximum(x, 1))).astype(np.int32),
                  np.where(si == ti, -1, -2)).astype(np.int32)
    lv = jnp.asarray(np.concatenate([lv, lv], axis=0))
    row = lambda w_: pl.BlockSpec((step_rows, w_), lambda *g: (ix(*g)[0] * nt + ix(*g)[1], 0))
    state = pl.BlockSpec((seqs, GLA_HEADS, GLA_DK, GLA_DV), lambda *g: (ix(*g)[0], 0, 0, 0))
    const = lambda shape: pl.BlockSpec(shape, lambda *g: (0, 0))
    return dict(
        kernel=functools.partial(_gla_kernel, tb=tb, levels=levels, blocks=per_seq, seqs=seqs,
                                 carried=carried),
        groups=batch // seqs, nt=nt, consts=(w, lv),
        in_specs=[row(GLA_KW), row(GLA_KW), row(GLA_VW), row(GLA_KW), row(GLA_VW), state,
                  const((1, GLA_VW)), const(w.shape), const(lv.shape)],
        out_specs=[row(GLA_VW), state],
        out_shape=[jax.ShapeDtypeStruct((n, GLA_VW), BF16),
                   jax.ShapeDtypeStruct((batch, GLA_HEADS, GLA_DK, GLA_DV), F32)],
        scratch_shapes=[pltpu.VMEM((GLA_DV, GLA_KW), F32)])


def _mixers_under_cache_stream(gla_p_args, fox_args, gla_s_args, gla_batch, gla_seq, batch, tn, past,
                               gla_rows=1024, tp=2048):
    nt_g, nt_f = gla_seq // gla_rows, past // tp
    steps = batch * nt_f
    every = steps // (gla_batch * nt_g)
    g = _gla_call(gla_batch, gla_seq, lambda i: (i // every // nt_g, i // every % nt_g),
                  rows_per_step=gla_rows)
    f = _fox_sample_call(batch, tn, past, tp, lambda i: (i // nt_f, i % nt_f))
    h = _gla_call(batch, tn, lambda i: (i // nt_f, 0), rows_per_step=tn)
    assert every * g['groups'] * nt_g == steps and g['nt'] == nt_g and f['nt'] == nt_f and h['nt'] == 1
    calls = (g, f, h)

    def kernel(*refs):
        i = pl.program_id(0)
        refs = list(refs)
        take = lambda key: [[refs.pop(0) for _ in c[key]] for c in calls]
        (g_in, f_in, h_in), (g_out, f_out, h_out) = take('in_specs'), take('out_specs')
        g_scr, f_scr, h_scr = take('scratch_shapes')
        @pl.when(i % every == 0)
        def _():
            g['kernel'](*g_in, *g_out, *g_scr, t=i // every % nt_g, nt=nt_g)

        f['kernel'](*f_in, *f_out, *f_scr, j=i % nt_f)

        @pl.when(i % nt_f == nt_f - 1)
        def _():
            h['kernel'](*h_in, *h_out, *h_scr)

    cat = lambda key: [x for c in calls for x in c[key]]
    return pl.pallas_call(
        kernel, grid=(steps,), in_specs=cat('in_specs'), out_specs=cat('out_specs'),
        out_shape=cat('out_shape'), scratch_shapes=cat('scratch_shapes'),
        compiler_params=pltpu.CompilerParams(dimension_semantics=("arbitrary",),
                                             vmem_limit_bytes=VMEM_LIMIT),
        name="mixers",
    )(*gla_p_args, *g['consts'], *fox_args[:5], *f['consts'], *fox_args[5:], *gla_s_args, *h['consts'])


def _ffn_kernel(x_ref, fo_ref, go_ref, wo_ref, g2_ref, wg_ref, wu_ref, wd_ref, gf_ref,
                y_ref, a_ref, *, chunk, fox_time_minor):
    tm = x_ref.shape[0]
    parts = fo_ref.shape[0] if fox_time_minor else 2
    halves = [slice(i * (tm // parts), (i + 1) * (tm // parts)) for i in range(parts)]
    y1 = []
    for i, rows in enumerate(halves):
        fox = (_dot_tn(fo_ref[i], wo_ref[0:FOX_WIDTH, :]) if fox_time_minor
               else _dot(fo_ref[rows, :], wo_ref[0:FOX_WIDTH, :]))
        y1.append(x_ref[rows, :] + fox + _dot(go_ref[rows, :], wo_ref[FOX_WIDTH:, :]))
    h2 = jnp.concatenate([_rms(y, g2_ref[...]).astype(BF16) for y in y1], axis=0)
    for c in range(D_FF // chunk):
        cs = slice(c * chunk, (c + 1) * chunk)
        u = _dot(h2, wg_ref[:, cs])
        w = _dot(h2, wu_ref[:, cs])
        a_ref[:, cs] = (u * _sigmoid(u) * w).astype(BF16)
    y2 = [y + _dot(a_ref[rows, :], wd_ref[...]) for y, rows in zip(y1, halves)]
    for y, rows in zip(y2, halves):
        y_ref[rows, :] = _rms(y, gf_ref[...])


def _ffn(x2d, fo, go, prm, tm=1024, chunk=256):
    n = x2d.shape[0]
    tm = min(tm, n)
    row = lambda w: pl.BlockSpec((tm, w), lambda i: (i, 0))
    const = lambda shape: pl.BlockSpec(shape, lambda i: (0, 0), pipeline_mode=pl.Buffered(1))
    fox_time_minor = fo.ndim == 4
    if fox_time_minor:
        per_tile = tm // fo.shape[3]
        tiles_per_seq = fo.shape[1] // per_tile
        assert per_tile * fo.shape[3] == tm and tiles_per_seq * per_tile == fo.shape[1]
        fo_spec = pl.BlockSpec((None, per_tile, FOX_WIDTH, fo.shape[3]),
                               lambda i: (i // tiles_per_seq, i % tiles_per_seq, 0, 0))
    else:
        fo_spec = row(FOX_WIDTH)
    return pl.pallas_call(
        functools.partial(_ffn_kernel, chunk=chunk, fox_time_minor=fox_time_minor),
        grid=(n // tm,),
        in_specs=[row(D_MODEL), fo_spec, row(GLA_VW), const((D_MODEL, D_MODEL)),
                  const((1, D_MODEL)), const((D_MODEL, D_FF)), const((D_MODEL, D_FF)),
                  const((D_FF, D_MODEL)), const((1, D_MODEL))],
        out_specs=row(D_MODEL),
        out_shape=jax.ShapeDtypeStruct((n, D_MODEL), F32),
        scratch_shapes=[pltpu.VMEM((tm, D_FF), BF16)],
        compiler_params=pltpu.CompilerParams(dimension_semantics=("arbitrary",),
                                             vmem_limit_bytes=VMEM_LIMIT),
        name="ffn",
    )(x2d, fo, go, prm['wo'], prm['g2'], prm['wg'], prm['wu'], prm['wd'], prm['gf'])


def _layer_params(layer, norm1_g, w_in, w_gate2, b_gate2, b_forget, gla_norm_g, w_out,
                  norm2_g, w_gate, w_up, w_down, final_norm_g):
    wt = jnp.transpose(w_in[layer])
    o_fl = 3 * FOX_WIDTH
    o_gq = o_fl + FOX_HEADS
    o_gg = o_gq + 2 * GLA_KW + GLA_VW
    o_gr = o_gg + GLA_GATE_RANK
    tail = 2 * GLA_GATE_RANK
    pad = jnp.zeros((tail - FOX_HEADS - GLA_GATE_RANK, D_MODEL), F32)
    w = jnp.concatenate([wt[:o_fl], wt[o_gq:o_gg], wt[o_gr:], wt[o_fl:o_gq], wt[o_gg:o_gr], pad], axis=0)
    wg2 = jnp.zeros((tail, GLA_KW), F32).at[FOX_HEADS:FOX_HEADS + GLA_GATE_RANK].set(w_gate2[layer])
    return dict(
        g1=norm1_g[layer].reshape(1, D_MODEL),
        w=w.astype(BF16),
        wg2=wg2.astype(BF16),
        bg2=b_gate2[layer].reshape(1, GLA_KW),
        bfc=b_forget[layer].reshape(FOX_HEADS, 1),
        gn=gla_norm_g[layer].reshape(1, GLA_VW),
        wo=w_out[layer].astype(BF16),
        g2=norm2_g[layer].reshape(1, D_MODEL),
        wg=w_gate[layer].astype(BF16),
        wu=w_up[layer].astype(BF16),
        wd=w_down[layer].astype(BF16),
        gf=final_norm_g.reshape(1, D_MODEL),
    )


def kernel(x_prompt, x_sample, cache_fox_k, cache_fox_v, cache_fox_logf, state_gla, norm1_g, w_in,
           w_gate2, b_gate2, b_forget, gla_norm_g, w_out, norm2_g, w_gate, w_up, w_down, final_norm_g):
    depth = w_in.shape[0]
    assert depth == 1, "the final rmsnorm is fused into the layer's ffn kernel"
    bp, tp_, _ = x_prompt.shape
    bs, ts, _ = x_sample.shape
    past = cache_fox_k.shape[2]
    layer = 0
    prm = _layer_params(layer, norm1_g, w_in, w_gate2, b_gate2, b_forget, gla_norm_g, w_out,
                        norm2_g, w_gate, w_up, w_down, final_norm_g)
    by_time = lambda a, b, t: a.reshape(FOX_HEADS, b, t).transpose(1, 2, 0)[None]

    xp = x_prompt.reshape(bp * tp_, D_MODEL)
    qt, kt_p, vt_p, lf_p, ct, gq, gk, gv, glog, gr = _proj(xp, bp, tp_, prm, True)
    fox_p = _fox_prompt(qt, kt_p, vt_p, ct, bp, tp_)
    s0 = jnp.zeros((bp, GLA_HEADS, GLA_DK, GLA_DV), F32)
    gla_p_args = (gq, gk, gv, glog, gr, s0, prm['gn'])

    xs = x_sample.reshape(bs * ts, D_MODEL)
    q, k_s, v_s, lf_s, ct, gq, gk, gv, glog, gr = _proj(xs, bs, ts, prm, False)
    cn = ct.reshape(FOX_HEADS, bs, ts).transpose(1, 0, 2)
    lft = cache_fox_logf[layer].astype(F32).transpose(0, 2, 1)
    ck = cache_fox_k[layer].transpose(0, 2, 3, 1)
    cv = cache_fox_v[layer].transpose(0, 2, 3, 1)
    gla_s_args = (gq, gk, gv, glog, gr, state_gla[layer].astype(F32), prm['gn'])
    gla_p, s_p, fox_s, gla_s, s_s = _mixers_under_cache_stream(
        gla_p_args, (q, k_s, v_s, cn, lft, ck, cv), gla_s_args, bp, tp_, bs, ts, past)
    y_p = _ffn(xp, fox_p, gla_p, prm)
    y_s = _ffn(xs, fox_s, gla_s, prm)

    heads = lambda a, b, t: a.reshape(1, b, t, FOX_HEADS, FOX_HEAD_DIM)
    return (y_p.reshape(bp, tp_, D_MODEL), y_s.reshape(bs, ts, D_MODEL),
            kt_p.transpose(0, 3, 1, 2)[None], vt_p.transpose(0, 3, 1, 2)[None],
            by_time(lf_p, bp, tp_), s_p[None],
            heads(k_s, bs, ts), heads(v_s, bs, ts), by_time(lf_s, bs, ts), s_s[None])
```

```python
import functools

import numpy as np
import jax
import jax.numpy as jnp
from jax import lax
from jax.experimental import pallas as pl
from jax.experimental.pallas import tpu as pltpu

D_MODEL = 1024
FOX_HEADS = 8
FOX_HEAD_DIM = 64
FOX_WIDTH = FOX_HEADS * FOX_HEAD_DIM
GLA_HEADS = 4
GLA_DK = 64
GLA_DV = 128
GLA_KW = GLA_HEADS * GLA_DK
GLA_VW = GLA_HEADS * GLA_DV
GLA_GATE_RANK = 16
GLA_GATE_TEMP = 16.0
D_FF = 2816
EPS = 1e-6

LANES = 128
LOG2E = 1.4426950408889634
VMEM_LIMIT = 56 * 1024 * 1024

F32 = jnp.float32
BF16 = jnp.bfloat16
NEG_BIG = -1e30


def _log_sigmoid(x):
    return jnp.minimum(x, 0.0) - jnp.log1p(jnp.exp(-jnp.abs(x)))


def _sigmoid(x):
    return 1.0 / (1.0 + jnp.exp(-x))


def _split3_f32(x):
    hi = x.astype(BF16).astype(F32)
    r = x - hi
    mid = r.astype(BF16).astype(F32)
    lo = (r - mid).astype(BF16).astype(F32)
    return hi, mid, lo


def _split2_f32(x):
    hi = x.astype(BF16).astype(F32)
    return hi, (x - hi).astype(BF16).astype(F32)


def _dot(a, b):
    return jnp.dot(a, b, preferred_element_type=F32)


def _dot_nt(a, b):
    return lax.dot_general(a, b, (((1,), (1,)), ((), ())), preferred_element_type=F32)


def _dot_tn(a, b):
    return lax.dot_general(a, b, (((0,), (0,)), ((), ())), preferred_element_type=F32)


def _rms(x, g):
    return x * lax.rsqrt(jnp.mean(x * x, axis=-1, keepdims=True) + EPS) * g


def _proj_kernel(x_ref, g1_ref, w_ref, wg2_ref, bg2_ref, bfc_ref, tri_ref,
                 *rest, tiles_per_seq, time_minor):
    o_kv, o_b = FOX_WIDTH, 3 * FOX_WIDTH
    o_tail = o_b + 2 * GLA_KW + 2 * GLA_VW
    carry_ref = rest[-1]

    @pl.when(pl.program_id(0) % tiles_per_seq == 0)
    def _():
        carry_ref[...] = jnp.zeros_like(carry_ref)

    h = _rms(x_ref[...], g1_ref[...]).astype(BF16)
    tm = h.shape[0]
    scale = FOX_HEAD_DIM ** -0.5

    zc = _dot_nt(h, w_ref[o_tail:, :])
    fl_t = _dot_nt(w_ref[o_tail:, :], h)[0:FOX_HEADS]

    if time_minor:
        (q_ref, k_ref, v_ref, lft_ref, ct_ref, gq_ref, gk_ref, gv_ref, glog_ref, gr_ref,
         carry_ref) = rest
        qt = (_dot_nt(w_ref[0:o_kv, :], h) * (scale * LOG2E)).astype(BF16)
        tq = q_ref.shape[2]
        for i in range(q_ref.shape[0]):
            q_ref[i] = qt[:, i * tq:(i + 1) * tq]
    else:
        (q_ref, k_ref, v_ref, lft_ref, ct_ref, gq_ref, gk_ref, gv_ref, glog_ref, gr_ref,
         carry_ref) = rest
        q_ref[...] = (_dot_nt(h, w_ref[0:o_kv, :]) * scale).astype(BF16)

    logf_t = _log_sigmoid(fl_t + bfc_ref[...])
    lft_ref[...] = logf_t
    parts = jnp.concatenate(_split3_f32(logf_t), axis=0).astype(BF16)
    gg = zc.astype(BF16)

    if time_minor:
        kvt = _dot_nt(w_ref[o_kv:o_b, :], h)
        k_ref[...] = kvt[:FOX_WIDTH].reshape(FOX_HEADS, FOX_HEAD_DIM, tm)
        v_ref[...] = kvt[FOX_WIDTH:].reshape(FOX_HEADS, FOX_HEAD_DIM, tm)
    else:
        kv = _dot_nt(h, w_ref[o_kv:o_b, :])
        k_ref[...] = kv[:, :FOX_WIDTH]
        v_ref[...] = kv[:, FOX_WIDTH:]

    gpre = _dot(gg, wg2_ref[...]) + bg2_ref[...]
    cs = _dot(parts, tri_ref[...])

    zb = _dot_nt(h, w_ref[o_b:o_tail, :])
    gq_ref[...] = zb[:, :GLA_KW]
    gk_ref[...] = zb[:, GLA_KW:2 * GLA_KW]
    gv_ref[...] = zb[:, 2 * GLA_KW:2 * GLA_KW + GLA_VW].astype(BF16)
    gr_ref[...] = zb[:, 2 * GLA_KW + GLA_VW:]

    glog_ref[...] = _log_sigmoid(gpre) * (1.0 / GLA_GATE_TEMP)

    ct = cs[0:8] + cs[8:16] + cs[16:24] + carry_ref[:, 0:1]
    ct_ref[...] = ct
    carry_ref[...] = jnp.broadcast_to(ct[:, tm - 1:], carry_ref.shape)


def _proj(x2d, batch, seq_len, prm, time_minor, tm=1024, tq=512):
    n = x2d.shape[0]
    tm = min(tm, n)
    tiles_per_seq = max(seq_len // tm, 1)
    per_tile = tm // tq
    idx = np.arange(tm)
    tri = ((idx[:, None] <= idx[None, :]) & (idx[:, None] // seq_len == idx[None, :] // seq_len))
    tri = jnp.asarray(tri.astype(np.float32), dtype=BF16)
    const = lambda a: pl.BlockSpec(a.shape, lambda i: (0, 0), pipeline_mode=pl.Buffered(1))
    row = lambda w: pl.BlockSpec((tm, w), lambda i: (i, 0))
    col = pl.BlockSpec((FOX_HEADS, tm), lambda i: (0, i))
    sds = jax.ShapeDtypeStruct
    if time_minor:
        kv_shape = sds((batch, FOX_HEADS, FOX_HEAD_DIM, seq_len), F32)
        kv_spec = pl.BlockSpec((None, FOX_HEADS, FOX_HEAD_DIM, tm),
                               lambda i: (i // tiles_per_seq, 0, 0, i % tiles_per_seq))
        fox = [(sds((batch, seq_len // tq, FOX_WIDTH, tq), BF16),
                pl.BlockSpec((None, per_tile, FOX_WIDTH, tq),
                             lambda i: (i // tiles_per_seq, i % tiles_per_seq, 0, 0))),
               (kv_shape, kv_spec), (kv_shape, kv_spec)]
    else:
        fox = [(sds((n, FOX_WIDTH), BF16), row(FOX_WIDTH)),
               (sds((n, FOX_WIDTH), F32), row(FOX_WIDTH)),
               (sds((n, FOX_WIDTH), F32), row(FOX_WIDTH))]
    outs = fox + [
        (sds((FOX_HEADS, n), F32), col),
        (sds((FOX_HEADS, n), F32), col),
        (sds((n, GLA_KW), F32), row(GLA_KW)),
        (sds((n, GLA_KW), F32), row(GLA_KW)),
        (sds((n, GLA_VW), BF16), row(GLA_VW)),
        (sds((n, GLA_KW), F32), row(GLA_KW)),
        (sds((n, GLA_VW), F32), row(GLA_VW)),
    ]
    names = ('g1', 'w', 'wg2', 'bg2', 'bfc')
    return pl.pallas_call(
        functools.partial(_proj_kernel, tiles_per_seq=tiles_per_seq, time_minor=time_minor),
        grid=(n // tm,),
        in_specs=[row(D_MODEL)] + [const(prm[k]) for k in names] + [const(tri)],
        out_specs=tuple(s for _, s in outs), out_shape=tuple(s for s, _ in outs),
        scratch_shapes=[pltpu.VMEM((FOX_HEADS, LANES), F32)],
        compiler_params=pltpu.CompilerParams(dimension_semantics=("arbitrary",),
                                             vmem_limit_bytes=VMEM_LIMIT),
        name="proj",
    )(x2d, *[prm[k] for k in names], tri)


def _fox_prompt_kernel(q_ref, k_ref, v_ref, c_ref, o_ref, kb_ref, vb_ref, s_ref, p_ref, acc_ref,
                       *, tk):
    g = pl.program_id(1)
    nq = q_ref.shape[0]
    nk = vb_ref.shape[1]
    pairs = q_ref.shape[1] // LANES
    hd = FOX_HEAD_DIM
    spare = (hd, 0)
    vrows = vb_ref.shape[2]

    parts = jnp.concatenate(_split3_f32(c_ref[...] * (-LOG2E)), axis=0).astype(BF16)
    r = lax.broadcasted_iota(jnp.int32, (3 * FOX_HEADS, LANES), 0)
    ln = lax.broadcasted_iota(jnp.int32, (3 * FOX_HEADS, LANES), 1)
    klane = lax.broadcasted_iota(jnp.int32, (k_ref.shape[2], LANES), 1)
    tail = jnp.where(lax.broadcasted_iota(jnp.int32, (vrows - hd, tk), 0) == 0, 1.0, 0.0)
    for pp in range(pairs):
        k = k_ref[2 * pp:2 * pp + 2].reshape(LANES, k_ref.shape[2]).T.astype(BF16)
        for hh in range(2):
            head = 2 * (pairs * g + pp) + hh
            place = (r % FOX_HEADS == head) & (ln == spare[hh] + r // FOX_HEADS)
            extra = _dot_tn(parts, jnp.where(place, 1.0, 0.0).astype(BF16))
            own = (klane < hd) if hh == 0 else (klane >= hd)
            kb_ref[2 * pp + hh] = jnp.where(own, k, extra.astype(BF16))
            for jj in range(nk):
                vt = v_ref[2 * pp + hh, :, jj * tk:(jj + 1) * tk]
                vb_ref[2 * pp + hh, jj] = jnp.concatenate([vt, tail], axis=0).astype(BF16)

    qrow = lax.broadcasted_iota(jnp.int32, (LANES, tk), 0)
    key = lax.broadcasted_iota(jnp.int32, (tk, tk), 0)
    qry = lax.broadcasted_iota(jnp.int32, (tk, tk), 1)
    causal = key <= qry

    chains = [(slot, part) for slot in range(2 * pairs) for part in range(2)]
    everyone = list(range(len(chains)))
    second = [n for n in everyone if chains[n][1] == 1]

    def q_operands(qi):
        qs = []
        for slot, part in chains:
            pp, hh = divmod(slot, 2)
            q = q_ref[qi, pp * LANES:(pp + 1) * LANES, part * tk:(part + 1) * tk]
            own = (qrow < hd) if hh == 0 else (qrow >= hd)
            ones = (qrow >= spare[hh]) & (qrow < spare[hh] + 3)
            qs.append(jnp.where(own, q, jnp.where(ones, 1.0, 0.0).astype(BF16)))
        return qs

    steps = []
    for qi in range(nq):
        steps += [(qi, j, everyone, None) for j in range(2 * qi)]
        steps += [(qi, 2 * qi, everyone, 0), (qi, 2 * qi + 1, second, 1)]

    qs_of = {}

    def scores(step, buf):
        qi, j, live, _ = step
        if qi not in qs_of:
            qs_of.clear()
            qs_of[qi] = q_operands(qi)
        for n in live:
            s_ref[buf, n] = _dot(kb_ref[chains[n][0], j * tk:(j + 1) * tk, :], qs_of[qi][n])

    def values(step, buf):
        _, j, live, _ = step
        return {n: _dot(vb_ref[chains[n][0], j], p_ref[buf, n]) for n in live}

    def finish(qi):
        for part in range(2):
            heads = [acc_ref[qi % 2, 2 * slot + part] for slot in range(2 * pairs)]
            o_ref[qi, :, part * tk:(part + 1) * tk] = jnp.concatenate(
                [a[:hd] / a[hd:hd + 1] for a in heads], axis=0).astype(BF16)

    def accumulate(step, alphas, pv):
        qi, j, live, _ = step
        for n in live:
            acc_ref[qi % 2, n] = pv[n] if j == 0 else alphas[n] * acc_ref[qi % 2, n] + pv[n]

    ms = {}
    scores(steps[0], 0)
    prev, prev_alphas = None, None
    for t, step in enumerate(steps):
        qi, j, live, masked_part = step
        if t + 1 < len(steps):
            scores(steps[t + 1], (t + 1) % 2)
        pv = values(prev, (t - 1) % 2) if prev is not None else None
        alphas = {}
        for n in live:
            s = s_ref[t % 2, n]
            if chains[n][1] == masked_part:
                s = jnp.where(causal, s, -jnp.inf)
            m_new = jnp.max(s, axis=0, keepdims=True)
            if j > 0:
                m_new = jnp.maximum(ms[n], m_new)
                alphas[n] = jnp.exp2(ms[n] - m_new)
            p_ref[t % 2, n] = jnp.exp2(s - m_new).astype(BF16)
            ms[n] = m_new
        if prev is not None:
            accumulate(prev, prev_alphas, pv)
            if prev[0] != qi:
                finish(prev[0])
        prev, prev_alphas = step, alphas
    accumulate(prev, prev_alphas, values(prev, (len(steps) - 1) % 2))
    finish(prev[0])


def _fox_prompt(qt, kt, vt, ct, batch, seq, tk=256, pairs=2):
    tq = 2 * tk
    nq = seq // tq
    nk = seq // tk
    heads = 2 * pairs
    chains = 2 * heads
    vrows = FOX_HEAD_DIM + 16
    assert qt.shape == (batch, nq, FOX_WIDTH, tq) and FOX_HEADS % heads == 0
    q_spec = pl.BlockSpec((None, nq, pairs * LANES, tq), lambda b, p: (b, 0, p, 0))
    kv_spec = pl.BlockSpec((None, heads, FOX_HEAD_DIM, seq), lambda b, p: (b, p, 0, 0))
    return pl.pallas_call(
        functools.partial(_fox_prompt_kernel, tk=tk),
        grid=(batch, FOX_HEADS // heads),
        in_specs=[q_spec, kv_spec, kv_spec,
                  pl.BlockSpec((FOX_HEADS, seq), lambda b, p: (0, b))],
        out_specs=q_spec,
        out_shape=jax.ShapeDtypeStruct((batch, nq, FOX_WIDTH, tq), BF16),
        scratch_shapes=[pltpu.VMEM((heads, seq, LANES), BF16), pltpu.VMEM((heads, nk, vrows, tk), BF16),
                        pltpu.VMEM((2, chains, tk, tk), F32), pltpu.VMEM((2, chains, tk, tk), BF16),
                        pltpu.VMEM((2, chains, vrows, tk), F32)],
        compiler_params=pltpu.CompilerParams(dimension_semantics=("arbitrary", "arbitrary"),
                                             vmem_limit_bytes=VMEM_LIMIT),
        name="fox_prompt",
    )(qt, kt, vt, ct)


def _fox_sample_kernel(q_ref, kn_ref, vn_ref, cn_ref, lft_ref, mlow_ref, ck_ref, cv_ref,
                       o_ref, qh_ref, suf_ref, m_ref, l_ref, acc_ref, *, tp, nt, tn, j=None):
    j = pl.program_id(1) if j is None else j
    blk = 2 * LANES
    hd = FOX_HEAD_DIM

    @pl.when(j == 0)
    def _init():
        for h in range(FOX_HEADS):
            qh_ref[h] = q_ref[:, h * hd:(h + 1) * hd]
        m_ref[...] = jnp.full(m_ref.shape, NEG_BIG, F32)
        l_ref[...] = jnp.zeros(l_ref.shape, F32)
        acc_ref[...] = jnp.zeros(acc_ref.shape, F32)
        x = lft_ref[...]
        parts = _split3_f32(x)
        per_tile = tp // blk
        nblk = nt * per_tile
        rows = jnp.concatenate([p_[:, b * blk:(b + 1) * blk] for b in range(nblk) for p_ in parts], axis=0)
        y = _dot(rows.astype(BF16), mlow_ref[...])
        carry = jnp.zeros((FOX_HEADS, 1), F32)
        for b in reversed(range(nblk)):
            yb = y[24 * b:24 * b + 8] + y[24 * b + 8:24 * b + 16] + y[24 * b + 16:24 * b + 24]
            off = (b % per_tile) * blk
            suf_ref[b // per_tile, :, off:off + blk] = yb + carry
            carry = carry + yb[:, 0:1] + x[:, b * blk:b * blk + 1]

    def update(blocks):
        m_old = m_ref[...]
        m_new = m_old
        for s, _ in blocks:
            m_new = jnp.maximum(m_new, jnp.max(s, axis=-1, keepdims=True))
        alpha = jnp.exp(m_old - m_new)
        l = alpha * l_ref[...]
        acc = alpha * acc_ref[...]
        for s, pv in blocks:
            pm = jnp.exp(s - m_new)
            l = l + jnp.sum(pm, axis=-1, keepdims=True)
            acc = acc + pv(pm.astype(BF16))
        return m_new, l, acc

    bmm = lambda a, b, ca, cb: lax.dot_general(a, b, (((ca,), (cb,)), ((0,), (0,))),
                                               preferred_element_type=F32)
    qh = qh_ref[...]

    def cache_block():
        kt = ck_ref[...].astype(BF16)
        vt = cv_ref[...].astype(BF16)
        s = bmm(qh, kt, 2, 1).reshape(FOX_HEADS * tn, tp) + jnp.repeat(suf_ref[j], tn, axis=0)
        return s.reshape(FOX_HEADS, tn, tp), lambda pm: bmm(pm, vt, 2, 2)

    def new_block():
        per_head = lambda ref: jnp.stack([ref[:, h * hd:(h + 1) * hd] for h in range(FOX_HEADS)],
                                         axis=0).astype(BF16)
        kn = per_head(kn_ref)
        vn = per_head(vn_ref)
        r = lax.broadcasted_iota(jnp.int32, (FOX_HEADS, tn, tn), 1)
        c = lax.broadcasted_iota(jnp.int32, (FOX_HEADS, tn, tn), 2)
        s = jnp.where(c <= r, bmm(qh, kn, 2, 2) - cn_ref[...][:, None, :], -jnp.inf)
        return s, lambda pm: bmm(pm, vn, 2, 1)

    @pl.when(j < nt - 1)
    def _():
        m, l, acc = update([cache_block()])
        m_ref[...] = m
        l_ref[...] = l
        acc_ref[...] = acc

    @pl.when(j == nt - 1)
    def _():
        _, l, acc = update([cache_block(), new_block()])
        o = acc / l
        o_ref[...] = jnp.concatenate([o[h] for h in range(FOX_HEADS)], axis=1).astype(BF16)


def _fox_sample_call(batch, tn, past, tp, ix):
    nt = past // tp
    blk = 2 * LANES
    idx = np.arange(blk)
    mlow = jnp.asarray((idx[:, None] > idx[None, :]).astype(np.float32), dtype=BF16)
    per_b = lambda w: pl.BlockSpec((tn, w), lambda *g: (ix(*g)[0], 0))
    cache = pl.BlockSpec((None, FOX_HEADS, FOX_HEAD_DIM, tp), lambda *g: (ix(*g)[0], 0, 0, ix(*g)[1]))
    return dict(
        kernel=functools.partial(_fox_sample_kernel, tp=tp, nt=nt, tn=tn), nt=nt, consts=(mlow,),
        in_specs=[per_b(FOX_WIDTH), per_b(FOX_WIDTH), per_b(FOX_WIDTH),
                  pl.BlockSpec((None, FOX_HEADS, tn), lambda *g: (ix(*g)[0], 0, 0)),
                  pl.BlockSpec((None, FOX_HEADS, past), lambda *g: (ix(*g)[0], 0, 0)),
                  pl.BlockSpec((blk, blk), lambda *g: (0, 0)),
                  cache, cache],
        out_specs=[per_b(FOX_WIDTH)],
        out_shape=[jax.ShapeDtypeStruct((batch * tn, FOX_WIDTH), BF16)],
        scratch_shapes=[pltpu.VMEM((FOX_HEADS, tn, FOX_HEAD_DIM), BF16),
                        pltpu.VMEM((nt, FOX_HEADS, tp), F32),
                        pltpu.VMEM((FOX_HEADS, tn, 1), F32), pltpu.VMEM((FOX_HEADS, tn, 1), F32),
                        pltpu.VMEM((FOX_HEADS, tn, FOX_HEAD_DIM), F32)])


def _gla_kernel(q_ref, k_ref, v_ref, g_ref, r_ref, s0_ref, gn_ref, w_ref, lv_ref,
                o_ref, s_ref, st_ref, *, tb, levels, blocks, seqs, carried, t=None, nt=None):
    to_work = lambda s0: s0.reshape(GLA_KW, GLA_DV).T
    from_work = lambda st: st.T.reshape(GLA_HEADS, GLA_DK, GLA_DV)
    if carried:
        t = pl.program_id(1) if t is None else t
        nt = pl.num_programs(1) if nt is None else nt

        @pl.when(t == 0)
        def _():
            st_ref[...] = to_work(s0_ref[0])

    for sq in range(seqs):
        st = st_ref[...] if carried else to_work(s0_ref[sq])
        for blk in range(blocks):
            rows = lambda ref: ref.at[pl.ds((sq * blocks + blk) * tb, tb), :]
            st = _gla_block(rows(q_ref), rows(k_ref), rows(v_ref), rows(g_ref), rows(r_ref), gn_ref,
                            w_ref, lv_ref, rows(o_ref), st, tb=tb, levels=levels)
        if carried:
            st_ref[...] = st

            @pl.when(t == nt - 1)
            def _():
                s_ref[0] = from_work(st)
        else:
            s_ref[sq] = from_work(st)


def _gla_block(q_ref, k_ref, v_ref, g_ref, r_ref, gn_ref, w_ref, lv_ref, o_ref, st, *, tb, levels):
    half = LANES // 2
    r = r_ref[...]
    gate = gn_ref[...] * (r * _sigmoid(r))

    g2 = g_ref[...] * LOG2E
    gp = jnp.concatenate(_split2_f32(g2), axis=0).astype(BF16)
    rows = w_ref.shape[0] // 2
    cum = jnp.concatenate([_dot(w_ref[0:rows, :], gp), _dot(w_ref[rows:, :], gp)], axis=0)
    dec = jnp.exp2(cum)
    from_start = dec[0:tb]
    to_end = jnp.exp2(cum[tb - 1:tb] - cum[0:tb])

    q = q_ref[...] * (GLA_DK ** -0.5)
    k = k_ref[...]
    v = v_ref[...]
    row = lax.broadcasted_iota(jnp.int32, (tb, GLA_KW), 0)
    low = lax.broadcasted_iota(jnp.int32, (tb, LANES), 1) < half
    lv = lv_ref[...]

    def pair_scores(xq, xk):
        outs = []
        for p in range(GLA_HEADS // 2):
            a = xq[:, p * LANES:(p + 1) * LANES]
            zero = jnp.zeros_like(a)
            lhs = jnp.concatenate([jnp.where(low, a, zero), jnp.where(low, zero, a)], axis=0)
            outs.append(_dot_nt(lhs, xk[:, p * LANES:(p + 1) * LANES]))
        return outs

    here = lv == -1
    a = [jnp.where(here, r_, 0.0) for r_ in pair_scores(q.astype(BF16), k.astype(BF16))]
    for l in range(levels):
        upper = ((row >> l) & 1) == 1
        e = jnp.where(upper, jnp.exp2(g2), 1.0) if l == 0 else dec[l * tb:(l + 1) * tb]
        x = (jnp.where(upper, q, k) * e).astype(BF16)
        here = lv == l
        a = [jnp.where(here, r_, a_) for r_, a_ in zip(pair_scores(x, x), a)]
    o = jnp.concatenate(
        [_dot(a[h // 2][(h % 2) * tb:(h % 2 + 1) * tb].astype(BF16), v[:, h * GLA_DV:(h + 1) * GLA_DV])
         for h in range(GLA_HEADS)], axis=1)

    lane = lax.broadcasted_iota(jnp.int32, (tb, GLA_KW), 1)
    head_sel = [(lane >= h * GLA_DK) & (lane < (h + 1) * GLA_DK) for h in range(GLA_HEADS)]
    qt = (q * from_start).astype(BF16)
    kt = (k * to_end).astype(BF16)
    zero = jnp.zeros_like(qt)
    q4 = jnp.concatenate([jnp.where(sel, qt, zero) for sel in head_sel], axis=0)
    oi = _dot_nt(q4, st.astype(BF16))
    o = o + jnp.concatenate([oi[h * tb:(h + 1) * tb] for h in range(GLA_HEADS)], axis=1)
    upd = None
    for h in range(GLA_HEADS):
        u = _dot_tn(v[:, h * GLA_DV:(h + 1) * GLA_DV], jnp.where(head_sel[h], kt, zero))
        upd = u if upd is None else upd + u
    st = from_start[tb - 1:tb, :] * st + upd

    outs = []
    for h in range(GLA_HEADS):
        oh = o[:, h * GLA_DV:(h + 1) * GLA_DV]
        outs.append(oh * lax.rsqrt(jnp.mean(oh * oh, axis=-1, keepdims=True) + EPS))
    o_ref[...] = (jnp.concatenate(outs, axis=1) * gate).astype(BF16)
    return st


def _gla_call(batch, seq, ix, tb=128, rows_per_step=1024):
    tb = min(tb, seq)
    step_rows = min(rows_per_step, batch * seq)
    carried = seq > step_rows
    seqs = 1 if carried else step_rows // seq
    per_seq = (step_rows if carried else seq) // tb
    nt = seq // (tb * per_seq)
    n = batch * seq
    levels = tb.bit_length() - 1
    assert tb == 1 << levels
    ti = np.arange(tb)[:, None]
    si = np.arange(tb)[None, :]
    blocks = [si <= ti]
    for l in range(1, levels):
        m = 1 << l
        mid = ti - ti % (2 * m) + m - 1
        upper = ti % (2 * m) >= m
        blocks.append(np.where(upper, (si > mid) & (si <= ti), (si > ti) & (si <= mid)))
    w = np.concatenate(blocks, axis=0).astype(np.float32)
    w = jnp.asarray(np.concatenate([w, w], axis=1), dtype=BF16)
    x = ti ^ si
    lv = np.where(si < ti, np.floor(np.log2(np.maximum(x, 1))).astype(np.int32),
                  np.where(si == ti, -1, -2)).astype(np.int32)
    lv = jnp.asarray(np.concatenate([lv, lv], axis=0))
    row = lambda w_: pl.BlockSpec((step_rows, w_), lambda *g: (ix(*g)[0] * nt + ix(*g)[1], 0))
    state = pl.BlockSpec((seqs, GLA_HEADS, GLA_DK, GLA_DV), lambda *g: (ix(*g)[0], 0, 0, 0))
    const = lambda shape: pl.BlockSpec(shape, lambda *g: (0, 0))
    return dict(
        kernel=functools.partial(_gla_kernel, tb=tb, levels=levels, blocks=per_seq, seqs=seqs,
                                 carried=carried),
        groups=batch // seqs, nt=nt, consts=(w, lv),
        in_specs=[row(GLA_KW), row(GLA_KW), row(GLA_VW), row(GLA_KW), row(GLA_VW), state,
                  const((1, GLA_VW)), const(w.shape), const(lv.shape)],
        out_specs=[row(GLA_VW), state],
        out_shape=[jax.ShapeDtypeStruct((n, GLA_VW), BF16),
                   jax.ShapeDtypeStruct((batch, GLA_HEADS, GLA_DK, GLA_DV), F32)],
        scratch_shapes=[pltpu.VMEM((GLA_DV, GLA_KW), F32)])


def _mixers_under_cache_stream(gla_p_args, fox_args, gla_s_args, gla_batch, gla_seq, batch, tn, past,
                               gla_rows=256, tp=2048):
    nt_g, nt_f = gla_seq // gla_rows, past // tp
    g = _gla_call(gla_batch, gla_seq, lambda i: (i // nt_g, i % nt_g), rows_per_step=gla_rows)
    f = _fox_sample_call(batch, tn, past, tp, lambda i: (i // nt_f, i % nt_f))
    h = _gla_call(batch, tn, lambda i: (i // nt_f, 0), rows_per_step=tn)
    steps = g['groups'] * nt_g
    assert steps == batch * nt_f and g['nt'] == nt_g and f['nt'] == nt_f and h['nt'] == 1
    calls = (g, f, h)

    def kernel(*refs):
        i = pl.program_id(0)
        refs = list(refs)
        take = lambda key: [[refs.pop(0) for _ in c[key]] for c in calls]
        (g_in, f_in, h_in), (g_out, f_out, h_out) = take('in_specs'), take('out_specs')
        g_scr, f_scr, h_scr = take('scratch_shapes')
        g['kernel'](*g_in, *g_out, *g_scr, t=i % nt_g, nt=nt_g)
        f['kernel'](*f_in, *f_out, *f_scr, j=i % nt_f)

        @pl.when(i % nt_f == nt_f - 1)
        def _():
            h['kernel'](*h_in, *h_out, *h_scr)

    cat = lambda key: [x for c in calls for x in c[key]]
    return pl.pallas_call(
        kernel, grid=(steps,), in_specs=cat('in_specs'), out_specs=cat('out_specs'),
        out_shape=cat('out_shape'), scratch_shapes=cat('scratch_shapes'),
        compiler_params=pltpu.CompilerParams(dimension_semantics=("arbitrary",),
                                             vmem_limit_bytes=VMEM_LIMIT),
        name="mixers",
    )(*gla_p_args, *g['consts'], *fox_args[:5], *f['consts'], *fox_args[5:], *gla_s_args, *h['consts'])


def _ffn_kernel(x_ref, fo_ref, go_ref, wo_ref, g2_ref, wg_ref, wu_ref, wd_ref, gf_ref,
                y_ref, a_ref, *, chunk, fox_time_minor):
    tm = x_ref.shape[0]
    parts = fo_ref.shape[0] if fox_time_minor else 2
    halves = [slice(i * (tm // parts), (i + 1) * (tm // parts)) for i in range(parts)]
    y1 = []
    for i, rows in enumerate(halves):
        fox = (_dot_tn(fo_ref[i], wo_ref[0:FOX_WIDTH, :]) if fox_time_minor
               else _dot(fo_ref[rows, :], wo_ref[0:FOX_WIDTH, :]))
        y1.append(x_ref[rows, :] + fox + _dot(go_ref[rows, :], wo_ref[FOX_WIDTH:, :]))
    h2 = jnp.concatenate([_rms(y, g2_ref[...]).astype(BF16) for y in y1], axis=0)
    for c in range(D_FF // chunk):
        cs = slice(c * chunk, (c + 1) * chunk)
        u = _dot(h2, wg_ref[:, cs])
        w = _dot(h2, wu_ref[:, cs])
        a_ref[:, cs] = (u * _sigmoid(u) * w).astype(BF16)
    y2 = [y + _dot(a_ref[rows, :], wd_ref[...]) for y, rows in zip(y1, halves)]
    for y, rows in zip(y2, halves):
        y_ref[rows, :] = _rms(y, gf_ref[...])


def _ffn(x2d, fo, go, prm, tm=1024, chunk=256):
    n = x2d.shape[0]
    tm = min(tm, n)
    row = lambda w: pl.BlockSpec((tm, w), lambda i: (i, 0))
    const = lambda shape: pl.BlockSpec(shape, lambda i: (0, 0), pipeline_mode=pl.Buffered(1))
    fox_time_minor = fo.ndim == 4
    if fox_time_minor:
        per_tile = tm // fo.shape[3]
        tiles_per_seq = fo.shape[1] // per_tile
        assert per_tile * fo.shape[3] == tm and tiles_per_seq * per_tile == fo.shape[1]
        fo_spec = pl.BlockSpec((None, per_tile, FOX_WIDTH, fo.shape[3]),
                               lambda i: (i // tiles_per_seq, i % tiles_per_seq, 0, 0))
    else:
        fo_spec = row(FOX_WIDTH)
    return pl.pallas_call(
        functools.partial(_ffn_kernel, chunk=chunk, fox_time_minor=fox_time_minor),
        grid=(n // tm,),
        in_specs=[row(D_MODEL), fo_spec, row(GLA_VW), const((D_MODEL, D_MODEL)),
                  const((1, D_MODEL)), const((D_MODEL, D_FF)), const((D_MODEL, D_FF)),
                  const((D_FF, D_MODEL)), const((1, D_MODEL))],
        out_specs=row(D_MODEL),
        out_shape=jax.ShapeDtypeStruct((n, D_MODEL), F32),
        scratch_shapes=[pltpu.VMEM((tm, D_FF), BF16)],
        compiler_params=pltpu.CompilerParams(dimension_semantics=("arbitrary",),
                                             vmem_limit_bytes=VMEM_LIMIT),
        name="ffn",
    )(x2d, fo, go, prm['wo'], prm['g2'], prm['wg'], prm['wu'], prm['wd'], prm['gf'])


def _layer_params(layer, norm1_g, w_in, w_gate2, b_gate2, b_forget, gla_norm_g, w_out,
                  norm2_g, w_gate, w_up, w_down, final_norm_g):
    wt = jnp.transpose(w_in[layer])
    o_fl = 3 * FOX_WIDTH
    o_gq = o_fl + FOX_HEADS
    o_gg = o_gq + 2 * GLA_KW + GLA_VW
    o_gr = o_gg + GLA_GATE_RANK
    tail = 2 * GLA_GATE_RANK
    pad = jnp.zeros((tail - FOX_HEADS - GLA_GATE_RANK, D_MODEL), F32)
    w = jnp.concatenate([wt[:o_fl], wt[o_gq:o_gg], wt[o_gr:], wt[o_fl:o_gq], wt[o_gg:o_gr], pad], axis=0)
    wg2 = jnp.zeros((tail, GLA_KW), F32).at[FOX_HEADS:FOX_HEADS + GLA_GATE_RANK].set(w_gate2[layer])
    return dict(
        g1=norm1_g[layer].reshape(1, D_MODEL),
        w=w.astype(BF16),
        wg2=wg2.astype(BF16),
        bg2=b_gate2[layer].reshape(1, GLA_KW),
        bfc=b_forget[layer].reshape(FOX_HEADS, 1),
        gn=gla_norm_g[layer].reshape(1, GLA_VW),
        wo=w_out[layer].astype(BF16),
        g2=norm2_g[layer].reshape(1, D_MODEL),
        wg=w_gate[layer].astype(BF16),
        wu=w_up[layer].astype(BF16),
        wd=w_down[layer].astype(BF16),
        gf=final_norm_g.reshape(1, D_MODEL),
    )


def kernel(x_prompt, x_sample, cache_fox_k, cache_fox_v, cache_fox_logf, state_gla, norm1_g, w_in,
           w_gate2, b_gate2, b_forget, gla_norm_g, w_out, norm2_g, w_gate, w_up, w_down, final_norm_g):
    depth = w_in.shape[0]
    assert depth == 1, "the final rmsnorm is fused into the layer's ffn kernel"
    bp, tp_, _ = x_prompt.shape
    bs, ts, _ = x_sample.shape
    past = cache_fox_k.shape[2]
    layer = 0
    prm = _layer_params(layer, norm1_g, w_in, w_gate2, b_gate2, b_forget, gla_norm_g, w_out,
                        norm2_g, w_gate, w_up, w_down, final_norm_g)
    by_time = lambda a, b, t: a.reshape(FOX_HEADS, b, t).transpose(1, 2, 0)[None]

    xp = x_prompt.reshape(bp * tp_, D_MODEL)
    qt, kt_p, vt_p, lf_p, ct, gq, gk, gv, glog, gr = _proj(xp, bp, tp_, prm, True)
    fox_p = _fox_prompt(qt, kt_p, vt_p, ct, bp, tp_)
    s0 = jnp.zeros((bp, GLA_HEADS, GLA_DK, GLA_DV), F32)
    gla_p_args = (gq, gk, gv, glog, gr, s0, prm['gn'])

    xs = x_sample.reshape(bs * ts, D_MODEL)
    q, k_s, v_s, lf_s, ct, gq, gk, gv, glog, gr = _proj(xs, bs, ts, prm, False)
    cn = ct.reshape(FOX_HEADS, bs, ts).transpose(1, 0, 2)
    lft = cache_fox_logf[layer].astype(F32).transpose(0, 2, 1)
    ck = cache_fox_k[layer].transpose(0, 2, 3, 1)
    cv = cache_fox_v[layer].transpose(0, 2, 3, 1)
    gla_s_args = (gq, gk, gv, glog, gr, state_gla[layer].astype(F32), prm['gn'])
    gla_p, s_p, fox_s, gla_s, s_s = _mixers_under_cache_stream(
        gla_p_args, (q, k_s, v_s, cn, lft, ck, cv), gla_s_args, bp, tp_, bs, ts, past)
    y_p = _ffn(xp, fox_p, gla_p, prm)
    y_s = _ffn(xs, fox_s, gla_s, prm)

    heads = lambda a, b, t: a.reshape(1, b, t, FOX_HEADS, FOX_HEAD_DIM)
    return (y_p.reshape(bp, tp_, D_MODEL), y_s.reshape(bs, ts, D_MODEL),
            kt_p.transpose(0, 3, 1, 2)[None], vt_p.transpose(0, 3, 1, 2)[None],
            by_time(lf_p, bp, tp_), s_p[None],
            heads(k_s, bs, ts), heads(v_s, bs, ts), by_time(lf_s, bs, ts), s_s[None])
```

```python
import functools

import numpy as np
import jax
import jax.numpy as jnp
from jax import lax
from jax.experimental import pallas as pl
from jax.experimental.pallas import tpu as pltpu

D_MODEL = 1024
FOX_HEADS = 8
FOX_HEAD_DIM = 64
FOX_WIDTH = FOX_HEADS * FOX_HEAD_DIM
GLA_HEADS = 4
GLA_DK = 64
GLA_DV = 128
GLA_KW = GLA_HEADS * GLA_DK
GLA_VW = GLA_HEADS * GLA_DV
GLA_GATE_RANK = 16
GLA_GATE_TEMP = 16.0
D_FF = 2816
EPS = 1e-6

LANES = 128
LOG2E = 1.4426950408889634
VMEM_LIMIT = 56 * 1024 * 1024

F32 = jnp.float32
BF16 = jnp.bfloat16
NEG_BIG = -1e30


def _log_sigmoid(x):
    return jnp.minimum(x, 0.0) - jnp.log1p(jnp.exp(-jnp.abs(x)))


def _sigmoid(x):
    return 1.0 / (1.0 + jnp.exp(-x))


def _split3_f32(x):
    hi = x.astype(BF16).astype(F32)
    r = x - hi
    mid = r.astype(BF16).astype(F32)
    lo = (r - mid).astype(BF16).astype(F32)
    return hi, mid, lo


def _split2_f32(x):
    hi = x.astype(BF16).astype(F32)
    return hi, (x - hi).astype(BF16).astype(F32)


def _dot(a, b):
    return jnp.dot(a, b, preferred_element_type=F32)


def _dot_nt(a, b):
    return lax.dot_general(a, b, (((1,), (1,)), ((), ())), preferred_element_type=F32)


def _dot_tn(a, b):
    return lax.dot_general(a, b, (((0,), (0,)), ((), ())), preferred_element_type=F32)


def _rms(x, g):
    return x * lax.rsqrt(jnp.mean(x * x, axis=-1, keepdims=True) + EPS) * g


def _proj_kernel(x_ref, g1_ref, w_ref, wg2_ref, bg2_ref, bfc_ref, tri_ref,
                 *rest, tiles_per_seq, time_minor):
    o_kv, o_b = FOX_WIDTH, 3 * FOX_WIDTH
    o_tail = o_b + 2 * GLA_KW + 2 * GLA_VW
    carry_ref = rest[-1]

    @pl.when(pl.program_id(0) % tiles_per_seq == 0)
    def _():
        carry_ref[...] = jnp.zeros_like(carry_ref)

    h = _rms(x_ref[...], g1_ref[...]).astype(BF16)
    tm = h.shape[0]
    scale = FOX_HEAD_DIM ** -0.5

    zc = _dot_nt(h, w_ref[o_tail:, :])
    fl_t = _dot_nt(w_ref[o_tail:, :], h)[0:FOX_HEADS]

    if time_minor:
        (q_ref, k_ref, v_ref, lft_ref, ct_ref, gq_ref, gk_ref, gv_ref, glog_ref, gr_ref,
         carry_ref) = rest
        qt = (_dot_nt(w_ref[0:o_kv, :], h) * (scale * LOG2E)).astype(BF16)
        tq = q_ref.shape[2]
        for i in range(q_ref.shape[0]):
            q_ref[i] = qt[:, i * tq:(i + 1) * tq]
    else:
        (q_ref, k_ref, v_ref, lft_ref, ct_ref, gq_ref, gk_ref, gv_ref, glog_ref, gr_ref,
         carry_ref) = rest
        q_ref[...] = (_dot_nt(h, w_ref[0:o_kv, :]) * scale).astype(BF16)

    logf_t = _log_sigmoid(fl_t + bfc_ref[...])
    lft_ref[...] = logf_t
    parts = jnp.concatenate(_split3_f32(logf_t), axis=0).astype(BF16)
    gg = zc.astype(BF16)

    if time_minor:
        kvt = _dot_nt(w_ref[o_kv:o_b, :], h)
        k_ref[...] = kvt[:FOX_WIDTH].reshape(FOX_HEADS, FOX_HEAD_DIM, tm)
        v_ref[...] = kvt[FOX_WIDTH:].reshape(FOX_HEADS, FOX_HEAD_DIM, tm)
    else:
        kv = _dot_nt(h, w_ref[o_kv:o_b, :])
        k_ref[...] = kv[:, :FOX_WIDTH]
        v_ref[...] = kv[:, FOX_WIDTH:]

    gpre = _dot(gg, wg2_ref[...]) + bg2_ref[...]
    cs = _dot(parts, tri_ref[...])

    zb = _dot_nt(h, w_ref[o_b:o_tail, :])
    gq_ref[...] = zb[:, :GLA_KW]
    gk_ref[...] = zb[:, GLA_KW:2 * GLA_KW]
    gv_ref[...] = zb[:, 2 * GLA_KW:2 * GLA_KW + GLA_VW].astype(BF16)
    gr_ref[...] = zb[:, 2 * GLA_KW + GLA_VW:]

    glog_ref[...] = _log_sigmoid(gpre) * (1.0 / GLA_GATE_TEMP)

    ct = cs[0:8] + cs[8:16] + cs[16:24] + carry_ref[:, 0:1]
    ct_ref[...] = ct
    carry_ref[...] = jnp.broadcast_to(ct[:, tm - 1:], carry_ref.shape)


def _proj(x2d, batch, seq_len, prm, time_minor, tm=1024, tq=512):
    n = x2d.shape[0]
    tm = min(tm, n)
    tiles_per_seq = max(seq_len // tm, 1)
    per_tile = tm // tq
    idx = np.arange(tm)
    tri = ((idx[:, None] <= idx[None, :]) & (idx[:, None] // seq_len == idx[None, :] // seq_len))
    tri = jnp.asarray(tri.astype(np.float32), dtype=BF16)
    const = lambda a: pl.BlockSpec(a.shape, lambda i: (0, 0), pipeline_mode=pl.Buffered(1))
    row = lambda w: pl.BlockSpec((tm, w), lambda i: (i, 0))
    col = pl.BlockSpec((FOX_HEADS, tm), lambda i: (0, i))
    sds = jax.ShapeDtypeStruct
    if time_minor:
        kv_shape = sds((batch, FOX_HEADS, FOX_HEAD_DIM, seq_len), F32)
        kv_spec = pl.BlockSpec((None, FOX_HEADS, FOX_HEAD_DIM, tm),
                               lambda i: (i // tiles_per_seq, 0, 0, i % tiles_per_seq))
        fox = [(sds((batch, seq_len // tq, FOX_WIDTH, tq), BF16),
                pl.BlockSpec((None, per_tile, FOX_WIDTH, tq),
                             lambda i: (i // tiles_per_seq, i % tiles_per_seq, 0, 0))),
               (kv_shape, kv_spec), (kv_shape, kv_spec)]
    else:
        fox = [(sds((n, FOX_WIDTH), BF16), row(FOX_WIDTH)),
               (sds((n, FOX_WIDTH), F32), row(FOX_WIDTH)),
               (sds((n, FOX_WIDTH), F32), row(FOX_WIDTH))]
    outs = fox + [
        (sds((FOX_HEADS, n), F32), col),
        (sds((FOX_HEADS, n), F32), col),
        (sds((n, GLA_KW), F32), row(GLA_KW)),
        (sds((n, GLA_KW), F32), row(GLA_KW)),
        (sds((n, GLA_VW), BF16), row(GLA_VW)),
        (sds((n, GLA_KW), F32), row(GLA_KW)),
        (sds((n, GLA_VW), F32), row(GLA_VW)),
    ]
    names = ('g1', 'w', 'wg2', 'bg2', 'bfc')
    return pl.pallas_call(
        functools.partial(_proj_kernel, tiles_per_seq=tiles_per_seq, time_minor=time_minor),
        grid=(n // tm,),
        in_specs=[row(D_MODEL)] + [const(prm[k]) for k in names] + [const(tri)],
        out_specs=tuple(s for _, s in outs), out_shape=tuple(s for s, _ in outs),
        scratch_shapes=[pltpu.VMEM((FOX_HEADS, LANES), F32)],
        compiler_params=pltpu.CompilerParams(dimension_semantics=("arbitrary",),
                                             vmem_limit_bytes=VMEM_LIMIT),
        name="proj",
    )(x2d, *[prm[k] for k in names], tri)


def _fox_prompt_kernel(q_ref, k_ref, v_ref, c_ref, o_ref, kb_ref, vb_ref, s_ref, p_ref, acc_ref,
                       *, tk):
    g = pl.program_id(1)
    nq = q_ref.shape[0]
    nk = vb_ref.shape[1]
    pairs = q_ref.shape[1] // LANES
    hd = FOX_HEAD_DIM
    spare = (hd, 0)
    vrows = vb_ref.shape[2]

    parts = jnp.concatenate(_split3_f32(c_ref[...] * (-LOG2E)), axis=0).astype(BF16)
    r = lax.broadcasted_iota(jnp.int32, (3 * FOX_HEADS, LANES), 0)
    ln = lax.broadcasted_iota(jnp.int32, (3 * FOX_HEADS, LANES), 1)
    klane = lax.broadcasted_iota(jnp.int32, (k_ref.shape[2], LANES), 1)
    tail = jnp.where(lax.broadcasted_iota(jnp.int32, (vrows - hd, tk), 0) == 0, 1.0, 0.0)
    for pp in range(pairs):
        k = k_ref[2 * pp:2 * pp + 2].reshape(LANES, k_ref.shape[2]).T.astype(BF16)
        for hh in range(2):
            head = 2 * (pairs * g + pp) + hh
            place = (r % FOX_HEADS == head) & (ln == spare[hh] + r // FOX_HEADS)
            extra = _dot_tn(parts, jnp.where(place, 1.0, 0.0).astype(BF16))
            own = (klane < hd) if hh == 0 else (klane >= hd)
            kb_ref[2 * pp + hh] = jnp.where(own, k, extra.astype(BF16))
            for jj in range(nk):
                vt = v_ref[2 * pp + hh, :, jj * tk:(jj + 1) * tk]
                vb_ref[2 * pp + hh, jj] = jnp.concatenate([vt, tail], axis=0).astype(BF16)

    qrow = lax.broadcasted_iota(jnp.int32, (LANES, tk), 0)
    key = lax.broadcasted_iota(jnp.int32, (tk, tk), 0)
    qry = lax.broadcasted_iota(jnp.int32, (tk, tk), 1)
    causal = key <= qry

    chains = [(slot, part) for slot in range(2 * pairs) for part in range(2)]
    everyone = list(range(len(chains)))
    second = [n for n in everyone if chains[n][1] == 1]

    def q_operands(qi):
        qs = []
        for slot, part in chains:
            pp, hh = divmod(slot, 2)
            q = q_ref[qi, pp * LANES:(pp + 1) * LANES, part * tk:(part + 1) * tk]
            own = (qrow < hd) if hh == 0 else (qrow >= hd)
            ones = (qrow >= spare[hh]) & (qrow < spare[hh] + 3)
            qs.append(jnp.where(own, q, jnp.where(ones, 1.0, 0.0).astype(BF16)))
        return qs

    steps = []
    for qi in range(nq):
        steps += [(qi, j, everyone, None) for j in range(2 * qi)]
        steps += [(qi, 2 * qi, everyone, 0), (qi, 2 * qi + 1, second, 1)]

    qs_of = {}

    def scores(step, buf):
        qi, j, live, _ = step
        if qi not in qs_of:
            qs_of.clear()
            qs_of[qi] = q_operands(qi)
        for n in live:
            s_ref[buf, n] = _dot(kb_ref[chains[n][0], j * tk:(j + 1) * tk, :], qs_of[qi][n])

    def values(step, buf):
        _, j, live, _ = step
        return {n: _dot(vb_ref[chains[n][0], j], p_ref[buf, n]) for n in live}

    def finish(qi):
        for part in range(2):
            heads = [acc_ref[qi % 2, 2 * slot + part] for slot in range(2 * pairs)]
            o_ref[qi, :, part * tk:(part + 1) * tk] = jnp.concatenate(
                [a[:hd] / a[hd:hd + 1] for a in heads], axis=0).astype(BF16)

    def accumulate(step, alphas, pv):
        qi, j, live, _ = step
        for n in live:
            acc_ref[qi % 2, n] = pv[n] if j == 0 else alphas[n] * acc_ref[qi % 2, n] + pv[n]

    ms = {}
    scores(steps[0], 0)
    prev, prev_alphas = None, None
    for t, step in enumerate(steps):
        qi, j, live, masked_part = step
        if t + 1 < len(steps):
            scores(steps[t + 1], (t + 1) % 2)
        pv = values(prev, (t - 1) % 2) if prev is not None else None
        alphas = {}
        for n in live:
            s = s_ref[t % 2, n]
            if chains[n][1] == masked_part:
                s = jnp.where(causal, s, -jnp.inf)
            m_new = jnp.max(s, axis=0, keepdims=True)
            if j > 0:
                m_new = jnp.maximum(ms[n], m_new)
                alphas[n] = jnp.exp2(ms[n] - m_new)
            p_ref[t % 2, n] = jnp.exp2(s - m_new).astype(BF16)
            ms[n] = m_new
        if prev is not None:
            accumulate(prev, prev_alphas, pv)
            if prev[0] != qi:
                finish(prev[0])
        prev, prev_alphas = step, alphas
    accumulate(prev, prev_alphas, values(prev, (len(steps) - 1) % 2))
    finish(prev[0])


def _fox_prompt(qt, kt, vt, ct, batch, seq, tk=256, pairs=2):
    tq = 2 * tk
    nq = seq // tq
    nk = seq // tk
    heads = 2 * pairs
    chains = 2 * heads
    vrows = FOX_HEAD_DIM + 16
    assert qt.shape == (batch, nq, FOX_WIDTH, tq) and FOX_HEADS % heads == 0
    q_spec = pl.BlockSpec((None, nq, pairs * LANES, tq), lambda b, p: (b, 0, p, 0))
    kv_spec = pl.BlockSpec((None, heads, FOX_HEAD_DIM, seq), lambda b, p: (b, p, 0, 0))
    return pl.pallas_call(
        functools.partial(_fox_prompt_kernel, tk=tk),
        grid=(batch, FOX_HEADS // heads),
        in_specs=[q_spec, kv_spec, kv_spec,
                  pl.BlockSpec((FOX_HEADS, seq), lambda b, p: (0, b))],
        out_specs=q_spec,
        out_shape=jax.ShapeDtypeStruct((batch, nq, FOX_WIDTH, tq), BF16),
        scratch_shapes=[pltpu.VMEM((heads, seq, LANES), BF16), pltpu.VMEM((heads, nk, vrows, tk), BF16),
                        pltpu.VMEM((2, chains, tk, tk), F32), pltpu.VMEM((2, chains, tk, tk), BF16),
                        pltpu.VMEM((2, chains, vrows, tk), F32)],
        compiler_params=pltpu.CompilerParams(dimension_semantics=("arbitrary", "arbitrary"),
                                             vmem_limit_bytes=VMEM_LIMIT),
        name="fox_prompt",
    )(qt, kt, vt, ct)


def _fox_sample_kernel(q_ref, kn_ref, vn_ref, cn_ref, lft_ref, mlow_ref, ck_ref, cv_ref,
                       o_ref, qh_ref, suf_ref, m_ref, l_ref, acc_ref, *, tp, nt, tn, j=None):
    j = pl.program_id(1) if j is None else j
    blk = 2 * LANES
    hd = FOX_HEAD_DIM

    @pl.when(j == 0)
    def _init():
        for h in range(FOX_HEADS):
            qh_ref[h] = q_ref[:, h * hd:(h + 1) * hd]
        m_ref[...] = jnp.full(m_ref.shape, NEG_BIG, F32)
        l_ref[...] = jnp.zeros(l_ref.shape, F32)
        acc_ref[...] = jnp.zeros(acc_ref.shape, F32)
        x = lft_ref[...]
        parts = _split3_f32(x)
        per_tile = tp // blk
        nblk = nt * per_tile
        rows = jnp.concatenate([p_[:, b * blk:(b + 1) * blk] for b in range(nblk) for p_ in parts], axis=0)
        y = _dot(rows.astype(BF16), mlow_ref[...])
        carry = jnp.zeros((FOX_HEADS, 1), F32)
        for b in reversed(range(nblk)):
            yb = y[24 * b:24 * b + 8] + y[24 * b + 8:24 * b + 16] + y[24 * b + 16:24 * b + 24]
            off = (b % per_tile) * blk
            suf_ref[b // per_tile, :, off:off + blk] = yb + carry
            carry = carry + yb[:, 0:1] + x[:, b * blk:b * blk + 1]

    def update(blocks):
        m_old = m_ref[...]
        m_new = m_old
        for s, _ in blocks:
            m_new = jnp.maximum(m_new, jnp.max(s, axis=-1, keepdims=True))
        alpha = jnp.exp(m_old - m_new)
        l = alpha * l_ref[...]
        acc = alpha * acc_ref[...]
        for s, pv in blocks:
            pm = jnp.exp(s - m_new)
            l = l + jnp.sum(pm, axis=-1, keepdims=True)
            acc = acc + pv(pm.astype(BF16))
        return m_new, l, acc

    bmm = lambda a, b, ca, cb: lax.dot_general(a, b, (((ca,), (cb,)), ((0,), (0,))),
                                               preferred_element_type=F32)
    qh = qh_ref[...]

    def cache_block():
        kt = ck_ref[...].astype(BF16)
        vt = cv_ref[...].astype(BF16)
        s = bmm(qh, kt, 2, 1).reshape(FOX_HEADS * tn, tp) + jnp.repeat(suf_ref[j], tn, axis=0)
        return s.reshape(FOX_HEADS, tn, tp), lambda pm: bmm(pm, vt, 2, 2)

    def new_block():
        per_head = lambda ref: jnp.stack([ref[:, h * hd:(h + 1) * hd] for h in range(FOX_HEADS)],
                                         axis=0).astype(BF16)
        kn = per_head(kn_ref)
        vn = per_head(vn_ref)
        r = lax.broadcasted_iota(jnp.int32, (FOX_HEADS, tn, tn), 1)
        c = lax.broadcasted_iota(jnp.int32, (FOX_HEADS, tn, tn), 2)
        s = jnp.where(c <= r, bmm(qh, kn, 2, 2) - cn_ref[...][:, None, :], -jnp.inf)
        return s, lambda pm: bmm(pm, vn, 2, 1)

    @pl.when(j < nt - 1)
    def _():
        m, l, acc = update([cache_block()])
        m_ref[...] = m
        l_ref[...] = l
        acc_ref[...] = acc

    @pl.when(j == nt - 1)
    def _():
        _, l, acc = update([cache_block(), new_block()])
        o = acc / l
        o_ref[...] = jnp.concatenate([o[h] for h in range(FOX_HEADS)], axis=1).astype(BF16)


def _fox_sample_call(batch, tn, past, tp, ix):
    nt = past // tp
    blk = 2 * LANES
    idx = np.arange(blk)
    mlow = jnp.asarray((idx[:, None] > idx[None, :]).astype(np.float32), dtype=BF16)
    per_b = lambda w: pl.BlockSpec((tn, w), lambda *g: (ix(*g)[0], 0))
    cache = pl.BlockSpec((None, FOX_HEADS, FOX_HEAD_DIM, tp), lambda *g: (ix(*g)[0], 0, 0, ix(*g)[1]))
    return dict(
        kernel=functools.partial(_fox_sample_kernel, tp=tp, nt=nt, tn=tn), nt=nt, consts=(mlow,),
        in_specs=[per_b(FOX_WIDTH), per_b(FOX_WIDTH), per_b(FOX_WIDTH),
                  pl.BlockSpec((None, FOX_HEADS, tn), lambda *g: (ix(*g)[0], 0, 0)),
                  pl.BlockSpec((None, FOX_HEADS, past), lambda *g: (ix(*g)[0], 0, 0)),
                  pl.BlockSpec((blk, blk), lambda *g: (0, 0)),
                  cache, cache],
        out_specs=[per_b(FOX_WIDTH)],
        out_shape=[jax.ShapeDtypeStruct((batch * tn, FOX_WIDTH), BF16)],
        scratch_shapes=[pltpu.VMEM((FOX_HEADS, tn, FOX_HEAD_DIM), BF16),
                        pltpu.VMEM((nt, FOX_HEADS, tp), F32),
                        pltpu.VMEM((FOX_HEADS, tn, 1), F32), pltpu.VMEM((FOX_HEADS, tn, 1), F32),
                        pltpu.VMEM((FOX_HEADS, tn, FOX_HEAD_DIM), F32)])


def _gla_kernel(q_ref, k_ref, v_ref, g_ref, r_ref, s0_ref, gn_ref, w_ref, lv_ref,
                o_ref, s_ref, st_ref, *, tb, levels, blocks, seqs, carried, t=None, nt=None):
    to_work = lambda s0: s0.reshape(GLA_KW, GLA_DV).T
    from_work = lambda st: st.T.reshape(GLA_HEADS, GLA_DK, GLA_DV)
    if carried:
        t = pl.program_id(1) if t is None else t
        nt = pl.num_programs(1) if nt is None else nt

        @pl.when(t == 0)
        def _():
            st_ref[...] = to_work(s0_ref[0])

    for sq in range(seqs):
        st = st_ref[...] if carried else to_work(s0_ref[sq])
        for blk in range(blocks):
            rows = lambda ref: ref.at[pl.ds((sq * blocks + blk) * tb, tb), :]
            st = _gla_block(rows(q_ref), rows(k_ref), rows(v_ref), rows(g_ref), rows(r_ref), gn_ref,
                            w_ref, lv_ref, rows(o_ref), st, tb=tb, levels=levels)
        if carried:
            st_ref[...] = st

            @pl.when(t == nt - 1)
            def _():
                s_ref[0] = from_work(st)
        else:
            s_ref[sq] = from_work(st)


def _gla_block(q_ref, k_ref, v_ref, g_ref, r_ref, gn_ref, w_ref, lv_ref, o_ref, st, *, tb, levels):
    half = LANES // 2
    r = r_ref[...]
    gate = gn_ref[...] * (r * _sigmoid(r))

    g2 = g_ref[...] * LOG2E
    gp = jnp.concatenate(_split2_f32(g2), axis=0).astype(BF16)
    rows = w_ref.shape[0] // 2
    cum = jnp.concatenate([_dot(w_ref[0:rows, :], gp), _dot(w_ref[rows:, :], gp)], axis=0)
    dec = jnp.exp2(cum)
    from_start = dec[0:tb]
    to_end = jnp.exp2(cum[tb - 1:tb] - cum[0:tb])

    q = q_ref[...] * (GLA_DK ** -0.5)
    k = k_ref[...]
    v = v_ref[...]
    row = lax.broadcasted_iota(jnp.int32, (tb, GLA_KW), 0)
    low = lax.broadcasted_iota(jnp.int32, (tb, LANES), 1) < half
    lv = lv_ref[...]

    def pair_scores(xq, xk):
        outs = []
        for p in range(GLA_HEADS // 2):
            a = xq[:, p * LANES:(p + 1) * LANES]
            zero = jnp.zeros_like(a)
            lhs = jnp.concatenate([jnp.where(low, a, zero), jnp.where(low, zero, a)], axis=0)
            outs.append(_dot_nt(lhs, xk[:, p * LANES:(p + 1) * LANES]))
        return outs

    here = lv == -1
    a = [jnp.where(here, r_, 0.0) for r_ in pair_scores(q.astype(BF16), k.astype(BF16))]
    for l in range(levels):
        upper = ((row >> l) & 1) == 1
        e = jnp.where(upper, jnp.exp2(g2), 1.0) if l == 0 else dec[l * tb:(l + 1) * tb]
        x = (jnp.where(upper, q, k) * e).astype(BF16)
        here = lv == l
        a = [jnp.where(here, r_, a_) for r_, a_ in zip(pair_scores(x, x), a)]
    o = jnp.concatenate(
        [_dot(a[h // 2][(h % 2) * tb:(h % 2 + 1) * tb].astype(BF16), v[:, h * GLA_DV:(h + 1) * GLA_DV])
         for h in range(GLA_HEADS)], axis=1)

    lane = lax.broadcasted_iota(jnp.int32, (tb, GLA_KW), 1)
    head_sel = [(lane >= h * GLA_DK) & (lane < (h + 1) * GLA_DK) for h in range(GLA_HEADS)]
    qt = (q * from_start).astype(BF16)
    kt = (k * to_end).astype(BF16)
    zero = jnp.zeros_like(qt)
    q4 = jnp.concatenate([jnp.where(sel, qt, zero) for sel in head_sel], axis=0)
    oi = _dot_nt(q4, st.astype(BF16))
    o = o + jnp.concatenate([oi[h * tb:(h + 1) * tb] for h in range(GLA_HEADS)], axis=1)
    upd = None
    for h in range(GLA_HEADS):
        u = _dot_tn(v[:, h * GLA_DV:(h + 1) * GLA_DV], jnp.where(head_sel[h], kt, zero))
        upd = u if upd is None else upd + u
    st = from_start[tb - 1:tb, :] * st + upd

    outs = []
    for h in range(GLA_HEADS):
        oh = o[:, h * GLA_DV:(h + 1) * GLA_DV]
        outs.append(oh * lax.rsqrt(jnp.mean(oh * oh, axis=-1, keepdims=True) + EPS))
    o_ref[...] = (jnp.concatenate(outs, axis=1) * gate).astype(BF16)
    return st


def _gla_call(batch, seq, ix, tb=128, rows_per_step=1024):
    tb = min(tb, seq)
    step_rows = min(rows_per_step, batch * seq)
    carried = seq > step_rows
    seqs = 1 if carried else step_rows // seq
    per_seq = (step_rows if carried else seq) // tb
    nt = seq // (tb * per_seq)
    n = batch * seq
    levels = tb.bit_length() - 1
    assert tb == 1 << levels
    ti = np.arange(tb)[:, None]
    si = np.arange(tb)[None, :]
    blocks = [si <= ti]
    for l in range(1, levels):
        m = 1 << l
        mid = ti - ti % (2 * m) + m - 1
        upper = ti % (2 * m) >= m
        blocks.append(np.where(upper, (si > mid) & (si <= ti), (si > ti) & (si <= mid)))
    w = np.concatenate(blocks, axis=0).astype(np.float32)
    w = jnp.asarray(np.concatenate([w, w], axis=1), dtype=BF16)
    x = ti ^ si
    lv = np.where(si < ti, np.floor(np.log2(np.maximum(x, 1))).astype(np.int32),
                  np.where(si == ti, -1, -2)).astype(np.int32)
    lv = jnp.asarray(np.concatenate([lv, lv], axis=0))
    row = lambda w_: pl.BlockSpec((step_rows, w_), lambda *g: (ix(*g)[0] * nt + ix(*g)[1], 0))
    state = pl.BlockSpec((seqs, GLA_HEADS, GLA_DK, GLA_DV), lambda *g: (ix(*g)[0], 0, 0, 0))
    const = lambda shape: pl.BlockSpec(shape, lambda *g: (0, 0))
    return dict(
        kernel=functools.partial(_gla_kernel, tb=tb, levels=levels, blocks=per_seq, seqs=seqs,
                                 carried=carried),
        groups=batch // seqs, nt=nt, consts=(w, lv),
        in_specs=[row(GLA_KW), row(GLA_KW), row(GLA_VW), row(GLA_KW), row(GLA_VW), state,
                  const((1, GLA_VW)), const(w.shape), const(lv.shape)],
        out_specs=[row(GLA_VW), state],
        out_shape=[jax.ShapeDtypeStruct((n, GLA_VW), BF16),
                   jax.ShapeDtypeStruct((batch, GLA_HEADS, GLA_DK, GLA_DV), F32)],
        scratch_shapes=[pltpu.VMEM((GLA_DV, GLA_KW), F32)])


def _mixers_under_cache_stream(gla_p_args, fox_args, gla_s_args, gla_batch, gla_seq, batch, tn, past,
                               gla_rows=512, tp=4096):
    nt_g, nt_f = gla_seq // gla_rows, past // tp
    g = _gla_call(gla_batch, gla_seq, lambda i: (i // nt_g, i % nt_g), rows_per_step=gla_rows)
    f = _fox_sample_call(batch, tn, past, tp, lambda i: (i // nt_f, i % nt_f))
    h = _gla_call(batch, tn, lambda i: (i // nt_f, 0), rows_per_step=tn)
    steps = g['groups'] * nt_g
    assert steps == batch * nt_f and g['nt'] == nt_g and f['nt'] == nt_f and h['nt'] == 1
    calls = (g, f, h)

    def kernel(*refs):
        i = pl.program_id(0)
        refs = list(refs)
        take = lambda key: [[refs.pop(0) for _ in c[key]] for c in calls]
        (g_in, f_in, h_in), (g_out, f_out, h_out) = take('in_specs'), take('out_specs')
        g_scr, f_scr, h_scr = take('scratch_shapes')
        g['kernel'](*g_in, *g_out, *g_scr, t=i % nt_g, nt=nt_g)
        f['kernel'](*f_in, *f_out, *f_scr, j=i % nt_f)

        @pl.when(i % nt_f == nt_f - 1)
        def _():
            h['kernel'](*h_in, *h_out, *h_scr)

    cat = lambda key: [x for c in calls for x in c[key]]
    return pl.pallas_call(
        kernel, grid=(steps,), in_specs=cat('in_specs'), out_specs=cat('out_specs'),
        out_shape=cat('out_shape'), scratch_shapes=cat('scratch_shapes'),
        compiler_params=pltpu.CompilerParams(dimension_semantics=("arbitrary",),
                                             vmem_limit_bytes=VMEM_LIMIT),
        name="mixers",
    )(*gla_p_args, *g['consts'], *fox_args[:5], *f['consts'], *fox_args[5:], *gla_s_args, *h['consts'])


def _ffn_kernel(x_ref, fo_ref, go_ref, wo_ref, g2_ref, wg_ref, wu_ref, wd_ref, gf_ref,
                y_ref, a_ref, *, chunk, fox_time_minor):
    tm = x_ref.shape[0]
    parts = fo_ref.shape[0] if fox_time_minor else 2
    halves = [slice(i * (tm // parts), (i + 1) * (tm // parts)) for i in range(parts)]
    y1 = []
    for i, rows in enumerate(halves):
        fox = (_dot_tn(fo_ref[i], wo_ref[0:FOX_WIDTH, :]) if fox_time_minor
               else _dot(fo_ref[rows, :], wo_ref[0:FOX_WIDTH, :]))
        y1.append(x_ref[rows, :] + fox + _dot(go_ref[rows, :], wo_ref[FOX_WIDTH:, :]))
    h2 = jnp.concatenate([_rms(y, g2_ref[...]).astype(BF16) for y in y1], axis=0)
    for c in range(D_FF // chunk):
        cs = slice(c * chunk, (c + 1) * chunk)
        u = _dot(h2, wg_ref[:, cs])
        w = _dot(h2, wu_ref[:, cs])
        a_ref[:, cs] = (u * _sigmoid(u) * w).astype(BF16)
    y2 = [y + _dot(a_ref[rows, :], wd_ref[...]) for y, rows in zip(y1, halves)]
    for y, rows in zip(y2, halves):
        y_ref[rows, :] = _rms(y, gf_ref[...])


def _ffn(x2d, fo, go, prm, tm=1024, chunk=256):
    n = x2d.shape[0]
    tm = min(tm, n)
    row = lambda w: pl.BlockSpec((tm, w), lambda i: (i, 0))
    const = lambda shape: pl.BlockSpec(shape, lambda i: (0, 0), pipeline_mode=pl.Buffered(1))
    fox_time_minor = fo.ndim == 4
    if fox_time_minor:
        per_tile = tm // fo.shape[3]
        tiles_per_seq = fo.shape[1] // per_tile
        assert per_tile * fo.shape[3] == tm and tiles_per_seq * per_tile == fo.shape[1]
        fo_spec = pl.BlockSpec((None, per_tile, FOX_WIDTH, fo.shape[3]),
                               lambda i: (i // tiles_per_seq, i % tiles_per_seq, 0, 0))
    else:
        fo_spec = row(FOX_WIDTH)
    return pl.pallas_call(
        functools.partial(_ffn_kernel, chunk=chunk, fox_time_minor=fox_time_minor),
        grid=(n // tm,),
        in_specs=[row(D_MODEL), fo_spec, row(GLA_VW), const((D_MODEL, D_MODEL)),
                  const((1, D_MODEL)), const((D_MODEL, D_FF)), const((D_MODEL, D_FF)),
                  const((D_FF, D_MODEL)), const((1, D_MODEL))],
        out_specs=row(D_MODEL),
        out_shape=jax.ShapeDtypeStruct((n, D_MODEL), F32),
        scratch_shapes=[pltpu.VMEM((tm, D_FF), BF16)],
        compiler_params=pltpu.CompilerParams(dimension_semantics=("arbitrary",),
                                             vmem_limit_bytes=VMEM_LIMIT),
        name="ffn",
    )(x2d, fo, go, prm['wo'], prm['g2'], prm['wg'], prm['wu'], prm['wd'], prm['gf'])


def _layer_params(layer, norm1_g, w_in, w_gate2, b_gate2, b_forget, gla_norm_g, w_out,
                  norm2_g, w_gate, w_up, w_down, final_norm_g):
    wt = jnp.transpose(w_in[layer])
    o_fl = 3 * FOX_WIDTH
    o_gq = o_fl + FOX_HEADS
    o_gg = o_gq + 2 * GLA_KW + GLA_VW
    o_gr = o_gg + GLA_GATE_RANK
    tail = 2 * GLA_GATE_RANK
    pad = jnp.zeros((tail - FOX_HEADS - GLA_GATE_RANK, D_MODEL), F32)
    w = jnp.concatenate([wt[:o_fl], wt[o_gq:o_gg], wt[o_gr:], wt[o_fl:o_gq], wt[o_gg:o_gr], pad], axis=0)
    wg2 = jnp.zeros((tail, GLA_KW), F32).at[FOX_HEADS:FOX_HEADS + GLA_GATE_RANK].set(w_gate2[layer])
    return dict(
        g1=norm1_g[layer].reshape(1, D_MODEL),
        w=w.astype(BF16),
        wg2=wg2.astype(BF16),
        bg2=b_gate2[layer].reshape(1, GLA_KW),
        bfc=b_forget[layer].reshape(FOX_HEADS, 1),
        gn=gla_norm_g[layer].reshape(1, GLA_VW),
        wo=w_out[layer].astype(BF16),
        g2=norm2_g[layer].reshape(1, D_MODEL),
        wg=w_gate[layer].astype(BF16),
        wu=w_up[layer].astype(BF16),
        wd=w_down[layer].astype(BF16),
        gf=final_norm_g.reshape(1, D_MODEL),
    )


def kernel(x_prompt, x_sample, cache_fox_k, cache_fox_v, cache_fox_logf, state_gla, norm1_g, w_in,
           w_gate2, b_gate2, b_forget, gla_norm_g, w_out, norm2_g, w_gate, w_up, w_down, final_norm_g):
    depth = w_in.shape[0]
    assert depth == 1, "the final rmsnorm is fused into the layer's ffn kernel"
    bp, tp_, _ = x_prompt.shape
    bs, ts, _ = x_sample.shape
    past = cache_fox_k.shape[2]
    layer = 0
    prm = _layer_params(layer, norm1_g, w_in, w_gate2, b_gate2, b_forget, gla_norm_g, w_out,
                        norm2_g, w_gate, w_up, w_down, final_norm_g)
    by_time = lambda a, b, t: a.reshape(FOX_HEADS, b, t).transpose(1, 2, 0)[None]

    xp = x_prompt.reshape(bp * tp_, D_MODEL)
    qt, kt_p, vt_p, lf_p, ct, gq, gk, gv, glog, gr = _proj(xp, bp, tp_, prm, True)
    fox_p = _fox_prompt(qt, kt_p, vt_p, ct, bp, tp_)
    s0 = jnp.zeros((bp, GLA_HEADS, GLA_DK, GLA_DV), F32)
    gla_p_args = (gq, gk, gv, glog, gr, s0, prm['gn'])

    xs = x_sample.reshape(bs * ts, D_MODEL)
    q, k_s, v_s, lf_s, ct, gq, gk, gv, glog, gr = _proj(xs, bs, ts, prm, False)
    cn = ct.reshape(FOX_HEADS, bs, ts).transpose(1, 0, 2)
    lft = cache_fox_logf[layer].astype(F32).transpose(0, 2, 1)
    ck = cache_fox_k[layer].transpose(0, 2, 3, 1)
    cv = cache_fox_v[layer].transpose(0, 2, 3, 1)
    gla_s_args = (gq, gk, gv, glog, gr, state_gla[layer].astype(F32), prm['gn'])
    gla_p, s_p, fox_s, gla_s, s_s = _mixers_under_cache_stream(
        gla_p_args, (q, k_s, v_s, cn, lft, ck, cv), gla_s_args, bp, tp_, bs, ts, past)
    y_p = _ffn(xp, fox_p, gla_p, prm)
    y_s = _ffn(xs, fox_s, gla_s, prm)

    heads = lambda a, b, t: a.reshape(1, b, t, FOX_HEADS, FOX_HEAD_DIM)
    return (y_p.reshape(bp, tp_, D_MODEL), y_s.reshape(bs, ts, D_MODEL),
            kt_p.transpose(0, 3, 1, 2)[None], vt_p.transpose(0, 3, 1, 2)[None],
            by_time(lf_p, bp, tp_), s_p[None],
            heads(k_s, bs, ts), heads(v_s, bs, ts), by_time(lf_s, bs, ts), s_s[None])
```

```python
import functools

import numpy as np
import jax
import jax.numpy as jnp
from jax import lax
from jax.experimental import pallas as pl
from jax.experimental.pallas import tpu as pltpu

D_MODEL = 1024
FOX_HEADS = 8
FOX_HEAD_DIM = 64
FOX_WIDTH = FOX_HEADS * FOX_HEAD_DIM
GLA_HEADS = 4
GLA_DK = 64
GLA_DV = 128
GLA_KW = GLA_HEADS * GLA_DK
GLA_VW = GLA_HEADS * GLA_DV
GLA_GATE_RANK = 16
GLA_GATE_TEMP = 16.0
D_FF = 2816
EPS = 1e-6

LANES = 128
LOG2E = 1.4426950408889634
VMEM_LIMIT = 56 * 1024 * 1024

F32 = jnp.float32
BF16 = jnp.bfloat16
NEG_BIG = -1e30


def _log_sigmoid(x):
    return jnp.minimum(x, 0.0) - jnp.log1p(jnp.exp(-jnp.abs(x)))


def _sigmoid(x):
    return 1.0 / (1.0 + jnp.exp(-x))


def _split3_f32(x):
    hi = x.astype(BF16).astype(F32)
    r = x - hi
    mid = r.astype(BF16).astype(F32)
    lo = (r - mid).astype(BF16).astype(F32)
    return hi, mid, lo


def _split2_f32(x):
    hi = x.astype(BF16).astype(F32)
    return hi, (x - hi).astype(BF16).astype(F32)


def _dot(a, b):
    return jnp.dot(a, b, preferred_element_type=F32)


def _dot_nt(a, b):
    return lax.dot_general(a, b, (((1,), (1,)), ((), ())), preferred_element_type=F32)


def _dot_tn(a, b):
    return lax.dot_general(a, b, (((0,), (0,)), ((), ())), preferred_element_type=F32)


def _rms(x, g):
    return x * lax.rsqrt(jnp.mean(x * x, axis=-1, keepdims=True) + EPS) * g


def _proj_kernel(x_ref, g1_ref, w_ref, wg2_ref, bg2_ref, bfc_ref, tri_ref,
                 *rest, tiles_per_seq, time_minor):
    o_kv, o_b = FOX_WIDTH, 3 * FOX_WIDTH
    o_tail = o_b + 2 * GLA_KW + 2 * GLA_VW
    carry_ref = rest[-1]

    @pl.when(pl.program_id(0) % tiles_per_seq == 0)
    def _():
        carry_ref[...] = jnp.zeros_like(carry_ref)

    h = _rms(x_ref[...], g1_ref[...]).astype(BF16)
    tm = h.shape[0]
    scale = FOX_HEAD_DIM ** -0.5

    zc = _dot_nt(h, w_ref[o_tail:, :])
    fl_t = _dot_nt(w_ref[o_tail:, :], h)[0:FOX_HEADS]

    if time_minor:
        (q_ref, k_ref, v_ref, lft_ref, ct_ref, gq_ref, gk_ref, gv_ref, glog_ref, gr_ref,
         carry_ref) = rest
        qt = (_dot_nt(w_ref[0:o_kv, :], h) * (scale * LOG2E)).astype(BF16)
        tq = q_ref.shape[2]
        for i in range(q_ref.shape[0]):
            q_ref[i] = qt[:, i * tq:(i + 1) * tq]
    else:
        (q_ref, k_ref, v_ref, lft_ref, ct_ref, gq_ref, gk_ref, gv_ref, glog_ref, gr_ref,
         carry_ref) = rest
        q_ref[...] = (_dot_nt(h, w_ref[0:o_kv, :]) * scale).astype(BF16)

    logf_t = _log_sigmoid(fl_t + bfc_ref[...])
    lft_ref[...] = logf_t
    parts = jnp.concatenate(_split3_f32(logf_t), axis=0).astype(BF16)
    gg = zc.astype(BF16)

    if time_minor:
        kvt = _dot_nt(w_ref[o_kv:o_b, :], h)
        k_ref[...] = kvt[:FOX_WIDTH].reshape(FOX_HEADS, FOX_HEAD_DIM, tm)
        v_ref[...] = kvt[FOX_WIDTH:].reshape(FOX_HEADS, FOX_HEAD_DIM, tm)
    else:
        kv = _dot_nt(h, w_ref[o_kv:o_b, :])
        k_ref[...] = kv[:, :FOX_WIDTH]
        v_ref[...] = kv[:, FOX_WIDTH:]

    gpre = _dot(gg, wg2_ref[...]) + bg2_ref[...]
    cs = _dot(parts, tri_ref[...])

    zb = _dot_nt(h, w_ref[o_b:o_tail, :])
    gq_ref[...] = zb[:, :GLA_KW]
    gk_ref[...] = zb[:, GLA_KW:2 * GLA_KW]
    gv_ref[...] = zb[:, 2 * GLA_KW:2 * GLA_KW + GLA_VW].astype(BF16)
    gr_ref[...] = zb[:, 2 * GLA_KW + GLA_VW:]

    glog_ref[...] = _log_sigmoid(gpre) * (1.0 / GLA_GATE_TEMP)

    ct = cs[0:8] + cs[8:16] + cs[16:24] + carry_ref[:, 0:1]
    ct_ref[...] = ct
    carry_ref[...] = jnp.broadcast_to(ct[:, tm - 1:], carry_ref.shape)


def _proj(x2d, batch, seq_len, prm, time_minor, tm=1024, tq=512):
    n = x2d.shape[0]
    tm = min(tm, n)
    tiles_per_seq = max(seq_len // tm, 1)
    per_tile = tm // tq
    idx = np.arange(tm)
    tri = ((idx[:, None] <= idx[None, :]) & (idx[:, None] // seq_len == idx[None, :] // seq_len))
    tri = jnp.asarray(tri.astype(np.float32), dtype=BF16)
    const = lambda a: pl.BlockSpec(a.shape, lambda i: (0, 0), pipeline_mode=pl.Buffered(1))
    row = lambda w: pl.BlockSpec((tm, w), lambda i: (i, 0))
    col = pl.BlockSpec((FOX_HEADS, tm), lambda i: (0, i))
    sds = jax.ShapeDtypeStruct
    if time_minor:
        kv_shape = sds((batch, FOX_HEADS, FOX_HEAD_DIM, seq_len), F32)
        kv_spec = pl.BlockSpec((None, FOX_HEADS, FOX_HEAD_DIM, tm),
                               lambda i: (i // tiles_per_seq, 0, 0, i % tiles_per_seq))
        fox = [(sds((batch, seq_len // tq, FOX_WIDTH, tq), BF16),
                pl.BlockSpec((None, per_tile, FOX_WIDTH, tq),
                             lambda i: (i // tiles_per_seq, i % tiles_per_seq, 0, 0))),
               (kv_shape, kv_spec), (kv_shape, kv_spec)]
    else:
        fox = [(sds((n, FOX_WIDTH), BF16), row(FOX_WIDTH)),
               (sds((n, FOX_WIDTH), F32), row(FOX_WIDTH)),
               (sds((n, FOX_WIDTH), F32), row(FOX_WIDTH))]
    outs = fox + [
        (sds((FOX_HEADS, n), F32), col),
        (sds((FOX_HEADS, n), F32), col),
        (sds((n, GLA_KW), F32), row(GLA_KW)),
        (sds((n, GLA_KW), F32), row(GLA_KW)),
        (sds((n, GLA_VW), BF16), row(GLA_VW)),
        (sds((n, GLA_KW), F32), row(GLA_KW)),
        (sds((n, GLA_VW), F32), row(GLA_VW)),
    ]
    names = ('g1', 'w', 'wg2', 'bg2', 'bfc')
    return pl.pallas_call(
        functools.partial(_proj_kernel, tiles_per_seq=tiles_per_seq, time_minor=time_minor),
        grid=(n // tm,),
        in_specs=[row(D_MODEL)] + [const(prm[k]) for k in names] + [const(tri)],
        out_specs=tuple(s for _, s in outs), out_shape=tuple(s for s, _ in outs),
        scratch_shapes=[pltpu.VMEM((FOX_HEADS, LANES), F32)],
        compiler_params=pltpu.CompilerParams(dimension_semantics=("arbitrary",),
                                             vmem_limit_bytes=VMEM_LIMIT),
        name="proj",
    )(x2d, *[prm[k] for k in names], tri)


def _fox_prompt_kernel(q_ref, k_ref, v_ref, c_ref, o_ref, kb_ref, vb_ref, s_ref, p_ref, acc_ref,
                       *, tk):
    g = pl.program_id(1)
    nq = q_ref.shape[0]
    nk = vb_ref.shape[1]
    pairs = q_ref.shape[1] // LANES
    hd = FOX_HEAD_DIM
    spare = (hd, 0)
    vrows = vb_ref.shape[2]

    parts = jnp.concatenate(_split3_f32(c_ref[...] * (-LOG2E)), axis=0).astype(BF16)
    r = lax.broadcasted_iota(jnp.int32, (3 * FOX_HEADS, LANES), 0)
    ln = lax.broadcasted_iota(jnp.int32, (3 * FOX_HEADS, LANES), 1)
    klane = lax.broadcasted_iota(jnp.int32, (k_ref.shape[2], LANES), 1)
    tail = jnp.where(lax.broadcasted_iota(jnp.int32, (vrows - hd, tk), 0) == 0, 1.0, 0.0)
    for pp in range(pairs):
        k = k_ref[2 * pp:2 * pp + 2].reshape(LANES, k_ref.shape[2]).T.astype(BF16)
        for hh in range(2):
            head = 2 * (pairs * g + pp) + hh
            place = (r % FOX_HEADS == head) & (ln == spare[hh] + r // FOX_HEADS)
            extra = _dot_tn(parts, jnp.where(place, 1.0, 0.0).astype(BF16))
            own = (klane < hd) if hh == 0 else (klane >= hd)
            kb_ref[2 * pp + hh] = jnp.where(own, k, extra.astype(BF16))
            for jj in range(nk):
                vt = v_ref[2 * pp + hh, :, jj * tk:(jj + 1) * tk]
                vb_ref[2 * pp + hh, jj] = jnp.concatenate([vt, tail], axis=0).astype(BF16)

    qrow = lax.broadcasted_iota(jnp.int32, (LANES, tk), 0)
    key = lax.broadcasted_iota(jnp.int32, (tk, tk), 0)
    qry = lax.broadcasted_iota(jnp.int32, (tk, tk), 1)
    causal = key <= qry

    chains = [(slot, part) for slot in range(2 * pairs) for part in range(2)]
    everyone = list(range(len(chains)))
    second = [n for n in everyone if chains[n][1] == 1]

    def q_operands(qi):
        qs = []
        for slot, part in chains:
            pp, hh = divmod(slot, 2)
            q = q_ref[qi, pp * LANES:(pp + 1) * LANES, part * tk:(part + 1) * tk]
            own = (qrow < hd) if hh == 0 else (qrow >= hd)
            ones = (qrow >= spare[hh]) & (qrow < spare[hh] + 3)
            qs.append(jnp.where(own, q, jnp.where(ones, 1.0, 0.0).astype(BF16)))
        return qs

    steps = []
    for qi in range(nq):
        steps += [(qi, j, everyone, None) for j in range(2 * qi)]
        steps += [(qi, 2 * qi, everyone, 0), (qi, 2 * qi + 1, second, 1)]

    qs_of = {}

    def scores(step, buf):
        qi, j, live, _ = step
        if qi not in qs_of:
            qs_of.clear()
            qs_of[qi] = q_operands(qi)
        for n in live:
            s_ref[buf, n] = _dot(kb_ref[chains[n][0], j * tk:(j + 1) * tk, :], qs_of[qi][n])

    def values(step, buf):
        _, j, live, _ = step
        return {n: _dot(vb_ref[chains[n][0], j], p_ref[buf, n]) for n in live}

    def finish(qi):
        for part in range(2):
            heads = [acc_ref[qi % 2, 2 * slot + part] for slot in range(2 * pairs)]
            o_ref[qi, :, part * tk:(part + 1) * tk] = jnp.concatenate(
                [a[:hd] / a[hd:hd + 1] for a in heads], axis=0).astype(BF16)

    def accumulate(step, alphas, pv):
        qi, j, live, _ = step
        for n in live:
            acc_ref[qi % 2, n] = pv[n] if j == 0 else alphas[n] * acc_ref[qi % 2, n] + pv[n]

    ms = {}
    scores(steps[0], 0)
    prev, prev_alphas = None, None
    for t, step in enumerate(steps):
        qi, j, live, masked_part = step
        if t + 1 < len(steps):
            scores(steps[t + 1], (t + 1) % 2)
        pv = values(prev, (t - 1) % 2) if prev is not None else None
        alphas = {}
        for n in live:
            s = s_ref[t % 2, n]
            if chains[n][1] == masked_part:
                s = jnp.where(causal, s, -jnp.inf)
            m_new = jnp.max(s, axis=0, keepdims=True)
            if j > 0:
                m_new = jnp.maximum(ms[n], m_new)
                alphas[n] = jnp.exp2(ms[n] - m_new)
            p_ref[t % 2, n] = jnp.exp2(s - m_new).astype(BF16)
            ms[n] = m_new
        if prev is not None:
            accumulate(prev, prev_alphas, pv)
            if prev[0] != qi:
                finish(prev[0])
        prev, prev_alphas = step, alphas
    accumulate(prev, prev_alphas, values(prev, (len(steps) - 1) % 2))
    finish(prev[0])


def _fox_prompt(qt, kt, vt, ct, batch, seq, tk=256, pairs=2):
    tq = 2 * tk
    nq = seq // tq
    nk = seq // tk
    heads = 2 * pairs
    chains = 2 * heads
    vrows = FOX_HEAD_DIM + 16
    assert qt.shape == (batch, nq, FOX_WIDTH, tq) and FOX_HEADS % heads == 0
    q_spec = pl.BlockSpec((None, nq, pairs * LANES, tq), lambda b, p: (b, 0, p, 0))
    kv_spec = pl.BlockSpec((None, heads, FOX_HEAD_DIM, seq), lambda b, p: (b, p, 0, 0))
    return pl.pallas_call(
        functools.partial(_fox_prompt_kernel, tk=tk),
        grid=(batch, FOX_HEADS // heads),
        in_specs=[q_spec, kv_spec, kv_spec,
                  pl.BlockSpec((FOX_HEADS, seq), lambda b, p: (0, b))],
        out_specs=q_spec,
        out_shape=jax.ShapeDtypeStruct((batch, nq, FOX_WIDTH, tq), BF16),
        scratch_shapes=[pltpu.VMEM((heads, seq, LANES), BF16), pltpu.VMEM((heads, nk, vrows, tk), BF16),
                        pltpu.VMEM((2, chains, tk, tk), F32), pltpu.VMEM((2, chains, tk, tk), BF16),
                        pltpu.VMEM((2, chains, vrows, tk), F32)],
        compiler_params=pltpu.CompilerParams(dimension_semantics=("arbitrary", "arbitrary"),
                                             vmem_limit_bytes=VMEM_LIMIT),
        name="fox_prompt",
    )(qt, kt, vt, ct)


def _fox_sample_kernel(q_ref, kn_ref, vn_ref, cn_ref, lft_ref, mlow_ref, ck_ref, cv_ref,
                       o_ref, qh_ref, suf_ref, m_ref, l_ref, acc_ref, *, tp, nt, tn, j=None):
    j = pl.program_id(1) if j is None else j
    blk = 2 * LANES
    hd = FOX_HEAD_DIM

    @pl.when(j == 0)
    def _init():
        for h in range(FOX_HEADS):
            qh_ref[h] = q_ref[:, h * hd:(h + 1) * hd]
        m_ref[...] = jnp.full(m_ref.shape, NEG_BIG, F32)
        l_ref[...] = jnp.zeros(l_ref.shape, F32)
        acc_ref[...] = jnp.zeros(acc_ref.shape, F32)
        x = lft_ref[...]
        parts = _split3_f32(x)
        per_tile = tp // blk
        nblk = nt * per_tile
        rows = jnp.concatenate([p_[:, b * blk:(b + 1) * blk] for b in range(nblk) for p_ in parts], axis=0)
        y = _dot(rows.astype(BF16), mlow_ref[...])
        carry = jnp.zeros((FOX_HEADS, 1), F32)
        for b in reversed(range(nblk)):
            yb = y[24 * b:24 * b + 8] + y[24 * b + 8:24 * b + 16] + y[24 * b + 16:24 * b + 24]
            off = (b % per_tile) * blk
            suf_ref[b // per_tile, :, off:off + blk] = yb + carry
            carry = carry + yb[:, 0:1] + x[:, b * blk:b * blk + 1]

    def update(blocks):
        m_old = m_ref[...]
        m_new = m_old
        for s, _ in blocks:
            m_new = jnp.maximum(m_new, jnp.max(s, axis=-1, keepdims=True))
        alpha = jnp.exp(m_old - m_new)
        l = alpha * l_ref[...]
        acc = alpha * acc_ref[...]
        for s, pv in blocks:
            pm = jnp.exp(s - m_new)
            l = l + jnp.sum(pm, axis=-1, keepdims=True)
            acc = acc + pv(pm.astype(BF16))
        return m_new, l, acc

    bmm = lambda a, b, ca, cb: lax.dot_general(a, b, (((ca,), (cb,)), ((0,), (0,))),
                                               preferred_element_type=F32)
    qh = qh_ref[...]

    def cache_block():
        kt = ck_ref[...].astype(BF16)
        vt = cv_ref[...].astype(BF16)
        s = bmm(qh, kt, 2, 1).reshape(FOX_HEADS * tn, tp) + jnp.repeat(suf_ref[j], tn, axis=0)
        return s.reshape(FOX_HEADS, tn, tp), lambda pm: bmm(pm, vt, 2, 2)

    def new_block():
        per_head = lambda ref: jnp.stack([ref[:, h * hd:(h + 1) * hd] for h in range(FOX_HEADS)],
                                         axis=0).astype(BF16)
        kn = per_head(kn_ref)
        vn = per_head(vn_ref)
        r = lax.broadcasted_iota(jnp.int32, (FOX_HEADS, tn, tn), 1)
        c = lax.broadcasted_iota(jnp.int32, (FOX_HEADS, tn, tn), 2)
        s = jnp.where(c <= r, bmm(qh, kn, 2, 2) - cn_ref[...][:, None, :], -jnp.inf)
        return s, lambda pm: bmm(pm, vn, 2, 1)

    @pl.when(j < nt - 1)
    def _():
        m, l, acc = update([cache_block()])
        m_ref[...] = m
        l_ref[...] = l
        acc_ref[...] = acc

    @pl.when(j == nt - 1)
    def _():
        _, l, acc = update([cache_block(), new_block()])
        o = acc / l
        o_ref[...] = jnp.concatenate([o[h] for h in range(FOX_HEADS)], axis=1).astype(BF16)


def _fox_sample_call(batch, tn, past, tp, ix):
    nt = past // tp
    blk = 2 * LANES
    idx = np.arange(blk)
    mlow = jnp.asarray((idx[:, None] > idx[None, :]).astype(np.float32), dtype=BF16)
    per_b = lambda w: pl.BlockSpec((tn, w), lambda *g: (ix(*g)[0], 0))
    cache = pl.BlockSpec((None, FOX_HEADS, FOX_HEAD_DIM, tp), lambda *g: (ix(*g)[0], 0, 0, ix(*g)[1]))
    return dict(
        kernel=functools.partial(_fox_sample_kernel, tp=tp, nt=nt, tn=tn), nt=nt, consts=(mlow,),
        in_specs=[per_b(FOX_WIDTH), per_b(FOX_WIDTH), per_b(FOX_WIDTH),
                  pl.BlockSpec((None, FOX_HEADS, tn), lambda *g: (ix(*g)[0], 0, 0)),
                  pl.BlockSpec((None, FOX_HEADS, past), lambda *g: (ix(*g)[0], 0, 0)),
                  pl.BlockSpec((blk, blk), lambda *g: (0, 0)),
                  cache, cache],
        out_specs=[per_b(FOX_WIDTH)],
        out_shape=[jax.ShapeDtypeStruct((batch * tn, FOX_WIDTH), BF16)],
        scratch_shapes=[pltpu.VMEM((FOX_HEADS, tn, FOX_HEAD_DIM), BF16),
                        pltpu.VMEM((nt, FOX_HEADS, tp), F32),
                        pltpu.VMEM((FOX_HEADS, tn, 1), F32), pltpu.VMEM((FOX_HEADS, tn, 1), F32),
                        pltpu.VMEM((FOX_HEADS, tn, FOX_HEAD_DIM), F32)])


def _gla_kernel(q_ref, k_ref, v_ref, g_ref, r_ref, s0_ref, gn_ref, w_ref, lv_ref,
                o_ref, s_ref, st_ref, *, tb, levels, blocks, seqs, carried, t=None, nt=None):
    to_work = lambda s0: s0.reshape(GLA_KW, GLA_DV).T
    from_work = lambda st: st.T.reshape(GLA_HEADS, GLA_DK, GLA_DV)
    if carried:
        t = pl.program_id(1) if t is None else t
        nt = pl.num_programs(1) if nt is None else nt

        @pl.when(t == 0)
        def _():
            st_ref[...] = to_work(s0_ref[0])

    for sq in range(seqs):
        st = st_ref[...] if carried else to_work(s0_ref[sq])
        for blk in range(blocks):
            rows = lambda ref: ref.at[pl.ds((sq * blocks + blk) * tb, tb), :]
            st = _gla_block(rows(q_ref), rows(k_ref), rows(v_ref), rows(g_ref), rows(r_ref), gn_ref,
                            w_ref, lv_ref, rows(o_ref), st, tb=tb, levels=levels)
        if carried:
            st_ref[...] = st

            @pl.when(t == nt - 1)
            def _():
                s_ref[0] = from_work(st)
        else:
            s_ref[sq] = from_work(st)


def _gla_block(q_ref, k_ref, v_ref, g_ref, r_ref, gn_ref, w_ref, lv_ref, o_ref, st, *, tb, levels):
    half = LANES // 2
    r = r_ref[...]
    gate = gn_ref[...] * (r * _sigmoid(r))

    g2 = g_ref[...] * LOG2E
    gp = jnp.concatenate(_split2_f32(g2), axis=0).astype(BF16)
    rows = w_ref.shape[0] // 2
    cum = jnp.concatenate([_dot(w_ref[0:rows, :], gp), _dot(w_ref[rows:, :], gp)], axis=0)
    dec = jnp.exp2(cum)
    from_start = dec[0:tb]
    to_end = jnp.exp2(cum[tb - 1:tb] - cum[0:tb])

    q = q_ref[...] * (GLA_DK ** -0.5)
    k = k_ref[...]
    v = v_ref[...]
    row = lax.broadcasted_iota(jnp.int32, (tb, GLA_KW), 0)
    low = lax.broadcasted_iota(jnp.int32, (tb, LANES), 1) < half
    lv = lv_ref[...]

    def pair_scores(xq, xk):
        outs = []
        for p in range(GLA_HEADS // 2):
            a = xq[:, p * LANES:(p + 1) * LANES]
            zero = jnp.zeros_like(a)
            lhs = jnp.concatenate([jnp.where(low, a, zero), jnp.where(low, zero, a)], axis=0)
            outs.append(_dot_nt(lhs, xk[:, p * LANES:(p + 1) * LANES]))
        return outs

    here = lv == -1
    a = [jnp.where(here, r_, 0.0) for r_ in pair_scores(q.astype(BF16), k.astype(BF16))]
    for l in range(levels):
        upper = ((row >> l) & 1) == 1
        e = jnp.where(upper, jnp.exp2(g2), 1.0) if l == 0 else dec[l * tb:(l + 1) * tb]
        x = (jnp.where(upper, q, k) * e).astype(BF16)
        here = lv == l
        a = [jnp.where(here, r_, a_) for r_, a_ in zip(pair_scores(x, x), a)]
    o = jnp.concatenate(
        [_dot(a[h // 2][(h % 2) * tb:(h % 2 + 1) * tb].astype(BF16), v[:, h * GLA_DV:(h + 1) * GLA_DV])
         for h in range(GLA_HEADS)], axis=1)

    lane = lax.broadcasted_iota(jnp.int32, (tb, GLA_KW), 1)
    head_sel = [(lane >= h * GLA_DK) & (lane < (h + 1) * GLA_DK) for h in range(GLA_HEADS)]
    qt = (q * from_start).astype(BF16)
    kt = (k * to_end).astype(BF16)
    zero = jnp.zeros_like(qt)
    q4 = jnp.concatenate([jnp.where(sel, qt, zero) for sel in head_sel], axis=0)
    oi = _dot_nt(q4, st.astype(BF16))
    o = o + jnp.concatenate([oi[h * tb:(h + 1) * tb] for h in range(GLA_HEADS)], axis=1)
    upd = None
    for h in range(GLA_HEADS):
        u = _dot_tn(v[:, h * GLA_DV:(h + 1) * GLA_DV], jnp.where(head_sel[h], kt, zero))
        upd = u if upd is None else upd + u
    st = from_start[tb - 1:tb, :] * st + upd

    outs = []
    for h in range(GLA_HEADS):
        oh = o[:, h * GLA_DV:(h + 1) * GLA_DV]
        outs.append(oh * lax.rsqrt(jnp.mean(oh * oh, axis=-1, keepdims=True) + EPS))
    o_ref[...] = (jnp.concatenate(outs, axis=1) * gate).astype(BF16)
    return st


def _gla_call(batch, seq, ix, tb=128, rows_per_step=1024):
    tb = min(tb, seq)
    step_rows = min(rows_per_step, batch * seq)
    carried = seq > step_rows
    seqs = 1 if carried else step_rows // seq
    per_seq = (step_rows if carried else seq) // tb
    nt = seq // (tb * per_seq)
    n = batch * seq
    levels = tb.bit_length() - 1
    assert tb == 1 << levels
    ti = np.arange(tb)[:, None]
    si = np.arange(tb)[None, :]
    blocks = [si <= ti]
    for l in range(1, levels):
        m = 1 << l
        mid = ti - ti % (2 * m) + m - 1
        upper = ti % (2 * m) >= m
        blocks.append(np.where(upper, (si > mid) & (si <= ti), (si > ti) & (si <= mid)))
    w = np.concatenate(blocks, axis=0).astype(np.float32)
    w = jnp.asarray(np.concatenate([w, w], axis=1), dtype=BF16)
    x = ti ^ si
    lv = np.where(si < ti, np.floor(np.log2(np.maximum(x, 1))).astype(np.int32),
                  np.where(si == ti, -1, -2)).astype(np.int32)
    lv = jnp.asarray(np.concatenate([lv, lv], axis=0))
    row = lambda w_: pl.BlockSpec((step_rows, w_), lambda *g: (ix(*g)[0] * nt + ix(*g)[1], 0))
    state = pl.BlockSpec((seqs, GLA_HEADS, GLA_DK, GLA_DV), lambda *g: (ix(*g)[0], 0, 0, 0))
    const = lambda shape: pl.BlockSpec(shape, lambda *g: (0, 0))
    return dict(
        kernel=functools.partial(_gla_kernel, tb=tb, levels=levels, blocks=per_seq, seqs=seqs,
                                 carried=carried),
        groups=batch // seqs, nt=nt, consts=(w, lv),
        in_specs=[row(GLA_KW), row(GLA_KW), row(GLA_VW), row(GLA_KW), row(GLA_VW), state,
                  const((1, GLA_VW)), const(w.shape), const(lv.shape)],
        out_specs=[row(GLA_VW), state],
        out_shape=[jax.ShapeDtypeStruct((n, GLA_VW), BF16),
                   jax.ShapeDtypeStruct((batch, GLA_HEADS, GLA_DK, GLA_DV), F32)],
        scratch_shapes=[pltpu.VMEM((GLA_DV, GLA_KW), F32)])


def _mixers_under_cache_stream(gla_p_args, fox_args, gla_s_args, gla_batch, gla_seq, batch, tn, past,
                               gla_rows=512, tp=4096):
    nt_g, nt_f = gla_seq // gla_rows, past // tp
    g = _gla_call(gla_batch, gla_seq, lambda i: (i // nt_g, i % nt_g), rows_per_step=gla_rows)
    f = _fox_sample_call(batch, tn, past, tp, lambda i: (i // nt_f, i % nt_f))
    h = _gla_call(batch, tn, lambda i: (i // nt_f, 0), rows_per_step=tn)
    steps = g['groups'] * nt_g
    assert steps == batch * nt_f and g['nt'] == nt_g and f['nt'] == nt_f and h['nt'] == 1
    calls = (g, f, h)

    def kernel(*refs):
        i = pl.program_id(0)
        refs = list(refs)
        take = lambda key: [[refs.pop(0) for _ in c[key]] for c in calls]
        (g_in, f_in, h_in), (g_out, f_out, h_out) = take('in_specs'), take('out_specs')
        g_scr, f_scr, h_scr = take('scratch_shapes')
        g['kernel'](*g_in, *g_out, *g_scr, t=i % nt_g, nt=nt_g)
        f['kernel'](*f_in, *f_out, *f_scr, j=i % nt_f)

        @pl.when(i % nt_f == nt_f - 1)
        def _():
            h['kernel'](*h_in, *h_out, *h_scr)

    cat = lambda key: [x for c in calls for x in c[key]]
    return pl.pallas_call(
        kernel, grid=(steps,), in_specs=cat('in_specs'), out_specs=cat('out_specs'),
        out_shape=cat('out_shape'), scratch_shapes=cat('scratch_shapes'),
        compiler_params=pltpu.CompilerParams(dimension_semantics=("arbitrary",),
                                             vmem_limit_bytes=VMEM_LIMIT),
        name="mixers",
    )(*gla_p_args, *g['consts'], *fox_args[:5], *f['consts'], *fox_args[5:], *gla_s_args, *h['consts'])


def _ffn_kernel(x_ref, fo_ref, go_ref, wo_ref, g2_ref, wg_ref, wu_ref, wd_ref, gf_ref,
                y_ref, a_ref, *, chunk, fox_time_minor):
    tm = x_ref.shape[0]
    parts = fo_ref.shape[0] if fox_time_minor else 2
    halves = [slice(i * (tm // parts), (i + 1) * (tm // parts)) for i in range(parts)]
    y1 = []
    for i, rows in enumerate(halves):
        fox = (_dot_tn(fo_ref[i], wo_ref[0:FOX_WIDTH, :]) if fox_time_minor
               else _dot(fo_ref[rows, :], wo_ref[0:FOX_WIDTH, :]))
        y1.append(x_ref[rows, :] + fox + _dot(go_ref[rows, :], wo_ref[FOX_WIDTH:, :]))
    h2 = jnp.concatenate([_rms(y, g2_ref[...]).astype(BF16) for y in y1], axis=0)
    for c in range(D_FF // chunk):
        cs = slice(c * chunk, (c + 1) * chunk)
        u = _dot(h2, wg_ref[:, cs])
        w = _dot(h2, wu_ref[:, cs])
        a_ref[:, cs] = (u * _sigmoid(u) * w).astype(BF16)
    y2 = [y + _dot(a_ref[rows, :], wd_ref[...]) for y, rows in zip(y1, halves)]
    for y, rows in zip(y2, halves):
        y_ref[rows, :] = _rms(y, gf_ref[...])


def _ffn(x2d, fo, go, prm, tm=1024, chunk=256):
    n = x2d.shape[0]
    tm = min(tm, n)
    row = lambda w: pl.BlockSpec((tm, w), lambda i: (i, 0))
    const = lambda shape: pl.BlockSpec(shape, lambda i: (0, 0), pipeline_mode=pl.Buffered(1))
    fox_time_minor = fo.ndim == 4
    if fox_time_minor:
        per_tile = tm // fo.shape[3]
        tiles_per_seq = fo.shape[1] // per_tile
        assert per_tile * fo.shape[3] == tm and tiles_per_seq * per_tile == fo.shape[1]
        fo_spec = pl.BlockSpec((None, per_tile, FOX_WIDTH, fo.shape[3]),
                               lambda i: (i // tiles_per_seq, i % tiles_per_seq, 0, 0))
    else:
        fo_spec = row(FOX_WIDTH)
    return pl.pallas_call(
        functools.partial(_ffn_kernel, chunk=chunk, fox_time_minor=fox_time_minor),
        grid=(n // tm,),
        in_specs=[row(D_MODEL), fo_spec, row(GLA_VW), const((D_MODEL, D_MODEL)),
                  const((1, D_MODEL)), const((D_MODEL, D_FF)), const((D_MODEL, D_FF)),
                  const((D_FF, D_MODEL)), const((1, D_MODEL))],
        out_specs=row(D_MODEL),
        out_shape=jax.ShapeDtypeStruct((n, D_MODEL), F32),
        scratch_shapes=[pltpu.VMEM((tm, D_FF), BF16)],
        compiler_params=pltpu.CompilerParams(dimension_semantics=("parallel",),
                                             vmem_limit_bytes=VMEM_LIMIT),
        name="ffn",
    )(x2d, fo, go, prm['wo'], prm['g2'], prm['wg'], prm['wu'], prm['wd'], prm['gf'])


def _layer_params(layer, norm1_g, w_in, w_gate2, b_gate2, b_forget, gla_norm_g, w_out,
                  norm2_g, w_gate, w_up, w_down, final_norm_g):
    wt = jnp.transpose(w_in[layer])
    o_fl = 3 * FOX_WIDTH
    o_gq = o_fl + FOX_HEADS
    o_gg = o_gq + 2 * GLA_KW + GLA_VW
    o_gr = o_gg + GLA_GATE_RANK
    tail = 2 * GLA_GATE_RANK
    pad = jnp.zeros((tail - FOX_HEADS - GLA_GATE_RANK, D_MODEL), F32)
    w = jnp.concatenate([wt[:o_fl], wt[o_gq:o_gg], wt[o_gr:], wt[o_fl:o_gq], wt[o_gg:o_gr], pad], axis=0)
    wg2 = jnp.zeros((tail, GLA_KW), F32).at[FOX_HEADS:FOX_HEADS + GLA_GATE_RANK].set(w_gate2[layer])
    return dict(
        g1=norm1_g[layer].reshape(1, D_MODEL),
        w=w.astype(BF16),
        wg2=wg2.astype(BF16),
        bg2=b_gate2[layer].reshape(1, GLA_KW),
        bfc=b_forget[layer].reshape(FOX_HEADS, 1),
        gn=gla_norm_g[layer].reshape(1, GLA_VW),
        wo=w_out[layer].astype(BF16),
        g2=norm2_g[layer].reshape(1, D_MODEL),
        wg=w_gate[layer].astype(BF16),
        wu=w_up[layer].astype(BF16),
        wd=w_down[layer].astype(BF16),
        gf=final_norm_g.reshape(1, D_MODEL),
    )


def kernel(x_prompt, x_sample, cache_fox_k, cache_fox_v, cache_fox_logf, state_gla, norm1_g, w_in,
           w_gate2, b_gate2, b_forget, gla_norm_g, w_out, norm2_g, w_gate, w_up, w_down, final_norm_g):
    depth = w_in.shape[0]
    assert depth == 1, "the final rmsnorm is fused into the layer's ffn kernel"
    bp, tp_, _ = x_prompt.shape
    bs, ts, _ = x_sample.shape
    past = cache_fox_k.shape[2]
    layer = 0
    prm = _layer_params(layer, norm1_g, w_in, w_gate2, b_gate2, b_forget, gla_norm_g, w_out,
                        norm2_g, w_gate, w_up, w_down, final_norm_g)
    by_time = lambda a, b, t: a.reshape(FOX_HEADS, b, t).transpose(1, 2, 0)[None]

    xp = x_prompt.reshape(bp * tp_, D_MODEL)
    qt, kt_p, vt_p, lf_p, ct, gq, gk, gv, glog, gr = _proj(xp, bp, tp_, prm, True)
    fox_p = _fox_prompt(qt, kt_p, vt_p, ct, bp, tp_)
    s0 = jnp.zeros((bp, GLA_HEADS, GLA_DK, GLA_DV), F32)
    gla_p_args = (gq, gk, gv, glog, gr, s0, prm['gn'])

    xs = x_sample.reshape(bs * ts, D_MODEL)
    q, k_s, v_s, lf_s, ct, gq, gk, gv, glog, gr = _proj(xs, bs, ts, prm, False)
    cn = ct.reshape(FOX_HEADS, bs, ts).transpose(1, 0, 2)
    lft = cache_fox_logf[layer].astype(F32).transpose(0, 2, 1)
    ck = cache_fox_k[layer].transpose(0, 2, 3, 1)
    cv = cache_fox_v[layer].transpose(0, 2, 3, 1)
    gla_s_args = (gq, gk, gv, glog, gr, state_gla[layer].astype(F32), prm['gn'])
    gla_p, s_p, fox_s, gla_s, s_s = _mixers_under_cache_stream(
        gla_p_args, (q, k_s, v_s, cn, lft, ck, cv), gla_s_args, bp, tp_, bs, ts, past)
    y_p = _ffn(xp, fox_p, gla_p, prm)
    y_s = _ffn(xs, fox_s, gla_s, prm)

    heads = lambda a, b, t: a.reshape(1, b, t, FOX_HEADS, FOX_HEAD_DIM)
    return (y_p.reshape(bp, tp_, D_MODEL), y_s.reshape(bs, ts, D_MODEL),
            kt_p.transpose(0, 3, 1, 2)[None], vt_p.transpose(0, 3, 1, 2)[None],
            by_time(lf_p, bp, tp_), s_p[None],
            heads(k_s, bs, ts), heads(v_s, bs, ts), by_time(lf_s, bs, ts), s_s[None])
```
